```python
import math
import jax, jax.numpy as jnp
from jax import lax
import numpy as np

D_MODEL = 1024
BATCH = 32
SEQ = 2048
DEPTH = 4

GRID_W = 64
CTX_LEN = 256
HEAD_DIM = 64
Q_BLOCK = 128
ROPE_THETA = 10000.0
LN_EPS = 1e-6
RMS_EPS = 1e-6
NEG_INF = -1e30

NA_HEADS = 8
NA_WIN_H = 8
NA_WIN_W = 16
GQA_HEADS = 8
GQA_KV_HEADS = 2
GQA_GROUP = GQA_HEADS // GQA_KV_HEADS
DIFF_HEADS = 4
DIFF_V_DIM = 2 * HEAD_DIM
MLA_HEADS = 8
MLA_Q_RANK = 256
MLA_KV_RANK = 128
MLA_NOPE = 64
MLA_ROPE = 32
MLA_V = 64
MLA_QK = MLA_NOPE + MLA_ROPE
N_EXPERTS = 32
TOP_K = 4
D_EXPERT = 1024
SWIGLU_ALPHA = 1.702
SWIGLU_LIMIT = 7.0

DEEPNORM_ALPHA = (2 * DEPTH) ** 0.25
DEEPNORM_BETA = (8 * DEPTH) ** -0.25
N_EVEN = (DEPTH + 1) // 2
N_ODD = DEPTH // 2

NA_W = NA_HEADS * HEAD_DIM
GQA_Q_W = GQA_HEADS * HEAD_DIM
GQA_KV_W = GQA_KV_HEADS * HEAD_DIM
EVEN_SPLITS = (NA_W, NA_W, NA_W, GQA_Q_W, GQA_KV_W, GQA_KV_W)
EVEN_IN = sum(EVEN_SPLITS)
EVEN_MIX = NA_W + GQA_Q_W
DIFF_QK_W = DIFF_HEADS * 2 * HEAD_DIM
DIFF_V_W = DIFF_HEADS * DIFF_V_DIM
ODD_SPLITS = (DIFF_QK_W, DIFF_QK_W, DIFF_V_W, MLA_Q_RANK, MLA_KV_RANK, MLA_ROPE)
ODD_IN = sum(ODD_SPLITS)
ODD_MIX = DIFF_V_W + MLA_HEADS * MLA_V

kernel_name = "hybrid_na_gqa_diff_mla_moe_dit"


def layer_norm(x, g, b):
    xf = x.astype(jnp.float32)
    mu = jnp.mean(xf, axis=-1, keepdims=True)
    var = jnp.mean(jnp.square(xf - mu), axis=-1, keepdims=True)
    y = (xf - mu) * lax.rsqrt(var + LN_EPS) * g.astype(jnp.float32) + b.astype(jnp.float32)
    return y.astype(x.dtype)


def rms_norm(x, g):
    xf = x.astype(jnp.float32)
    y = xf * lax.rsqrt(jnp.mean(jnp.square(xf), axis=-1, keepdims=True) + RMS_EPS)
    return (y * g.astype(jnp.float32)).astype(x.dtype)


def _split(x, sizes):
    return jnp.split(x, [int(s) for s in np.cumsum(sizes)[:-1]], axis=-1)


def _heads(x, n):
    B, T, _ = x.shape
    return x.reshape(B, T, n, -1).transpose(0, 2, 1, 3)


def _merge_heads(x):
    B, H, T, d = x.shape
    return x.transpose(0, 2, 1, 3).reshape(B, T, H * d)


def axial_rope_tables(row_pos, col_pos, dim):
    quarter = dim // 4
    inv_freq = ROPE_THETA ** (-jnp.arange(quarter, dtype=jnp.float32) / quarter)
    ang_r = row_pos.astype(jnp.float32)[:, None] * inv_freq
    ang_c = col_pos.astype(jnp.float32)[:, None] * inv_freq
    return (jnp.cos(ang_r), jnp.sin(ang_r), jnp.cos(ang_c), jnp.sin(ang_c))


def _rotate(x, cos, sin):
    x1, x2 = jnp.split(x, 2, axis=-1)
    return jnp.concatenate([x1 * cos - x2 * sin, x1 * sin + x2 * cos], axis=-1)


def apply_axial_rope(x, tables):
    cr, sr, cc, sc = tables
    xr, xc = jnp.split(x.astype(jnp.float32), 2, axis=-1)
    return jnp.concatenate([_rotate(xr, cr, sr), _rotate(xc, cc, sc)], axis=-1).astype(x.dtype)


def _probs(q, k, scale):
    s = jnp.einsum("bkgqd,bkld->bkgql", q, k)
    return jax.nn.softmax(s.astype(jnp.float32) * scale, axis=-1)


def _joint_probs(q, k_lat, k_ctx, scale):
    s = jnp.concatenate([jnp.einsum("bkgqd,bksd->bkgqs", q, k_lat),
                         jnp.einsum("bkgqd,bkld->bkgql", q, k_ctx)], axis=-1)
    return jax.nn.softmax(s.astype(jnp.float32) * scale, axis=-1)


def _joint_values(p, v_lat, v_ctx):
    S = v_lat.shape[2]
    p = p.astype(v_lat.dtype)
    return (jnp.einsum("bkgqs,bksd->bkgqd", p[..., :S], v_lat)
            + jnp.einsum("bkgql,bkld->bkgqd", p[..., S:], v_ctx))


def _query_blocks(q):
    B, K, G, S, d = q.shape
    return q.reshape(B, K, G, S // Q_BLOCK, Q_BLOCK, d).transpose(3, 0, 1, 2, 4, 5)


def _merge_blocks(o):
    nb, B, K, G, Qb, d = o.shape
    return o.transpose(1, 2, 3, 0, 4, 5).reshape(B, K, G, nb * Qb, d)


def latent_attention(q, k_lat, v_lat, k_ctx, v_ctx, scale):
    out = lax.map(lambda qb: _joint_values(_joint_probs(qb, k_lat, k_ctx, scale), v_lat, v_ctx),
                  _query_blocks(q))
    return _merge_blocks(out)


def context_attention(q, k, v, scale):
    p = _probs(q, k, scale).astype(v.dtype)
    return jnp.einsum("bkgql,bkld->bkgqd", p, v)


def diff_latent_attention(q1, q2, k1, k2, v, k1c, k2c, vc, lam, scale):
    def one(args):
        a1, a2 = args
        p = _joint_probs(a1, k1, k1c, scale) - lam * _joint_probs(a2, k2, k2c, scale)
        return _joint_values(p, v, vc)
    return _merge_blocks(lax.map(one, (_query_blocks(q1), _query_blocks(q2))))


def diff_context_attention(q1, q2, k1, k2, v, lam, scale):
    p = _probs(q1, k1, scale) - lam * _probs(q2, k2, scale)
    return jnp.einsum("bkgql,bkld->bkgqd", p.astype(v.dtype), v)


def neighbourhood_attention(q, k, v, k_ctx, v_ctx, rpb, rows):
    B, H, S, d = q.shape
    scale = d ** -0.5
    wh = min(NA_WIN_H, rows)
    qg = q.reshape(B, H, rows, GRID_W, d).transpose(2, 0, 1, 3, 4)
    kg = k.reshape(B, H, rows, GRID_W, d)
    vg = v.reshape(B, H, rows, GRID_W, d)
    cols = jnp.arange(GRID_W)
    col_start = jnp.clip(cols - NA_WIN_W // 2, 0, GRID_W - NA_WIN_W)
    col_mask = (cols[None, :] >= col_start[:, None]) & (cols[None, :] < col_start[:, None] + NA_WIN_W)
    col_idx = jnp.clip(cols[None, :] - cols[:, None] + NA_WIN_W - 1, 0, 2 * NA_WIN_W - 2)
    rpb = rpb.astype(jnp.float32)

    def one(args):
        r, qr = args
        rs = jnp.clip(r - NA_WIN_H // 2, 0, rows - wh)
        kr = lax.dynamic_slice_in_dim(kg, rs, wh, axis=2)
        vr = lax.dynamic_slice_in_dim(vg, rs, wh, axis=2)
        row_idx = rs + jnp.arange(wh) - r + NA_WIN_H - 1
        bias = rpb[:, row_idx][:, :, col_idx].transpose(0, 2, 1, 3)
        s_loc = jnp.einsum("bhqd,bhrkd->bhqrk", qr, kr).astype(jnp.float32) * scale + bias[None]
        s_loc = jnp.where(col_mask[:, None, :], s_loc, NEG_INF).reshape(B, H, GRID_W, wh * GRID_W)
        s_ctx = jnp.einsum("bhqd,bhld->bhql", qr, k_ctx).astype(jnp.float32) * scale
        p = jax.nn.softmax(jnp.concatenate([s_loc, s_ctx], axis=-1), axis=-1).astype(v.dtype)
        p_loc = p[..., :wh * GRID_W].reshape(B, H, GRID_W, wh, GRID_W)
        return (jnp.einsum("bhqrk,bhrkd->bhqd", p_loc, vr)
                + jnp.einsum("bhql,bhld->bhqd", p[..., wh * GRID_W:], v_ctx))

    out = lax.map(one, (jnp.arange(rows), qg))
    return out.transpose(1, 2, 0, 3, 4).reshape(B, H, S, d)


def even_mixer(h_lat, h_ctx, w_in, w_out, rpb, q_gain, k_gain, rope64, rows, with_ctx):
    B, S, _ = h_lat.shape
    L = h_ctx.shape[1]
    scale = HEAD_DIM ** -0.5
    lat = _split(h_lat @ w_in, EVEN_SPLITS)
    ctp = _split(h_ctx @ w_in, EVEN_SPLITS)
    na_kc, na_vc = _heads(ctp[1], NA_HEADS), _heads(ctp[2], NA_HEADS)
    na_lat = neighbourhood_attention(_heads(lat[0], NA_HEADS), _heads(lat[1], NA_HEADS),
                                     _heads(lat[2], NA_HEADS), na_kc, na_vc, rpb, rows)
    gq = apply_axial_rope(rms_norm(_heads(lat[3], GQA_HEADS), q_gain), rope64)
    gk = apply_axial_rope(rms_norm(_heads(lat[4], GQA_KV_HEADS), k_gain), rope64)
    gv = _heads(lat[5], GQA_KV_HEADS)
    gkc = rms_norm(_heads(ctp[4], GQA_KV_HEADS), k_gain)
    gvc = _heads(ctp[5], GQA_KV_HEADS)
    g_lat = latent_attention(gq.reshape(B, GQA_KV_HEADS, GQA_GROUP, S, HEAD_DIM), gk, gv, gkc, gvc, scale)
    g_lat = g_lat.reshape(B, GQA_HEADS, S, HEAD_DIM)
    out_lat = jnp.concatenate([_merge_heads(na_lat), _merge_heads(g_lat)], axis=-1) @ w_out
    out_ctx = None
    if with_ctx:
        na_ctx = context_attention(_heads(ctp[0], NA_HEADS)[:, :, None], na_kc, na_vc, scale)[:, :, 0]
        gqc = rms_norm(_heads(ctp[3], GQA_HEADS), q_gain).reshape(B, GQA_KV_HEADS, GQA_GROUP, L, HEAD_DIM)
        g_ctx = context_attention(gqc, gkc, gvc, scale).reshape(B, GQA_HEADS, L, HEAD_DIM)
        out_ctx = jnp.concatenate([_merge_heads(na_ctx), _merge_heads(g_ctx)], axis=-1) @ w_out
    return out_lat, out_ctx


def _mla_q(cq, q_gain, w_uq, rope):
    q = _heads(rms_norm(cq, q_gain) @ w_uq, MLA_HEADS)
    q_nope, q_pe = q[..., :MLA_NOPE], q[..., MLA_NOPE:]
    if rope is not None:
        q_pe = apply_axial_rope(q_pe, rope)
    return jnp.concatenate([q_nope, q_pe], axis=-1)


def _mla_kv(ckv, k_pe, kv_gain, w_ukv, rope):
    B, T, _ = ckv.shape
    kv = _heads(rms_norm(ckv, kv_gain) @ w_ukv, MLA_HEADS)
    k_nope, v = kv[..., :MLA_NOPE], kv[..., MLA_NOPE:]
    k_pe = k_pe[:, None]
    if rope is not None:
        k_pe = apply_axial_rope(k_pe, rope)
    k = jnp.concatenate([k_nope, jnp.broadcast_to(k_pe, (B, MLA_HEADS, T, MLA_ROPE))], axis=-1)
    return k, v


def odd_mixer(h_lat, h_ctx, w_in, w_out, lq1, lk1, lq2, lk2, subln, mq_gain, w_uq, mkv_gain, w_ukv,
              lam_init, rope64, rope32, with_ctx):
    lat = _split(h_lat @ w_in, ODD_SPLITS)
    ctp = _split(h_ctx @ w_in, ODD_SPLITS)
    d_scale = HEAD_DIM ** -0.5
    m_scale = MLA_QK ** -0.5
    lam = (jnp.exp(jnp.sum(lq1.astype(jnp.float32) * lk1.astype(jnp.float32)))
           - jnp.exp(jnp.sum(lq2.astype(jnp.float32) * lk2.astype(jnp.float32))) + lam_init)
    rot = lambda t: apply_axial_rope(t, rope64)
    dq, dk, dv = _heads(lat[0], DIFF_HEADS), _heads(lat[1], DIFF_HEADS), _heads(lat[2], DIFF_HEADS)
    dkc, dvc = _heads(ctp[1], DIFF_HEADS), _heads(ctp[2], DIFF_HEADS)
    k1c, k2c = dkc[..., :HEAD_DIM], dkc[..., HEAD_DIM:]
    d_lat = diff_latent_attention(rot(dq[..., :HEAD_DIM])[:, :, None], rot(dq[..., HEAD_DIM:])[:, :, None],
                                  rot(dk[..., :HEAD_DIM]), rot(dk[..., HEAD_DIM:]), dv,
                                  k1c, k2c, dvc, lam, d_scale)[:, :, 0]
    d_lat = rms_norm(d_lat, subln) * (1.0 - lam_init)
    mk, mv = _mla_kv(lat[4], lat[5], mkv_gain, w_ukv, rope32)
    mkc, mvc = _mla_kv(ctp[4], ctp[5], mkv_gain, w_ukv, None)
    mq = _mla_q(lat[3], mq_gain, w_uq, rope32)
    m_lat = latent_attention(mq[:, :, None], mk, mv, mkc, mvc, m_scale)[:, :, 0]
    out_lat = jnp.concatenate([_merge_heads(d_lat), _merge_heads(m_lat)], axis=-1) @ w_out
    out_ctx = None
    if with_ctx:
        dqc = _heads(ctp[0], DIFF_HEADS)
        d_ctx = diff_context_attention(dqc[..., :HEAD_DIM][:, :, None], dqc[..., HEAD_DIM:][:, :, None],
                                       k1c, k2c, dvc, lam, d_scale)[:, :, 0]
        d_ctx = rms_norm(d_ctx, subln) * (1.0 - lam_init)
        mqc = _mla_q(ctp[3], mq_gain, w_uq, None)
        m_ctx = context_attention(mqc[:, :, None], mkc, mvc, m_scale)[:, :, 0]
        out_ctx = jnp.concatenate([_merge_heads(d_ctx), _merge_heads(m_ctx)], axis=-1) @ w_out
    return out_lat, out_ctx


def moe(h, router_w, router_b, w_gu, b_gu, w_down, b_down):
    B, T, D = h.shape
    t = h.reshape(B * T, D)
    logits = (t @ router_w + router_b).astype(jnp.float32)
    top_val, top_idx = lax.top_k(logits, TOP_K)
    top_w = jax.nn.softmax(top_val, axis=-1)
    gates = jnp.sum(jax.nn.one_hot(top_idx, N_EXPERTS, dtype=jnp.float32) * top_w[..., None], axis=1)

    def expert(acc, args):
        wgu, bgu, wd, bd, g = args
        gu = t @ wgu + bgu
        x_glu = jnp.minimum(gu[..., 0::2], SWIGLU_LIMIT)
        x_lin = jnp.clip(gu[..., 1::2], -SWIGLU_LIMIT, SWIGLU_LIMIT)
        y = ((x_lin + 1.0) * (x_glu * jax.nn.sigmoid(SWIGLU_ALPHA * x_glu))) @ wd + bd
        return acc + g[:, None] * y.astype(jnp.float32), None

    acc, _ = lax.scan(expert, jnp.zeros((B * T, D), jnp.float32), (w_gu, b_gu, w_down, b_down, gates.T))
    return acc.astype(h.dtype).reshape(B, T, D)


def _modulation(cvec, w, b):
    return jax.nn.silu(cvec) @ w + b


def setup_inputs(seed: int = 0) -> dict:
    key = jax.random.key(seed)
    keys = iter(jax.random.split(key, 48))

    def nrm(shape, scale):
        return jax.random.normal(next(keys), shape, jnp.float32) * scale

    def gain(shape):
        return 1.0 + nrm(shape, 0.05)

    D = D_MODEL
    beta = DEEPNORM_BETA
    return {
        "x": nrm((BATCH, SEQ, D), 1.0),
        "c": nrm((BATCH, D), 1.0),
        "ctx": nrm((BATCH, CTX_LEN, D), 1.0),
        "c_ctx": nrm((D,), 1.0),
        "mod_w": nrm((DEPTH, D, 6 * D), D ** -0.5),
        "mod_b": nrm((DEPTH, 6 * D), 0.02),
        "ln1_g": gain((DEPTH, D)),
        "ln1_b": nrm((DEPTH, D), 0.02),
        "ln2_g": gain((DEPTH, D)),
        "ln2_b": nrm((DEPTH, D), 0.02),
        "even_w_in": nrm((N_EVEN, D, EVEN_IN), D ** -0.5),
        "even_w_out": nrm((N_EVEN, EVEN_MIX, D), beta * EVEN_MIX ** -0.5),
        "na_rpb": nrm((N_EVEN, NA_HEADS, 2 * NA_WIN_H - 1, 2 * NA_WIN_W - 1), 0.2),
        "gqa_q_gain": gain((N_EVEN, HEAD_DIM)),
        "gqa_k_gain": gain((N_EVEN, HEAD_DIM)),
        "odd_w_in": nrm((N_ODD, D, ODD_IN), D ** -0.5),
        "odd_w_out": nrm((N_ODD, ODD_MIX, D), beta * ODD_MIX ** -0.5),
        "diff_lq1": nrm((N_ODD, HEAD_DIM), 0.1),
        "diff_lk1": nrm((N_ODD, HEAD_DIM), 0.1),
        "diff_lq2": nrm((N_ODD, HEAD_DIM), 0.1),
        "diff_lk2": nrm((N_ODD, HEAD_DIM), 0.1),
        "diff_subln": gain((N_ODD, DIFF_V_DIM)),
        "mla_q_gain": gain((N_ODD, MLA_Q_RANK)),
        "mla_w_uq": nrm((N_ODD, MLA_Q_RANK, MLA_HEADS * MLA_QK), MLA_Q_RANK ** -0.5),
        "mla_kv_gain": gain((N_ODD, MLA_KV_RANK)),
        "mla_w_ukv": nrm((N_ODD, MLA_KV_RANK, MLA_HEADS * (MLA_NOPE + MLA_V)), MLA_KV_RANK ** -0.5),
        "router_w": nrm((DEPTH, D, N_EXPERTS), D ** -0.5),
        "router_b": nrm((DEPTH, N_EXPERTS), 0.01),
        "exp_w_gu": nrm((DEPTH, N_EXPERTS, D, 2 * D_EXPERT), D ** -0.5),
        "exp_b_gu": nrm((DEPTH, N_EXPERTS, 2 * D_EXPERT), 0.01),
        "exp_w_down": nrm((DEPTH, N_EXPERTS, D_EXPERT, D), beta * D_EXPERT ** -0.5),
        "exp_b_down": nrm((DEPTH, N_EXPERTS, D), 0.01),
    }


def reference(x, c, ctx, c_ctx, mod_w, mod_b, ln1_g, ln1_b, ln2_g, ln2_b,
              even_w_in, even_w_out, na_rpb, gqa_q_gain, gqa_k_gain,
              odd_w_in, odd_w_out, diff_lq1, diff_lk1, diff_lq2, diff_lk2, diff_subln,
              mla_q_gain, mla_w_uq, mla_kv_gain, mla_w_ukv,
              router_w, router_b, exp_w_gu, exp_b_gu, exp_w_down, exp_b_down):
    B, S, D = x.shape
    L = ctx.shape[1]
    rows = S // GRID_W
    pos = jnp.arange(S)
    row_pos, col_pos = pos // GRID_W, pos % GRID_W
    rope64 = axial_rope_tables(row_pos, col_pos, HEAD_DIM)
    rope32 = axial_rope_tables(row_pos, col_pos, MLA_ROPE)
    for i in range(DEPTH):
        last = i == DEPTH - 1
        j = i // 2
        mod_lat = _modulation(c, mod_w[i], mod_b[i])[:, None, :]
        mod_ctx = _modulation(c_ctx, mod_w[i], mod_b[i])[None, None, :]
        sh1, sc1, g1, sh2, sc2, g2 = jnp.split(mod_lat, 6, axis=-1)
        csh1, csc1, cg1, csh2, csc2, cg2 = jnp.split(mod_ctx, 6, axis=-1)
        h_lat = x * (1.0 + sc1) + sh1
        h_ctx = ctx * (1.0 + csc1) + csh1
        if i % 2 == 0:
            o_lat, o_ctx = even_mixer(h_lat, h_ctx, even_w_in[j], even_w_out[j], na_rpb[j],
                                      gqa_q_gain[j], gqa_k_gain[j], rope64, rows, not last)
        else:
            lam_init = 0.8 - 0.6 * math.exp(-0.3 * i)
            o_lat, o_ctx = odd_mixer(h_lat, h_ctx, odd_w_in[j], odd_w_out[j], diff_lq1[j], diff_lk1[j],
                                     diff_lq2[j], diff_lk2[j], diff_subln[j], mla_q_gain[j], mla_w_uq[j],
                                     mla_kv_gain[j], mla_w_ukv[j], lam_init, rope64, rope32, not last)
        x = layer_norm(DEEPNORM_ALPHA * x + g1 * o_lat, ln1_g[i], ln1_b[i])
        if last:
            y = moe(x * (1.0 + sc2) + sh2, router_w[i], router_b[i],
                    exp_w_gu[i], exp_b_gu[i], exp_w_down[i], exp_b_down[i])
            x = layer_norm(DEEPNORM_ALPHA * x + g2 * y, ln2_g[i], ln2_b[i])
        else:
            ctx = layer_norm(DEEPNORM_ALPHA * ctx + cg1 * o_ctx, ln1_g[i], ln1_b[i])
            h2 = jnp.concatenate([ctx * (1.0 + csc2) + csh2, x * (1.0 + sc2) + sh2], axis=1)
            y = moe(h2, router_w[i], router_b[i], exp_w_gu[i], exp_b_gu[i], exp_w_down[i], exp_b_down[i])
            ctx = layer_norm(DEEPNORM_ALPHA * ctx + cg2 * y[:, :L], ln2_g[i], ln2_b[i])
            x = layer_norm(DEEPNORM_ALPHA * x + g2 * y[:, L:], ln2_g[i], ln2_b[i])
    return x
```

```python
import functools
import math

import numpy as np
import jax
import jax.numpy as jnp
from jax import lax
from jax.experimental import pallas as pl
from jax.experimental.pallas import tpu as pltpu

F32 = jnp.float32
BF16 = jnp.bfloat16

GRID_W = 64
HEAD_DIM = 64
ROPE_THETA = 10000.0
LN_EPS = 1e-6
RMS_EPS = 1e-6
NEG_INF = -1e30
NA_HEADS = 8
NA_WIN_H = 8
NA_WIN_W = 16
GQA_HEADS = 8
GQA_KV_HEADS = 2
DIFF_HEADS = 4
MLA_HEADS = 8
MLA_Q_RANK = 256
MLA_KV_RANK = 128
MLA_NOPE = 64
MLA_ROPE = 32
MLA_V = 64
MLA_QK = MLA_NOPE + MLA_ROPE
TOP_K = 4
SWIGLU_ALPHA = 1.702
SWIGLU_LIMIT = 7.0

LANES = 128
VMEM_LIMIT = 56 * 1024 * 1024
MOE_TILE = 512


def _cparams(n_axes):
    return pltpu.CompilerParams(dimension_semantics=("arbitrary",) * n_axes,
                                vmem_limit_bytes=VMEM_LIMIT)


def _dot(a, b):
    return jnp.dot(a, b, preferred_element_type=F32)


def _dot_nt(a, b):
    return lax.dot_general(a, b, (((1,), (1,)), ((), ())), preferred_element_type=F32)


def _mod_kernel(c_ref, w_ref, b_ref, o_ref):
    cv = c_ref[...]
    a = (cv * jax.nn.sigmoid(cv)).astype(BF16)
    o_ref[0] = _dot(a, w_ref[0].astype(BF16)) + b_ref[0]


def _modulation_all(c_all, mod_w, mod_b):
    depth, d, n = mod_w.shape
    rows = c_all.shape[0]
    tn = 1536
    return pl.pallas_call(
        _mod_kernel,
        grid=(depth, n // tn),
        in_specs=[pl.BlockSpec((rows, d), lambda i, j: (0, 0)),
                  pl.BlockSpec((1, d, tn), lambda i, j: (i, 0, j)),
                  pl.BlockSpec((1, 1, tn), lambda i, j: (i, 0, j))],
        out_specs=pl.BlockSpec((1, rows, tn), lambda i, j: (i, 0, j)),
        out_shape=jax.ShapeDtypeStruct((depth, rows, n), F32),
        compiler_params=_cparams(2),
    )(c_all, mod_w, mod_b.reshape(depth, 1, n))


def _rope_tables(s, dim):
    pos = jnp.arange(s)
    row = (pos // GRID_W).astype(F32)[:, None]
    col = (pos % GRID_W).astype(F32)[:, None]
    quarter = dim // 4
    inv_freq = ROPE_THETA ** (-jnp.arange(quarter, dtype=F32) / quarter)
    ar, ac = row * inv_freq, col * inv_freq
    cos = jnp.concatenate([jnp.cos(ar), jnp.cos(ar), jnp.cos(ac), jnp.cos(ac)], axis=-1)
    sin = jnp.concatenate([-jnp.sin(ar), jnp.sin(ar), -jnp.sin(ac), jnp.sin(ac)], axis=-1)
    return cos, sin


def _swap_perm(dim):
    q = dim // 4
    idx = np.arange(dim)
    return np.where((idx % (2 * q)) < q, idx + q, idx - q)


def _swap_cols(w, dim):
    n = w.shape[-1] // dim
    perm = (np.arange(n)[:, None] * dim + _swap_perm(dim)[None, :]).reshape(-1)
    return w[..., perm]


def _block_ones(n, blk):
    i = np.arange(n) // blk
    return jnp.asarray((i[:, None] == i[None, :]).astype(np.float32), dtype=BF16)


E_NAQ, E_NAK, E_NAV, E_GQ, E_GQS, E_GK, E_GKS, E_GV, E_END = (
    0, 512, 1024, 1536, 2048, 2560, 2816, 3072, 3328)


def _even_weights(w_in, q_gain, k_gain):
    naq, nak, nav, gq, gk, gv = jnp.split(w_in, [512, 1024, 1536, 2048, 2176], axis=-1)
    k0, k1 = gk[:, :64], gk[:, 64:]
    v0, v1 = gv[:, :64], gv[:, 64:]
    gk2 = jnp.concatenate([k0, k1, k1, k0], axis=-1)
    gv2 = jnp.concatenate([v0, v1, v1, v0], axis=-1)
    w = jnp.concatenate([naq * 0.125, nak, nav, gq, _swap_cols(gq, 64),
                         gk2, _swap_cols(gk2, 64), gv2], axis=-1).astype(BF16)
    return w


def _even_tables(s, q_gain, k_gain, rope):
    sw = _swap_perm(64)
    if rope:
        cos, sin = _rope_tables(s, 64)
    else:
        cos, sin = jnp.ones((s, 64), F32), jnp.zeros((s, 64), F32)
    qa = jnp.tile(cos * q_gain[None, :] * 0.125, (1, 8))
    qb = jnp.tile(sin * q_gain[sw][None, :] * 0.125, (1, 8))
    ka = jnp.tile(cos * k_gain[None, :], (1, 4))
    kb = jnp.tile(sin * k_gain[sw][None, :], (1, 4))
    return qa, qb, ka, kb


def _inproj_even_kernel(x_ref, sc_ref, sh_ref, w_ref, g512_ref, g256_ref,
                        qa_ref, qb_ref, ka_ref, kb_ref,
                        naq_ref, nak_ref, nav_ref, gq_ref, gk_ref, gv_ref):
    h = (x_ref[0] * (1.0 + sc_ref[0]) + sh_ref[0]).astype(BF16)
    naq_ref[0] = _dot(h, w_ref[:, E_NAQ:E_NAK]).astype(BF16)
    nak_ref[0] = _dot(h, w_ref[:, E_NAK:E_NAV]).astype(BF16)
    nav_ref[0] = _dot(h, w_ref[:, E_NAV:E_GQ]).astype(BF16)
    gv_ref[0] = _dot(h, w_ref[:, E_GV:E_END]).astype(BF16)
    y = _dot(h, w_ref[:, E_GQ:E_GQS])
    ys = _dot(h, w_ref[:, E_GQS:E_GK])
    r = lax.rsqrt(_dot((y * y).astype(BF16), g512_ref[...]) * (1.0 / HEAD_DIM) + RMS_EPS)
    gq_ref[0] = (r * (y * qa_ref[...] + ys * qb_ref[...])).astype(BF16)
    y = _dot(h, w_ref[:, E_GK:E_GKS])
    ys = _dot(h, w_ref[:, E_GKS:E_GV])
    r = lax.rsqrt(_dot((y * y).astype(BF16), g256_ref[...]) * (1.0 / HEAD_DIM) + RMS_EPS)
    gk_ref[0] = (r * (y * ka_ref[...] + ys * kb_ref[...])).astype(BF16)


def _mod_spec(d, batched):
    if batched:
        return pl.BlockSpec((1, 1, d), lambda b, s: (b, 0, 0))
    return pl.BlockSpec((1, 1, d), lambda b, s: (0, 0, 0))


def _const_spec(shape):
    nd = len(shape)
    return pl.BlockSpec(shape, lambda b, s: (0,) * nd)


def _inproj_even(x, sc, sh, w, tables, ts):
    bsz, s, d = x.shape
    batched = sc.shape[0] > 1
    qa, qb, ka, kb = tables
    tok = lambda n: pl.BlockSpec((1, ts, n), lambda b, i: (b, i, 0))
    tab = lambda n: pl.BlockSpec((ts, n), lambda b, i: (i, 0))
    widths = (512, 512, 512, 512, 256, 256)
    return pl.pallas_call(
        _inproj_even_kernel,
        grid=(bsz, s // ts),
        in_specs=[tok(d), _mod_spec(d, batched), _mod_spec(d, batched), _const_spec(w.shape),
                  _const_spec((512, 512)), _const_spec((256, 256)),
                  tab(512), tab(512), tab(256), tab(256)],
        out_specs=[tok(n) for n in widths],
        out_shape=[jax.ShapeDtypeStruct((bsz, s, n), BF16) for n in widths],
        compiler_params=_cparams(2),
    )(x, sc, sh, w, _block_ones(512, 64), _block_ones(256, 64), qa, qb, ka, kb)


O_DQ, O_DQS, O_DK, O_DKS, O_DV, O_CQ, O_CKV, O_PE, O_END = (
    0, 512, 1024, 1536, 2048, 2560, 2816, 2944, 3072)


def _odd_weights(w_in, w_uq, w_ukv):
    dq, dk, dv, cq, ckv, kpe = jnp.split(w_in, [512, 1024, 1536, 1792, 1920], axis=-1)
    d = w_in.shape[0]
    pe_slot = jnp.concatenate([kpe, _swap_cols(kpe, 32), jnp.zeros((d, 64), F32)], axis=-1)
    w = jnp.concatenate([dq, _swap_cols(dq, 64), dk, _swap_cols(dk, 64), dv, cq, ckv, pe_slot],
                        axis=-1).astype(BF16)
    uq = w_uq.reshape(MLA_Q_RANK, MLA_HEADS, MLA_QK)
    z32 = jnp.zeros((MLA_Q_RANK, MLA_HEADS, 32), F32)
    z64 = jnp.zeros((MLA_Q_RANK, MLA_HEADS, 64), F32)
    uq_pad = jnp.concatenate([uq, z32], axis=-1).reshape(MLA_Q_RANK, MLA_HEADS * LANES)
    uq_sw = jnp.concatenate([z64, _swap_cols(uq[..., MLA_NOPE:], 32), z32],
                            axis=-1).reshape(MLA_Q_RANK, MLA_HEADS * LANES)
    wuq2 = jnp.concatenate([uq_pad, uq_sw], axis=-1).astype(BF16)
    ukv = w_ukv.reshape(MLA_KV_RANK, MLA_HEADS, MLA_NOPE + MLA_V)
    zk = jnp.zeros((MLA_KV_RANK, MLA_HEADS, 64), F32)
    wk_pad = jnp.concatenate([ukv[..., :MLA_NOPE], zk], axis=-1).reshape(MLA_KV_RANK, MLA_HEADS * LANES)
    place = np.zeros((LANES, MLA_HEADS, LANES), np.float32)
    for j in range(MLA_ROPE):
        place[j, :, MLA_NOPE + j] = 1.0
        place[MLA_ROPE + j, :, MLA_NOPE + j] = 1.0
    wk2 = jnp.concatenate([wk_pad, jnp.asarray(place.reshape(LANES, MLA_HEADS * LANES))],
                          axis=0).astype(BF16)
    wv = ukv[..., MLA_NOPE:].reshape(MLA_KV_RANK, MLA_HEADS * MLA_V).astype(BF16)
    return w, wuq2, wk2, wv


def _odd_tables(s, rope):
    m_scale = MLA_QK ** -0.5
    if rope:
        cos64, sin64 = _rope_tables(s, 64)
        cos32, sin32 = _rope_tables(s, 32)
    else:
        cos64, sin64 = jnp.ones((s, 64), F32), jnp.zeros((s, 64), F32)
        cos32, sin32 = jnp.ones((s, 32), F32), jnp.zeros((s, 32), F32)
    one64, z32, z64 = jnp.ones((s, 64), F32), jnp.zeros((s, 32), F32), jnp.zeros((s, 64), F32)
    dcos, dsin = jnp.tile(cos64, (1, 8)), jnp.tile(sin64, (1, 8))
    qa = jnp.tile(jnp.concatenate([one64, cos32, z32], axis=-1) * m_scale, (1, MLA_HEADS))
    qb = jnp.tile(jnp.concatenate([z64, sin32, z32], axis=-1) * m_scale, (1, MLA_HEADS))
    pe = jnp.concatenate([cos32, sin32, z64], axis=-1)
    return dcos, dsin, qa, qb, pe


def _inproj_odd_kernel(x_ref, sc_ref, sh_ref, w_ref, wuq_ref, wk_ref, wv_ref, qg_ref, kvg_ref,
                       dcos_ref, dsin_ref, qa_ref, qb_ref, pe_ref,
                       dq_ref, dk_ref, dv_ref, mq_ref, mk_ref, mv_ref):
    h = (x_ref[0] * (1.0 + sc_ref[0]) + sh_ref[0]).astype(BF16)
    dcos, dsin = dcos_ref[...], dsin_ref[...]
    y = _dot(h, w_ref[:, O_DQ:O_DQS])
    ys = _dot(h, w_ref[:, O_DQS:O_DK])
    dq_ref[0] = ((y * dcos + ys * dsin) * 0.125).astype(BF16)
    y = _dot(h, w_ref[:, O_DK:O_DKS])
    ys = _dot(h, w_ref[:, O_DKS:O_DV])
    dk_ref[0] = (y * dcos + ys * dsin).astype(BF16)
    dv_ref[0] = _dot(h, w_ref[:, O_DV:O_CQ]).astype(BF16)
    cq = _dot(h, w_ref[:, O_CQ:O_CKV])
    nq = cq * lax.rsqrt(jnp.mean(cq * cq, axis=-1, keepdims=True) + RMS_EPS) * qg_ref[...]
    y2 = _dot(nq.astype(BF16), wuq_ref[...])
    half = MLA_HEADS * LANES
    mq_ref[0] = (y2[:, :half] * qa_ref[...] + y2[:, half:] * qb_ref[...]).astype(BF16)
    ckv = _dot(h, w_ref[:, O_CKV:O_PE])
    nk = ckv * lax.rsqrt(jnp.mean(ckv * ckv, axis=-1, keepdims=True) + RMS_EPS) * kvg_ref[...]
    pe = _dot(h, w_ref[:, O_PE:O_END]) * pe_ref[...]
    nkb = nk.astype(BF16)
    cat = jnp.concatenate([nkb, pe.astype(BF16)], axis=-1)
    mk_ref[0] = _dot(cat, wk_ref[...]).astype(BF16)
    mv_ref[0] = _dot(nkb, wv_ref[...]).astype(BF16)


def _inproj_odd(x, sc, sh, weights, gains, tables, ts):
    bsz, s, d = x.shape
    batched = sc.shape[0] > 1
    w, wuq2, wk2, wv = weights
    qg, kvg = gains
    dcos, dsin, qa, qb, pe = tables
    tok = lambda n: pl.BlockSpec((1, ts, n), lambda b, i: (b, i, 0))
    tab = lambda n: pl.BlockSpec((ts, n), lambda b, i: (i, 0))
    widths = (512, 512, 512, 1024, 1024, 512)
    return pl.pallas_call(
        _inproj_odd_kernel,
        grid=(bsz, s // ts),
        in_specs=[tok(d), _mod_spec(d, batched), _mod_spec(d, batched), _const_spec(w.shape),
                  _const_spec(wuq2.shape), _const_spec(wk2.shape), _const_spec(wv.shape),
                  _const_spec((1, MLA_Q_RANK)), _const_spec((1, MLA_KV_RANK)),
                  tab(512), tab(512), tab(1024), tab(1024), tab(128)],
        out_specs=[tok(n) for n in widths],
        out_shape=[jax.ShapeDtypeStruct((bsz, s, n), BF16) for n in widths],
        compiler_params=_cparams(2),
    )(x, sc, sh, w, wuq2, wk2, wv, qg.reshape(1, -1), kvg.reshape(1, -1), dcos, dsin, qa, qb, pe)


def _half_masks():
    lane = lax.broadcasted_iota(jnp.int32, (1, LANES), 1)
    return lane < HEAD_DIM, lane >= HEAD_DIM


def _scores(qm, ks):
    ss = [_dot_nt(qm, k) for k in ks]
    m = functools.reduce(jnp.maximum, [jnp.max(s, axis=-1, keepdims=True) for s in ss])
    es = [jnp.exp(s - m) for s in ss]
    l = functools.reduce(lambda a, b: a + b, [jnp.sum(e, axis=-1, keepdims=True) for e in es])
    return es, l


def _attend(qm, ks, vs):
    es, l = _scores(qm, ks)
    o = functools.reduce(lambda a, b: a + b, [_dot(e.astype(BF16), v) for e, v in zip(es, vs)])
    return o * (1.0 / l)


def _na_bias_table(rpb):
    cols = np.arange(GRID_W)
    col_start = np.clip(cols - NA_WIN_W // 2, 0, GRID_W - NA_WIN_W)
    col_mask = (cols[None, :] >= col_start[:, None]) & (cols[None, :] < col_start[:, None] + NA_WIN_W)
    col_idx = np.clip(cols[None, :] - cols[:, None] + NA_WIN_W - 1, 0, 2 * NA_WIN_W - 2)
    off = np.arange(NA_WIN_H)
    ridx = np.arange(NA_WIN_H)[None, :] - off[:, None] + NA_WIN_H - 1
    t = rpb.astype(F32)[:, ridx]
    t = t[..., col_idx]
    t = t.transpose(1, 0, 3, 2, 4)
    t = jnp.where(col_mask[None, None, :, None, :], t, NEG_INF)
    return t.reshape(NA_WIN_H, NA_HEADS, GRID_W, NA_WIN_H * GRID_W)


def _na_kernel(rows, q_ref, k_ref, v_ref, kc_ref, vc_ref, bt_ref, o_ref):
    r = pl.program_id(1)
    rs = jnp.clip(r - NA_WIN_H // 2, 0, rows - NA_WIN_H)
    start = pl.multiple_of(rs * GRID_W, GRID_W)
    win = pl.ds(start, NA_WIN_H * GRID_W)
    m0, m1 = _half_masks()
    for j in range(NA_HEADS // 2):
        sl = slice(j * LANES, (j + 1) * LANES)
        qp = q_ref[0, :, sl]
        kp, vp = k_ref[0, win, sl], v_ref[0, win, sl]
        kcp, vcp = kc_ref[0, :, sl], vc_ref[0, :, sl]
        outs = []
        for par, msk in ((0, m0), (1, m1)):
            qm = jnp.where(msk, qp, jnp.zeros_like(qp))
            s_loc = _dot_nt(qm, kp) + bt_ref[0, 2 * j + par]
            s_ctx = _dot_nt(qm, kcp)
            m = jnp.maximum(jnp.max(s_loc, axis=-1, keepdims=True), jnp.max(s_ctx, axis=-1, keepdims=True))
            e_loc, e_ctx = jnp.exp(s_loc - m), jnp.exp(s_ctx - m)
            l = jnp.sum(e_loc, axis=-1, keepdims=True) + jnp.sum(e_ctx, axis=-1, keepdims=True)
            o = _dot(e_loc.astype(BF16), vp) + _dot(e_ctx.astype(BF16), vcp)
            outs.append(o * (1.0 / l))
        o_ref[0, :, sl] = jnp.where(m0, outs[0], outs[1]).astype(BF16)


def _na_attention(q, k, v, kc, vc, bt):
    bsz, s, w = q.shape
    l = kc.shape[1]
    rows = s // GRID_W
    assert rows >= NA_WIN_H
    full = lambda n: pl.BlockSpec((1, n, w), lambda b, r: (b, 0, 0))

    def bt_map(b, r):
        return (r - jnp.clip(r - NA_WIN_H // 2, 0, rows - NA_WIN_H), 0, 0, 0)

    return pl.pallas_call(
        functools.partial(_na_kernel, rows),
        grid=(bsz, rows),
        in_specs=[pl.BlockSpec((1, GRID_W, w), lambda b, r: (b, r, 0)),
                  full(s), full(s), full(l), full(l),
                  pl.BlockSpec((1,) + bt.shape[1:], bt_map)],
        out_specs=pl.BlockSpec((1, GRID_W, w), lambda b, r: (b, r, 0)),
        out_shape=jax.ShapeDtypeStruct((bsz, s, w), BF16),
        compiler_params=_cparams(2),
    )(q, k, v, kc, vc, bt)


def _slot_attn_kernel(n_heads, q_slot, has_lat, *refs):
    if has_lat:
        q_ref, k_ref, v_ref, kc_ref, vc_ref, o_ref = refs
    else:
        q_ref, kc_ref, vc_ref, o_ref = refs
    m0, m1 = _half_masks()
    for j in range(n_heads // 2):
        vsl = slice(j * LANES, (j + 1) * LANES)
        outs = []
        for par, msk in ((0, m0), (1, m1)):
            h = 2 * j + par
            if q_slot:
                ksl = slice(h * LANES, (h + 1) * LANES)
                qm = q_ref[0, :, ksl]
            else:
                ksl = vsl
                qp = q_ref[0, :, vsl]
                qm = jnp.where(msk, qp, jnp.zeros_like(qp))
            ks, vs = [kc_ref[0, :, ksl]], [vc_ref[0, :, vsl]]
            if has_lat:
                ks.insert(0, k_ref[0, :, ksl])
                vs.insert(0, v_ref[0, :, vsl])
            outs.append(_attend(qm, ks, vs))
        o_ref[0, :, vsl] = jnp.where(m0, outs[0], outs[1]).astype(BF16)


def _slot_attention(q, k, v, kc, vc, n_heads, q_slot, tq):
    bsz, sq, wq = q.shape
    l, wk, wv = kc.shape[1], kc.shape[2], vc.shape[2]
    has_lat = k is not None
    qspec = pl.BlockSpec((1, tq, wq), lambda b, i: (b, i, 0))
    full = lambda n, w: pl.BlockSpec((1, n, w), lambda b, i: (b, 0, 0))
    in_specs, args = [qspec], [q]
    if has_lat:
        s = k.shape[1]
        in_specs += [full(s, wk), full(s, wv)]
        args += [k, v]
    in_specs += [full(l, wk), full(l, wv)]
    args += [kc, vc]
    return pl.pallas_call(
        functools.partial(_slot_attn_kernel, n_heads, q_slot, has_lat),
        grid=(bsz, sq // tq),
        in_specs=in_specs,
        out_specs=pl.BlockSpec((1, tq, wv), lambda b, i: (b, i, 0)),
        out_shape=jax.ShapeDtypeStruct((bsz, sq, wv), BF16),
        compiler_params=_cparams(2),
    )(*args)


def _gqa_kernel(has_lat, tq, *refs):
    if has_lat:
        q_ref, k_ref, v_ref, kc_ref, vc_ref, o_ref = refs
    else:
        q_ref, kc_ref, vc_ref, o_ref = refs
    masks = _half_masks()
    res = {}
    for g in range(GQA_KV_HEADS):
        for var in range(2):
            par = g if var == 0 else 1 - g
            heads = (4 * g + par, 4 * g + 2 + par)
            vsl = slice(var * LANES, (var + 1) * LANES)
            qs = []
            for h in heads:
                qp = q_ref[0, :, (h // 2) * LANES:(h // 2 + 1) * LANES]
                qs.append(jnp.where(masks[par], qp, jnp.zeros_like(qp)))
            qm = jnp.concatenate(qs, axis=0)
            ks, vs = [kc_ref[0, :, vsl]], [vc_ref[0, :, vsl]]
            if has_lat:
                ks.insert(0, k_ref[0, :, vsl])
                vs.insert(0, v_ref[0, :, vsl])
            o = _attend(qm, ks, vs)
            res[heads[0]] = o[:tq]
            res[heads[1]] = o[tq:]
    for j in range(GQA_HEADS // 2):
        o_ref[0, :, j * LANES:(j + 1) * LANES] = jnp.where(
            masks[0], res[2 * j], res[2 * j + 1]).astype(BF16)


def _gqa_attention(q, k2, v2, k2c, v2c, tq):
    bsz, sq, wq = q.shape
    l = k2c.shape[1]
    has_lat = k2 is not None
    full = lambda n: pl.BlockSpec((1, n, 2 * LANES), lambda b, i: (b, 0, 0))
    in_specs, args = [pl.BlockSpec((1, tq, wq), lambda b, i: (b, i, 0))], [q]
    if has_lat:
        in_specs += [full(k2.shape[1])] * 2
        args += [k2, v2]
    in_specs += [full(l)] * 2
    args += [k2c, v2c]
    return pl.pallas_call(
        functools.partial(_gqa_kernel, has_lat, tq),
        grid=(bsz, sq // tq),
        in_specs=in_specs,
        out_specs=pl.BlockSpec((1, tq, wq), lambda b, i: (b, i, 0)),
        out_shape=jax.ShapeDtypeStruct((bsz, sq, wq), BF16),
        compiler_params=_cparams(2),
    )(*args)


def _diff_kernel(has_lat, tq, lam_init, *refs):
    if has_lat:
        q_ref, k_ref, v_ref, kc_ref, vc_ref, lq1, lk1, lq2, lk2, sub_ref, o_ref = refs
    else:
        q_ref, kc_ref, vc_ref, lq1, lk1, lq2, lk2, sub_ref, o_ref = refs
    lam = (jnp.exp(jnp.sum(lq1[...] * lk1[...], axis=-1, keepdims=True))
           - jnp.exp(jnp.sum(lq2[...] * lk2[...], axis=-1, keepdims=True)) + lam_init)
    m0, m1 = _half_masks()
    for h in range(DIFF_HEADS):
        sl = slice(h * LANES, (h + 1) * LANES)
        qp = q_ref[0, :, sl]
        zero = jnp.zeros_like(qp)
        qm = jnp.concatenate([jnp.where(m0, qp, zero), jnp.where(m1, qp, zero)], axis=0)
        ks, vs = [kc_ref[0, :, sl]], [vc_ref[0, :, sl]]
        if has_lat:
            ks.insert(0, k_ref[0, :, sl])
            vs.insert(0, v_ref[0, :, sl])
        es, l = _scores(qm, ks)
        inv = 1.0 / l
        c1, c2 = inv[:tq], lam * inv[tq:]
        o = None
        for e, v in zip(es, vs):
            p = (e[:tq] * c1 - e[tq:] * c2).astype(BF16)
            t = _dot(p, v)
            o = t if o is None else o + t
        o = o * lax.rsqrt(jnp.mean(o * o, axis=-1, keepdims=True) + RMS_EPS) * sub_ref[...]
        o_ref[0, :, sl] = (o * (1.0 - lam_init)).astype(BF16)


def _diff_attention(q, k, v, kc, vc, lams, subln, lam_init, tq):
    bsz, sq, w = q.shape
    l = kc.shape[1]
    has_lat = k is not None
    full = lambda n: pl.BlockSpec((1, n, w), lambda b, i: (b, 0, 0))
    in_specs, args = [pl.BlockSpec((1, tq, w), lambda b, i: (b, i, 0))], [q]
    if has_lat:
        in_specs += [full(k.shape[1])] * 2
        args += [k, v]
    in_specs += [full(l)] * 2 + [_const_spec((1, HEAD_DIM))] * 4 + [_const_spec((1, LANES))]
    args += [kc, vc] + [a.reshape(1, -1).astype(F32) for a in lams] + [subln.reshape(1, -1).astype(F32)]
    return pl.pallas_call(
        functools.partial(_diff_kernel, has_lat, tq, lam_init),
        grid=(bsz, sq // tq),
        in_specs=in_specs,
        out_specs=pl.BlockSpec((1, tq, w), lambda b, i: (b, i, 0)),
        out_shape=jax.ShapeDtypeStruct((bsz, sq, w), BF16),
        compiler_params=_cparams(2),
    )(*args)


def _layer_norm(z, g, b):
    mu = jnp.mean(z, axis=-1, keepdims=True)
    zc = z - mu
    var = jnp.mean(zc * zc, axis=-1, keepdims=True)
    return zc * lax.rsqrt(var + LN_EPS) * g + b


def _outproj_kernel(alpha, o1_ref, o2_ref, w1_ref, w2_ref, x_ref, g1_ref, lng_ref, lnb_ref,
                    sc2_ref, sh2_ref, rwh_ref, rwl_ref, rb_ref,
                    xo_ref, h2_ref, idx_ref, gw_ref):
    o = _dot(o1_ref[0], w1_ref[...]) + _dot(o2_ref[0], w2_ref[...])
    xn = _layer_norm(alpha * x_ref[0] + g1_ref[0] * o, lng_ref[...], lnb_ref[...])
    xo_ref[0] = xn
    h2 = xn * (1.0 + sc2_ref[0]) + sh2_ref[0]
    hi = h2.astype(BF16)
    h2_ref[0] = hi
    lo = (h2 - hi.astype(F32)).astype(BF16)
    logits = (_dot(hi, rwh_ref[...]) + _dot(lo, rwh_ref[...]) + _dot(hi, rwl_ref[...])) + rb_ref[...]
    lane = lax.broadcasted_iota(jnp.int32, logits.shape, 1).astype(F32)
    vals, idxs = [], []
    cur = logits
    for _ in range(TOP_K):
        m = jnp.max(cur, axis=-1, keepdims=True)
        ik = jnp.min(jnp.where(cur == m, lane, float(LANES)), axis=-1, keepdims=True)
        vals.append(m)
        idxs.append(ik)
        cur = jnp.where(lane == ik, -jnp.inf, cur)
    ws = [jnp.exp(v - vals[0]) for v in vals]
    inv = 1.0 / functools.reduce(lambda a, b: a + b, ws)
    idx_out = jnp.zeros_like(logits)
    w_out = jnp.zeros_like(logits)
    for k in range(TOP_K):
        idx_out = jnp.where(lane == float(k), idxs[k], idx_out)
        w_out = jnp.where(lane == float(k), ws[k] * inv, w_out)
    idx_ref[0] = idx_out.astype(jnp.int32)
    gw_ref[0] = w_out


def _outproj_ln_router(o1, o2, w_out, x, g1, lng, lnb, sc2, sh2, router, alpha, ts):
    bsz, s, d = x.shape
    batched = g1.shape[0] > 1
    rwh, rwl, rb = router
    w1, w2 = w_out[:512].astype(BF16), w_out[512:].astype(BF16)
    tok = lambda n: pl.BlockSpec((1, ts, n), lambda b, i: (b, i, 0))
    ms = _mod_spec(d, batched)
    return pl.pallas_call(
        functools.partial(_outproj_kernel, alpha),
        grid=(bsz, s // ts),
        in_specs=[tok(512), tok(512), _const_spec((512, d)), _const_spec((512, d)), tok(d), ms,
                  _const_spec((1, d)), _const_spec((1, d)), ms, ms,
                  _const_spec((d, LANES)), _const_spec((d, LANES)), _const_spec((1, LANES))],
        out_specs=[tok(d), tok(d), tok(LANES), tok(LANES)],
        out_shape=[jax.ShapeDtypeStruct((bsz, s, d), F32), jax.ShapeDtypeStruct((bsz, s, d), BF16),
                   jax.ShapeDtypeStruct((bsz, s, LANES), jnp.int32),
                   jax.ShapeDtypeStruct((bsz, s, LANES), F32)],
        compiler_params=_cparams(2),
    )(o1, o2, w1, w2, x, g1, lng.reshape(1, d), lnb.reshape(1, d), sc2, sh2, rwh, rwl, rb)


def _router_weights(router_w, router_b):
    d, e = router_w.shape
    wp = jnp.zeros((d, LANES), F32).at[:, :e].set(router_w)
    hi = wp.astype(BF16)
    lo = (wp - hi.astype(F32)).astype(BF16)
    rb = jnp.full((1, LANES), -jnp.inf, F32).at[0, :e].set(router_b)
    return hi, lo, rb


def _ffn_kernel(te_ref, tv_ref, x_ref, wg_ref, wl_ref, bg_ref, bl_ref, wd_ref, bd_ref, y_ref):
    j = pl.program_id(0)

    @pl.when(tv_ref[j] > 0)
    def _():
        x = x_ref[...]
        glu = jnp.minimum(_dot(x, wg_ref[0]) + bg_ref[0], SWIGLU_LIMIT)
        lin = jnp.clip(_dot(x, wl_ref[0]) + bl_ref[0], -SWIGLU_LIMIT, SWIGLU_LIMIT)
        a = (lin + 1.0) * (glu * jax.nn.sigmoid(SWIGLU_ALPHA * glu))
        y_ref[...] = _dot(a.astype(BF16), wd_ref[0]) + bd_ref[0]

    @pl.when(tv_ref[j] == 0)
    def _():
        y_ref[...] = jnp.zeros_like(y_ref)


def _expert_ffn(xs, tile_expert, tile_valid, wg, wl, bg, bl, wd, bd, tm):
    p, d = xs.shape
    e, _, f = wg.shape
    wspec = lambda a, b: pl.BlockSpec((1, a, b), lambda j, te, tv: (te[j], 0, 0))
    return pl.pallas_call(
        _ffn_kernel,
        grid_spec=pltpu.PrefetchScalarGridSpec(
            num_scalar_prefetch=2,
            grid=(p // tm,),
            in_specs=[pl.BlockSpec((tm, d), lambda j, te, tv: (j, 0)),
                      wspec(d, f), wspec(d, f), wspec(1, f), wspec(1, f), wspec(f, d), wspec(1, d)],
            out_specs=pl.BlockSpec((tm, d), lambda j, te, tv: (j, 0))),
        out_shape=jax.ShapeDtypeStruct((p, d), F32),
        compiler_params=_cparams(1),
    )(tile_expert, tile_valid, xs, wg, wl, bg.reshape(e, 1, f), bl.reshape(e, 1, f), wd,
      bd.reshape(e, 1, d))


def _moe_plan(top_idx, n_experts, tm):
    t = top_idx.shape[0]
    a = t * TOP_K
    e_flat = top_idx.reshape(a)
    order = jnp.argsort(e_flat, stable=True).astype(jnp.int32)
    inv = jnp.argsort(order).astype(jnp.int32)
    cnt = jnp.sum((e_flat[:, None] == jnp.arange(n_experts)[None, :]).astype(jnp.int32), axis=0)
    pc = ((cnt + tm - 1) // tm) * tm
    pend = jnp.cumsum(pc)
    pstart = pend - pc
    cstart = jnp.cumsum(cnt) - cnt
    dpos = pstart[e_flat] + (inv - cstart[e_flat])
    p = a + n_experts * tm
    rows = jnp.arange(p, dtype=jnp.int32)
    row_e = jnp.minimum(jnp.sum((rows[:, None] >= pend[None, :]).astype(jnp.int32), axis=1), n_experts - 1)
    rank = rows - pstart[row_e]
    valid = rank < cnt[row_e]
    src = jnp.where(valid, order[jnp.clip(cstart[row_e] + rank, 0, a - 1)] // TOP_K, 0)
    tile_start = jnp.arange(p // tm, dtype=jnp.int32) * tm
    tile_expert = jnp.minimum(jnp.sum((tile_start[:, None] >= pend[None, :]).astype(jnp.int32), axis=1),
                              n_experts - 1).astype(jnp.int32)
    tile_valid = (tile_start < pend[-1]).astype(jnp.int32)
    return src.astype(jnp.int32), dpos.astype(jnp.int32), tile_expert, tile_valid


def _combine_kernel(alpha, yg_ref, gw_ref, x_ref, g2_ref, lng_ref, lnb_ref, o_ref):
    gw = gw_ref[...]
    y = yg_ref[0] * gw[:, 0:1]
    for k in range(1, TOP_K):
        y = y + yg_ref[k] * gw[:, k:k + 1]
    o_ref[0] = _layer_norm(alpha * x_ref[0] + g2_ref[0] * y, lng_ref[...], lnb_ref[...])


def _combine_ln(yg, gw, x, g2, lng, lnb, alpha, row_offset, ts):
    bsz, s, d = x.shape
    batched = g2.shape[0] > 1
    nblk = s // ts
    off = row_offset // ts
    return pl.pallas_call(
        functools.partial(_combine_kernel, alpha),
        grid=(bsz, nblk),
        in_specs=[pl.BlockSpec((TOP_K, ts, d), lambda b, i: (0, off + b * nblk + i, 0)),
                  pl.BlockSpec((ts, LANES), lambda b, i: (off + b * nblk + i, 0)),
                  pl.BlockSpec((1, ts, d), lambda b, i: (b, i, 0)),
                  _mod_spec(d, batched), _const_spec((1, d)), _const_spec((1, d))],
        out_specs=pl.BlockSpec((1, ts, d), lambda b, i: (b, i, 0)),
        out_shape=jax.ShapeDtypeStruct((bsz, s, d), F32),
        compiler_params=_cparams(2),
    )(yg, gw, x, g2, lng.reshape(1, d), lnb.reshape(1, d))


def kernel(x, c, ctx, c_ctx, mod_w, mod_b, ln1_g, ln1_b, ln2_g, ln2_b, even_w_in, even_w_out, na_rpb, gqa_q_gain, gqa_k_gain, odd_w_in, odd_w_out, diff_lq1, diff_lk1, diff_lq2, diff_lk2, diff_subln, mla_q_gain, mla_w_uq, mla_kv_gain, mla_w_ukv, router_w, router_b, exp_w_gu, exp_b_gu, exp_w_down, exp_b_down):
    bsz, s, d = x.shape
    l = ctx.shape[1]
    depth = mod_w.shape[0]
    n_experts = router_w.shape[-1]
    alpha = (2 * depth) ** 0.25
    ts = min(512, s)
    tm = MOE_TILE

    pad = (-(bsz + 1)) % 8
    c_all = jnp.concatenate([c, c_ctx[None, :], jnp.zeros((pad, d), F32)], axis=0)
    mod = _modulation_all(c_all, mod_w, mod_b)

    for i in range(depth):
        last = i == depth - 1
        j = i // 2
        ml = [mod[i, :bsz, k * d:(k + 1) * d].reshape(bsz, 1, d) for k in range(6)]
        mc = [mod[i, bsz:bsz + 1, k * d:(k + 1) * d].reshape(1, 1, d) for k in range(6)]
        sh1, sc1, g1, sh2, sc2, g2 = ml
        csh1, csc1, cg1, csh2, csc2, cg2 = mc

        if i % 2 == 0:
            w = _even_weights(even_w_in[j], gqa_q_gain[j], gqa_k_gain[j])
            naq, nak, nav, gq, gk, gv = _inproj_even(
                x, sc1, sh1, w, _even_tables(s, gqa_q_gain[j], gqa_k_gain[j], True), ts)
            cnaq, cnak, cnav, cgq, cgk, cgv = _inproj_even(
                ctx, csc1, csh1, w, _even_tables(l, gqa_q_gain[j], gqa_k_gain[j], False), l)
            o1 = _na_attention(naq, nak, nav, cnak, cnav, _na_bias_table(na_rpb[j]))
            o2 = _gqa_attention(gq, gk, gv, cgk, cgv, min(256, s))
            if not last:
                co1 = _slot_attention(cnaq, None, None, cnak, cnav, NA_HEADS, False, l)
                co2 = _gqa_attention(cgq, None, None, cgk, cgv, l)
            w_out = even_w_out[j]
        else:
            lam_init = 0.8 - 0.6 * math.exp(-0.3 * i)
            weights = _odd_weights(odd_w_in[j], mla_w_uq[j], mla_w_ukv[j])
            gains = (mla_q_gain[j], mla_kv_gain[j])
            dq, dk, dv, mq, mk, mv = _inproj_odd(x, sc1, sh1, weights, gains, _odd_tables(s, True), ts)
            cdq, cdk, cdv, cmq, cmk, cmv = _inproj_odd(ctx, csc1, csh1, weights, gains,
                                                       _odd_tables(l, False), l)
            lams = (diff_lq1[j], diff_lk1[j], diff_lq2[j], diff_lk2[j])
            o1 = _diff_attention(dq, dk, dv, cdk, cdv, lams, diff_subln[j], lam_init, min(256, s))
            o2 = _slot_attention(mq, mk, mv, cmk, cmv, MLA_HEADS, True, min(512, s))
            if not last:
                co1 = _diff_attention(cdq, None, None, cdk, cdv, lams, diff_subln[j], lam_init, l)
                co2 = _slot_attention(cmq, None, None, cmk, cmv, MLA_HEADS, True, l)
            w_out = odd_w_out[j]

        router = _router_weights(router_w[i], router_b[i])
        x, h2, ridx, rgw = _outproj_ln_router(o1, o2, w_out, x, g1, ln1_g[i], ln1_b[i], sc2, sh2,
                                              router, alpha, ts)
        h2 = h2.reshape(bsz * s, d)
        ridx = ridx.reshape(bsz * s, LANES)
        rgw = rgw.reshape(bsz * s, LANES)
        if not last:
            ctx, ch2, cidx, cgw = _outproj_ln_router(co1, co2, w_out, ctx, cg1, ln1_g[i], ln1_b[i],
                                                     csc2, csh2, router, alpha, l)
            h2 = jnp.concatenate([h2, ch2.reshape(bsz * l, d)], axis=0)
            ridx = jnp.concatenate([ridx, cidx.reshape(bsz * l, LANES)], axis=0)
            rgw = jnp.concatenate([rgw, cgw.reshape(bsz * l, LANES)], axis=0)

        src, dpos, tile_expert, tile_valid = _moe_plan(ridx[:, :TOP_K], n_experts, tm)
        xs = jnp.take(h2, src, axis=0)
        wg = exp_w_gu[i][:, :, 0::2].astype(BF16)
        wl = exp_w_gu[i][:, :, 1::2].astype(BF16)
        bg, bl = exp_b_gu[i][:, 0::2], exp_b_gu[i][:, 1::2]
        ys = _expert_ffn(xs, tile_expert, tile_valid, wg, wl, bg, bl,
                         exp_w_down[i].astype(BF16), exp_b_down[i], tm)
        t = h2.shape[0]
        yg = jnp.take(ys, dpos.reshape(t, TOP_K).T, axis=0)

        x = _combine_ln(yg, rgw, x, g2, ln2_g[i], ln2_b[i], alpha, 0, ts)
        if not last:
            ctx = _combine_ln(yg, rgw, ctx, cg2, ln2_g[i], ln2_b[i], alpha, bsz * s, l)
    return x
```

```python
import functools
import math

import numpy as np
import jax
import jax.numpy as jnp
from jax import lax
from jax.experimental import pallas as pl
from jax.experimental.pallas import tpu as pltpu

F32 = jnp.float32
BF16 = jnp.bfloat16

GRID_W = 64
HEAD_DIM = 64
ROPE_THETA = 10000.0
LN_EPS = 1e-6
RMS_EPS = 1e-6
NEG_INF = -1e30
NA_HEADS = 8
NA_WIN_H = 8
NA_WIN_W = 16
GQA_HEADS = 8
GQA_KV_HEADS = 2
DIFF_HEADS = 4
MLA_HEADS = 8
MLA_Q_RANK = 256
MLA_KV_RANK = 128
MLA_NOPE = 64
MLA_ROPE = 32
MLA_V = 64
MLA_QK = MLA_NOPE + MLA_ROPE
TOP_K = 4
SWIGLU_ALPHA = 1.702
SWIGLU_LIMIT = 7.0

LANES = 128
VMEM_LIMIT = 56 * 1024 * 1024
MOE_TILE = 512
GU_BLOCK = 512


def _cparams(n_axes):
    return pltpu.CompilerParams(dimension_semantics=("arbitrary",) * n_axes,
                                vmem_limit_bytes=VMEM_LIMIT)


def _dot(a, b):
    return jnp.dot(a, b, preferred_element_type=F32)


def _dot_nt(a, b):
    return lax.dot_general(a, b, (((1,), (1,)), ((), ())), preferred_element_type=F32)


def _mod_kernel(c_ref, w_ref, b_ref, o_ref):
    cv = c_ref[...]
    a = (cv * jax.nn.sigmoid(cv)).astype(BF16)
    o_ref[0] = _dot(a, w_ref[0].astype(BF16)) + b_ref[0]


def _modulation_all(c_all, mod_w, mod_b):
    depth, d, n = mod_w.shape
    rows = c_all.shape[0]
    tn = 1536
    return pl.pallas_call(
        _mod_kernel,
        grid=(depth, n // tn),
        in_specs=[pl.BlockSpec((rows, d), lambda i, j: (0, 0)),
                  pl.BlockSpec((1, d, tn), lambda i, j: (i, 0, j)),
                  pl.BlockSpec((1, 1, tn), lambda i, j: (i, 0, j))],
        out_specs=pl.BlockSpec((1, rows, tn), lambda i, j: (i, 0, j)),
        out_shape=jax.ShapeDtypeStruct((depth, rows, n), F32),
        compiler_params=_cparams(2),
    )(c_all, mod_w, mod_b.reshape(depth, 1, n))


def _rope_tables(s, dim):
    pos = jnp.arange(s)
    row = (pos // GRID_W).astype(F32)[:, None]
    col = (pos % GRID_W).astype(F32)[:, None]
    quarter = dim // 4
    inv_freq = ROPE_THETA ** (-jnp.arange(quarter, dtype=F32) / quarter)
    ar, ac = row * inv_freq, col * inv_freq
    cos = jnp.concatenate([jnp.cos(ar), jnp.cos(ar), jnp.cos(ac), jnp.cos(ac)], axis=-1)
    sin = jnp.concatenate([-jnp.sin(ar), jnp.sin(ar), -jnp.sin(ac), jnp.sin(ac)], axis=-1)
    return cos, sin


def _swap_perm(dim):
    q = dim // 4
    idx = np.arange(dim)
    return np.where((idx % (2 * q)) < q, idx + q, idx - q)


def _swap_cols(w, dim):
    n = w.shape[-1] // dim
    perm = (np.arange(n)[:, None] * dim + _swap_perm(dim)[None, :]).reshape(-1)
    return w[..., perm]


def _block_ones(n, blk):
    i = np.arange(n) // blk
    return jnp.asarray((i[:, None] == i[None, :]).astype(np.float32), dtype=BF16)


E_NAQ, E_NAK, E_NAV, E_GQ, E_GQS, E_GK, E_GKS, E_GV, E_END = (
    0, 512, 1024, 1536, 2048, 2560, 2816, 3072, 3328)


def _even_weights(w_in, q_gain, k_gain):
    naq, nak, nav, gq, gk, gv = jnp.split(w_in, [512, 1024, 1536, 2048, 2176], axis=-1)
    k0, k1 = gk[:, :64], gk[:, 64:]
    v0, v1 = gv[:, :64], gv[:, 64:]
    gk2 = jnp.concatenate([k0, k1, k1, k0], axis=-1)
    gv2 = jnp.concatenate([v0, v1, v1, v0], axis=-1)
    w = jnp.concatenate([naq * 0.125, nak, nav, gq, _swap_cols(gq, 64),
                         gk2, _swap_cols(gk2, 64), gv2], axis=-1).astype(BF16)
    return w


def _even_tables(s, q_gain, k_gain, rope):
    sw = _swap_perm(64)
    if rope:
        cos, sin = _rope_tables(s, 64)
    else:
        cos, sin = jnp.ones((s, 64), F32), jnp.zeros((s, 64), F32)
    qa = jnp.tile(cos * q_gain[None, :] * 0.125, (1, 8))
    qb = jnp.tile(sin * q_gain[sw][None, :] * 0.125, (1, 8))
    ka = jnp.tile(cos * k_gain[None, :], (1, 4))
    kb = jnp.tile(sin * k_gain[sw][None, :], (1, 4))
    return qa, qb, ka, kb


def _inproj_even_kernel(x_ref, sc_ref, sh_ref, w_ref, g512_ref, g256_ref,
                        qa_ref, qb_ref, ka_ref, kb_ref,
                        naq_ref, nak_ref, nav_ref, gq_ref, gk_ref, gv_ref):
    h = (x_ref[0] * (1.0 + sc_ref[0]) + sh_ref[0]).astype(BF16)
    naq_ref[0] = _dot(h, w_ref[:, E_NAQ:E_NAK]).astype(BF16)
    nak_ref[0] = _dot(h, w_ref[:, E_NAK:E_NAV]).astype(BF16)
    nav_ref[0] = _dot(h, w_ref[:, E_NAV:E_GQ]).astype(BF16)
    gv_ref[0] = _dot(h, w_ref[:, E_GV:E_END]).astype(BF16)
    y = _dot(h, w_ref[:, E_GQ:E_GQS])
    ys = _dot(h, w_ref[:, E_GQS:E_GK])
    r = lax.rsqrt(_dot((y * y).astype(BF16), g512_ref[...]) * (1.0 / HEAD_DIM) + RMS_EPS)
    gq_ref[0] = (r * (y * qa_ref[...] + ys * qb_ref[...])).astype(BF16)
    y = _dot(h, w_ref[:, E_GK:E_GKS])
    ys = _dot(h, w_ref[:, E_GKS:E_GV])
    r = lax.rsqrt(_dot((y * y).astype(BF16), g256_ref[...]) * (1.0 / HEAD_DIM) + RMS_EPS)
    gk_ref[0] = (r * (y * ka_ref[...] + ys * kb_ref[...])).astype(BF16)


def _mod_spec(d, batched):
    if batched:
        return pl.BlockSpec((1, 1, d), lambda b, s: (b, 0, 0))
    return pl.BlockSpec((1, 1, d), lambda b, s: (0, 0, 0))


def _const_spec(shape):
    nd = len(shape)
    return pl.BlockSpec(shape, lambda b, s: (0,) * nd)


def _inproj_even(x, sc, sh, w, tables, ts):
    bsz, s, d = x.shape
    batched = sc.shape[0] > 1
    qa, qb, ka, kb = tables
    tok = lambda n: pl.BlockSpec((1, ts, n), lambda b, i: (b, i, 0))
    tab = lambda n: pl.BlockSpec((ts, n), lambda b, i: (i, 0))
    widths = (512, 512, 512, 512, 256, 256)
    return pl.pallas_call(
        _inproj_even_kernel,
        grid=(bsz, s // ts),
        in_specs=[tok(d), _mod_spec(d, batched), _mod_spec(d, batched), _const_spec(w.shape),
                  _const_spec((512, 512)), _const_spec((256, 256)),
                  tab(512), tab(512), tab(256), tab(256)],
        out_specs=[tok(n) for n in widths],
        out_shape=[jax.ShapeDtypeStruct((bsz, s, n), BF16) for n in widths],
        compiler_params=_cparams(2),
    )(x, sc, sh, w, _block_ones(512, 64), _block_ones(256, 64), qa, qb, ka, kb)


O_DQ, O_DQS, O_DK, O_DKS, O_DV, O_CQ, O_CKV, O_PE, O_END = (
    0, 512, 1024, 1536, 2048, 2560, 2816, 2944, 3072)


def _odd_weights(w_in, w_uq, w_ukv):
    dq, dk, dv, cq, ckv, kpe = jnp.split(w_in, [512, 1024, 1536, 1792, 1920], axis=-1)
    d = w_in.shape[0]
    pe_slot = jnp.concatenate([kpe, _swap_cols(kpe, 32), jnp.zeros((d, 64), F32)], axis=-1)
    w = jnp.concatenate([dq, _swap_cols(dq, 64), dk, _swap_cols(dk, 64), dv, cq, ckv, pe_slot],
                        axis=-1).astype(BF16)
    uq = w_uq.reshape(MLA_Q_RANK, MLA_HEADS, MLA_QK)
    z32 = jnp.zeros((MLA_Q_RANK, MLA_HEADS, 32), F32)
    z64 = jnp.zeros((MLA_Q_RANK, MLA_HEADS, 64), F32)
    uq_pad = jnp.concatenate([uq, z32], axis=-1).reshape(MLA_Q_RANK, MLA_HEADS * LANES)
    uq_sw = jnp.concatenate([z64, _swap_cols(uq[..., MLA_NOPE:], 32), z32],
                            axis=-1).reshape(MLA_Q_RANK, MLA_HEADS * LANES)
    wuq2 = jnp.concatenate([uq_pad, uq_sw], axis=-1).astype(BF16)
    ukv = w_ukv.reshape(MLA_KV_RANK, MLA_HEADS, MLA_NOPE + MLA_V)
    zk = jnp.zeros((MLA_KV_RANK, MLA_HEADS, 64), F32)
    wk_pad = jnp.concatenate([ukv[..., :MLA_NOPE], zk], axis=-1).reshape(MLA_KV_RANK, MLA_HEADS * LANES)
    place = np.zeros((LANES, MLA_HEADS, LANES), np.float32)
    for j in range(MLA_ROPE):
        place[j, :, MLA_NOPE + j] = 1.0
        place[MLA_ROPE + j, :, MLA_NOPE + j] = 1.0
    wk2 = jnp.concatenate([wk_pad, jnp.asarray(place.reshape(LANES, MLA_HEADS * LANES))],
                          axis=0).astype(BF16)
    wv = ukv[..., MLA_NOPE:].reshape(MLA_KV_RANK, MLA_HEADS * MLA_V).astype(BF16)
    return w, wuq2, wk2, wv


def _odd_tables(s, rope):
    m_scale = MLA_QK ** -0.5
    if rope:
        cos64, sin64 = _rope_tables(s, 64)
        cos32, sin32 = _rope_tables(s, 32)
    else:
        cos64, sin64 = jnp.ones((s, 64), F32), jnp.zeros((s, 64), F32)
        cos32, sin32 = jnp.ones((s, 32), F32), jnp.zeros((s, 32), F32)
    one64, z32, z64 = jnp.ones((s, 64), F32), jnp.zeros((s, 32), F32), jnp.zeros((s, 64), F32)
    dcos, dsin = jnp.tile(cos64, (1, 8)), jnp.tile(sin64, (1, 8))
    qa = jnp.tile(jnp.concatenate([one64, cos32, z32], axis=-1) * m_scale, (1, MLA_HEADS))
    qb = jnp.tile(jnp.concatenate([z64, sin32, z32], axis=-1) * m_scale, (1, MLA_HEADS))
    pe = jnp.concatenate([cos32, sin32, z64], axis=-1)
    return dcos, dsin, qa, qb, pe


def _inproj_odd_kernel(x_ref, sc_ref, sh_ref, w_ref, wuq_ref, wk_ref, wv_ref, qg_ref, kvg_ref,
                       dcos_ref, dsin_ref, qa_ref, qb_ref, pe_ref,
                       dq_ref, dk_ref, dv_ref, mq_ref, mk_ref, mv_ref):
    h = (x_ref[0] * (1.0 + sc_ref[0]) + sh_ref[0]).astype(BF16)
    dcos, dsin = dcos_ref[...], dsin_ref[...]
    y = _dot(h, w_ref[:, O_DQ:O_DQS])
    ys = _dot(h, w_ref[:, O_DQS:O_DK])
    dq_ref[0] = ((y * dcos + ys * dsin) * 0.125).astype(BF16)
    y = _dot(h, w_ref[:, O_DK:O_DKS])
    ys = _dot(h, w_ref[:, O_DKS:O_DV])
    dk_ref[0] = (y * dcos + ys * dsin).astype(BF16)
    dv_ref[0] = _dot(h, w_ref[:, O_DV:O_CQ]).astype(BF16)
    cq = _dot(h, w_ref[:, O_CQ:O_CKV])
    nq = cq * lax.rsqrt(jnp.mean(cq * cq, axis=-1, keepdims=True) + RMS_EPS) * qg_ref[...]
    y2 = _dot(nq.astype(BF16), wuq_ref[...])
    half = MLA_HEADS * LANES
    mq_ref[0] = (y2[:, :half] * qa_ref[...] + y2[:, half:] * qb_ref[...]).astype(BF16)
    ckv = _dot(h, w_ref[:, O_CKV:O_PE])
    nk = ckv * lax.rsqrt(jnp.mean(ckv * ckv, axis=-1, keepdims=True) + RMS_EPS) * kvg_ref[...]
    pe = _dot(h, w_ref[:, O_PE:O_END]) * pe_ref[...]
    nkb = nk.astype(BF16)
    cat = jnp.concatenate([nkb, pe.astype(BF16)], axis=-1)
    mk_ref[0] = _dot(cat, wk_ref[...]).astype(BF16)
    mv_ref[0] = _dot(nkb, wv_ref[...]).astype(BF16)


def _inproj_odd(x, sc, sh, weights, gains, tables, ts):
    bsz, s, d = x.shape
    batched = sc.shape[0] > 1
    w, wuq2, wk2, wv = weights
    qg, kvg = gains
    dcos, dsin, qa, qb, pe = tables
    tok = lambda n: pl.BlockSpec((1, ts, n), lambda b, i: (b, i, 0))
    tab = lambda n: pl.BlockSpec((ts, n), lambda b, i: (i, 0))
    widths = (512, 512, 512, 1024, 1024, 512)
    return pl.pallas_call(
        _inproj_odd_kernel,
        grid=(bsz, s // ts),
        in_specs=[tok(d), _mod_spec(d, batched), _mod_spec(d, batched), _const_spec(w.shape),
                  _const_spec(wuq2.shape), _const_spec(wk2.shape), _const_spec(wv.shape),
                  _const_spec((1, MLA_Q_RANK)), _const_spec((1, MLA_KV_RANK)),
                  tab(512), tab(512), tab(1024), tab(1024), tab(128)],
        out_specs=[tok(n) for n in widths],
        out_shape=[jax.ShapeDtypeStruct((bsz, s, n), BF16) for n in widths],
        compiler_params=_cparams(2),
    )(x, sc, sh, w, wuq2, wk2, wv, qg.reshape(1, -1), kvg.reshape(1, -1), dcos, dsin, qa, qb, pe)


def _half_masks():
    lane = lax.broadcasted_iota(jnp.int32, (1, LANES), 1)
    return lane < HEAD_DIM, lane >= HEAD_DIM


def _scores(qm, ks):
    ss = [_dot_nt(qm, k) for k in ks]
    m = functools.reduce(jnp.maximum, [jnp.max(s, axis=-1, keepdims=True) for s in ss])
    es = [jnp.exp(s - m) for s in ss]
    l = functools.reduce(lambda a, b: a + b, [jnp.sum(e, axis=-1, keepdims=True) for e in es])
    return es, l


def _attend(qm, ks, vs):
    es, l = _scores(qm, ks)
    o = functools.reduce(lambda a, b: a + b, [_dot(e.astype(BF16), v) for e, v in zip(es, vs)])
    return o * (1.0 / l)


def _na_bias_table(rpb):
    cols = np.arange(GRID_W)
    col_start = np.clip(cols - NA_WIN_W // 2, 0, GRID_W - NA_WIN_W)
    col_mask = (cols[None, :] >= col_start[:, None]) & (cols[None, :] < col_start[:, None] + NA_WIN_W)
    col_idx = np.clip(cols[None, :] - cols[:, None] + NA_WIN_W - 1, 0, 2 * NA_WIN_W - 2)
    off = np.arange(NA_WIN_H)
    ridx = np.arange(NA_WIN_H)[None, :] - off[:, None] + NA_WIN_H - 1
    t = rpb.astype(F32)[:, ridx]
    t = t[..., col_idx]
    t = t.transpose(1, 0, 3, 2, 4)
    t = jnp.where(col_mask[None, None, :, None, :], t, NEG_INF)
    return t.reshape(NA_WIN_H, NA_HEADS, GRID_W, NA_WIN_H * GRID_W)


def _na_kernel(rows, q_ref, k_ref, v_ref, kc_ref, vc_ref, bt_ref, o_ref):
    r = pl.program_id(1)
    rs = jnp.clip(r - NA_WIN_H // 2, 0, rows - NA_WIN_H)
    start = pl.multiple_of(rs * GRID_W, GRID_W)
    win = pl.ds(start, NA_WIN_H * GRID_W)
    m0, m1 = _half_masks()
    for j in range(NA_HEADS // 2):
        sl = slice(j * LANES, (j + 1) * LANES)
        qp = q_ref[0, :, sl]
        kp, vp = k_ref[0, win, sl], v_ref[0, win, sl]
        kcp, vcp = kc_ref[0, :, sl], vc_ref[0, :, sl]
        outs = []
        for par, msk in ((0, m0), (1, m1)):
            qm = jnp.where(msk, qp, jnp.zeros_like(qp))
            s_loc = _dot_nt(qm, kp) + bt_ref[0, 2 * j + par]
            s_ctx = _dot_nt(qm, kcp)
            m = jnp.maximum(jnp.max(s_loc, axis=-1, keepdims=True), jnp.max(s_ctx, axis=-1, keepdims=True))
            e_loc, e_ctx = jnp.exp(s_loc - m), jnp.exp(s_ctx - m)
            l = jnp.sum(e_loc, axis=-1, keepdims=True) + jnp.sum(e_ctx, axis=-1, keepdims=True)
            o = _dot(e_loc.astype(BF16), vp) + _dot(e_ctx.astype(BF16), vcp)
            outs.append(o * (1.0 / l))
        o_ref[0, :, sl] = jnp.where(m0, outs[0], outs[1]).astype(BF16)


def _na_attention(q, k, v, kc, vc, bt):
    bsz, s, w = q.shape
    l = kc.shape[1]
    rows = s // GRID_W
    assert rows >= NA_WIN_H
    full = lambda n: pl.BlockSpec((1, n, w), lambda b, r: (b, 0, 0))

    def bt_map(b, r):
        return (r - jnp.clip(r - NA_WIN_H // 2, 0, rows - NA_WIN_H), 0, 0, 0)

    return pl.pallas_call(
        functools.partial(_na_kernel, rows),
        grid=(bsz, rows),
        in_specs=[pl.BlockSpec((1, GRID_W, w), lambda b, r: (b, r, 0)),
                  full(s), full(s), full(l), full(l),
                  pl.BlockSpec((1,) + bt.shape[1:], bt_map)],
        out_specs=pl.BlockSpec((1, GRID_W, w), lambda b, r: (b, r, 0)),
        out_shape=jax.ShapeDtypeStruct((bsz, s, w), BF16),
        compiler_params=_cparams(2),
    )(q, k, v, kc, vc, bt)


def _slot_attn_kernel(n_heads, q_slot, has_lat, *refs):
    if has_lat:
        q_ref, k_ref, v_ref, kc_ref, vc_ref, o_ref = refs
    else:
        q_ref, kc_ref, vc_ref, o_ref = refs
    m0, m1 = _half_masks()
    for j in range(n_heads // 2):
        vsl = slice(j * LANES, (j + 1) * LANES)
        outs = []
        for par, msk in ((0, m0), (1, m1)):
            h = 2 * j + par
            if q_slot:
                ksl = slice(h * LANES, (h + 1) * LANES)
                qm = q_ref[0, :, ksl]
            else:
                ksl = vsl
                qp = q_ref[0, :, vsl]
                qm = jnp.where(msk, qp, jnp.zeros_like(qp))
            ks, vs = [kc_ref[0, :, ksl]], [vc_ref[0, :, vsl]]
            if has_lat:
                ks.insert(0, k_ref[0, :, ksl])
                vs.insert(0, v_ref[0, :, vsl])
            outs.append(_attend(qm, ks, vs))
        o_ref[0, :, vsl] = jnp.where(m0, outs[0], outs[1]).astype(BF16)


def _slot_attention(q, k, v, kc, vc, n_heads, q_slot, tq):
    bsz, sq, wq = q.shape
    l, wk, wv = kc.shape[1], kc.shape[2], vc.shape[2]
    has_lat = k is not None
    qspec = pl.BlockSpec((1, tq, wq), lambda b, i: (b, i, 0))
    full = lambda n, w: pl.BlockSpec((1, n, w), lambda b, i: (b, 0, 0))
    in_specs, args = [qspec], [q]
    if has_lat:
        s = k.shape[1]
        in_specs += [full(s, wk), full(s, wv)]
        args += [k, v]
    in_specs += [full(l, wk), full(l, wv)]
    args += [kc, vc]
    return pl.pallas_call(
        functools.partial(_slot_attn_kernel, n_heads, q_slot, has_lat),
        grid=(bsz, sq // tq),
        in_specs=in_specs,
        out_specs=pl.BlockSpec((1, tq, wv), lambda b, i: (b, i, 0)),
        out_shape=jax.ShapeDtypeStruct((bsz, sq, wv), BF16),
        compiler_params=_cparams(2),
    )(*args)


def _gqa_kernel(has_lat, tq, *refs):
    if has_lat:
        q_ref, k_ref, v_ref, kc_ref, vc_ref, o_ref = refs
    else:
        q_ref, kc_ref, vc_ref, o_ref = refs
    masks = _half_masks()
    res = {}
    for g in range(GQA_KV_HEADS):
        for var in range(2):
            par = g if var == 0 else 1 - g
            heads = (4 * g + par, 4 * g + 2 + par)
            vsl = slice(var * LANES, (var + 1) * LANES)
            qs = []
            for h in heads:
                qp = q_ref[0, :, (h // 2) * LANES:(h // 2 + 1) * LANES]
                qs.append(jnp.where(masks[par], qp, jnp.zeros_like(qp)))
            qm = jnp.concatenate(qs, axis=0)
            ks, vs = [kc_ref[0, :, vsl]], [vc_ref[0, :, vsl]]
            if has_lat:
                ks.insert(0, k_ref[0, :, vsl])
                vs.insert(0, v_ref[0, :, vsl])
            o = _attend(qm, ks, vs)
            res[heads[0]] = o[:tq]
            res[heads[1]] = o[tq:]
    for j in range(GQA_HEADS // 2):
        o_ref[0, :, j * LANES:(j + 1) * LANES] = jnp.where(
            masks[0], res[2 * j], res[2 * j + 1]).astype(BF16)


def _gqa_attention(q, k2, v2, k2c, v2c, tq):
    bsz, sq, wq = q.shape
    l = k2c.shape[1]
    has_lat = k2 is not None
    full = lambda n: pl.BlockSpec((1, n, 2 * LANES), lambda b, i: (b, 0, 0))
    in_specs, args = [pl.BlockSpec((1, tq, wq), lambda b, i: (b, i, 0))], [q]
    if has_lat:
        in_specs += [full(k2.shape[1])] * 2
        args += [k2, v2]
    in_specs += [full(l)] * 2
    args += [k2c, v2c]
    return pl.pallas_call(
        functools.partial(_gqa_kernel, has_lat, tq),
        grid=(bsz, sq // tq),
        in_specs=in_specs,
        out_specs=pl.BlockSpec((1, tq, wq), lambda b, i: (b, i, 0)),
        out_shape=jax.ShapeDtypeStruct((bsz, sq, wq), BF16),
        compiler_params=_cparams(2),
    )(*args)


def _diff_kernel(has_lat, tq, lam_init, *refs):
    if has_lat:
        q_ref, k_ref, v_ref, kc_ref, vc_ref, lq1, lk1, lq2, lk2, sub_ref, o_ref = refs
    else:
        q_ref, kc_ref, vc_ref, lq1, lk1, lq2, lk2, sub_ref, o_ref = refs
    lam = (jnp.exp(jnp.sum(lq1[...] * lk1[...], axis=-1, keepdims=True))
           - jnp.exp(jnp.sum(lq2[...] * lk2[...], axis=-1, keepdims=True)) + lam_init)
    m0, m1 = _half_masks()
    for h in range(DIFF_HEADS):
        sl = slice(h * LANES, (h + 1) * LANES)
        qp = q_ref[0, :, sl]
        zero = jnp.zeros_like(qp)
        qm = jnp.concatenate([jnp.where(m0, qp, zero), jnp.where(m1, qp, zero)], axis=0)
        ks, vs = [kc_ref[0, :, sl]], [vc_ref[0, :, sl]]
        if has_lat:
            ks.insert(0, k_ref[0, :, sl])
            vs.insert(0, v_ref[0, :, sl])
        es, l = _scores(qm, ks)
        inv = 1.0 / l
        c1, c2 = inv[:tq], lam * inv[tq:]
        o = None
        for e, v in zip(es, vs):
            p = (e[:tq] * c1 - e[tq:] * c2).astype(BF16)
            t = _dot(p, v)
            o = t if o is None else o + t
        o = o * lax.rsqrt(jnp.mean(o * o, axis=-1, keepdims=True) + RMS_EPS) * sub_ref[...]
        o_ref[0, :, sl] = (o * (1.0 - lam_init)).astype(BF16)


def _diff_attention(q, k, v, kc, vc, lams, subln, lam_init, tq):
    bsz, sq, w = q.shape
    l = kc.shape[1]
    has_lat = k is not None
    full = lambda n: pl.BlockSpec((1, n, w), lambda b, i: (b, 0, 0))
    in_specs, args = [pl.BlockSpec((1, tq, w), lambda b, i: (b, i, 0))], [q]
    if has_lat:
        in_specs += [full(k.shape[1])] * 2
        args += [k, v]
    in_specs += [full(l)] * 2 + [_const_spec((1, HEAD_DIM))] * 4 + [_const_spec((1, LANES))]
    args += [kc, vc] + [a.reshape(1, -1).astype(F32) for a in lams] + [subln.reshape(1, -1).astype(F32)]
    return pl.pallas_call(
        functools.partial(_diff_kernel, has_lat, tq, lam_init),
        grid=(bsz, sq // tq),
        in_specs=in_specs,
        out_specs=pl.BlockSpec((1, tq, w), lambda b, i: (b, i, 0)),
        out_shape=jax.ShapeDtypeStruct((bsz, sq, w), BF16),
        compiler_params=_cparams(2),
    )(*args)


def _layer_norm(z, g, b):
    mu = jnp.mean(z, axis=-1, keepdims=True)
    zc = z - mu
    var = jnp.mean(zc * zc, axis=-1, keepdims=True)
    return zc * lax.rsqrt(var + LN_EPS) * g + b


def _outproj_kernel(alpha, o1_ref, o2_ref, w1_ref, w2_ref, x_ref, g1_ref, lng_ref, lnb_ref,
                    sc2_ref, sh2_ref, rwh_ref, rwl_ref, rb_ref,
                    xo_ref, h2_ref, idx_ref, gw_ref):
    o = _dot(o1_ref[0], w1_ref[...]) + _dot(o2_ref[0], w2_ref[...])
    xn = _layer_norm(alpha * x_ref[0] + g1_ref[0] * o, lng_ref[...], lnb_ref[...])
    xo_ref[0] = xn
    h2 = xn * (1.0 + sc2_ref[0]) + sh2_ref[0]
    hi = h2.astype(BF16)
    h2_ref[0] = hi
    lo = (h2 - hi.astype(F32)).astype(BF16)
    logits = (_dot(hi, rwh_ref[...]) + _dot(lo, rwh_ref[...]) + _dot(hi, rwl_ref[...])) + rb_ref[...]
    lane = lax.broadcasted_iota(jnp.int32, logits.shape, 1).astype(F32)
    vals, idxs = [], []
    cur = logits
    for _ in range(TOP_K):
        m = jnp.max(cur, axis=-1, keepdims=True)
        ik = jnp.min(jnp.where(cur == m, lane, float(LANES)), axis=-1, keepdims=True)
        vals.append(m)
        idxs.append(ik)
        cur = jnp.where(lane == ik, -jnp.inf, cur)
    ws = [jnp.exp(v - vals[0]) for v in vals]
    inv = 1.0 / functools.reduce(lambda a, b: a + b, ws)
    idx_out = jnp.zeros_like(logits)
    w_out = jnp.zeros_like(logits)
    for k in range(TOP_K):
        idx_out = jnp.where(lane == float(k), idxs[k], idx_out)
        w_out = jnp.where(lane == float(k), ws[k] * inv, w_out)
    idx_ref[0] = idx_out.astype(jnp.int32)
    gw_ref[0] = w_out


def _outproj_ln_router(o1, o2, w_out, x, g1, lng, lnb, sc2, sh2, router, alpha, ts):
    bsz, s, d = x.shape
    batched = g1.shape[0] > 1
    rwh, rwl, rb = router
    w1, w2 = w_out[:512].astype(BF16), w_out[512:].astype(BF16)
    tok = lambda n: pl.BlockSpec((1, ts, n), lambda b, i: (b, i, 0))
    ms = _mod_spec(d, batched)
    return pl.pallas_call(
        functools.partial(_outproj_kernel, alpha),
        grid=(bsz, s // ts),
        in_specs=[tok(512), tok(512), _const_spec((512, d)), _const_spec((512, d)), tok(d), ms,
                  _const_spec((1, d)), _const_spec((1, d)), ms, ms,
                  _const_spec((d, LANES)), _const_spec((d, LANES)), _const_spec((1, LANES))],
        out_specs=[tok(d), tok(d), tok(LANES), tok(LANES)],
        out_shape=[jax.ShapeDtypeStruct((bsz, s, d), F32), jax.ShapeDtypeStruct((bsz, s, d), BF16),
                   jax.ShapeDtypeStruct((bsz, s, LANES), jnp.int32),
                   jax.ShapeDtypeStruct((bsz, s, LANES), F32)],
        compiler_params=_cparams(2),
    )(o1, o2, w1, w2, x, g1, lng.reshape(1, d), lnb.reshape(1, d), sc2, sh2, rwh, rwl, rb)


def _router_weights(router_w, router_b):
    d, e = router_w.shape
    wp = jnp.zeros((d, LANES), F32).at[:, :e].set(router_w)
    hi = wp.astype(BF16)
    lo = (wp - hi.astype(F32)).astype(BF16)
    rb = jnp.full((1, LANES), -jnp.inf, F32).at[0, :e].set(router_b)
    return hi, lo, rb


def _deinterleave_perm():
    p = np.zeros((GU_BLOCK, GU_BLOCK), np.float32)
    m = np.arange(GU_BLOCK // 2)
    p[2 * m, m] = 1.0
    p[2 * m + 1, GU_BLOCK // 2 + m] = 1.0
    return jnp.asarray(p, dtype=BF16)


def _ffn_kernel(te_ref, tv_ref, x_ref, wgu_ref, bgu_ref, wd_ref, bd_ref, perm_ref, y_ref, wgu_s, wd_s):
    j = pl.program_id(0)
    n_blk = wgu_s.shape[1] // GU_BLOCK
    half = GU_BLOCK // 2

    @pl.when(jnp.logical_or(j == 0, te_ref[j] != te_ref[jnp.maximum(j - 1, 0)]))
    def _():
        for b in range(n_blk):
            sl = slice(b * GU_BLOCK, (b + 1) * GU_BLOCK)
            wgu_s[:, sl] = _dot(wgu_ref[0, 0, :, sl].astype(BF16), perm_ref[...]).astype(BF16)
        wd_s[...] = wd_ref[0, 0].astype(BF16)

    @pl.when(tv_ref[j] > 0)
    def _():
        x = x_ref[...]
        acts = []
        for b in range(n_blk):
            sl = slice(b * GU_BLOCK, (b + 1) * GU_BLOCK)
            gu = _dot(x, wgu_s[:, sl]) + bgu_ref[0, :, sl]
            glu = jnp.minimum(gu[:, :half], SWIGLU_LIMIT)
            lin = jnp.clip(gu[:, half:], -SWIGLU_LIMIT, SWIGLU_LIMIT)
            acts.append(((lin + 1.0) * (glu * jax.nn.sigmoid(SWIGLU_ALPHA * glu))).astype(BF16))
        a = jnp.concatenate(acts, axis=-1)
        y_ref[...] = _dot(a, wd_s[...]) + bd_ref[0]

    @pl.when(tv_ref[j] == 0)
    def _():
        y_ref[...] = jnp.zeros_like(y_ref)


def _expert_ffn(xs, tile_expert, tile_valid, layer, w_gu, b_gu, w_down, b_down, tm):
    p, d = xs.shape
    _, e, _, f2 = w_gu.shape
    f = f2 // 2
    half = GU_BLOCK // 2
    bgu = jnp.stack([b_gu[:, 0::2].reshape(e, f // half, half),
                     b_gu[:, 1::2].reshape(e, f // half, half)], axis=2).reshape(e, 1, f2)
    wspec = lambda a, b: pl.BlockSpec((1, 1, a, b), lambda j, te, tv: (layer, te[j], 0, 0))
    bspec = lambda b: pl.BlockSpec((1, 1, b), lambda j, te, tv: (te[j], 0, 0))
    return pl.pallas_call(
        _ffn_kernel,
        grid_spec=pltpu.PrefetchScalarGridSpec(
            num_scalar_prefetch=2,
            grid=(p // tm,),
            in_specs=[pl.BlockSpec((tm, d), lambda j, te, tv: (j, 0)),
                      wspec(d, f2), bspec(f2), wspec(f, d), bspec(d),
                      pl.BlockSpec((GU_BLOCK, GU_BLOCK), lambda j, te, tv: (0, 0))],
            out_specs=pl.BlockSpec((tm, d), lambda j, te, tv: (j, 0)),
            scratch_shapes=[pltpu.VMEM((d, f2), BF16), pltpu.VMEM((f, d), BF16)]),
        out_shape=jax.ShapeDtypeStruct((p, d), F32),
        compiler_params=_cparams(1),
    )(tile_expert, tile_valid, xs, w_gu, bgu, w_down, b_down.reshape(e, 1, d), _deinterleave_perm())


def _moe_plan(top_idx, n_experts, tm):
    t = top_idx.shape[0]
    a = t * TOP_K
    e_flat = top_idx.reshape(a)
    order = jnp.argsort(e_flat, stable=True).astype(jnp.int32)
    inv = jnp.argsort(order).astype(jnp.int32)
    cnt = jnp.sum((e_flat[:, None] == jnp.arange(n_experts)[None, :]).astype(jnp.int32), axis=0)
    pc = ((cnt + tm - 1) // tm) * tm
    pend = jnp.cumsum(pc)
    pstart = pend - pc
    cstart = jnp.cumsum(cnt) - cnt
    dpos = pstart[e_flat] + (inv - cstart[e_flat])
    p = a + n_experts * tm
    rows = jnp.arange(p, dtype=jnp.int32)
    row_e = jnp.minimum(jnp.sum((rows[:, None] >= pend[None, :]).astype(jnp.int32), axis=1), n_experts - 1)
    rank = rows - pstart[row_e]
    valid = rank < cnt[row_e]
    src = jnp.where(valid, order[jnp.clip(cstart[row_e] + rank, 0, a - 1)] // TOP_K, 0)
    tile_start = jnp.arange(p // tm, dtype=jnp.int32) * tm
    tile_expert = jnp.minimum(jnp.sum((tile_start[:, None] >= pend[None, :]).astype(jnp.int32), axis=1),
                              n_experts - 1).astype(jnp.int32)
    tile_valid = (tile_start < pend[-1]).astype(jnp.int32)
    return src.astype(jnp.int32), dpos.astype(jnp.int32), tile_expert, tile_valid


def _combine_kernel(alpha, yg_ref, gw_ref, x_ref, g2_ref, lng_ref, lnb_ref, o_ref):
    gw = gw_ref[...]
    y = yg_ref[0] * gw[:, 0:1]
    for k in range(1, TOP_K):
        y = y + yg_ref[k] * gw[:, k:k + 1]
    o_ref[0] = _layer_norm(alpha * x_ref[0] + g2_ref[0] * y, lng_ref[...], lnb_ref[...])


def _combine_ln(yg, gw, x, g2, lng, lnb, alpha, row_offset, ts):
    bsz, s, d = x.shape
    batched = g2.shape[0] > 1
    nblk = s // ts
    off = row_offset // ts
    return pl.pallas_call(
        functools.partial(_combine_kernel, alpha),
        grid=(bsz, nblk),
        in_specs=[pl.BlockSpec((TOP_K, ts, d), lambda b, i: (0, off + b * nblk + i, 0)),
                  pl.BlockSpec((ts, LANES), lambda b, i: (off + b * nblk + i, 0)),
                  pl.BlockSpec((1, ts, d), lambda b, i: (b, i, 0)),
                  _mod_spec(d, batched), _const_spec((1, d)), _const_spec((1, d))],
        out_specs=pl.BlockSpec((1, ts, d), lambda b, i: (b, i, 0)),
        out_shape=jax.ShapeDtypeStruct((bsz, s, d), F32),
        compiler_params=_cparams(2),
    )(yg, gw, x, g2, lng.reshape(1, d), lnb.reshape(1, d))


def kernel(x, c, ctx, c_ctx, mod_w, mod_b, ln1_g, ln1_b, ln2_g, ln2_b, even_w_in, even_w_out, na_rpb, gqa_q_gain, gqa_k_gain, odd_w_in, odd_w_out, diff_lq1, diff_lk1, diff_lq2, diff_lk2, diff_subln, mla_q_gain, mla_w_uq, mla_kv_gain, mla_w_ukv, router_w, router_b, exp_w_gu, exp_b_gu, exp_w_down, exp_b_down):
    bsz, s, d = x.shape
    l = ctx.shape[1]
    depth = mod_w.shape[0]
    n_experts = router_w.shape[-1]
    alpha = (2 * depth) ** 0.25
    ts = min(512, s)
    tm = MOE_TILE

    pad = (-(bsz + 1)) % 8
    c_all = jnp.concatenate([c, c_ctx[None, :], jnp.zeros((pad, d), F32)], axis=0)
    mod = _modulation_all(c_all, mod_w, mod_b)

    for i in range(depth):
        last = i == depth - 1
        j = i // 2
        ml = [mod[i, :bsz, k * d:(k + 1) * d].reshape(bsz, 1, d) for k in range(6)]
        mc = [mod[i, bsz:bsz + 1, k * d:(k + 1) * d].reshape(1, 1, d) for k in range(6)]
        sh1, sc1, g1, sh2, sc2, g2 = ml
        csh1, csc1, cg1, csh2, csc2, cg2 = mc

        if i % 2 == 0:
            w = _even_weights(even_w_in[j], gqa_q_gain[j], gqa_k_gain[j])
            naq, nak, nav, gq, gk, gv = _inproj_even(
                x, sc1, sh1, w, _even_tables(s, gqa_q_gain[j], gqa_k_gain[j], True), ts)
            cnaq, cnak, cnav, cgq, cgk, cgv = _inproj_even(
                ctx, csc1, csh1, w, _even_tables(l, gqa_q_gain[j], gqa_k_gain[j], False), l)
            o1 = _na_attention(naq, nak, nav, cnak, cnav, _na_bias_table(na_rpb[j]))
            o2 = _gqa_attention(gq, gk, gv, cgk, cgv, min(256, s))
            if not last:
                co1 = _slot_attention(cnaq, None, None, cnak, cnav, NA_HEADS, False, l)
                co2 = _gqa_attention(cgq, None, None, cgk, cgv, l)
            w_out = even_w_out[j]
        else:
            lam_init = 0.8 - 0.6 * math.exp(-0.3 * i)
            weights = _odd_weights(odd_w_in[j], mla_w_uq[j], mla_w_ukv[j])
            gains = (mla_q_gain[j], mla_kv_gain[j])
            dq, dk, dv, mq, mk, mv = _inproj_odd(x, sc1, sh1, weights, gains, _odd_tables(s, True), ts)
            cdq, cdk, cdv, cmq, cmk, cmv = _inproj_odd(ctx, csc1, csh1, weights, gains,
                                                       _odd_tables(l, False), l)
            lams = (diff_lq1[j], diff_lk1[j], diff_lq2[j], diff_lk2[j])
            o1 = _diff_attention(dq, dk, dv, cdk, cdv, lams, diff_subln[j], lam_init, min(256, s))
            o2 = _slot_attention(mq, mk, mv, cmk, cmv, MLA_HEADS, True, min(512, s))
            if not last:
                co1 = _diff_attention(cdq, None, None, cdk, cdv, lams, diff_subln[j], lam_init, l)
                co2 = _slot_attention(cmq, None, None, cmk, cmv, MLA_HEADS, True, l)
            w_out = odd_w_out[j]

        router = _router_weights(router_w[i], router_b[i])
        x, h2, ridx, rgw = _outproj_ln_router(o1, o2, w_out, x, g1, ln1_g[i], ln1_b[i], sc2, sh2,
                                              router, alpha, ts)
        h2 = h2.reshape(bsz * s, d)
        ridx = ridx.reshape(bsz * s, LANES)
        rgw = rgw.reshape(bsz * s, LANES)
        if not last:
            ctx, ch2, cidx, cgw = _outproj_ln_router(co1, co2, w_out, ctx, cg1, ln1_g[i], ln1_b[i],
                                                     csc2, csh2, router, alpha, l)
            h2 = jnp.concatenate([h2, ch2.reshape(bsz * l, d)], axis=0)
            ridx = jnp.concatenate([ridx, cidx.reshape(bsz * l, LANES)], axis=0)
            rgw = jnp.concatenate([rgw, cgw.reshape(bsz * l, LANES)], axis=0)

        src, dpos, tile_expert, tile_valid = _moe_plan(ridx[:, :TOP_K], n_experts, tm)
        xs = h2.at[src].get(mode="promise_in_bounds")
        ys = _expert_ffn(xs, tile_expert, tile_valid, i, exp_w_gu, exp_b_gu[i],
                         exp_w_down, exp_b_down[i], tm)
        t = h2.shape[0]
        yg = ys.at[dpos.reshape(t, TOP_K).T].get(mode="promise_in_bounds")

        x = _combine_ln(yg, rgw, x, g2, ln2_g[i], ln2_b[i], alpha, 0, ts)
        if not last:
            ctx = _combine_ln(yg, rgw, ctx, cg2, ln2_g[i], ln2_b[i], alpha, bsz * s, l)
    return x
```

```python
import functools
import math

import numpy as np
import jax
import jax.numpy as jnp
from jax import lax
from jax.experimental import pallas as pl
from jax.experimental.pallas import tpu as pltpu

F32 = jnp.float32
BF16 = jnp.bfloat16

GRID_W = 64
HEAD_DIM = 64
ROPE_THETA = 10000.0
LN_EPS = 1e-6
RMS_EPS = 1e-6
NEG_INF = -1e30
NA_HEADS = 8
NA_WIN_H = 8
NA_WIN_W = 16
GQA_HEADS = 8
GQA_KV_HEADS = 2
DIFF_HEADS = 4
MLA_HEADS = 8
MLA_Q_RANK = 256
MLA_KV_RANK = 128
MLA_NOPE = 64
MLA_ROPE = 32
MLA_V = 64
MLA_QK = MLA_NOPE + MLA_ROPE
TOP_K = 4
SWIGLU_ALPHA = 1.702
SWIGLU_LIMIT = 7.0

LANES = 128
VMEM_LIMIT = 56 * 1024 * 1024
MOE_TILE = 512
GU_BLOCK = 512


def _cparams(n_axes):
    return pltpu.CompilerParams(dimension_semantics=("arbitrary",) * n_axes,
                                vmem_limit_bytes=VMEM_LIMIT)


def _dot(a, b):
    return jnp.dot(a, b, preferred_element_type=F32)


def _dot_nt(a, b):
    return lax.dot_general(a, b, (((1,), (1,)), ((), ())), preferred_element_type=F32)


def _mod_kernel(c_ref, w_ref, b_ref, o_ref):
    cv = c_ref[...]
    a = (cv * jax.nn.sigmoid(cv)).astype(BF16)
    o_ref[0] = _dot(a, w_ref[0].astype(BF16)) + b_ref[0]


def _modulation_all(c_all, mod_w, mod_b):
    depth, d, n = mod_w.shape
    rows = c_all.shape[0]
    tn = 1536
    return pl.pallas_call(
        _mod_kernel,
        grid=(depth, n // tn),
        in_specs=[pl.BlockSpec((rows, d), lambda i, j: (0, 0)),
                  pl.BlockSpec((1, d, tn), lambda i, j: (i, 0, j)),
                  pl.BlockSpec((1, 1, tn), lambda i, j: (i, 0, j))],
        out_specs=pl.BlockSpec((1, rows, tn), lambda i, j: (i, 0, j)),
        out_shape=jax.ShapeDtypeStruct((depth, rows, n), F32),
        compiler_params=_cparams(2),
    )(c_all, mod_w, mod_b.reshape(depth, 1, n))


def _rope_tables(s, dim):
    pos = jnp.arange(s)
    row = (pos // GRID_W).astype(F32)[:, None]
    col = (pos % GRID_W).astype(F32)[:, None]
    quarter = dim // 4
    inv_freq = ROPE_THETA ** (-jnp.arange(quarter, dtype=F32) / quarter)
    ar, ac = row * inv_freq, col * inv_freq
    cos = jnp.concatenate([jnp.cos(ar), jnp.cos(ar), jnp.cos(ac), jnp.cos(ac)], axis=-1)
    sin = jnp.concatenate([-jnp.sin(ar), jnp.sin(ar), -jnp.sin(ac), jnp.sin(ac)], axis=-1)
    return cos, sin


def _swap_perm(dim):
    q = dim // 4
    idx = np.arange(dim)
    return np.where((idx % (2 * q)) < q, idx + q, idx - q)


def _swap_cols(w, dim):
    n = w.shape[-1] // dim
    perm = (np.arange(n)[:, None] * dim + _swap_perm(dim)[None, :]).reshape(-1)
    return w[..., perm]


def _block_ones(n, blk):
    i = np.arange(n) // blk
    return jnp.asarray((i[:, None] == i[None, :]).astype(np.float32), dtype=BF16)


E_NAQ, E_NAK, E_NAV, E_GQ, E_GQS, E_GK, E_GKS, E_GV, E_END = (
    0, 512, 1024, 1536, 2048, 2560, 2816, 3072, 3328)


def _even_weights(w_in):
    naq, nak, nav, gq, gk, gv = jnp.split(w_in, [512, 1024, 1536, 2048, 2176], axis=-1)
    k0, k1 = gk[:, :64], gk[:, 64:]
    v0, v1 = gv[:, :64], gv[:, 64:]
    gk2 = jnp.concatenate([k0, k1, k1, k0], axis=-1)
    gv2 = jnp.concatenate([v0, v1, v1, v0], axis=-1)
    w = jnp.concatenate([naq * 0.125, nak, nav, gq, _swap_cols(gq, 64),
                         gk2, _swap_cols(gk2, 64), gv2], axis=-1).astype(BF16)
    return w


def _even_tables(s, q_gain, k_gain, rope):
    sw = _swap_perm(64)
    if rope:
        cos, sin = _rope_tables(s, 64)
    else:
        cos, sin = jnp.ones((s, 64), F32), jnp.zeros((s, 64), F32)
    qa = jnp.tile(cos * q_gain[None, :] * 0.125, (1, 8))
    qb = jnp.tile(sin * q_gain[sw][None, :] * 0.125, (1, 8))
    ka = jnp.tile(cos * k_gain[None, :], (1, 4))
    kb = jnp.tile(sin * k_gain[sw][None, :], (1, 4))
    return qa, qb, ka, kb


def _inproj_even_kernel(x_ref, sc_ref, sh_ref, w_ref, g512_ref, g256_ref,
                        qa_ref, qb_ref, ka_ref, kb_ref,
                        naq_ref, nak_ref, nav_ref, gq_ref, gk_ref, gv_ref):
    h = (x_ref[0] * (1.0 + sc_ref[0]) + sh_ref[0]).astype(BF16)
    naq_ref[0] = _dot(h, w_ref[:, E_NAQ:E_NAK]).astype(BF16)
    nak_ref[0] = _dot(h, w_ref[:, E_NAK:E_NAV]).astype(BF16)
    nav_ref[0] = _dot(h, w_ref[:, E_NAV:E_GQ]).astype(BF16)
    gv_ref[0] = _dot(h, w_ref[:, E_GV:E_END]).astype(BF16)
    y = _dot(h, w_ref[:, E_GQ:E_GQS])
    ys = _dot(h, w_ref[:, E_GQS:E_GK])
    r = lax.rsqrt(_dot((y * y).astype(BF16), g512_ref[...]) * (1.0 / HEAD_DIM) + RMS_EPS)
    gq_ref[0] = (r * (y * qa_ref[...] + ys * qb_ref[...])).astype(BF16)
    y = _dot(h, w_ref[:, E_GK:E_GKS])
    ys = _dot(h, w_ref[:, E_GKS:E_GV])
    r = lax.rsqrt(_dot((y * y).astype(BF16), g256_ref[...]) * (1.0 / HEAD_DIM) + RMS_EPS)
    gk_ref[0] = (r * (y * ka_ref[...] + ys * kb_ref[...])).astype(BF16)


def _mod_spec(d, batched):
    if batched:
        return pl.BlockSpec((1, 1, d), lambda b, s: (b, 0, 0))
    return pl.BlockSpec((1, 1, d), lambda b, s: (0, 0, 0))


def _const_spec(shape):
    nd = len(shape)
    return pl.BlockSpec(shape, lambda b, s: (0,) * nd)


def _inproj_even(x, sc, sh, w, tables, ts):
    bsz, s, d = x.shape
    batched = sc.shape[0] > 1
    qa, qb, ka, kb = tables
    tok = lambda n: pl.BlockSpec((1, ts, n), lambda b, i: (b, i, 0))
    tab = lambda n: pl.BlockSpec((ts, n), lambda b, i: (i, 0))
    widths = (512, 512, 512, 512, 256, 256)
    return pl.pallas_call(
        _inproj_even_kernel,
        grid=(bsz, s // ts),
        in_specs=[tok(d), _mod_spec(d, batched), _mod_spec(d, batched), _const_spec(w.shape),
                  _const_spec((512, 512)), _const_spec((256, 256)),
                  tab(512), tab(512), tab(256), tab(256)],
        out_specs=[tok(n) for n in widths],
        out_shape=[jax.ShapeDtypeStruct((bsz, s, n), BF16) for n in widths],
        compiler_params=_cparams(2),
    )(x, sc, sh, w, _block_ones(512, 64), _block_ones(256, 64), qa, qb, ka, kb)


O_DQ, O_DQS, O_DK, O_DKS, O_DV, O_CQ, O_CKV, O_PE, O_END = (
    0, 512, 1024, 1536, 2048, 2560, 2816, 2944, 3072)


def _odd_weights(w_in, w_uq, w_ukv):
    dq, dk, dv, cq, ckv, kpe = jnp.split(w_in, [512, 1024, 1536, 1792, 1920], axis=-1)
    d = w_in.shape[0]
    pe_slot = jnp.concatenate([kpe, _swap_cols(kpe, 32), jnp.zeros((d, 64), F32)], axis=-1)
    w = jnp.concatenate([dq, _swap_cols(dq, 64), dk, _swap_cols(dk, 64), dv, cq, ckv, pe_slot],
                        axis=-1).astype(BF16)
    uq = w_uq.reshape(MLA_Q_RANK, MLA_HEADS, MLA_QK)
    z32 = jnp.zeros((MLA_Q_RANK, MLA_HEADS, 32), F32)
    z64 = jnp.zeros((MLA_Q_RANK, MLA_HEADS, 64), F32)
    uq_pad = jnp.concatenate([uq, z32], axis=-1).reshape(MLA_Q_RANK, MLA_HEADS * LANES)
    uq_sw = jnp.concatenate([z64, _swap_cols(uq[..., MLA_NOPE:], 32), z32],
                            axis=-1).reshape(MLA_Q_RANK, MLA_HEADS * LANES)
    wuq2 = jnp.concatenate([uq_pad, uq_sw], axis=-1).astype(BF16)
    ukv = w_ukv.reshape(MLA_KV_RANK, MLA_HEADS, MLA_NOPE + MLA_V)
    zk = jnp.zeros((MLA_KV_RANK, MLA_HEADS, 64), F32)
    wk_pad = jnp.concatenate([ukv[..., :MLA_NOPE], zk], axis=-1).reshape(MLA_KV_RANK, MLA_HEADS * LANES)
    place = np.zeros((LANES, MLA_HEADS, LANES), np.float32)
    for j in range(MLA_ROPE):
        place[j, :, MLA_NOPE + j] = 1.0
        place[MLA_ROPE + j, :, MLA_NOPE + j] = 1.0
    wk2 = jnp.concatenate([wk_pad, jnp.asarray(place.reshape(LANES, MLA_HEADS * LANES))],
                          axis=0).astype(BF16)
    wv = ukv[..., MLA_NOPE:].reshape(MLA_KV_RANK, MLA_HEADS * MLA_V).astype(BF16)
    return w, wuq2, wk2, wv


def _odd_tables(s, rope):
    m_scale = MLA_QK ** -0.5
    if rope:
        cos64, sin64 = _rope_tables(s, 64)
        cos32, sin32 = _rope_tables(s, 32)
    else:
        cos64, sin64 = jnp.ones((s, 64), F32), jnp.zeros((s, 64), F32)
        cos32, sin32 = jnp.ones((s, 32), F32), jnp.zeros((s, 32), F32)
    one64, z32, z64 = jnp.ones((s, 64), F32), jnp.zeros((s, 32), F32), jnp.zeros((s, 64), F32)
    dcos, dsin = jnp.tile(cos64, (1, 8)), jnp.tile(sin64, (1, 8))
    qa = jnp.tile(jnp.concatenate([one64, cos32, z32], axis=-1) * m_scale, (1, MLA_HEADS))
    qb = jnp.tile(jnp.concatenate([z64, sin32, z32], axis=-1) * m_scale, (1, MLA_HEADS))
    pe = jnp.concatenate([cos32, sin32, z64], axis=-1)
    return dcos, dsin, qa, qb, pe


def _inproj_odd_kernel(x_ref, sc_ref, sh_ref, w_ref, wuq_ref, wk_ref, wv_ref, qg_ref, kvg_ref,
                       dcos_ref, dsin_ref, qa_ref, qb_ref, pe_ref,
                       dq_ref, dk_ref, dv_ref, mq_ref, mk_ref, mv_ref):
    h = (x_ref[0] * (1.0 + sc_ref[0]) + sh_ref[0]).astype(BF16)
    dcos, dsin = dcos_ref[...], dsin_ref[...]
    y = _dot(h, w_ref[:, O_DQ:O_DQS])
    ys = _dot(h, w_ref[:, O_DQS:O_DK])
    dq_ref[0] = ((y * dcos + ys * dsin) * 0.125).astype(BF16)
    y = _dot(h, w_ref[:, O_DK:O_DKS])
    ys = _dot(h, w_ref[:, O_DKS:O_DV])
    dk_ref[0] = (y * dcos + ys * dsin).astype(BF16)
    dv_ref[0] = _dot(h, w_ref[:, O_DV:O_CQ]).astype(BF16)
    cq = _dot(h, w_ref[:, O_CQ:O_CKV])
    nq = cq * lax.rsqrt(jnp.mean(cq * cq, axis=-1, keepdims=True) + RMS_EPS) * qg_ref[...]
    y2 = _dot(nq.astype(BF16), wuq_ref[...])
    half = MLA_HEADS * LANES
    mq_ref[0] = (y2[:, :half] * qa_ref[...] + y2[:, half:] * qb_ref[...]).astype(BF16)
    ckv = _dot(h, w_ref[:, O_CKV:O_PE])
    nk = ckv * lax.rsqrt(jnp.mean(ckv * ckv, axis=-1, keepdims=True) + RMS_EPS) * kvg_ref[...]
    pe = _dot(h, w_ref[:, O_PE:O_END]) * pe_ref[...]
    nkb = nk.astype(BF16)
    cat = jnp.concatenate([nkb, pe.astype(BF16)], axis=-1)
    mk_ref[0] = _dot(cat, wk_ref[...]).astype(BF16)
    mv_ref[0] = _dot(nkb, wv_ref[...]).astype(BF16)


def _inproj_odd(x, sc, sh, weights, gains, tables, ts):
    bsz, s, d = x.shape
    batched = sc.shape[0] > 1
    w, wuq2, wk2, wv = weights
    qg, kvg = gains
    dcos, dsin, qa, qb, pe = tables
    tok = lambda n: pl.BlockSpec((1, ts, n), lambda b, i: (b, i, 0))
    tab = lambda n: pl.BlockSpec((ts, n), lambda b, i: (i, 0))
    widths = (512, 512, 512, 1024, 1024, 512)
    return pl.pallas_call(
        _inproj_odd_kernel,
        grid=(bsz, s // ts),
        in_specs=[tok(d), _mod_spec(d, batched), _mod_spec(d, batched), _const_spec(w.shape),
                  _const_spec(wuq2.shape), _const_spec(wk2.shape), _const_spec(wv.shape),
                  _const_spec((1, MLA_Q_RANK)), _const_spec((1, MLA_KV_RANK)),
                  tab(512), tab(512), tab(1024), tab(1024), tab(128)],
        out_specs=[tok(n) for n in widths],
        out_shape=[jax.ShapeDtypeStruct((bsz, s, n), BF16) for n in widths],
        compiler_params=_cparams(2),
    )(x, sc, sh, w, wuq2, wk2, wv, qg.reshape(1, -1), kvg.reshape(1, -1), dcos, dsin, qa, qb, pe)


def _half_masks():
    lane = lax.broadcasted_iota(jnp.int32, (1, LANES), 1)
    return lane < HEAD_DIM, lane >= HEAD_DIM


def _scores(qm, ks):
    ss = [_dot_nt(qm, k) for k in ks]
    m = functools.reduce(jnp.maximum, [jnp.max(s, axis=-1, keepdims=True) for s in ss])
    es = [jnp.exp(s - m) for s in ss]
    l = functools.reduce(lambda a, b: a + b, [jnp.sum(e, axis=-1, keepdims=True) for e in es])
    return es, l


def _attend(qm, ks, vs):
    es, l = _scores(qm, ks)
    o = functools.reduce(lambda a, b: a + b, [_dot(e.astype(BF16), v) for e, v in zip(es, vs)])
    return o * (1.0 / l)


NA_RB = 4
NA_WIN_ROWS = NA_RB + NA_WIN_H - 1


def _na_block_plan(rows):
    plan = []
    for blk in range(rows // NA_RB):
        r0 = blk * NA_RB
        rs = [int(np.clip(r0 + i - NA_WIN_H // 2, 0, rows - NA_WIN_H)) for i in range(NA_RB)]
        ws = int(np.clip(r0 - NA_WIN_H // 2, 0, rows - NA_WIN_ROWS))
        pat = tuple((rs[i] - ws, r0 + i - ws) for i in range(NA_RB))
        assert all(0 <= o and o + NA_WIN_H <= NA_WIN_ROWS for o, _ in pat)
        plan.append(pat)
    assert all(p == plan[1] for p in plan[1:-1])
    return (plan[0], plan[1], plan[-1])


def _na_bias_table(rpb, rows):
    cols = np.arange(GRID_W)
    col_start = np.clip(cols - NA_WIN_W // 2, 0, GRID_W - NA_WIN_W)
    col_mask = (cols[None, :] >= col_start[:, None]) & (cols[None, :] < col_start[:, None] + NA_WIN_W)
    col_idx = np.clip(cols[None, :] - cols[:, None] + NA_WIN_W - 1, 0, 2 * NA_WIN_W - 2)
    wr = np.arange(NA_WIN_ROWS)
    tables = []
    for pat in _na_block_plan(rows):
        off = np.array([o for o, _ in pat])[:, None]
        rq = np.array([r for _, r in pat])[:, None]
        row_ok = (wr[None, :] >= off) & (wr[None, :] < off + NA_WIN_H)
        ridx = np.clip(wr[None, :] - rq + NA_WIN_H - 1, 0, 2 * NA_WIN_H - 2)
        t = rpb.astype(F32)[:, ridx]
        t = t[..., col_idx]
        t = t.transpose(0, 1, 3, 2, 4)
        ok = row_ok[None, :, None, :, None] & col_mask[None, None, :, None, :]
        t = jnp.where(ok, t, NEG_INF)
        tables.append(t.reshape(NA_HEADS, NA_RB * GRID_W, NA_WIN_ROWS * GRID_W))
    return jnp.stack(tables)


def _na_kernel(rows, q_ref, k_ref, v_ref, kc_ref, vc_ref, bt_ref, o_ref):
    r0 = pl.program_id(1) * NA_RB
    ws = jnp.clip(r0 - NA_WIN_H // 2, 0, rows - NA_WIN_ROWS)
    win = pl.ds(pl.multiple_of(ws * GRID_W, GRID_W), NA_WIN_ROWS * GRID_W)
    m0, m1 = _half_masks()
    for j in range(NA_HEADS // 2):
        sl = slice(j * LANES, (j + 1) * LANES)
        qp = q_ref[0, :, sl]
        kp, vp = k_ref[0, win, sl], v_ref[0, win, sl]
        kcp, vcp = kc_ref[0, :, sl], vc_ref[0, :, sl]
        outs = []
        for par, msk in ((0, m0), (1, m1)):
            qm = jnp.where(msk, qp, jnp.zeros_like(qp))
            s_loc = _dot_nt(qm, kp) + bt_ref[0, 2 * j + par]
            s_ctx = _dot_nt(qm, kcp)
            m = jnp.maximum(jnp.max(s_loc, axis=-1, keepdims=True), jnp.max(s_ctx, axis=-1, keepdims=True))
            e_loc, e_ctx = jnp.exp(s_loc - m), jnp.exp(s_ctx - m)
            l = jnp.sum(e_loc, axis=-1, keepdims=True) + jnp.sum(e_ctx, axis=-1, keepdims=True)
            o = _dot(e_loc.astype(BF16), vp) + _dot(e_ctx.astype(BF16), vcp)
            outs.append(o * (1.0 / l))
        o_ref[0, :, sl] = jnp.where(m0, outs[0], outs[1]).astype(BF16)


def _na_attention(q, k, v, kc, vc, rpb):
    bsz, s, w = q.shape
    l = kc.shape[1]
    rows = s // GRID_W
    nblk = rows // NA_RB
    assert rows % NA_RB == 0 and rows >= NA_WIN_ROWS and nblk >= 3
    bt = _na_bias_table(rpb, rows)
    tq = NA_RB * GRID_W
    full = lambda n: pl.BlockSpec((1, n, w), lambda b, r: (b, 0, 0))

    def bt_map(b, r):
        return ((r > 0).astype(jnp.int32) + (r == nblk - 1).astype(jnp.int32), 0, 0, 0)

    return pl.pallas_call(
        functools.partial(_na_kernel, rows),
        grid=(bsz, nblk),
        in_specs=[pl.BlockSpec((1, tq, w), lambda b, r: (b, r, 0)),
                  full(s), full(s), full(l), full(l),
                  pl.BlockSpec((1,) + bt.shape[1:], bt_map)],
        out_specs=pl.BlockSpec((1, tq, w), lambda b, r: (b, r, 0)),
        out_shape=jax.ShapeDtypeStruct((bsz, s, w), BF16),
        compiler_params=_cparams(2),
    )(q, k, v, kc, vc, bt)


def _slot_attn_kernel(n_heads, q_slot, has_lat, *refs):
    if has_lat:
        q_ref, k_ref, v_ref, kc_ref, vc_ref, o_ref = refs
    else:
        q_ref, kc_ref, vc_ref, o_ref = refs
    m0, m1 = _half_masks()
    for j in range(n_heads // 2):
        vsl = slice(j * LANES, (j + 1) * LANES)
        outs = []
        for par, msk in ((0, m0), (1, m1)):
            h = 2 * j + par
            if q_slot:
                ksl = slice(h * LANES, (h + 1) * LANES)
                qm = q_ref[0, :, ksl]
            else:
                ksl = vsl
                qp = q_ref[0, :, vsl]
                qm = jnp.where(msk, qp, jnp.zeros_like(qp))
            ks, vs = [kc_ref[0, :, ksl]], [vc_ref[0, :, vsl]]
            if has_lat:
                ks.insert(0, k_ref[0, :, ksl])
                vs.insert(0, v_ref[0, :, vsl])
            outs.append(_attend(qm, ks, vs))
        o_ref[0, :, vsl] = jnp.where(m0, outs[0], outs[1]).astype(BF16)


def _slot_attention(q, k, v, kc, vc, n_heads, q_slot, tq):
    bsz, sq, wq = q.shape
    l, wk, wv = kc.shape[1], kc.shape[2], vc.shape[2]
    has_lat = k is not None
    qspec = pl.BlockSpec((1, tq, wq), lambda b, i: (b, i, 0))
    full = lambda n, w: pl.BlockSpec((1, n, w), lambda b, i: (b, 0, 0))
    in_specs, args = [qspec], [q]
    if has_lat:
        s = k.shape[1]
        in_specs += [full(s, wk), full(s, wv)]
        args += [k, v]
    in_specs += [full(l, wk), full(l, wv)]
    args += [kc, vc]
    return pl.pallas_call(
        functools.partial(_slot_attn_kernel, n_heads, q_slot, has_lat),
        grid=(bsz, sq // tq),
        in_specs=in_specs,
        out_specs=pl.BlockSpec((1, tq, wv), lambda b, i: (b, i, 0)),
        out_shape=jax.ShapeDtypeStruct((bsz, sq, wv), BF16),
        compiler_params=_cparams(2),
    )(*args)


def _gqa_kernel(has_lat, tq, *refs):
    if has_lat:
        q_ref, k_ref, v_ref, kc_ref, vc_ref, o_ref = refs
    else:
        q_ref, kc_ref, vc_ref, o_ref = refs
    masks = _half_masks()
    res = {}
    for g in range(GQA_KV_HEADS):
        for var in range(2):
            par = g if var == 0 else 1 - g
            heads = (4 * g + par, 4 * g + 2 + par)
            vsl = slice(var * LANES, (var + 1) * LANES)
            qs = []
            for h in heads:
                qp = q_ref[0, :, (h // 2) * LANES:(h // 2 + 1) * LANES]
                qs.append(jnp.where(masks[par], qp, jnp.zeros_like(qp)))
            qm = jnp.concatenate(qs, axis=0)
            ks, vs = [kc_ref[0, :, vsl]], [vc_ref[0, :, vsl]]
            if has_lat:
                ks.insert(0, k_ref[0, :, vsl])
                vs.insert(0, v_ref[0, :, vsl])
            o = _attend(qm, ks, vs)
            res[heads[0]] = o[:tq]
            res[heads[1]] = o[tq:]
    for j in range(GQA_HEADS // 2):
        o_ref[0, :, j * LANES:(j + 1) * LANES] = jnp.where(
            masks[0], res[2 * j], res[2 * j + 1]).astype(BF16)


def _gqa_attention(q, k2, v2, k2c, v2c, tq):
    bsz, sq, wq = q.shape
    l = k2c.shape[1]
    has_lat = k2 is not None
    full = lambda n: pl.BlockSpec((1, n, 2 * LANES), lambda b, i: (b, 0, 0))
    in_specs, args = [pl.BlockSpec((1, tq, wq), lambda b, i: (b, i, 0))], [q]
    if has_lat:
        in_specs += [full(k2.shape[1])] * 2
        args += [k2, v2]
    in_specs += [full(l)] * 2
    args += [k2c, v2c]
    return pl.pallas_call(
        functools.partial(_gqa_kernel, has_lat, tq),
        grid=(bsz, sq // tq),
        in_specs=in_specs,
        out_specs=pl.BlockSpec((1, tq, wq), lambda b, i: (b, i, 0)),
        out_shape=jax.ShapeDtypeStruct((bsz, sq, wq), BF16),
        compiler_params=_cparams(2),
    )(*args)


def _diff_kernel(has_lat, tq, lam_init, *refs):
    if has_lat:
        q_ref, k_ref, v_ref, kc_ref, vc_ref, lq1, lk1, lq2, lk2, sub_ref, o_ref = refs
    else:
        q_ref, kc_ref, vc_ref, lq1, lk1, lq2, lk2, sub_ref, o_ref = refs
    lam = (jnp.exp(jnp.sum(lq1[...] * lk1[...], axis=-1, keepdims=True))
           - jnp.exp(jnp.sum(lq2[...] * lk2[...], axis=-1, keepdims=True)) + lam_init)
    m0, m1 = _half_masks()
    for h in range(DIFF_HEADS):
        sl = slice(h * LANES, (h + 1) * LANES)
        qp = q_ref[0, :, sl]
        zero = jnp.zeros_like(qp)
        qm = jnp.concatenate([jnp.where(m0, qp, zero), jnp.where(m1, qp, zero)], axis=0)
        ks, vs = [kc_ref[0, :, sl]], [vc_ref[0, :, sl]]
        if has_lat:
            ks.insert(0, k_ref[0, :, sl])
            vs.insert(0, v_ref[0, :, sl])
        es, l = _scores(qm, ks)
        inv = 1.0 / l
        c1, c2 = inv[:tq], lam * inv[tq:]
        o = None
        for e, v in zip(es, vs):
            p = (e[:tq] * c1 - e[tq:] * c2).astype(BF16)
            t = _dot(p, v)
            o = t if o is None else o + t
        o = o * lax.rsqrt(jnp.mean(o * o, axis=-1, keepdims=True) + RMS_EPS) * sub_ref[...]
        o_ref[0, :, sl] = (o * (1.0 - lam_init)).astype(BF16)


def _diff_attention(q, k, v, kc, vc, lams, subln, lam_init, tq):
    bsz, sq, w = q.shape
    l = kc.shape[1]
    has_lat = k is not None
    full = lambda n: pl.BlockSpec((1, n, w), lambda b, i: (b, 0, 0))
    in_specs, args = [pl.BlockSpec((1, tq, w), lambda b, i: (b, i, 0))], [q]
    if has_lat:
        in_specs += [full(k.shape[1])] * 2
        args += [k, v]
    in_specs += [full(l)] * 2 + [_const_spec((1, HEAD_DIM))] * 4 + [_const_spec((1, LANES))]
    args += [kc, vc] + [a.reshape(1, -1).astype(F32) for a in lams] + [subln.reshape(1, -1).astype(F32)]
    return pl.pallas_call(
        functools.partial(_diff_kernel, has_lat, tq, lam_init),
        grid=(bsz, sq // tq),
        in_specs=in_specs,
        out_specs=pl.BlockSpec((1, tq, w), lambda b, i: (b, i, 0)),
        out_shape=jax.ShapeDtypeStruct((bsz, sq, w), BF16),
        compiler_params=_cparams(2),
    )(*args)


def _layer_norm(z, g, b):
    mu = jnp.mean(z, axis=-1, keepdims=True)
    zc = z - mu
    var = jnp.mean(zc * zc, axis=-1, keepdims=True)
    return zc * lax.rsqrt(var + LN_EPS) * g + b


def _outproj_kernel(alpha, o1_ref, o2_ref, w1_ref, w2_ref, x_ref, g1_ref, lng_ref, lnb_ref,
                    sc2_ref, sh2_ref, rwh_ref, rwl_ref, rb_ref, tri_ref, cnt0_ref,
                    xo_ref, h2_ref, idx_ref, rank_ref, gw_ref, cnt_ref, carry_ref):
    o = _dot(o1_ref[0], w1_ref[...]) + _dot(o2_ref[0], w2_ref[...])
    xn = _layer_norm(alpha * x_ref[0] + g1_ref[0] * o, lng_ref[...], lnb_ref[...])
    xo_ref[0] = xn
    h2 = xn * (1.0 + sc2_ref[0]) + sh2_ref[0]
    hi = h2.astype(BF16)
    h2_ref[0] = hi
    lo = (h2 - hi.astype(F32)).astype(BF16)
    logits = (_dot(hi, rwh_ref[...]) + _dot(lo, rwh_ref[...]) + _dot(hi, rwl_ref[...])) + rb_ref[...]
    lane = lax.broadcasted_iota(jnp.int32, logits.shape, 1).astype(F32)
    vals, idxs = [], []
    cur = logits
    for _ in range(TOP_K):
        m = jnp.max(cur, axis=-1, keepdims=True)
        ik = jnp.min(jnp.where(cur == m, lane, float(LANES)), axis=-1, keepdims=True)
        vals.append(m)
        idxs.append(ik)
        cur = jnp.where(lane == ik, -jnp.inf, cur)
    ws = [jnp.exp(v - vals[0]) for v in vals]
    inv = 1.0 / functools.reduce(lambda a, b: a + b, ws)

    @pl.when(jnp.logical_and(pl.program_id(0) == 0, pl.program_id(1) == 0))
    def _():
        carry_ref[...] = cnt0_ref[...]

    sel = [lane == ik for ik in idxs]
    onehot = functools.reduce(lambda a, b: a + b, [m.astype(F32) for m in sel])
    before = _dot(tri_ref[...], onehot.astype(BF16)) + carry_ref[...]
    idx_out = jnp.zeros_like(logits)
    rank_out = jnp.zeros_like(logits)
    w_out = jnp.zeros_like(logits)
    for k in range(TOP_K):
        rk = jnp.sum(jnp.where(sel[k], before, 0.0), axis=-1, keepdims=True)
        idx_out = jnp.where(lane == float(k), idxs[k], idx_out)
        rank_out = jnp.where(lane == float(k), rk, rank_out)
        w_out = jnp.where(lane == float(k), ws[k] * inv, w_out)
    idx_ref[0] = idx_out.astype(jnp.int32)
    rank_ref[0] = rank_out.astype(jnp.int32)
    gw_ref[0] = w_out
    carry_ref[...] += jnp.sum(onehot, axis=0, keepdims=True)
    cnt_ref[...] = carry_ref[...]


def _outproj_ln_router(o1, o2, w_out, x, g1, lng, lnb, sc2, sh2, router, cnt0, alpha, ts):
    bsz, s, d = x.shape
    batched = g1.shape[0] > 1
    rwh, rwl, rb = router
    w1, w2 = w_out[:512].astype(BF16), w_out[512:].astype(BF16)
    tri = jnp.asarray(np.tril(np.ones((ts, ts), np.float32), -1), dtype=BF16)
    tok = lambda n: pl.BlockSpec((1, ts, n), lambda b, i: (b, i, 0))
    ms = _mod_spec(d, batched)
    return pl.pallas_call(
        functools.partial(_outproj_kernel, alpha),
        grid=(bsz, s // ts),
        in_specs=[tok(512), tok(512), _const_spec((512, d)), _const_spec((512, d)), tok(d), ms,
                  _const_spec((1, d)), _const_spec((1, d)), ms, ms,
                  _const_spec((d, LANES)), _const_spec((d, LANES)), _const_spec((1, LANES)),
                  _const_spec((ts, ts)), _const_spec((1, LANES))],
        out_specs=[tok(d), tok(d), tok(LANES), tok(LANES), tok(LANES), _const_spec((1, LANES))],
        out_shape=[jax.ShapeDtypeStruct((bsz, s, d), F32), jax.ShapeDtypeStruct((bsz, s, d), BF16),
                   jax.ShapeDtypeStruct((bsz, s, LANES), jnp.int32),
                   jax.ShapeDtypeStruct((bsz, s, LANES), jnp.int32),
                   jax.ShapeDtypeStruct((bsz, s, LANES), F32),
                   jax.ShapeDtypeStruct((1, LANES), F32)],
        scratch_shapes=[pltpu.VMEM((1, LANES), F32)],
        compiler_params=_cparams(2),
    )(o1, o2, w1, w2, x, g1, lng.reshape(1, d), lnb.reshape(1, d), sc2, sh2, rwh, rwl, rb, tri, cnt0)


def _router_weights(router_w, router_b):
    d, e = router_w.shape
    wp = jnp.zeros((d, LANES), F32).at[:, :e].set(router_w)
    hi = wp.astype(BF16)
    lo = (wp - hi.astype(F32)).astype(BF16)
    rb = jnp.full((1, LANES), -jnp.inf, F32).at[0, :e].set(router_b)
    return hi, lo, rb


def _deinterleave_perm():
    p = np.zeros((GU_BLOCK, GU_BLOCK), np.float32)
    m = np.arange(GU_BLOCK // 2)
    p[2 * m, m] = 1.0
    p[2 * m + 1, GU_BLOCK // 2 + m] = 1.0
    return jnp.asarray(p, dtype=BF16)


def _ffn_kernel(te_ref, tv_ref, x_ref, wgu_ref, bgu_ref, wd_ref, bd_ref, perm_ref, y_ref, wgu_s, wd_s):
    j = pl.program_id(0)
    n_blk = wgu_s.shape[1] // GU_BLOCK
    half = GU_BLOCK // 2

    @pl.when(jnp.logical_or(j == 0, te_ref[j] != te_ref[jnp.maximum(j - 1, 0)]))
    def _():
        for b in range(n_blk):
            sl = slice(b * GU_BLOCK, (b + 1) * GU_BLOCK)
            wgu_s[:, sl] = _dot(wgu_ref[0, 0, :, sl].astype(BF16), perm_ref[...]).astype(BF16)
        wd_s[...] = wd_ref[0, 0].astype(BF16)

    @pl.when(tv_ref[j] > 0)
    def _():
        x = x_ref[...]
        acts = []
        for b in range(n_blk):
            sl = slice(b * GU_BLOCK, (b + 1) * GU_BLOCK)
            gu = _dot(x, wgu_s[:, sl]) + bgu_ref[0, :, sl]
            glu = jnp.minimum(gu[:, :half], SWIGLU_LIMIT)
            lin = jnp.clip(gu[:, half:], -SWIGLU_LIMIT, SWIGLU_LIMIT)
            acts.append(((lin + 1.0) * (glu * jax.nn.sigmoid(SWIGLU_ALPHA * glu))).astype(BF16))
        a = jnp.concatenate(acts, axis=-1)
        y_ref[...] = (_dot(a, wd_s[...]) + bd_ref[0]).astype(BF16)

    @pl.when(tv_ref[j] == 0)
    def _():
        y_ref[...] = jnp.zeros_like(y_ref)


def _expert_ffn(xs, tile_expert, tile_valid, layer, w_gu, b_gu, w_down, b_down, tm):
    p, d = xs.shape
    _, e, _, f2 = w_gu.shape
    f = f2 // 2
    half = GU_BLOCK // 2
    bgu = jnp.stack([b_gu[:, 0::2].reshape(e, f // half, half),
                     b_gu[:, 1::2].reshape(e, f // half, half)], axis=2).reshape(e, 1, f2)
    wspec = lambda a, b: pl.BlockSpec((1, 1, a, b), lambda j, te, tv: (layer, te[j], 0, 0))
    bspec = lambda b: pl.BlockSpec((1, 1, b), lambda j, te, tv: (te[j], 0, 0))
    return pl.pallas_call(
        _ffn_kernel,
        grid_spec=pltpu.PrefetchScalarGridSpec(
            num_scalar_prefetch=2,
            grid=(p // tm,),
            in_specs=[pl.BlockSpec((tm, d), lambda j, te, tv: (j, 0)),
                      wspec(d, f2), bspec(f2), wspec(f, d), bspec(d),
                      pl.BlockSpec((GU_BLOCK, GU_BLOCK), lambda j, te, tv: (0, 0))],
            out_specs=pl.BlockSpec((tm, d), lambda j, te, tv: (j, 0)),
            scratch_shapes=[pltpu.VMEM((d, f2), BF16), pltpu.VMEM((f, d), BF16)]),
        out_shape=jax.ShapeDtypeStruct((p, d), BF16),
        compiler_params=_cparams(1),
    )(tile_expert, tile_valid, xs, w_gu, bgu, w_down, b_down.reshape(e, 1, d), _deinterleave_perm())


def _moe_plan(idx4, rank4, counts, tm):
    t = idx4.shape[0]
    n_experts = counts.shape[0]
    pc = ((counts + tm - 1) // tm) * tm
    pend = jnp.cumsum(pc)
    pstart = pend - pc
    dpos = pstart[idx4] + rank4
    p = t * TOP_K + n_experts * tm
    tok = jnp.broadcast_to(jnp.arange(t, dtype=jnp.int32)[:, None], (t, TOP_K))
    src = jnp.zeros((p,), jnp.int32).at[dpos.reshape(-1)].set(
        tok.reshape(-1), unique_indices=True, mode="promise_in_bounds")
    tile_start = jnp.arange(p // tm, dtype=jnp.int32) * tm
    tile_expert = jnp.minimum(jnp.sum((tile_start[:, None] >= pend[None, :]).astype(jnp.int32), axis=1),
                              n_experts - 1).astype(jnp.int32)
    tile_valid = (tile_start < pend[-1]).astype(jnp.int32)
    return src, dpos.astype(jnp.int32), tile_expert, tile_valid


def _combine_kernel(alpha, yg_ref, gw_ref, x_ref, g2_ref, lng_ref, lnb_ref, o_ref):
    gw = gw_ref[...]
    y = yg_ref[0].astype(F32) * gw[:, 0:1]
    for k in range(1, TOP_K):
        y = y + yg_ref[k].astype(F32) * gw[:, k:k + 1]
    o_ref[0] = _layer_norm(alpha * x_ref[0] + g2_ref[0] * y, lng_ref[...], lnb_ref[...])


def _combine_ln(yg, gw, x, g2, lng, lnb, alpha, row_offset, ts):
    bsz, s, d = x.shape
    batched = g2.shape[0] > 1
    nblk = s // ts
    off = row_offset // ts
    return pl.pallas_call(
        functools.partial(_combine_kernel, alpha),
        grid=(bsz, nblk),
        in_specs=[pl.BlockSpec((TOP_K, ts, d), lambda b, i: (0, off + b * nblk + i, 0)),
                  pl.BlockSpec((ts, LANES), lambda b, i: (off + b * nblk + i, 0)),
                  pl.BlockSpec((1, ts, d), lambda b, i: (b, i, 0)),
                  _mod_spec(d, batched), _const_spec((1, d)), _const_spec((1, d))],
        out_specs=pl.BlockSpec((1, ts, d), lambda b, i: (b, i, 0)),
        out_shape=jax.ShapeDtypeStruct((bsz, s, d), F32),
        compiler_params=_cparams(2),
    )(yg, gw, x, g2, lng.reshape(1, d), lnb.reshape(1, d))


def kernel(x, c, ctx, c_ctx, mod_w, mod_b, ln1_g, ln1_b, ln2_g, ln2_b, even_w_in, even_w_out, na_rpb, gqa_q_gain, gqa_k_gain, odd_w_in, odd_w_out, diff_lq1, diff_lk1, diff_lq2, diff_lk2, diff_subln, mla_q_gain, mla_w_uq, mla_kv_gain, mla_w_ukv, router_w, router_b, exp_w_gu, exp_b_gu, exp_w_down, exp_b_down):
    bsz, s, d = x.shape
    l = ctx.shape[1]
    depth = mod_w.shape[0]
    n_experts = router_w.shape[-1]
    alpha = (2 * depth) ** 0.25
    ts = min(512, s)
    tm = MOE_TILE

    pad = (-(bsz + 1)) % 8
    c_all = jnp.concatenate([c, c_ctx[None, :], jnp.zeros((pad, d), F32)], axis=0)
    mod = _modulation_all(c_all, mod_w, mod_b)

    for i in range(depth):
        last = i == depth - 1
        j = i // 2
        ml = [mod[i, :bsz, k * d:(k + 1) * d].reshape(bsz, 1, d) for k in range(6)]
        mc = [mod[i, bsz:bsz + 1, k * d:(k + 1) * d].reshape(1, 1, d) for k in range(6)]
        sh1, sc1, g1, sh2, sc2, g2 = ml
        csh1, csc1, cg1, csh2, csc2, cg2 = mc

        if i % 2 == 0:
            w = _even_weights(even_w_in[j])
            naq, nak, nav, gq, gk, gv = _inproj_even(
                x, sc1, sh1, w, _even_tables(s, gqa_q_gain[j], gqa_k_gain[j], True), ts)
            cnaq, cnak, cnav, cgq, cgk, cgv = _inproj_even(
                ctx, csc1, csh1, w, _even_tables(l, gqa_q_gain[j], gqa_k_gain[j], False), l)
            o1 = _na_attention(naq, nak, nav, cnak, cnav, na_rpb[j])
            o2 = _gqa_attention(gq, gk, gv, cgk, cgv, min(256, s))
            if not last:
                co1 = _slot_attention(cnaq, None, None, cnak, cnav, NA_HEADS, False, l)
                co2 = _gqa_attention(cgq, None, None, cgk, cgv, l)
            w_out = even_w_out[j]
        else:
            lam_init = 0.8 - 0.6 * math.exp(-0.3 * i)
            weights = _odd_weights(odd_w_in[j], mla_w_uq[j], mla_w_ukv[j])
            gains = (mla_q_gain[j], mla_kv_gain[j])
            dq, dk, dv, mq, mk, mv = _inproj_odd(x, sc1, sh1, weights, gains, _odd_tables(s, True), ts)
            cdq, cdk, cdv, cmq, cmk, cmv = _inproj_odd(ctx, csc1, csh1, weights, gains,
                                                       _odd_tables(l, False), l)
            lams = (diff_lq1[j], diff_lk1[j], diff_lq2[j], diff_lk2[j])
            o1 = _diff_attention(dq, dk, dv, cdk, cdv, lams, diff_subln[j], lam_init, min(256, s))
            o2 = _slot_attention(mq, mk, mv, cmk, cmv, MLA_HEADS, True, min(512, s))
            if not last:
                co1 = _diff_attention(cdq, None, None, cdk, cdv, lams, diff_subln[j], lam_init, l)
                co2 = _slot_attention(cmq, None, None, cmk, cmv, MLA_HEADS, True, l)
            w_out = odd_w_out[j]

        router = _router_weights(router_w[i], router_b[i])
        cnt0 = jnp.zeros((1, LANES), F32)
        x, h2, ridx, rrank, rgw, cnt = _outproj_ln_router(
            o1, o2, w_out, x, g1, ln1_g[i], ln1_b[i], sc2, sh2, router, cnt0, alpha, ts)
        h2 = h2.reshape(bsz * s, d)
        flat = lambda a, n: a.reshape(bsz * n, LANES)
        ridx, rrank, rgw = flat(ridx, s), flat(rrank, s), flat(rgw, s)
        if not last:
            ctx, ch2, cidx, crank, cgw, cnt = _outproj_ln_router(
                co1, co2, w_out, ctx, cg1, ln1_g[i], ln1_b[i], csc2, csh2, router, cnt, alpha, l)
            h2 = jnp.concatenate([h2, ch2.reshape(bsz * l, d)], axis=0)
            ridx = jnp.concatenate([ridx, flat(cidx, l)], axis=0)
            rrank = jnp.concatenate([rrank, flat(crank, l)], axis=0)
            rgw = jnp.concatenate([rgw, flat(cgw, l)], axis=0)

        counts = cnt[0, :n_experts].astype(jnp.int32)
        src, dpos, tile_expert, tile_valid = _moe_plan(ridx[:, :TOP_K], rrank[:, :TOP_K], counts, tm)
        xs = h2.at[src].get(mode="promise_in_bounds")
        ys = _expert_ffn(xs, tile_expert, tile_valid, i, exp_w_gu, exp_b_gu[i],
                         exp_w_down, exp_b_down[i], tm)
        t = h2.shape[0]
        yg = ys.at[dpos.reshape(t, TOP_K).T].get(mode="promise_in_bounds")

        x = _combine_ln(yg, rgw, x, g2, ln2_g[i], ln2_b[i], alpha, 0, ts)
        if not last:
            ctx = _combine_ln(yg, rgw, ctx, cg2, ln2_g[i], ln2_b[i], alpha, bsz * s, l)
    return x
```

```python
import functools
import math

import numpy as np
import jax
import jax.numpy as jnp
from jax import lax
from jax.experimental import pallas as pl
from jax.experimental.pallas import tpu as pltpu
from jax.experimental.pallas import tpu_sc as plsc

F32 = jnp.float32
BF16 = jnp.bfloat16

GRID_W = 64
HEAD_DIM = 64
ROPE_THETA = 10000.0
LN_EPS = 1e-6
RMS_EPS = 1e-6
NEG_INF = -1e30
NA_HEADS = 8
NA_WIN_H = 8
NA_WIN_W = 16
GQA_HEADS = 8
GQA_KV_HEADS = 2
DIFF_HEADS = 4
MLA_HEADS = 8
MLA_Q_RANK = 256
MLA_KV_RANK = 128
MLA_NOPE = 64
MLA_ROPE = 32
MLA_V = 64
MLA_QK = MLA_NOPE + MLA_ROPE
TOP_K = 4
SWIGLU_ALPHA = 1.702
SWIGLU_LIMIT = 7.0

LANES = 128
VMEM_LIMIT = 56 * 1024 * 1024
MOE_TILE = 512
GU_BLOCK = 512


def _cparams(n_axes):
    return pltpu.CompilerParams(dimension_semantics=("arbitrary",) * n_axes,
                                vmem_limit_bytes=VMEM_LIMIT)


def _dot(a, b):
    return jnp.dot(a, b, preferred_element_type=F32)


def _dot_nt(a, b):
    return lax.dot_general(a, b, (((1,), (1,)), ((), ())), preferred_element_type=F32)


def _mod_kernel(c_ref, w_ref, b_ref, o_ref):
    cv = c_ref[...]
    a = (cv * jax.nn.sigmoid(cv)).astype(BF16)
    o_ref[0] = _dot(a, w_ref[0].astype(BF16)) + b_ref[0]


def _modulation_all(c_all, mod_w, mod_b):
    depth, d, n = mod_w.shape
    rows = c_all.shape[0]
    tn = 1536
    return pl.pallas_call(
        _mod_kernel,
        grid=(depth, n // tn),
        in_specs=[pl.BlockSpec((rows, d), lambda i, j: (0, 0)),
                  pl.BlockSpec((1, d, tn), lambda i, j: (i, 0, j)),
                  pl.BlockSpec((1, 1, tn), lambda i, j: (i, 0, j))],
        out_specs=pl.BlockSpec((1, rows, tn), lambda i, j: (i, 0, j)),
        out_shape=jax.ShapeDtypeStruct((depth, rows, n), F32),
        compiler_params=_cparams(2),
    )(c_all, mod_w, mod_b.reshape(depth, 1, n))


def _rope_tables(s, dim):
    pos = jnp.arange(s)
    row = (pos // GRID_W).astype(F32)[:, None]
    col = (pos % GRID_W).astype(F32)[:, None]
    quarter = dim // 4
    inv_freq = ROPE_THETA ** (-jnp.arange(quarter, dtype=F32) / quarter)
    ar, ac = row * inv_freq, col * inv_freq
    cos = jnp.concatenate([jnp.cos(ar), jnp.cos(ar), jnp.cos(ac), jnp.cos(ac)], axis=-1)
    sin = jnp.concatenate([-jnp.sin(ar), jnp.sin(ar), -jnp.sin(ac), jnp.sin(ac)], axis=-1)
    return cos, sin


def _swap_perm(dim):
    q = dim // 4
    idx = np.arange(dim)
    return np.where((idx % (2 * q)) < q, idx + q, idx - q)


def _swap_cols(w, dim):
    n = w.shape[-1] // dim
    perm = (np.arange(n)[:, None] * dim + _swap_perm(dim)[None, :]).reshape(-1)
    return w[..., perm]


def _block_ones(n, blk):
    i = np.arange(n) // blk
    return jnp.asarray((i[:, None] == i[None, :]).astype(np.float32), dtype=BF16)


E_NAQ, E_NAK, E_NAV, E_GQ, E_GQS, E_GK, E_GKS, E_GV, E_END = (
    0, 512, 1024, 1536, 2048, 2560, 2816, 3072, 3328)


def _even_weights(w_in):
    naq, nak, nav, gq, gk, gv = jnp.split(w_in, [512, 1024, 1536, 2048, 2176], axis=-1)
    k0, k1 = gk[:, :64], gk[:, 64:]
    v0, v1 = gv[:, :64], gv[:, 64:]
    gk2 = jnp.concatenate([k0, k1, k1, k0], axis=-1)
    gv2 = jnp.concatenate([v0, v1, v1, v0], axis=-1)
    w = jnp.concatenate([naq * 0.125, nak, nav, gq, _swap_cols(gq, 64),
                         gk2, _swap_cols(gk2, 64), gv2], axis=-1).astype(BF16)
    return w


def _even_tables(s, q_gain, k_gain, rope):
    sw = _swap_perm(64)
    if rope:
        cos, sin = _rope_tables(s, 64)
    else:
        cos, sin = jnp.ones((s, 64), F32), jnp.zeros((s, 64), F32)
    qa = jnp.tile(cos * q_gain[None, :] * 0.125, (1, 8))
    qb = jnp.tile(sin * q_gain[sw][None, :] * 0.125, (1, 8))
    ka = jnp.tile(cos * k_gain[None, :], (1, 4))
    kb = jnp.tile(sin * k_gain[sw][None, :], (1, 4))
    return qa, qb, ka, kb


def _inproj_even_kernel(x_ref, sc_ref, sh_ref, w_ref, g512_ref, g256_ref,
                        qa_ref, qb_ref, ka_ref, kb_ref,
                        naq_ref, nak_ref, nav_ref, gq_ref, gk_ref, gv_ref):
    h = (x_ref[0] * (1.0 + sc_ref[0]) + sh_ref[0]).astype(BF16)
    naq_ref[0] = _dot(h, w_ref[:, E_NAQ:E_NAK]).astype(BF16)
    nak_ref[0] = _dot(h, w_ref[:, E_NAK:E_NAV]).astype(BF16)
    nav_ref[0] = _dot(h, w_ref[:, E_NAV:E_GQ]).astype(BF16)
    gv_ref[0] = _dot(h, w_ref[:, E_GV:E_END]).astype(BF16)
    y = _dot(h, w_ref[:, E_GQ:E_GQS])
    ys = _dot(h, w_ref[:, E_GQS:E_GK])
    r = lax.rsqrt(_dot((y * y).astype(BF16), g512_ref[...]) * (1.0 / HEAD_DIM) + RMS_EPS)
    gq_ref[0] = (r * (y * qa_ref[...] + ys * qb_ref[...])).astype(BF16)
    y = _dot(h, w_ref[:, E_GK:E_GKS])
    ys = _dot(h, w_ref[:, E_GKS:E_GV])
    r = lax.rsqrt(_dot((y * y).astype(BF16), g256_ref[...]) * (1.0 / HEAD_DIM) + RMS_EPS)
    gk_ref[0] = (r * (y * ka_ref[...] + ys * kb_ref[...])).astype(BF16)


def _mod_spec(d, batched):
    if batched:
        return pl.BlockSpec((1, 1, d), lambda b, s: (b, 0, 0))
    return pl.BlockSpec((1, 1, d), lambda b, s: (0, 0, 0))


def _const_spec(shape):
    nd = len(shape)
    return pl.BlockSpec(shape, lambda b, s: (0,) * nd)


def _inproj_even(x, sc, sh, w, tables, ts):
    bsz, s, d = x.shape
    batched = sc.shape[0] > 1
    qa, qb, ka, kb = tables
    tok = lambda n: pl.BlockSpec((1, ts, n), lambda b, i: (b, i, 0))
    tab = lambda n: pl.BlockSpec((ts, n), lambda b, i: (i, 0))
    widths = (512, 512, 512, 512, 256, 256)
    return pl.pallas_call(
        _inproj_even_kernel,
        grid=(bsz, s // ts),
        in_specs=[tok(d), _mod_spec(d, batched), _mod_spec(d, batched), _const_spec(w.shape),
                  _const_spec((512, 512)), _const_spec((256, 256)),
                  tab(512), tab(512), tab(256), tab(256)],
        out_specs=[tok(n) for n in widths],
        out_shape=[jax.ShapeDtypeStruct((bsz, s, n), BF16) for n in widths],
        compiler_params=_cparams(2),
    )(x, sc, sh, w, _block_ones(512, 64), _block_ones(256, 64), qa, qb, ka, kb)


O_DQ, O_DQS, O_DK, O_DKS, O_DV, O_CQ, O_CKV, O_PE, O_END = (
    0, 512, 1024, 1536, 2048, 2560, 2816, 2944, 3072)


def _odd_weights(w_in, w_uq, w_ukv):
    dq, dk, dv, cq, ckv, kpe = jnp.split(w_in, [512, 1024, 1536, 1792, 1920], axis=-1)
    d = w_in.shape[0]
    pe_slot = jnp.concatenate([kpe, _swap_cols(kpe, 32), jnp.zeros((d, 64), F32)], axis=-1)
    w = jnp.concatenate([dq, _swap_cols(dq, 64), dk, _swap_cols(dk, 64), dv, cq, ckv, pe_slot],
                        axis=-1).astype(BF16)
    uq = w_uq.reshape(MLA_Q_RANK, MLA_HEADS, MLA_QK)
    z32 = jnp.zeros((MLA_Q_RANK, MLA_HEADS, 32), F32)
    z64 = jnp.zeros((MLA_Q_RANK, MLA_HEADS, 64), F32)
    uq_pad = jnp.concatenate([uq, z32], axis=-1).reshape(MLA_Q_RANK, MLA_HEADS * LANES)
    uq_sw = jnp.concatenate([z64, _swap_cols(uq[..., MLA_NOPE:], 32), z32],
                            axis=-1).reshape(MLA_Q_RANK, MLA_HEADS * LANES)
    wuq2 = jnp.concatenate([uq_pad, uq_sw], axis=-1).astype(BF16)
    ukv = w_ukv.reshape(MLA_KV_RANK, MLA_HEADS, MLA_NOPE + MLA_V)
    zk = jnp.zeros((MLA_KV_RANK, MLA_HEADS, 64), F32)
    wk_pad = jnp.concatenate([ukv[..., :MLA_NOPE], zk], axis=-1).reshape(MLA_KV_RANK, MLA_HEADS * LANES)
    place = np.zeros((LANES, MLA_HEADS, LANES), np.float32)
    for j in range(MLA_ROPE):
        place[j, :, MLA_NOPE + j] = 1.0
        place[MLA_ROPE + j, :, MLA_NOPE + j] = 1.0
    wk2 = jnp.concatenate([wk_pad, jnp.asarray(place.reshape(LANES, MLA_HEADS * LANES))],
                          axis=0).astype(BF16)
    wv = ukv[..., MLA_NOPE:].reshape(MLA_KV_RANK, MLA_HEADS * MLA_V).astype(BF16)
    return w, wuq2, wk2, wv


def _odd_tables(s, rope):
    m_scale = MLA_QK ** -0.5
    if rope:
        cos64, sin64 = _rope_tables(s, 64)
        cos32, sin32 = _rope_tables(s, 32)
    else:
        cos64, sin64 = jnp.ones((s, 64), F32), jnp.zeros((s, 64), F32)
        cos32, sin32 = jnp.ones((s, 32), F32), jnp.zeros((s, 32), F32)
    one64, z32, z64 = jnp.ones((s, 64), F32), jnp.zeros((s, 32), F32), jnp.zeros((s, 64), F32)
    dcos, dsin = jnp.tile(cos64, (1, 8)), jnp.tile(sin64, (1, 8))
    qa = jnp.tile(jnp.concatenate([one64, cos32, z32], axis=-1) * m_scale, (1, MLA_HEADS))
    qb = jnp.tile(jnp.concatenate([z64, sin32, z32], axis=-1) * m_scale, (1, MLA_HEADS))
    pe = jnp.concatenate([cos32, sin32, z64], axis=-1)
    return dcos, dsin, qa, qb, pe


def _inproj_odd_kernel(x_ref, sc_ref, sh_ref, w_ref, wuq_ref, wk_ref, wv_ref, qg_ref, kvg_ref,
                       dcos_ref, dsin_ref, qa_ref, qb_ref, pe_ref,
                       dq_ref, dk_ref, dv_ref, mq_ref, mk_ref, mv_ref):
    h = (x_ref[0] * (1.0 + sc_ref[0]) + sh_ref[0]).astype(BF16)
    dcos, dsin = dcos_ref[...], dsin_ref[...]
    y = _dot(h, w_ref[:, O_DQ:O_DQS])
    ys = _dot(h, w_ref[:, O_DQS:O_DK])
    dq_ref[0] = ((y * dcos + ys * dsin) * 0.125).astype(BF16)
    y = _dot(h, w_ref[:, O_DK:O_DKS])
    ys = _dot(h, w_ref[:, O_DKS:O_DV])
    dk_ref[0] = (y * dcos + ys * dsin).astype(BF16)
    dv_ref[0] = _dot(h, w_ref[:, O_DV:O_CQ]).astype(BF16)
    cq = _dot(h, w_ref[:, O_CQ:O_CKV])
    nq = cq * lax.rsqrt(jnp.mean(cq * cq, axis=-1, keepdims=True) + RMS_EPS) * qg_ref[...]
    y2 = _dot(nq.astype(BF16), wuq_ref[...])
    half = MLA_HEADS * LANES
    mq_ref[0] = (y2[:, :half] * qa_ref[...] + y2[:, half:] * qb_ref[...]).astype(BF16)
    ckv = _dot(h, w_ref[:, O_CKV:O_PE])
    nk = ckv * lax.rsqrt(jnp.mean(ckv * ckv, axis=-1, keepdims=True) + RMS_EPS) * kvg_ref[...]
    pe = _dot(h, w_ref[:, O_PE:O_END]) * pe_ref[...]
    nkb = nk.astype(BF16)
    cat = jnp.concatenate([nkb, pe.astype(BF16)], axis=-1)
    mk_ref[0] = _dot(cat, wk_ref[...]).astype(BF16)
    mv_ref[0] = _dot(nkb, wv_ref[...]).astype(BF16)


def _inproj_odd(x, sc, sh, weights, gains, tables, ts):
    bsz, s, d = x.shape
    batched = sc.shape[0] > 1
    w, wuq2, wk2, wv = weights
    qg, kvg = gains
    dcos, dsin, qa, qb, pe = tables
    tok = lambda n: pl.BlockSpec((1, ts, n), lambda b, i: (b, i, 0))
    tab = lambda n: pl.BlockSpec((ts, n), lambda b, i: (i, 0))
    widths = (512, 512, 512, 1024, 1024, 512)
    return pl.pallas_call(
        _inproj_odd_kernel,
        grid=(bsz, s // ts),
        in_specs=[tok(d), _mod_spec(d, batched), _mod_spec(d, batched), _const_spec(w.shape),
                  _const_spec(wuq2.shape), _const_spec(wk2.shape), _const_spec(wv.shape),
                  _const_spec((1, MLA_Q_RANK)), _const_spec((1, MLA_KV_RANK)),
                  tab(512), tab(512), tab(1024), tab(1024), tab(128)],
        out_specs=[tok(n) for n in widths],
        out_shape=[jax.ShapeDtypeStruct((bsz, s, n), BF16) for n in widths],
        compiler_params=_cparams(2),
    )(x, sc, sh, w, wuq2, wk2, wv, qg.reshape(1, -1), kvg.reshape(1, -1), dcos, dsin, qa, qb, pe)


def _half_masks():
    lane = lax.broadcasted_iota(jnp.int32, (1, LANES), 1)
    return lane < HEAD_DIM, lane >= HEAD_DIM


def _scores(qm, ks):
    ss = [_dot_nt(qm, k) for k in ks]
    m = functools.reduce(jnp.maximum, [jnp.max(s, axis=-1, keepdims=True) for s in ss])
    es = [jnp.exp(s - m) for s in ss]
    l = functools.reduce(lambda a, b: a + b, [jnp.sum(e, axis=-1, keepdims=True) for e in es])
    return es, l


def _attend(qm, ks, vs):
    es, l = _scores(qm, ks)
    o = functools.reduce(lambda a, b: a + b, [_dot(e.astype(BF16), v) for e, v in zip(es, vs)])
    return o * (1.0 / l)


NA_RB = 4
NA_WIN_ROWS = NA_RB + NA_WIN_H - 1


def _na_block_plan(rows):
    plan = []
    for blk in range(rows // NA_RB):
        r0 = blk * NA_RB
        rs = [int(np.clip(r0 + i - NA_WIN_H // 2, 0, rows - NA_WIN_H)) for i in range(NA_RB)]
        ws = int(np.clip(r0 - NA_WIN_H // 2, 0, rows - NA_WIN_ROWS))
        pat = tuple((rs[i] - ws, r0 + i - ws) for i in range(NA_RB))
        assert all(0 <= o and o + NA_WIN_H <= NA_WIN_ROWS for o, _ in pat)
        plan.append(pat)
    assert all(p == plan[1] for p in plan[1:-1])
    return (plan[0], plan[1], plan[-1])


def _na_bias_table(rpb, rows):
    cols = np.arange(GRID_W)
    col_start = np.clip(cols - NA_WIN_W // 2, 0, GRID_W - NA_WIN_W)
    col_mask = (cols[None, :] >= col_start[:, None]) & (cols[None, :] < col_start[:, None] + NA_WIN_W)
    col_idx = np.clip(cols[None, :] - cols[:, None] + NA_WIN_W - 1, 0, 2 * NA_WIN_W - 2)
    wr = np.arange(NA_WIN_ROWS)
    tables = []
    for pat in _na_block_plan(rows):
        off = np.array([o for o, _ in pat])[:, None]
        rq = np.array([r for _, r in pat])[:, None]
        row_ok = (wr[None, :] >= off) & (wr[None, :] < off + NA_WIN_H)
        ridx = np.clip(wr[None, :] - rq + NA_WIN_H - 1, 0, 2 * NA_WIN_H - 2)
        t = rpb.astype(F32)[:, ridx]
        t = t[..., col_idx]
        t = t.transpose(0, 1, 3, 2, 4)
        ok = row_ok[None, :, None, :, None] & col_mask[None, None, :, None, :]
        t = jnp.where(ok, t, NEG_INF)
        tables.append(t.reshape(NA_HEADS, NA_RB * GRID_W, NA_WIN_ROWS * GRID_W))
    return jnp.stack(tables)


def _na_kernel(rows, q_ref, k_ref, v_ref, kc_ref, vc_ref, bt_ref, o_ref):
    r0 = pl.program_id(1) * NA_RB
    ws = jnp.clip(r0 - NA_WIN_H // 2, 0, rows - NA_WIN_ROWS)
    win = pl.ds(pl.multiple_of(ws * GRID_W, GRID_W), NA_WIN_ROWS * GRID_W)
    m0, m1 = _half_masks()
    for j in range(NA_HEADS // 2):
        sl = slice(j * LANES, (j + 1) * LANES)
        qp = q_ref[0, :, sl]
        kp, vp = k_ref[0, win, sl], v_ref[0, win, sl]
        kcp, vcp = kc_ref[0, :, sl], vc_ref[0, :, sl]
        outs = []
        for par, msk in ((0, m0), (1, m1)):
            qm = jnp.where(msk, qp, jnp.zeros_like(qp))
            s_loc = _dot_nt(qm, kp) + bt_ref[0, 2 * j + par]
            s_ctx = _dot_nt(qm, kcp)
            m = jnp.maximum(jnp.max(s_loc, axis=-1, keepdims=True), jnp.max(s_ctx, axis=-1, keepdims=True))
            e_loc, e_ctx = jnp.exp(s_loc - m), jnp.exp(s_ctx - m)
            l = jnp.sum(e_loc, axis=-1, keepdims=True) + jnp.sum(e_ctx, axis=-1, keepdims=True)
            o = _dot(e_loc.astype(BF16), vp) + _dot(e_ctx.astype(BF16), vcp)
            outs.append(o * (1.0 / l))
        o_ref[0, :, sl] = jnp.where(m0, outs[0], outs[1]).astype(BF16)


def _na_attention(q, k, v, kc, vc, rpb):
    bsz, s, w = q.shape
    l = kc.shape[1]
    rows = s // GRID_W
    nblk = rows // NA_RB
    assert rows % NA_RB == 0 and rows >= NA_WIN_ROWS and nblk >= 3
    bt = _na_bias_table(rpb, rows)
    tq = NA_RB * GRID_W
    full = lambda n: pl.BlockSpec((1, n, w), lambda b, r: (b, 0, 0))

    def bt_map(b, r):
        return ((r > 0).astype(jnp.int32) + (r == nblk - 1).astype(jnp.int32), 0, 0, 0)

    return pl.pallas_call(
        functools.partial(_na_kernel, rows),
        grid=(bsz, nblk),
        in_specs=[pl.BlockSpec((1, tq, w), lambda b, r: (b, r, 0)),
                  full(s), full(s), full(l), full(l),
                  pl.BlockSpec((1,) + bt.shape[1:], bt_map)],
        out_specs=pl.BlockSpec((1, tq, w), lambda b, r: (b, r, 0)),
        out_shape=jax.ShapeDtypeStruct((bsz, s, w), BF16),
        compiler_params=_cparams(2),
    )(q, k, v, kc, vc, bt)


def _slot_attn_kernel(n_heads, q_slot, has_lat, *refs):
    if has_lat:
        q_ref, k_ref, v_ref, kc_ref, vc_ref, o_ref = refs
    else:
        q_ref, kc_ref, vc_ref, o_ref = refs
    m0, m1 = _half_masks()
    for j in range(n_heads // 2):
        vsl = slice(j * LANES, (j + 1) * LANES)
        outs = []
        for par, msk in ((0, m0), (1, m1)):
            h = 2 * j + par
            if q_slot:
                ksl = slice(h * LANES, (h + 1) * LANES)
                qm = q_ref[0, :, ksl]
            else:
                ksl = vsl
                qp = q_ref[0, :, vsl]
                qm = jnp.where(msk, qp, jnp.zeros_like(qp))
            ks, vs = [kc_ref[0, :, ksl]], [vc_ref[0, :, vsl]]
            if has_lat:
                ks.insert(0, k_ref[0, :, ksl])
                vs.insert(0, v_ref[0, :, vsl])
            outs.append(_attend(qm, ks, vs))
        o_ref[0, :, vsl] = jnp.where(m0, outs[0], outs[1]).astype(BF16)


def _slot_attention(q, k, v, kc, vc, n_heads, q_slot, tq):
    bsz, sq, wq = q.shape
    l, wk, wv = kc.shape[1], kc.shape[2], vc.shape[2]
    has_lat = k is not None
    qspec = pl.BlockSpec((1, tq, wq), lambda b, i: (b, i, 0))
    full = lambda n, w: pl.BlockSpec((1, n, w), lambda b, i: (b, 0, 0))
    in_specs, args = [qspec], [q]
    if has_lat:
        s = k.shape[1]
        in_specs += [full(s, wk), full(s, wv)]
        args += [k, v]
    in_specs += [full(l, wk), full(l, wv)]
    args += [kc, vc]
    return pl.pallas_call(
        functools.partial(_slot_attn_kernel, n_heads, q_slot, has_lat),
        grid=(bsz, sq // tq),
        in_specs=in_specs,
        out_specs=pl.BlockSpec((1, tq, wv), lambda b, i: (b, i, 0)),
        out_shape=jax.ShapeDtypeStruct((bsz, sq, wv), BF16),
        compiler_params=_cparams(2),
    )(*args)


def _gqa_kernel(has_lat, tq, *refs):
    if has_lat:
        q_ref, k_ref, v_ref, kc_ref, vc_ref, o_ref = refs
    else:
        q_ref, kc_ref, vc_ref, o_ref = refs
    masks = _half_masks()
    res = {}
    for g in range(GQA_KV_HEADS):
        for var in range(2):
            par = g if var == 0 else 1 - g
            heads = (4 * g + par, 4 * g + 2 + par)
            vsl = slice(var * LANES, (var + 1) * LANES)
            qs = []
            for h in heads:
                qp = q_ref[0, :, (h // 2) * LANES:(h // 2 + 1) * LANES]
                qs.append(jnp.where(masks[par], qp, jnp.zeros_like(qp)))
            qm = jnp.concatenate(qs, axis=0)
            ks, vs = [kc_ref[0, :, vsl]], [vc_ref[0, :, vsl]]
            if has_lat:
                ks.insert(0, k_ref[0, :, vsl])
                vs.insert(0, v_ref[0, :, vsl])
            o = _attend(qm, ks, vs)
            res[heads[0]] = o[:tq]
            res[heads[1]] = o[tq:]
    for j in range(GQA_HEADS // 2):
        o_ref[0, :, j * LANES:(j + 1) * LANES] = jnp.where(
            masks[0], res[2 * j], res[2 * j + 1]).astype(BF16)


def _gqa_attention(q, k2, v2, k2c, v2c, tq):
    bsz, sq, wq = q.shape
    l = k2c.shape[1]
    has_lat = k2 is not None
    full = lambda n: pl.BlockSpec((1, n, 2 * LANES), lambda b, i: (b, 0, 0))
    in_specs, args = [pl.BlockSpec((1, tq, wq), lambda b, i: (b, i, 0))], [q]
    if has_lat:
        in_specs += [full(k2.shape[1])] * 2
        args += [k2, v2]
    in_specs += [full(l)] * 2
    args += [k2c, v2c]
    return pl.pallas_call(
        functools.partial(_gqa_kernel, has_lat, tq),
        grid=(bsz, sq // tq),
        in_specs=in_specs,
        out_specs=pl.BlockSpec((1, tq, wq), lambda b, i: (b, i, 0)),
        out_shape=jax.ShapeDtypeStruct((bsz, sq, wq), BF16),
        compiler_params=_cparams(2),
    )(*args)


def _diff_kernel(has_lat, tq, lam_init, *refs):
    if has_lat:
        q_ref, k_ref, v_ref, kc_ref, vc_ref, lq1, lk1, lq2, lk2, sub_ref, o_ref = refs
    else:
        q_ref, kc_ref, vc_ref, lq1, lk1, lq2, lk2, sub_ref, o_ref = refs
    lam = (jnp.exp(jnp.sum(lq1[...] * lk1[...], axis=-1, keepdims=True))
           - jnp.exp(jnp.sum(lq2[...] * lk2[...], axis=-1, keepdims=True)) + lam_init)
    m0, m1 = _half_masks()
    for h in range(DIFF_HEADS):
        sl = slice(h * LANES, (h + 1) * LANES)
        qp = q_ref[0, :, sl]
        zero = jnp.zeros_like(qp)
        qm = jnp.concatenate([jnp.where(m0, qp, zero), jnp.where(m1, qp, zero)], axis=0)
        ks, vs = [kc_ref[0, :, sl]], [vc_ref[0, :, sl]]
        if has_lat:
            ks.insert(0, k_ref[0, :, sl])
            vs.insert(0, v_ref[0, :, sl])
        es, l = _scores(qm, ks)
        inv = 1.0 / l
        c1, c2 = inv[:tq], lam * inv[tq:]
        o = None
        for e, v in zip(es, vs):
            p = (e[:tq] * c1 - e[tq:] * c2).astype(BF16)
            t = _dot(p, v)
            o = t if o is None else o + t
        o = o * lax.rsqrt(jnp.mean(o * o, axis=-1, keepdims=True) + RMS_EPS) * sub_ref[...]
        o_ref[0, :, sl] = (o * (1.0 - lam_init)).astype(BF16)


def _diff_attention(q, k, v, kc, vc, lams, subln, lam_init, tq):
    bsz, sq, w = q.shape
    l = kc.shape[1]
    has_lat = k is not None
    full = lambda n: pl.BlockSpec((1, n, w), lambda b, i: (b, 0, 0))
    in_specs, args = [pl.BlockSpec((1, tq, w), lambda b, i: (b, i, 0))], [q]
    if has_lat:
        in_specs += [full(k.shape[1])] * 2
        args += [k, v]
    in_specs += [full(l)] * 2 + [_const_spec((1, HEAD_DIM))] * 4 + [_const_spec((1, LANES))]
    args += [kc, vc] + [a.reshape(1, -1).astype(F32) for a in lams] + [subln.reshape(1, -1).astype(F32)]
    return pl.pallas_call(
        functools.partial(_diff_kernel, has_lat, tq, lam_init),
        grid=(bsz, sq // tq),
        in_specs=in_specs,
        out_specs=pl.BlockSpec((1, tq, w), lambda b, i: (b, i, 0)),
        out_shape=jax.ShapeDtypeStruct((bsz, sq, w), BF16),
        compiler_params=_cparams(2),
    )(*args)


def _pack_bf16_pairs(v):
    w = v.shape[1] // 2
    hi = pltpu.bitcast(v[:, :w].astype(F32), jnp.int32)
    lo = pltpu.bitcast(v[:, w:].astype(F32), jnp.int32)
    return hi | lax.shift_right_logical(lo, 16)


def _unpack_bf16_pairs(u):
    hi = pltpu.bitcast(u & jnp.int32(-65536), F32)
    lo = pltpu.bitcast(lax.shift_left(u, 16), F32)
    return hi, lo


def _layer_norm(z, g, b):
    mu = jnp.mean(z, axis=-1, keepdims=True)
    zc = z - mu
    var = jnp.mean(zc * zc, axis=-1, keepdims=True)
    return zc * lax.rsqrt(var + LN_EPS) * g + b


def _outproj_kernel(alpha, o1_ref, o2_ref, w1_ref, w2_ref, x_ref, g1_ref, lng_ref, lnb_ref,
                    sc2_ref, sh2_ref, rwh_ref, rwl_ref, rb_ref, tri_ref, cnt0_ref,
                    xo_ref, h2a_ref, h2b_ref, idx_ref, rank_ref, gw_ref, cnt_ref, carry_ref):
    o = _dot(o1_ref[0], w1_ref[...]) + _dot(o2_ref[0], w2_ref[...])
    xn = _layer_norm(alpha * x_ref[0] + g1_ref[0] * o, lng_ref[...], lnb_ref[...])
    xo_ref[0] = xn
    h2 = xn * (1.0 + sc2_ref[0]) + sh2_ref[0]
    hi = h2.astype(BF16)
    packed = _pack_bf16_pairs(hi)
    q = packed.shape[1] // 2
    h2a_ref[0] = packed[:, :q]
    h2b_ref[0] = packed[:, q:]
    lo = (h2 - hi.astype(F32)).astype(BF16)
    logits = (_dot(hi, rwh_ref[...]) + _dot(lo, rwh_ref[...]) + _dot(hi, rwl_ref[...])) + rb_ref[...]
    lane = lax.broadcasted_iota(jnp.int32, logits.shape, 1).astype(F32)
    vals, idxs = [], []
    cur = logits
    for _ in range(TOP_K):
        m = jnp.max(cur, axis=-1, keepdims=True)
        ik = jnp.min(jnp.where(cur == m, lane, float(LANES)), axis=-1, keepdims=True)
        vals.append(m)
        idxs.append(ik)
        cur = jnp.where(lane == ik, -jnp.inf, cur)
    ws = [jnp.exp(v - vals[0]) for v in vals]
    inv = 1.0 / functools.reduce(lambda a, b: a + b, ws)

    @pl.when(jnp.logical_and(pl.program_id(0) == 0, pl.program_id(1) == 0))
    def _():
        carry_ref[...] = cnt0_ref[...]

    sel = [lane == ik for ik in idxs]
    onehot = functools.reduce(lambda a, b: a + b, [m.astype(F32) for m in sel])
    before = _dot(tri_ref[...], onehot.astype(BF16)) + carry_ref[...]
    idx_out = jnp.zeros_like(logits)
    rank_out = jnp.zeros_like(logits)
    w_out = jnp.zeros_like(logits)
    for k in range(TOP_K):
        rk = jnp.sum(jnp.where(sel[k], before, 0.0), axis=-1, keepdims=True)
        idx_out = jnp.where(lane == float(k), idxs[k], idx_out)
        rank_out = jnp.where(lane == float(k), rk, rank_out)
        w_out = jnp.where(lane == float(k), ws[k] * inv, w_out)
    idx_ref[0] = idx_out.astype(jnp.int32)
    rank_ref[0] = rank_out.astype(jnp.int32)
    gw_ref[0] = w_out
    carry_ref[...] += jnp.sum(onehot, axis=0, keepdims=True)
    cnt_ref[...] = carry_ref[...]


def _outproj_ln_router(o1, o2, w_out, x, g1, lng, lnb, sc2, sh2, router, cnt0, alpha, ts):
    bsz, s, d = x.shape
    batched = g1.shape[0] > 1
    rwh, rwl, rb = router
    w1, w2 = w_out[:512].astype(BF16), w_out[512:].astype(BF16)
    tri = jnp.asarray(np.tril(np.ones((ts, ts), np.float32), -1), dtype=BF16)
    tok = lambda n: pl.BlockSpec((1, ts, n), lambda b, i: (b, i, 0))
    ms = _mod_spec(d, batched)
    return pl.pallas_call(
        functools.partial(_outproj_kernel, alpha),
        grid=(bsz, s // ts),
        in_specs=[tok(512), tok(512), _const_spec((512, d)), _const_spec((512, d)), tok(d), ms,
                  _const_spec((1, d)), _const_spec((1, d)), ms, ms,
                  _const_spec((d, LANES)), _const_spec((d, LANES)), _const_spec((1, LANES)),
                  _const_spec((ts, ts)), _const_spec((1, LANES))],
        out_specs=[tok(d), tok(d // 4), tok(d // 4), tok(LANES), tok(LANES), tok(LANES),
                   _const_spec((1, LANES))],
        out_shape=[jax.ShapeDtypeStruct((bsz, s, d), F32),
                   jax.ShapeDtypeStruct((bsz, s, d // 4), jnp.int32),
                   jax.ShapeDtypeStruct((bsz, s, d // 4), jnp.int32),
                   jax.ShapeDtypeStruct((bsz, s, LANES), jnp.int32),
                   jax.ShapeDtypeStruct((bsz, s, LANES), jnp.int32),
                   jax.ShapeDtypeStruct((bsz, s, LANES), F32),
                   jax.ShapeDtypeStruct((1, LANES), F32)],
        scratch_shapes=[pltpu.VMEM((1, LANES), F32)],
        compiler_params=_cparams(2),
    )(o1, o2, w1, w2, x, g1, lng.reshape(1, d), lnb.reshape(1, d), sc2, sh2, rwh, rwl, rb, tri, cnt0)


def _router_weights(router_w, router_b):
    d, e = router_w.shape
    wp = jnp.zeros((d, LANES), F32).at[:, :e].set(router_w)
    hi = wp.astype(BF16)
    lo = (wp - hi.astype(F32)).astype(BF16)
    rb = jnp.full((1, LANES), -jnp.inf, F32).at[0, :e].set(router_b)
    return hi, lo, rb


def _deinterleave_perm():
    p = np.zeros((GU_BLOCK, GU_BLOCK), np.float32)
    m = np.arange(GU_BLOCK // 2)
    p[2 * m, m] = 1.0
    p[2 * m + 1, GU_BLOCK // 2 + m] = 1.0
    return jnp.asarray(p, dtype=BF16)


def _ffn_kernel(te_ref, tv_ref, xa_ref, xb_ref, wgu_ref, bgu_ref, wd_ref, bd_ref, perm_ref, y_ref,
                wgu_s, wd_s):
    j = pl.program_id(0)
    n_blk = wgu_s.shape[1] // GU_BLOCK
    half = GU_BLOCK // 2

    @pl.when(jnp.logical_or(j == 0, te_ref[j] != te_ref[jnp.maximum(j - 1, 0)]))
    def _():
        for b in range(n_blk):
            sl = slice(b * GU_BLOCK, (b + 1) * GU_BLOCK)
            wgu_s[:, sl] = _dot(wgu_ref[0, 0, :, sl].astype(BF16), perm_ref[...]).astype(BF16)
        wd_s[...] = wd_ref[0, 0].astype(BF16)

    @pl.when(tv_ref[j] > 0)
    def _():
        a_hi, a_lo = _unpack_bf16_pairs(xa_ref[...])
        b_hi, b_lo = _unpack_bf16_pairs(xb_ref[...])
        x = jnp.concatenate([a_hi, b_hi, a_lo, b_lo], axis=-1).astype(BF16)
        acts = []
        for b in range(n_blk):
            sl = slice(b * GU_BLOCK, (b + 1) * GU_BLOCK)
            gu = _dot(x, wgu_s[:, sl]) + bgu_ref[0, :, sl]
            glu = jnp.minimum(gu[:, :half], SWIGLU_LIMIT)
            lin = jnp.clip(gu[:, half:], -SWIGLU_LIMIT, SWIGLU_LIMIT)
            acts.append(((lin + 1.0) * (glu * jax.nn.sigmoid(SWIGLU_ALPHA * glu))).astype(BF16))
        a = jnp.concatenate(acts, axis=-1)
        y_ref[...] = (_dot(a, wd_s[...]) + bd_ref[0]).astype(BF16)

    @pl.when(tv_ref[j] == 0)
    def _():
        y_ref[...] = jnp.zeros_like(y_ref)


def _expert_ffn(xsa, xsb, tile_expert, tile_valid, layer, w_gu, b_gu, w_down, b_down, tm):
    p, q = xsa.shape
    d = 4 * q
    _, e, _, f2 = w_gu.shape
    f = f2 // 2
    half = GU_BLOCK // 2
    bgu = jnp.stack([b_gu[:, 0::2].reshape(e, f // half, half),
                     b_gu[:, 1::2].reshape(e, f // half, half)], axis=2).reshape(e, 1, f2)
    wspec = lambda a, b: pl.BlockSpec((1, 1, a, b), lambda j, te, tv: (layer, te[j], 0, 0))
    bspec = lambda b: pl.BlockSpec((1, 1, b), lambda j, te, tv: (te[j], 0, 0))
    return pl.pallas_call(
        _ffn_kernel,
        grid_spec=pltpu.PrefetchScalarGridSpec(
            num_scalar_prefetch=2,
            grid=(p // tm,),
            in_specs=[pl.BlockSpec((tm, q), lambda j, te, tv: (j, 0)),
                      pl.BlockSpec((tm, q), lambda j, te, tv: (j, 0)),
                      wspec(d, f2), bspec(f2), wspec(f, d), bspec(d),
                      pl.BlockSpec((GU_BLOCK, GU_BLOCK), lambda j, te, tv: (0, 0))],
            out_specs=pl.BlockSpec((tm, d), lambda j, te, tv: (j, 0)),
            scratch_shapes=[pltpu.VMEM((d, f2), BF16), pltpu.VMEM((f, d), BF16)]),
        out_shape=jax.ShapeDtypeStruct((p, d), BF16),
        compiler_params=_cparams(1),
    )(tile_expert, tile_valid, xsa, xsb, w_gu, bgu, w_down, b_down.reshape(e, 1, d),
      _deinterleave_perm())


def _moe_plan(idx4, rank4, counts, tm):
    t = idx4.shape[0]
    n_experts = counts.shape[0]
    pc = ((counts + tm - 1) // tm) * tm
    pend = jnp.cumsum(pc)
    pstart = pend - pc
    dpos_t = (pstart[idx4] + rank4).T.astype(jnp.int32)
    p = t * TOP_K + n_experts * tm
    tile_start = jnp.arange(p // tm, dtype=jnp.int32) * tm
    tile_expert = jnp.minimum(jnp.sum((tile_start[:, None] >= pend[None, :]).astype(jnp.int32), axis=1),
                              n_experts - 1).astype(jnp.int32)
    tile_valid = (tile_start < pend[-1]).astype(jnp.int32)
    return dpos_t, p, tile_expert, tile_valid


SC_WINDOW = 128


def _sc_dispatch(rows, dpos_t, p):
    t, w = rows.shape
    k = dpos_t.shape[0]
    mesh = plsc.VectorSubcoreMesh(core_axis_name="core", subcore_axis_name="subcore")

    @functools.partial(pl.kernel, out_type=jax.ShapeDtypeStruct((p, w), rows.dtype), mesh=mesh)
    def kern(x_hbm, *refs):
        i_hbms, o_hbm = refs[:k], refs[k]

        def body(x_vmem, *i_vmems):
            for iv in i_vmems:
                pltpu.sync_copy(x_vmem, o_hbm.at[iv.at[0]])

        pltpu.emit_pipeline(
            body,
            grid=(t // SC_WINDOW,),
            in_specs=[pl.BlockSpec((SC_WINDOW, w), index_map=lambda i: (i, 0))]
            + [pl.BlockSpec((1, SC_WINDOW), index_map=lambda i: (0, i)) for _ in range(k)],
            out_specs=[],
            core_axis_name=("core", "subcore"),
            dimension_semantics=(pltpu.PARALLEL,),
        )(x_hbm, *i_hbms)

    return kern(rows, *[dpos_t[kk:kk + 1] for kk in range(k)])


def _combine_kernel(alpha, yg_ref, gw_ref, x_ref, g2_ref, lng_ref, lnb_ref, o_ref):
    gw = gw_ref[...]
    y = yg_ref[0].astype(F32) * gw[:, 0:1]
    for k in range(1, TOP_K):
        y = y + yg_ref[k].astype(F32) * gw[:, k:k + 1]
    o_ref[0] = _layer_norm(alpha * x_ref[0] + g2_ref[0] * y, lng_ref[...], lnb_ref[...])


def _combine_ln(yg, gw, x, g2, lng, lnb, alpha, row_offset, ts):
    bsz, s, d = x.shape
    batched = g2.shape[0] > 1
    nblk = s // ts
    off = row_offset // ts
    return pl.pallas_call(
        functools.partial(_combine_kernel, alpha),
        grid=(bsz, nblk),
        in_specs=[pl.BlockSpec((TOP_K, ts, d), lambda b, i: (0, off + b * nblk + i, 0)),
                  pl.BlockSpec((ts, LANES), lambda b, i: (off + b * nblk + i, 0)),
                  pl.BlockSpec((1, ts, d), lambda b, i: (b, i, 0)),
                  _mod_spec(d, batched), _const_spec((1, d)), _const_spec((1, d))],
        out_specs=pl.BlockSpec((1, ts, d), lambda b, i: (b, i, 0)),
        out_shape=jax.ShapeDtypeStruct((bsz, s, d), F32),
        compiler_params=_cparams(2),
    )(yg, gw, x, g2, lng.reshape(1, d), lnb.reshape(1, d))


def kernel(x, c, ctx, c_ctx, mod_w, mod_b, ln1_g, ln1_b, ln2_g, ln2_b, even_w_in, even_w_out, na_rpb, gqa_q_gain, gqa_k_gain, odd_w_in, odd_w_out, diff_lq1, diff_lk1, diff_lq2, diff_lk2, diff_subln, mla_q_gain, mla_w_uq, mla_kv_gain, mla_w_ukv, router_w, router_b, exp_w_gu, exp_b_gu, exp_w_down, exp_b_down):
    bsz, s, d = x.shape
    l = ctx.shape[1]
    depth = mod_w.shape[0]
    n_experts = router_w.shape[-1]
    alpha = (2 * depth) ** 0.25
    ts = min(512, s)
    tm = MOE_TILE

    pad = (-(bsz + 1)) % 8
    c_all = jnp.concatenate([c, c_ctx[None, :], jnp.zeros((pad, d), F32)], axis=0)
    mod = _modulation_all(c_all, mod_w, mod_b)

    for i in range(depth):
        last = i == depth - 1
        j = i // 2
        ml = [mod[i, :bsz, k * d:(k + 1) * d].reshape(bsz, 1, d) for k in range(6)]
        mc = [mod[i, bsz:bsz + 1, k * d:(k + 1) * d].reshape(1, 1, d) for k in range(6)]
        sh1, sc1, g1, sh2, sc2, g2 = ml
        csh1, csc1, cg1, csh2, csc2, cg2 = mc

        if i % 2 == 0:
            w = _even_weights(even_w_in[j])
            naq, nak, nav, gq, gk, gv = _inproj_even(
                x, sc1, sh1, w, _even_tables(s, gqa_q_gain[j], gqa_k_gain[j], True), ts)
            cnaq, cnak, cnav, cgq, cgk, cgv = _inproj_even(
                ctx, csc1, csh1, w, _even_tables(l, gqa_q_gain[j], gqa_k_gain[j], False), l)
            o1 = _na_attention(naq, nak, nav, cnak, cnav, na_rpb[j])
            o2 = _gqa_attention(gq, gk, gv, cgk, cgv, min(256, s))
            if not last:
                co1 = _slot_attention(cnaq, None, None, cnak, cnav, NA_HEADS, False, l)
                co2 = _gqa_attention(cgq, None, None, cgk, cgv, l)
            w_out = even_w_out[j]
        else:
            lam_init = 0.8 - 0.6 * math.exp(-0.3 * i)
            weights = _odd_weights(odd_w_in[j], mla_w_uq[j], mla_w_ukv[j])
            gains = (mla_q_gain[j], mla_kv_gain[j])
            dq, dk, dv, mq, mk, mv = _inproj_odd(x, sc1, sh1, weights, gains, _odd_tables(s, True), ts)
            cdq, cdk, cdv, cmq, cmk, cmv = _inproj_odd(ctx, csc1, csh1, weights, gains,
                                                       _odd_tables(l, False), l)
            lams = (diff_lq1[j], diff_lk1[j], diff_lq2[j], diff_lk2[j])
            o1 = _diff_attention(dq, dk, dv, cdk, cdv, lams, diff_subln[j], lam_init, min(256, s))
            o2 = _slot_attention(mq, mk, mv, cmk, cmv, MLA_HEADS, True, min(512, s))
            if not last:
                co1 = _diff_attention(cdq, None, None, cdk, cdv, lams, diff_subln[j], lam_init, l)
                co2 = _slot_attention(cmq, None, None, cmk, cmv, MLA_HEADS, True, l)
            w_out = odd_w_out[j]

        router = _router_weights(router_w[i], router_b[i])
        cnt0 = jnp.zeros((1, LANES), F32)
        x, h2a, h2b, ridx, rrank, rgw, cnt = _outproj_ln_router(
            o1, o2, w_out, x, g1, ln1_g[i], ln1_b[i], sc2, sh2, router, cnt0, alpha, ts)
        flat = lambda a, n: a.reshape(bsz * n, a.shape[-1])
        h2a, h2b, ridx, rrank, rgw = [flat(a, s) for a in (h2a, h2b, ridx, rrank, rgw)]
        if not last:
            ctx, ch2a, ch2b, cidx, crank, cgw, cnt = _outproj_ln_router(
                co1, co2, w_out, ctx, cg1, ln1_g[i], ln1_b[i], csc2, csh2, router, cnt, alpha, l)
            cat = lambda a, b: jnp.concatenate([a, flat(b, l)], axis=0)
            h2a, h2b = cat(h2a, ch2a), cat(h2b, ch2b)
            ridx, rrank, rgw = cat(ridx, cidx), cat(rrank, crank), cat(rgw, cgw)

        counts = cnt[0, :n_experts].astype(jnp.int32)
        dpos_t, p, tile_expert, tile_valid = _moe_plan(ridx[:, :TOP_K], rrank[:, :TOP_K], counts, tm)
        xsa = _sc_dispatch(h2a, dpos_t, p)
        xsb = _sc_dispatch(h2b, dpos_t, p)
        ys = _expert_ffn(xsa, xsb, tile_expert, tile_valid, i, exp_w_gu, exp_b_gu[i],
                         exp_w_down, exp_b_down[i], tm)
        yg = ys.at[dpos_t].get(mode="promise_in_bounds")

        x = _combine_ln(yg, rgw, x, g2, ln2_g[i], ln2_b[i], alpha, 0, ts)
        if not last:
            ctx = _combine_ln(yg, rgw, ctx, cg2, ln2_g[i], ln2_b[i], alpha, bsz * s, l)
    return x
```

```python
import functools
import math

import numpy as np
import jax
import jax.numpy as jnp
from jax import lax
from jax.experimental import pallas as pl
from jax.experimental.pallas import tpu as pltpu
from jax.experimental.pallas import tpu_sc as plsc

F32 = jnp.float32
BF16 = jnp.bfloat16

GRID_W = 64
HEAD_DIM = 64
ROPE_THETA = 10000.0
LN_EPS = 1e-6
RMS_EPS = 1e-6
NEG_INF = -1e30
NA_HEADS = 8
NA_WIN_H = 8
NA_WIN_W = 16
GQA_HEADS = 8
GQA_KV_HEADS = 2
DIFF_HEADS = 4
MLA_HEADS = 8
MLA_Q_RANK = 256
MLA_KV_RANK = 128
MLA_NOPE = 64
MLA_ROPE = 32
MLA_V = 64
MLA_QK = MLA_NOPE + MLA_ROPE
TOP_K = 4
SWIGLU_ALPHA = 1.702
SWIGLU_LIMIT = 7.0
LOG2E = 1.4426950408889634

LANES = 128
VMEM_LIMIT = 56 * 1024 * 1024
MOE_TILE = 1024
GU_BLOCK = 512


def _cparams(n_axes):
    return pltpu.CompilerParams(dimension_semantics=("arbitrary",) * n_axes,
                                vmem_limit_bytes=VMEM_LIMIT)


def _dot(a, b):
    return jnp.dot(a, b, preferred_element_type=F32)


def _dot_nt(a, b):
    return lax.dot_general(a, b, (((1,), (1,)), ((), ())), preferred_element_type=F32)


def _mod_kernel(c_ref, w_ref, b_ref, o_ref):
    cv = c_ref[...]
    a = (cv * jax.nn.sigmoid(cv)).astype(BF16)
    o_ref[0] = _dot(a, w_ref[0].astype(BF16)) + b_ref[0]


def _modulation_all(c_all, mod_w, mod_b):
    depth, d, n = mod_w.shape
    rows = c_all.shape[0]
    tn = 1536
    return pl.pallas_call(
        _mod_kernel,
        grid=(depth, n // tn),
        in_specs=[pl.BlockSpec((rows, d), lambda i, j: (0, 0)),
                  pl.BlockSpec((1, d, tn), lambda i, j: (i, 0, j)),
                  pl.BlockSpec((1, 1, tn), lambda i, j: (i, 0, j))],
        out_specs=pl.BlockSpec((1, rows, tn), lambda i, j: (i, 0, j)),
        out_shape=jax.ShapeDtypeStruct((depth, rows, n), F32),
        compiler_params=_cparams(2),
    )(c_all, mod_w, mod_b.reshape(depth, 1, n))


def _rope_tables(s, dim):
    pos = jnp.arange(s)
    row = (pos // GRID_W).astype(F32)[:, None]
    col = (pos % GRID_W).astype(F32)[:, None]
    quarter = dim // 4
    inv_freq = ROPE_THETA ** (-jnp.arange(quarter, dtype=F32) / quarter)
    ar, ac = row * inv_freq, col * inv_freq
    cos = jnp.concatenate([jnp.cos(ar), jnp.cos(ar), jnp.cos(ac), jnp.cos(ac)], axis=-1)
    sin = jnp.concatenate([-jnp.sin(ar), jnp.sin(ar), -jnp.sin(ac), jnp.sin(ac)], axis=-1)
    return cos, sin


def _swap_perm(dim):
    q = dim // 4
    idx = np.arange(dim)
    return np.where((idx % (2 * q)) < q, idx + q, idx - q)


def _swap_cols(w, dim):
    n = w.shape[-1] // dim
    perm = (np.arange(n)[:, None] * dim + _swap_perm(dim)[None, :]).reshape(-1)
    return w[..., perm]


def _block_ones(n, blk):
    i = np.arange(n) // blk
    return jnp.asarray((i[:, None] == i[None, :]).astype(np.float32), dtype=BF16)


E_NAQ, E_NAK, E_NAV, E_GQ, E_GQS, E_GK, E_GKS, E_GV, E_END = (
    0, 512, 1024, 1536, 2048, 2560, 2816, 3072, 3328)


def _even_weights(w_in):
    naq, nak, nav, gq, gk, gv = jnp.split(w_in, [512, 1024, 1536, 2048, 2176], axis=-1)
    k0, k1 = gk[:, :64], gk[:, 64:]
    v0, v1 = gv[:, :64], gv[:, 64:]
    gk2 = jnp.concatenate([k0, k1, k1, k0], axis=-1)
    gv2 = jnp.concatenate([v0, v1, v1, v0], axis=-1)
    w = jnp.concatenate([naq * (0.125 * LOG2E), nak, nav, gq, _swap_cols(gq, 64),
                         gk2, _swap_cols(gk2, 64), gv2], axis=-1).astype(BF16)
    return w


def _even_tables(s, q_gain, k_gain, rope):
    sw = _swap_perm(64)
    if rope:
        cos, sin = _rope_tables(s, 64)
    else:
        cos, sin = jnp.ones((s, 64), F32), jnp.zeros((s, 64), F32)
    qa = jnp.tile(cos * q_gain[None, :] * (0.125 * LOG2E), (1, 8))
    qb = jnp.tile(sin * q_gain[sw][None, :] * (0.125 * LOG2E), (1, 8))
    ka = jnp.tile(cos * k_gain[None, :], (1, 4))
    kb = jnp.tile(sin * k_gain[sw][None, :], (1, 4))
    return qa, qb, ka, kb


def _inproj_even_kernel(x_ref, sc_ref, sh_ref, w_ref, g512_ref, g256_ref,
                        qa_ref, qb_ref, ka_ref, kb_ref,
                        naq_ref, nak_ref, nav_ref, gq_ref, gk_ref, gv_ref):
    h = (x_ref[0] * (1.0 + sc_ref[0]) + sh_ref[0]).astype(BF16)
    naq_ref[0] = _dot(h, w_ref[:, E_NAQ:E_NAK]).astype(BF16)
    nak_ref[0] = _dot(h, w_ref[:, E_NAK:E_NAV]).astype(BF16)
    nav_ref[0] = _dot(h, w_ref[:, E_NAV:E_GQ]).astype(BF16)
    gv_ref[0] = _dot(h, w_ref[:, E_GV:E_END]).astype(BF16)
    y = _dot(h, w_ref[:, E_GQ:E_GQS])
    ys = _dot(h, w_ref[:, E_GQS:E_GK])
    r = lax.rsqrt(_dot((y * y).astype(BF16), g512_ref[...]) * (1.0 / HEAD_DIM) + RMS_EPS)
    gq_ref[0] = (r * (y * qa_ref[...] + ys * qb_ref[...])).astype(BF16)
    y = _dot(h, w_ref[:, E_GK:E_GKS])
    ys = _dot(h, w_ref[:, E_GKS:E_GV])
    r = lax.rsqrt(_dot((y * y).astype(BF16), g256_ref[...]) * (1.0 / HEAD_DIM) + RMS_EPS)
    gk_ref[0] = (r * (y * ka_ref[...] + ys * kb_ref[...])).astype(BF16)


def _mod_spec(d, batched):
    if batched:
        return pl.BlockSpec((1, 1, d), lambda b, s: (b, 0, 0))
    return pl.BlockSpec((1, 1, d), lambda b, s: (0, 0, 0))


def _const_spec(shape):
    nd = len(shape)
    return pl.BlockSpec(shape, lambda b, s: (0,) * nd)


def _inproj_even(x, sc, sh, w, tables, ts):
    bsz, s, d = x.shape
    batched = sc.shape[0] > 1
    qa, qb, ka, kb = tables
    tok = lambda n: pl.BlockSpec((1, ts, n), lambda b, i: (b, i, 0))
    tab = lambda n: pl.BlockSpec((ts, n), lambda b, i: (i, 0))
    widths = (512, 512, 512, 512, 256, 256)
    return pl.pallas_call(
        _inproj_even_kernel,
        grid=(bsz, s // ts),
        in_specs=[tok(d), _mod_spec(d, batched), _mod_spec(d, batched), _const_spec(w.shape),
                  _const_spec((512, 512)), _const_spec((256, 256)),
                  tab(512), tab(512), tab(256), tab(256)],
        out_specs=[tok(n) for n in widths],
        out_shape=[jax.ShapeDtypeStruct((bsz, s, n), BF16) for n in widths],
        compiler_params=_cparams(2),
    )(x, sc, sh, w, _block_ones(512, 64), _block_ones(256, 64), qa, qb, ka, kb)


O_DQ, O_DQS, O_DK, O_DKS, O_DV, O_CQ, O_CKV, O_PE, O_END = (
    0, 512, 1024, 1536, 2048, 2560, 2816, 2944, 3072)


def _odd_weights(w_in, w_uq, w_ukv):
    dq, dk, dv, cq, ckv, kpe = jnp.split(w_in, [512, 1024, 1536, 1792, 1920], axis=-1)
    d = w_in.shape[0]
    pe_slot = jnp.concatenate([kpe, _swap_cols(kpe, 32), jnp.zeros((d, 64), F32)], axis=-1)
    w = jnp.concatenate([dq, _swap_cols(dq, 64), dk, _swap_cols(dk, 64), dv, cq, ckv, pe_slot],
                        axis=-1).astype(BF16)
    uq = w_uq.reshape(MLA_Q_RANK, MLA_HEADS, MLA_QK)
    z32 = jnp.zeros((MLA_Q_RANK, MLA_HEADS, 32), F32)
    z64 = jnp.zeros((MLA_Q_RANK, MLA_HEADS, 64), F32)
    uq_pad = jnp.concatenate([uq, z32], axis=-1).reshape(MLA_Q_RANK, MLA_HEADS * LANES)
    uq_sw = jnp.concatenate([z64, _swap_cols(uq[..., MLA_NOPE:], 32), z32],
                            axis=-1).reshape(MLA_Q_RANK, MLA_HEADS * LANES)
    wuq2 = jnp.concatenate([uq_pad, uq_sw], axis=-1).astype(BF16)
    ukv = w_ukv.reshape(MLA_KV_RANK, MLA_HEADS, MLA_NOPE + MLA_V)
    zk = jnp.zeros((MLA_KV_RANK, MLA_HEADS, 64), F32)
    wk_pad = jnp.concatenate([ukv[..., :MLA_NOPE], zk], axis=-1).reshape(MLA_KV_RANK, MLA_HEADS * LANES)
    place = np.zeros((LANES, MLA_HEADS, LANES), np.float32)
    for j in range(MLA_ROPE):
        place[j, :, MLA_NOPE + j] = 1.0
        place[MLA_ROPE + j, :, MLA_NOPE + j] = 1.0
    wk2 = jnp.concatenate([wk_pad, jnp.asarray(place.reshape(LANES, MLA_HEADS * LANES))],
                          axis=0).astype(BF16)
    wv = ukv[..., MLA_NOPE:].reshape(MLA_KV_RANK, MLA_HEADS * MLA_V).astype(BF16)
    return w, wuq2, wk2, wv


def _odd_tables(s, rope):
    m_scale = MLA_QK ** -0.5 * LOG2E
    if rope:
        cos64, sin64 = _rope_tables(s, 64)
        cos32, sin32 = _rope_tables(s, 32)
    else:
        cos64, sin64 = jnp.ones((s, 64), F32), jnp.zeros((s, 64), F32)
        cos32, sin32 = jnp.ones((s, 32), F32), jnp.zeros((s, 32), F32)
    one64, z32, z64 = jnp.ones((s, 64), F32), jnp.zeros((s, 32), F32), jnp.zeros((s, 64), F32)
    dcos, dsin = jnp.tile(cos64, (1, 8)), jnp.tile(sin64, (1, 8))
    qa = jnp.tile(jnp.concatenate([one64, cos32, z32], axis=-1) * m_scale, (1, MLA_HEADS))
    qb = jnp.tile(jnp.concatenate([z64, sin32, z32], axis=-1) * m_scale, (1, MLA_HEADS))
    pe = jnp.concatenate([cos32, sin32, z64], axis=-1)
    return dcos, dsin, qa, qb, pe


def _inproj_odd_kernel(x_ref, sc_ref, sh_ref, w_ref, wuq_ref, wk_ref, wv_ref, qg_ref, kvg_ref,
                       dcos_ref, dsin_ref, qa_ref, qb_ref, pe_ref,
                       dq_ref, dk_ref, dv_ref, mq_ref, mk_ref, mv_ref):
    h = (x_ref[0] * (1.0 + sc_ref[0]) + sh_ref[0]).astype(BF16)
    dcos, dsin = dcos_ref[...], dsin_ref[...]
    y = _dot(h, w_ref[:, O_DQ:O_DQS])
    ys = _dot(h, w_ref[:, O_DQS:O_DK])
    dq_ref[0] = ((y * dcos + ys * dsin) * (0.125 * LOG2E)).astype(BF16)
    y = _dot(h, w_ref[:, O_DK:O_DKS])
    ys = _dot(h, w_ref[:, O_DKS:O_DV])
    dk_ref[0] = (y * dcos + ys * dsin).astype(BF16)
    dv_ref[0] = _dot(h, w_ref[:, O_DV:O_CQ]).astype(BF16)
    cq = _dot(h, w_ref[:, O_CQ:O_CKV])
    nq = cq * lax.rsqrt(jnp.mean(cq * cq, axis=-1, keepdims=True) + RMS_EPS) * qg_ref[...]
    y2 = _dot(nq.astype(BF16), wuq_ref[...])
    half = MLA_HEADS * LANES
    mq_ref[0] = (y2[:, :half] * qa_ref[...] + y2[:, half:] * qb_ref[...]).astype(BF16)
    ckv = _dot(h, w_ref[:, O_CKV:O_PE])
    nk = ckv * lax.rsqrt(jnp.mean(ckv * ckv, axis=-1, keepdims=True) + RMS_EPS) * kvg_ref[...]
    pe = _dot(h, w_ref[:, O_PE:O_END]) * pe_ref[...]
    nkb = nk.astype(BF16)
    cat = jnp.concatenate([nkb, pe.astype(BF16)], axis=-1)
    mk_ref[0] = _dot(cat, wk_ref[...]).astype(BF16)
    mv_ref[0] = _dot(nkb, wv_ref[...]).astype(BF16)


def _inproj_odd(x, sc, sh, weights, gains, tables, ts):
    bsz, s, d = x.shape
    batched = sc.shape[0] > 1
    w, wuq2, wk2, wv = weights
    qg, kvg = gains
    dcos, dsin, qa, qb, pe = tables
    tok = lambda n: pl.BlockSpec((1, ts, n), lambda b, i: (b, i, 0))
    tab = lambda n: pl.BlockSpec((ts, n), lambda b, i: (i, 0))
    widths = (512, 512, 512, 1024, 1024, 512)
    return pl.pallas_call(
        _inproj_odd_kernel,
        grid=(bsz, s // ts),
        in_specs=[tok(d), _mod_spec(d, batched), _mod_spec(d, batched), _const_spec(w.shape),
                  _const_spec(wuq2.shape), _const_spec(wk2.shape), _const_spec(wv.shape),
                  _const_spec((1, MLA_Q_RANK)), _const_spec((1, MLA_KV_RANK)),
                  tab(512), tab(512), tab(1024), tab(1024), tab(128)],
        out_specs=[tok(n) for n in widths],
        out_shape=[jax.ShapeDtypeStruct((bsz, s, n), BF16) for n in widths],
        compiler_params=_cparams(2),
    )(x, sc, sh, w, wuq2, wk2, wv, qg.reshape(1, -1), kvg.reshape(1, -1), dcos, dsin, qa, qb, pe)


def _half_masks():
    lane = lax.broadcasted_iota(jnp.int32, (1, LANES), 1)
    return lane < HEAD_DIM, lane >= HEAD_DIM


def _scores(qm, ks):
    ss = [_dot_nt(qm, k) for k in ks]
    m = functools.reduce(jnp.maximum, [jnp.max(s, axis=-1, keepdims=True) for s in ss])
    es = [jnp.exp2(s - m) for s in ss]
    l = functools.reduce(lambda a, b: a + b, [jnp.sum(e, axis=-1, keepdims=True) for e in es])
    return es, l


def _attend(qm, ks, vs):
    es, l = _scores(qm, ks)
    o = functools.reduce(lambda a, b: a + b, [_dot(e.astype(BF16), v) for e, v in zip(es, vs)])
    return o * (1.0 / l)


NA_RB = 4
NA_WIN_ROWS = NA_RB + NA_WIN_H - 1


def _na_block_plan(rows):
    plan = []
    for blk in range(rows // NA_RB):
        r0 = blk * NA_RB
        rs = [int(np.clip(r0 + i - NA_WIN_H // 2, 0, rows - NA_WIN_H)) for i in range(NA_RB)]
        ws = int(np.clip(r0 - NA_WIN_H // 2, 0, rows - NA_WIN_ROWS))
        pat = tuple((rs[i] - ws, r0 + i - ws) for i in range(NA_RB))
        assert all(0 <= o and o + NA_WIN_H <= NA_WIN_ROWS for o, _ in pat)
        plan.append(pat)
    assert all(p == plan[1] for p in plan[1:-1])
    return (plan[0], plan[1], plan[-1])


def _na_bias_table(rpb, rows):
    cols = np.arange(GRID_W)
    col_start = np.clip(cols - NA_WIN_W // 2, 0, GRID_W - NA_WIN_W)
    col_mask = (cols[None, :] >= col_start[:, None]) & (cols[None, :] < col_start[:, None] + NA_WIN_W)
    col_idx = np.clip(cols[None, :] - cols[:, None] + NA_WIN_W - 1, 0, 2 * NA_WIN_W - 2)
    wr = np.arange(NA_WIN_ROWS)
    tables = []
    for pat in _na_block_plan(rows):
        off = np.array([o for o, _ in pat])[:, None]
        rq = np.array([r for _, r in pat])[:, None]
        row_ok = (wr[None, :] >= off) & (wr[None, :] < off + NA_WIN_H)
        ridx = np.clip(wr[None, :] - rq + NA_WIN_H - 1, 0, 2 * NA_WIN_H - 2)
        t = rpb.astype(F32)[:, ridx]
        t = t[..., col_idx]
        t = t.transpose(0, 1, 3, 2, 4)
        ok = row_ok[None, :, None, :, None] & col_mask[None, None, :, None, :]
        t = jnp.where(ok, t * LOG2E, NEG_INF)
        tables.append(t.reshape(NA_HEADS, NA_RB * GRID_W, NA_WIN_ROWS * GRID_W))
    return jnp.stack(tables)


def _na_kernel(rows, q_ref, k_ref, v_ref, kc_ref, vc_ref, bt_ref, o_ref):
    r0 = pl.program_id(1) * NA_RB
    ws = jnp.clip(r0 - NA_WIN_H // 2, 0, rows - NA_WIN_ROWS)
    win = pl.ds(pl.multiple_of(ws * GRID_W, GRID_W), NA_WIN_ROWS * GRID_W)
    m0, m1 = _half_masks()
    for j in range(NA_HEADS // 2):
        sl = slice(j * LANES, (j + 1) * LANES)
        qp = q_ref[0, :, sl]
        kp, vp = k_ref[0, win, sl], v_ref[0, win, sl]
        kcp, vcp = kc_ref[0, :, sl], vc_ref[0, :, sl]
        outs = []
        for par, msk in ((0, m0), (1, m1)):
            qm = jnp.where(msk, qp, jnp.zeros_like(qp))
            s_loc = _dot_nt(qm, kp) + bt_ref[0, 2 * j + par]
            s_ctx = _dot_nt(qm, kcp)
            m = jnp.maximum(jnp.max(s_loc, axis=-1, keepdims=True), jnp.max(s_ctx, axis=-1, keepdims=True))
            e_loc, e_ctx = jnp.exp2(s_loc - m), jnp.exp2(s_ctx - m)
            l = jnp.sum(e_loc, axis=-1, keepdims=True) + jnp.sum(e_ctx, axis=-1, keepdims=True)
            o = _dot(e_loc.astype(BF16), vp) + _dot(e_ctx.astype(BF16), vcp)
            outs.append(o * (1.0 / l))
        o_ref[0, :, sl] = jnp.where(m0, outs[0], outs[1]).astype(BF16)


def _na_attention(q, k, v, kc, vc, rpb):
    bsz, s, w = q.shape
    l = kc.shape[1]
    rows = s // GRID_W
    nblk = rows // NA_RB
    assert rows % NA_RB == 0 and rows >= NA_WIN_ROWS and nblk >= 3
    bt = _na_bias_table(rpb, rows)
    tq = NA_RB * GRID_W
    full = lambda n: pl.BlockSpec((1, n, w), lambda b, r: (b, 0, 0))

    def bt_map(b, r):
        return ((r > 0).astype(jnp.int32) + (r == nblk - 1).astype(jnp.int32), 0, 0, 0)

    return pl.pallas_call(
        functools.partial(_na_kernel, rows),
        grid=(bsz, nblk),
        in_specs=[pl.BlockSpec((1, tq, w), lambda b, r: (b, r, 0)),
                  full(s), full(s), full(l), full(l),
                  pl.BlockSpec((1,) + bt.shape[1:], bt_map)],
        out_specs=pl.BlockSpec((1, tq, w), lambda b, r: (b, r, 0)),
        out_shape=jax.ShapeDtypeStruct((bsz, s, w), BF16),
        compiler_params=_cparams(2),
    )(q, k, v, kc, vc, bt)


def _slot_attn_kernel(n_heads, q_slot, has_lat, *refs):
    if has_lat:
        q_ref, k_ref, v_ref, kc_ref, vc_ref, o_ref = refs
    else:
        q_ref, kc_ref, vc_ref, o_ref = refs
    m0, m1 = _half_masks()
    for j in range(n_heads // 2):
        vsl = slice(j * LANES, (j + 1) * LANES)
        outs = []
        for par, msk in ((0, m0), (1, m1)):
            h = 2 * j + par
            if q_slot:
                ksl = slice(h * LANES, (h + 1) * LANES)
                qm = q_ref[0, :, ksl]
            else:
                ksl = vsl
                qp = q_ref[0, :, vsl]
                qm = jnp.where(msk, qp, jnp.zeros_like(qp))
            ks, vs = [kc_ref[0, :, ksl]], [vc_ref[0, :, vsl]]
            if has_lat:
                ks.insert(0, k_ref[0, :, ksl])
                vs.insert(0, v_ref[0, :, vsl])
            outs.append(_attend(qm, ks, vs))
        o_ref[0, :, vsl] = jnp.where(m0, outs[0], outs[1]).astype(BF16)


def _slot_attention(q, k, v, kc, vc, n_heads, q_slot, tq):
    bsz, sq, wq = q.shape
    l, wk, wv = kc.shape[1], kc.shape[2], vc.shape[2]
    has_lat = k is not None
    qspec = pl.BlockSpec((1, tq, wq), lambda b, i: (b, i, 0))
    full = lambda n, w: pl.BlockSpec((1, n, w), lambda b, i: (b, 0, 0))
    in_specs, args = [qspec], [q]
    if has_lat:
        s = k.shape[1]
        in_specs += [full(s, wk), full(s, wv)]
        args += [k, v]
    in_specs += [full(l, wk), full(l, wv)]
    args += [kc, vc]
    return pl.pallas_call(
        functools.partial(_slot_attn_kernel, n_heads, q_slot, has_lat),
        grid=(bsz, sq // tq),
        in_specs=in_specs,
        out_specs=pl.BlockSpec((1, tq, wv), lambda b, i: (b, i, 0)),
        out_shape=jax.ShapeDtypeStruct((bsz, sq, wv), BF16),
        compiler_params=_cparams(2),
    )(*args)


def _gqa_kernel(has_lat, tq, *refs):
    if has_lat:
        q_ref, k_ref, v_ref, kc_ref, vc_ref, o_ref = refs
    else:
        q_ref, kc_ref, vc_ref, o_ref = refs
    masks = _half_masks()
    res = {}
    for g in range(GQA_KV_HEADS):
        for var in range(2):
            par = g if var == 0 else 1 - g
            heads = (4 * g + par, 4 * g + 2 + par)
            vsl = slice(var * LANES, (var + 1) * LANES)
            qs = []
            for h in heads:
                qp = q_ref[0, :, (h // 2) * LANES:(h // 2 + 1) * LANES]
                qs.append(jnp.where(masks[par], qp, jnp.zeros_like(qp)))
            qm = jnp.concatenate(qs, axis=0)
            ks, vs = [kc_ref[0, :, vsl]], [vc_ref[0, :, vsl]]
            if has_lat:
                ks.insert(0, k_ref[0, :, vsl])
                vs.insert(0, v_ref[0, :, vsl])
            o = _attend(qm, ks, vs)
            res[heads[0]] = o[:tq]
            res[heads[1]] = o[tq:]
    for j in range(GQA_HEADS // 2):
        o_ref[0, :, j * LANES:(j + 1) * LANES] = jnp.where(
            masks[0], res[2 * j], res[2 * j + 1]).astype(BF16)


def _gqa_attention(q, k2, v2, k2c, v2c, tq):
    bsz, sq, wq = q.shape
    l = k2c.shape[1]
    has_lat = k2 is not None
    full = lambda n: pl.BlockSpec((1, n, 2 * LANES), lambda b, i: (b, 0, 0))
    in_specs, args = [pl.BlockSpec((1, tq, wq), lambda b, i: (b, i, 0))], [q]
    if has_lat:
        in_specs += [full(k2.shape[1])] * 2
        args += [k2, v2]
    in_specs += [full(l)] * 2
    args += [k2c, v2c]
    return pl.pallas_call(
        functools.partial(_gqa_kernel, has_lat, tq),
        grid=(bsz, sq // tq),
        in_specs=in_specs,
        out_specs=pl.BlockSpec((1, tq, wq), lambda b, i: (b, i, 0)),
        out_shape=jax.ShapeDtypeStruct((bsz, sq, wq), BF16),
        compiler_params=_cparams(2),
    )(*args)


def _diff_kernel(has_lat, tq, lam_init, *refs):
    if has_lat:
        q_ref, k_ref, v_ref, kc_ref, vc_ref, lq1, lk1, lq2, lk2, sub_ref, o_ref = refs
    else:
        q_ref, kc_ref, vc_ref, lq1, lk1, lq2, lk2, sub_ref, o_ref = refs
    lam = (jnp.exp(jnp.sum(lq1[...] * lk1[...], axis=-1, keepdims=True))
           - jnp.exp(jnp.sum(lq2[...] * lk2[...], axis=-1, keepdims=True)) + lam_init)
    m0, m1 = _half_masks()
    for h in range(DIFF_HEADS):
        sl = slice(h * LANES, (h + 1) * LANES)
        qp = q_ref[0, :, sl]
        zero = jnp.zeros_like(qp)
        qm = jnp.concatenate([jnp.where(m0, qp, zero), jnp.where(m1, qp, zero)], axis=0)
        ks, vs = [kc_ref[0, :, sl]], [vc_ref[0, :, sl]]
        if has_lat:
            ks.insert(0, k_ref[0, :, sl])
            vs.insert(0, v_ref[0, :, sl])
        o2 = _attend(qm, ks, vs)
        o = o2[:tq] - lam * o2[tq:]
        o = o * lax.rsqrt(jnp.mean(o * o, axis=-1, keepdims=True) + RMS_EPS) * sub_ref[...]
        o_ref[0, :, sl] = (o * (1.0 - lam_init)).astype(BF16)


def _diff_attention(q, k, v, kc, vc, lams, subln, lam_init, tq):
    bsz, sq, w = q.shape
    l = kc.shape[1]
    has_lat = k is not None
    full = lambda n: pl.BlockSpec((1, n, w), lambda b, i: (b, 0, 0))
    in_specs, args = [pl.BlockSpec((1, tq, w), lambda b, i: (b, i, 0))], [q]
    if has_lat:
        in_specs += [full(k.shape[1])] * 2
        args += [k, v]
    in_specs += [full(l)] * 2 + [_const_spec((1, HEAD_DIM))] * 4 + [_const_spec((1, LANES))]
    args += [kc, vc] + [a.reshape(1, -1).astype(F32) for a in lams] + [subln.reshape(1, -1).astype(F32)]
    return pl.pallas_call(
        functools.partial(_diff_kernel, has_lat, tq, lam_init),
        grid=(bsz, sq // tq),
        in_specs=in_specs,
        out_specs=pl.BlockSpec((1, tq, w), lambda b, i: (b, i, 0)),
        out_shape=jax.ShapeDtypeStruct((bsz, sq, w), BF16),
        compiler_params=_cparams(2),
    )(*args)


def _pack_bf16_pairs(v):
    w = v.shape[1] // 2
    hi = pltpu.bitcast(v[:, :w].astype(F32), jnp.int32)
    lo = pltpu.bitcast(v[:, w:].astype(F32), jnp.int32)
    return hi | lax.shift_right_logical(lo, 16)


def _unpack_bf16_pairs(u):
    hi = pltpu.bitcast(u & jnp.int32(-65536), F32)
    lo = pltpu.bitcast(lax.shift_left(u, 16), F32)
    return hi, lo


def _layer_norm(z, g, b):
    mu = jnp.mean(z, axis=-1, keepdims=True)
    zc = z - mu
    var = jnp.mean(zc * zc, axis=-1, keepdims=True)
    return zc * lax.rsqrt(var + LN_EPS) * g + b


def _outproj_kernel(alpha, o1_ref, o2_ref, w1_ref, w2_ref, x_ref, g1_ref, lng_ref, lnb_ref,
                    sc2_ref, sh2_ref, rwh_ref, rwl_ref, rb_ref, tri_ref, cnt0_ref,
                    xo_ref, h2a_ref, h2b_ref, idx_ref, rank_ref, gw_ref, cnt_ref, carry_ref):
    o = _dot(o1_ref[0], w1_ref[...]) + _dot(o2_ref[0], w2_ref[...])
    xn = _layer_norm(alpha * x_ref[0] + g1_ref[0] * o, lng_ref[...], lnb_ref[...])
    xo_ref[0] = xn
    h2 = xn * (1.0 + sc2_ref[0]) + sh2_ref[0]
    hi = h2.astype(BF16)
    packed = _pack_bf16_pairs(hi)
    q = packed.shape[1] // 2
    h2a_ref[0] = packed[:, :q]
    h2b_ref[0] = packed[:, q:]
    lo = (h2 - hi.astype(F32)).astype(BF16)
    logits = (_dot(hi, rwh_ref[...]) + _dot(lo, rwh_ref[...]) + _dot(hi, rwl_ref[...])) + rb_ref[...]
    lane = lax.broadcasted_iota(jnp.int32, logits.shape, 1).astype(F32)
    vals, idxs = [], []
    cur = logits
    for _ in range(TOP_K):
        m = jnp.max(cur, axis=-1, keepdims=True)
        ik = jnp.min(jnp.where(cur == m, lane, float(LANES)), axis=-1, keepdims=True)
        vals.append(m)
        idxs.append(ik)
        cur = jnp.where(lane == ik, -jnp.inf, cur)
    ws = [jnp.exp(v - vals[0]) for v in vals]
    inv = 1.0 / functools.reduce(lambda a, b: a + b, ws)

    @pl.when(jnp.logical_and(pl.program_id(0) == 0, pl.program_id(1) == 0))
    def _():
        carry_ref[...] = cnt0_ref[...]

    sel = [lane == ik for ik in idxs]
    onehot = functools.reduce(lambda a, b: a + b, [m.astype(F32) for m in sel])
    before = _dot(tri_ref[...], onehot.astype(BF16)) + carry_ref[...]
    idx_out = jnp.zeros_like(logits)
    rank_out = jnp.zeros_like(logits)
    w_out = jnp.zeros_like(logits)
    for k in range(TOP_K):
        rk = jnp.sum(jnp.where(sel[k], before, 0.0), axis=-1, keepdims=True)
        idx_out = jnp.where(lane == float(k), idxs[k], idx_out)
        rank_out = jnp.where(lane == float(k), rk, rank_out)
        w_out = jnp.where(lane == float(k), ws[k] * inv, w_out)
    idx_ref[0] = idx_out.astype(jnp.int32)
    rank_ref[0] = rank_out.astype(jnp.int32)
    gw_ref[0] = w_out
    carry_ref[...] += jnp.sum(onehot, axis=0, keepdims=True)
    cnt_ref[...] = carry_ref[...]


def _outproj_ln_router(o1, o2, w_out, x, g1, lng, lnb, sc2, sh2, router, cnt0, alpha, ts):
    bsz, s, d = x.shape
    batched = g1.shape[0] > 1
    rwh, rwl, rb = router
    w1, w2 = w_out[:512].astype(BF16), w_out[512:].astype(BF16)
    tri = jnp.asarray(np.tril(np.ones((ts, ts), np.float32), -1), dtype=BF16)
    tok = lambda n: pl.BlockSpec((1, ts, n), lambda b, i: (b, i, 0))
    ms = _mod_spec(d, batched)
    return pl.pallas_call(
        functools.partial(_outproj_kernel, alpha),
        grid=(bsz, s // ts),
        in_specs=[tok(512), tok(512), _const_spec((512, d)), _const_spec((512, d)), tok(d), ms,
                  _const_spec((1, d)), _const_spec((1, d)), ms, ms,
                  _const_spec((d, LANES)), _const_spec((d, LANES)), _const_spec((1, LANES)),
                  _const_spec((ts, ts)), _const_spec((1, LANES))],
        out_specs=[tok(d), tok(d // 4), tok(d // 4), tok(LANES), tok(LANES), tok(LANES),
                   _const_spec((1, LANES))],
        out_shape=[jax.ShapeDtypeStruct((bsz, s, d), F32),
                   jax.ShapeDtypeStruct((bsz, s, d // 4), jnp.int32),
                   jax.ShapeDtypeStruct((bsz, s, d // 4), jnp.int32),
                   jax.ShapeDtypeStruct((bsz, s, LANES), jnp.int32),
                   jax.ShapeDtypeStruct((bsz, s, LANES), jnp.int32),
                   jax.ShapeDtypeStruct((bsz, s, LANES), F32),
                   jax.ShapeDtypeStruct((1, LANES), F32)],
        scratch_shapes=[pltpu.VMEM((1, LANES), F32)],
        compiler_params=_cparams(2),
    )(o1, o2, w1, w2, x, g1, lng.reshape(1, d), lnb.reshape(1, d), sc2, sh2, rwh, rwl, rb, tri, cnt0)


def _router_weights(router_w, router_b):
    d, e = router_w.shape
    wp = jnp.zeros((d, LANES), F32).at[:, :e].set(router_w)
    hi = wp.astype(BF16)
    lo = (wp - hi.astype(F32)).astype(BF16)
    rb = jnp.full((1, LANES), -jnp.inf, F32).at[0, :e].set(router_b)
    return hi, lo, rb


def _deinterleave_perm():
    p = np.zeros((GU_BLOCK, GU_BLOCK), np.float32)
    m = np.arange(GU_BLOCK // 2)
    p[2 * m, m] = 1.0
    p[2 * m + 1, GU_BLOCK // 2 + m] = 1.0
    return jnp.asarray(p, dtype=BF16)


def _ffn_kernel(te_ref, tv_ref, xa_ref, xb_ref, wgu_ref, bgu_ref, wd_ref, bd_ref, perm_ref, y_ref,
                wgu_s, wd_s):
    j = pl.program_id(0)
    n_blk = wgu_s.shape[1] // GU_BLOCK
    half = GU_BLOCK // 2

    @pl.when(jnp.logical_or(j == 0, te_ref[j] != te_ref[jnp.maximum(j - 1, 0)]))
    def _():
        for b in range(n_blk):
            sl = slice(b * GU_BLOCK, (b + 1) * GU_BLOCK)
            wgu_s[:, sl] = _dot(wgu_ref[0, 0, :, sl].astype(BF16), perm_ref[...]).astype(BF16)
        wd_s[...] = wd_ref[0, 0].astype(BF16)

    @pl.when(tv_ref[j] > 0)
    def _():
        a_hi, a_lo = _unpack_bf16_pairs(xa_ref[...])
        b_hi, b_lo = _unpack_bf16_pairs(xb_ref[...])
        x = jnp.concatenate([a_hi, b_hi, a_lo, b_lo], axis=-1).astype(BF16)
        acts = []
        for b in range(n_blk):
            sl = slice(b * GU_BLOCK, (b + 1) * GU_BLOCK)
            gu = _dot(x, wgu_s[:, sl]) + bgu_ref[0, :, sl]
            glu = jnp.minimum(gu[:, :half], SWIGLU_LIMIT)
            lin = jnp.clip(gu[:, half:], -SWIGLU_LIMIT, SWIGLU_LIMIT)
            acts.append(((lin + 1.0) * (glu * jax.nn.sigmoid(SWIGLU_ALPHA * glu))).astype(BF16))
        a = jnp.concatenate(acts, axis=-1)
        y_ref[...] = (_dot(a, wd_s[...]) + bd_ref[0]).astype(BF16)

    @pl.when(tv_ref[j] == 0)
    def _():
        y_ref[...] = jnp.zeros_like(y_ref)


def _expert_ffn(xsa, xsb, tile_expert, tile_valid, layer, w_gu, b_gu, w_down, b_down, tm):
    p, q = xsa.shape
    d = 4 * q
    _, e, _, f2 = w_gu.shape
    f = f2 // 2
    half = GU_BLOCK // 2
    bgu = jnp.stack([b_gu[:, 0::2].reshape(e, f // half, half),
                     b_gu[:, 1::2].reshape(e, f // half, half)], axis=2).reshape(e, 1, f2)
    wspec = lambda a, b: pl.BlockSpec((1, 1, a, b), lambda j, te, tv: (layer, te[j], 0, 0))
    bspec = lambda b: pl.BlockSpec((1, 1, b), lambda j, te, tv: (te[j], 0, 0))
    return pl.pallas_call(
        _ffn_kernel,
        grid_spec=pltpu.PrefetchScalarGridSpec(
            num_scalar_prefetch=2,
            grid=(p // tm,),
            in_specs=[pl.BlockSpec((tm, q), lambda j, te, tv: (j, 0)),
                      pl.BlockSpec((tm, q), lambda j, te, tv: (j, 0)),
                      wspec(d, f2), bspec(f2), wspec(f, d), bspec(d),
                      pl.BlockSpec((GU_BLOCK, GU_BLOCK), lambda j, te, tv: (0, 0))],
            out_specs=pl.BlockSpec((tm, d), lambda j, te, tv: (j, 0)),
            scratch_shapes=[pltpu.VMEM((d, f2), BF16), pltpu.VMEM((f, d), BF16)]),
        out_shape=jax.ShapeDtypeStruct((p, d), BF16),
        compiler_params=_cparams(1),
    )(tile_expert, tile_valid, xsa, xsb, w_gu, bgu, w_down, b_down.reshape(e, 1, d),
      _deinterleave_perm())


def _moe_plan(idx4, rank4, counts, tm):
    t = idx4.shape[0]
    n_experts = counts.shape[0]
    pc = ((counts + tm - 1) // tm) * tm
    pend = jnp.cumsum(pc)
    pstart = pend - pc
    dpos_t = (pstart[idx4] + rank4).T.astype(jnp.int32)
    p = t * TOP_K + n_experts * tm
    tile_start = jnp.arange(p // tm, dtype=jnp.int32) * tm
    tile_expert = jnp.minimum(jnp.sum((tile_start[:, None] >= pend[None, :]).astype(jnp.int32), axis=1),
                              n_experts - 1).astype(jnp.int32)
    tile_valid = (tile_start < pend[-1]).astype(jnp.int32)
    return dpos_t, p, tile_expert, tile_valid


SC_WINDOW = 128


def _sc_dispatch(rows, dpos_t, p):
    t, w = rows.shape
    k = dpos_t.shape[0]
    mesh = plsc.VectorSubcoreMesh(core_axis_name="core", subcore_axis_name="subcore")

    @functools.partial(pl.kernel, out_type=jax.ShapeDtypeStruct((p, w), rows.dtype), mesh=mesh)
    def kern(x_hbm, *refs):
        i_hbms, o_hbm = refs[:k], refs[k]

        def body(x_vmem, *i_vmems):
            for iv in i_vmems:
                pltpu.sync_copy(x_vmem, o_hbm.at[iv.at[0]])

        pltpu.emit_pipeline(
            body,
            grid=(t // SC_WINDOW,),
            in_specs=[pl.BlockSpec((SC_WINDOW, w), index_map=lambda i: (i, 0))]
            + [pl.BlockSpec((1, SC_WINDOW), index_map=lambda i: (0, i)) for _ in range(k)],
            out_specs=[],
            core_axis_name=("core", "subcore"),
            dimension_semantics=(pltpu.PARALLEL,),
        )(x_hbm, *i_hbms)

    return kern(rows, *[dpos_t[kk:kk + 1] for kk in range(k)])


def _combine_kernel(alpha, yg_ref, gw_ref, x_ref, g2_ref, lng_ref, lnb_ref, o_ref):
    gw = gw_ref[...]
    y = yg_ref[0].astype(F32) * gw[:, 0:1]
    for k in range(1, TOP_K):
        y = y + yg_ref[k].astype(F32) * gw[:, k:k + 1]
    o_ref[0] = _layer_norm(alpha * x_ref[0] + g2_ref[0] * y, lng_ref[...], lnb_ref[...])


def _combine_ln(yg, gw, x, g2, lng, lnb, alpha, row_offset, ts):
    bsz, s, d = x.shape
    batched = g2.shape[0] > 1
    nblk = s // ts
    off = row_offset // ts
    return pl.pallas_call(
        functools.partial(_combine_kernel, alpha),
        grid=(bsz, nblk),
        in_specs=[pl.BlockSpec((TOP_K, ts, d), lambda b, i: (0, off + b * nblk + i, 0)),
                  pl.BlockSpec((ts, LANES), lambda b, i: (off + b * nblk + i, 0)),
                  pl.BlockSpec((1, ts, d), lambda b, i: (b, i, 0)),
                  _mod_spec(d, batched), _const_spec((1, d)), _const_spec((1, d))],
        out_specs=pl.BlockSpec((1, ts, d), lambda b, i: (b, i, 0)),
        out_shape=jax.ShapeDtypeStruct((bsz, s, d), F32),
        compiler_params=_cparams(2),
    )(yg, gw, x, g2, lng.reshape(1, d), lnb.reshape(1, d))


def kernel(x, c, ctx, c_ctx, mod_w, mod_b, ln1_g, ln1_b, ln2_g, ln2_b, even_w_in, even_w_out, na_rpb, gqa_q_gain, gqa_k_gain, odd_w_in, odd_w_out, diff_lq1, diff_lk1, diff_lq2, diff_lk2, diff_subln, mla_q_gain, mla_w_uq, mla_kv_gain, mla_w_ukv, router_w, router_b, exp_w_gu, exp_b_gu, exp_w_down, exp_b_down):
    bsz, s, d = x.shape
    l = ctx.shape[1]
    depth = mod_w.shape[0]
    n_experts = router_w.shape[-1]
    alpha = (2 * depth) ** 0.25
    ts = min(512, s)
    tm = MOE_TILE

    pad = (-(bsz + 1)) % 8
    c_all = jnp.concatenate([c, c_ctx[None, :], jnp.zeros((pad, d), F32)], axis=0)
    mod = _modulation_all(c_all, mod_w, mod_b)

    for i in range(depth):
        last = i == depth - 1
        j = i // 2
        ml = [mod[i, :bsz, k * d:(k + 1) * d].reshape(bsz, 1, d) for k in range(6)]
        mc = [mod[i, bsz:bsz + 1, k * d:(k + 1) * d].reshape(1, 1, d) for k in range(6)]
        sh1, sc1, g1, sh2, sc2, g2 = ml
        csh1, csc1, cg1, csh2, csc2, cg2 = mc

        if i % 2 == 0:
            w = _even_weights(even_w_in[j])
            naq, nak, nav, gq, gk, gv = _inproj_even(
                x, sc1, sh1, w, _even_tables(s, gqa_q_gain[j], gqa_k_gain[j], True), ts)
            cnaq, cnak, cnav, cgq, cgk, cgv = _inproj_even(
                ctx, csc1, csh1, w, _even_tables(l, gqa_q_gain[j], gqa_k_gain[j], False), l)
            o1 = _na_attention(naq, nak, nav, cnak, cnav, na_rpb[j])
            o2 = _gqa_attention(gq, gk, gv, cgk, cgv, min(256, s))
            if not last:
                co1 = _slot_attention(cnaq, None, None, cnak, cnav, NA_HEADS, False, l)
                co2 = _gqa_attention(cgq, None, None, cgk, cgv, l)
            w_out = even_w_out[j]
        else:
            lam_init = 0.8 - 0.6 * math.exp(-0.3 * i)
            weights = _odd_weights(odd_w_in[j], mla_w_uq[j], mla_w_ukv[j])
            gains = (mla_q_gain[j], mla_kv_gain[j])
            dq, dk, dv, mq, mk, mv = _inproj_odd(x, sc1, sh1, weights, gains, _odd_tables(s, True), ts)
            cdq, cdk, cdv, cmq, cmk, cmv = _inproj_odd(ctx, csc1, csh1, weights, gains,
                                                       _odd_tables(l, False), l)
            lams = (diff_lq1[j], diff_lk1[j], diff_lq2[j], diff_lk2[j])
            o1 = _diff_attention(dq, dk, dv, cdk, cdv, lams, diff_subln[j], lam_init, min(256, s))
            o2 = _slot_attention(mq, mk, mv, cmk, cmv, MLA_HEADS, True, min(512, s))
            if not last:
                co1 = _diff_attention(cdq, None, None, cdk, cdv, lams, diff_subln[j], lam_init, l)
                co2 = _slot_attention(cmq, None, None, cmk, cmv, MLA_HEADS, True, l)
            w_out = odd_w_out[j]

        router = _router_weights(router_w[i], router_b[i])
        cnt0 = jnp.zeros((1, LANES), F32)
        x, h2a, h2b, ridx, rrank, rgw, cnt = _outproj_ln_router(
            o1, o2, w_out, x, g1, ln1_g[i], ln1_b[i], sc2, sh2, router, cnt0, alpha, ts)
        flat = lambda a, n: a.reshape(bsz * n, a.shape[-1])
        h2a, h2b, ridx, rrank, rgw = [flat(a, s) for a in (h2a, h2b, ridx, rrank, rgw)]
        if not last:
            ctx, ch2a, ch2b, cidx, crank, cgw, cnt = _outproj_ln_router(
                co1, co2, w_out, ctx, cg1, ln1_g[i], ln1_b[i], csc2, csh2, router, cnt, alpha, l)
            cat = lambda a, b: jnp.concatenate([a, flat(b, l)], axis=0)
            h2a, h2b = cat(h2a, ch2a), cat(h2b, ch2b)
            ridx, rrank, rgw = cat(ridx, cidx), cat(rrank, crank), cat(rgw, cgw)

        counts = cnt[0, :n_experts].astype(jnp.int32)
        dpos_t, p, tile_expert, tile_valid = _moe_plan(ridx[:, :TOP_K], rrank[:, :TOP_K], counts, tm)
        xsa = _sc_dispatch(h2a, dpos_t, p)
        xsb = _sc_dispatch(h2b, dpos_t, p)
        ys = _expert_ffn(xsa, xsb, tile_expert, tile_valid, i, exp_w_gu, exp_b_gu[i],
                         exp_w_down, exp_b_down[i], tm)
        yg = ys.at[dpos_t].get(mode="promise_in_bounds")

        x = _combine_ln(yg, rgw, x, g2, ln2_g[i], ln2_b[i], alpha, 0, ts)
        if not last:
            ctx = _combine_ln(yg, rgw, ctx, cg2, ln2_g[i], ln2_b[i], alpha, bsz * s, l)
    return x
```

```python
import functools
import math

import numpy as np
import jax
import jax.numpy as jnp
from jax import lax
from jax.experimental import pallas as pl
from jax.experimental.pallas import tpu as pltpu
from jax.experimental.pallas import tpu_sc as plsc

F32 = jnp.float32
BF16 = jnp.bfloat16

GRID_W = 64
HEAD_DIM = 64
ROPE_THETA = 10000.0
LN_EPS = 1e-6
RMS_EPS = 1e-6
NEG_INF = -1e30
NA_HEADS = 8
NA_WIN_H = 8
NA_WIN_W = 16
GQA_HEADS = 8
GQA_KV_HEADS = 2
DIFF_HEADS = 4
MLA_HEADS = 8
MLA_Q_RANK = 256
MLA_KV_RANK = 128
MLA_NOPE = 64
MLA_ROPE = 32
MLA_V = 64
MLA_QK = MLA_NOPE + MLA_ROPE
TOP_K = 4
SWIGLU_ALPHA = 1.702
SWIGLU_LIMIT = 7.0
LOG2E = 1.4426950408889634

LANES = 128
VMEM_LIMIT = 56 * 1024 * 1024
MOE_TILE = 1024
GU_BLOCK = 512
ATTN_SUB = 2


def _cparams(n_axes):
    return pltpu.CompilerParams(dimension_semantics=("arbitrary",) * n_axes,
                                vmem_limit_bytes=VMEM_LIMIT)


def _dot(a, b):
    return jnp.dot(a, b, preferred_element_type=F32)


def _dot_nt(a, b):
    return lax.dot_general(a, b, (((1,), (1,)), ((), ())), preferred_element_type=F32)


def _mod_kernel(c_ref, w_ref, b_ref, o_ref):
    cv = c_ref[...]
    a = (cv * jax.nn.sigmoid(cv)).astype(BF16)
    o_ref[0] = _dot(a, w_ref[0].astype(BF16)) + b_ref[0]


def _modulation_all(c_all, mod_w, mod_b):
    depth, d, n = mod_w.shape
    rows = c_all.shape[0]
    tn = 1536
    return pl.pallas_call(
        _mod_kernel,
        grid=(depth, n // tn),
        in_specs=[pl.BlockSpec((rows, d), lambda i, j: (0, 0)),
                  pl.BlockSpec((1, d, tn), lambda i, j: (i, 0, j)),
                  pl.BlockSpec((1, 1, tn), lambda i, j: (i, 0, j))],
        out_specs=pl.BlockSpec((1, rows, tn), lambda i, j: (i, 0, j)),
        out_shape=jax.ShapeDtypeStruct((depth, rows, n), F32),
        compiler_params=_cparams(2),
    )(c_all, mod_w, mod_b.reshape(depth, 1, n))


def _rope_tables(s, dim):
    pos = jnp.arange(s)
    row = (pos // GRID_W).astype(F32)[:, None]
    col = (pos % GRID_W).astype(F32)[:, None]
    quarter = dim // 4
    inv_freq = ROPE_THETA ** (-jnp.arange(quarter, dtype=F32) / quarter)
    ar, ac = row * inv_freq, col * inv_freq
    cos = jnp.concatenate([jnp.cos(ar), jnp.cos(ar), jnp.cos(ac), jnp.cos(ac)], axis=-1)
    sin = jnp.concatenate([-jnp.sin(ar), jnp.sin(ar), -jnp.sin(ac), jnp.sin(ac)], axis=-1)
    return cos, sin


def _swap_perm(dim):
    q = dim // 4
    idx = np.arange(dim)
    return np.where((idx % (2 * q)) < q, idx + q, idx - q)


def _swap_cols(w, dim):
    n = w.shape[-1] // dim
    perm = (np.arange(n)[:, None] * dim + _swap_perm(dim)[None, :]).reshape(-1)
    return w[..., perm]


def _block_ones(n, blk):
    i = np.arange(n) // blk
    return jnp.asarray((i[:, None] == i[None, :]).astype(np.float32), dtype=BF16)


E_NAQ, E_NAK, E_NAV, E_GQ, E_GQS, E_GK, E_GKS, E_GV, E_END = (
    0, 512, 1024, 1536, 2048, 2560, 2816, 3072, 3328)


def _even_weights(w_in):
    naq, nak, nav, gq, gk, gv = jnp.split(w_in, [512, 1024, 1536, 2048, 2176], axis=-1)
    k0, k1 = gk[:, :64], gk[:, 64:]
    v0, v1 = gv[:, :64], gv[:, 64:]
    gk2 = jnp.concatenate([k0, k1, k1, k0], axis=-1)
    gv2 = jnp.concatenate([v0, v1, v1, v0], axis=-1)
    w = jnp.concatenate([naq * (0.125 * LOG2E), nak, nav, gq, _swap_cols(gq, 64),
                         gk2, _swap_cols(gk2, 64), gv2], axis=-1).astype(BF16)
    return w


def _even_tables(s, q_gain, k_gain, rope):
    sw = _swap_perm(64)
    if rope:
        cos, sin = _rope_tables(s, 64)
    else:
        cos, sin = jnp.ones((s, 64), F32), jnp.zeros((s, 64), F32)
    qa = jnp.tile(cos * q_gain[None, :] * (0.125 * LOG2E), (1, 8))
    qb = jnp.tile(sin * q_gain[sw][None, :] * (0.125 * LOG2E), (1, 8))
    ka = jnp.tile(cos * k_gain[None, :], (1, 4))
    kb = jnp.tile(sin * k_gain[sw][None, :], (1, 4))
    return qa, qb, ka, kb


def _inproj_even_kernel(x_ref, sc_ref, sh_ref, w_ref, g512_ref, g256_ref,
                        qa_ref, qb_ref, ka_ref, kb_ref,
                        naq_ref, nak_ref, nav_ref, gq_ref, gk_ref, gv_ref):
    h = (x_ref[0] * (1.0 + sc_ref[0]) + sh_ref[0]).astype(BF16)
    naq_ref[0] = _dot(h, w_ref[:, E_NAQ:E_NAK]).astype(BF16)
    nak_ref[0] = _dot(h, w_ref[:, E_NAK:E_NAV]).astype(BF16)
    nav_ref[0] = _dot(h, w_ref[:, E_NAV:E_GQ]).astype(BF16)
    gv_ref[0] = _dot(h, w_ref[:, E_GV:E_END]).astype(BF16)
    y = _dot(h, w_ref[:, E_GQ:E_GQS])
    ys = _dot(h, w_ref[:, E_GQS:E_GK])
    r = lax.rsqrt(_dot((y * y).astype(BF16), g512_ref[...]) * (1.0 / HEAD_DIM) + RMS_EPS)
    gq_ref[0] = (r * (y * qa_ref[...] + ys * qb_ref[...])).astype(BF16)
    y = _dot(h, w_ref[:, E_GK:E_GKS])
    ys = _dot(h, w_ref[:, E_GKS:E_GV])
    r = lax.rsqrt(_dot((y * y).astype(BF16), g256_ref[...]) * (1.0 / HEAD_DIM) + RMS_EPS)
    gk_ref[0] = (r * (y * ka_ref[...] + ys * kb_ref[...])).astype(BF16)


def _mod_spec(d, batched):
    if batched:
        return pl.BlockSpec((1, 1, d), lambda b, s: (b, 0, 0))
    return pl.BlockSpec((1, 1, d), lambda b, s: (0, 0, 0))


def _const_spec(shape):
    nd = len(shape)
    return pl.BlockSpec(shape, lambda b, s: (0,) * nd)


def _inproj_even(x, sc, sh, w, tables, ts):
    bsz, s, d = x.shape
    batched = sc.shape[0] > 1
    qa, qb, ka, kb = tables
    tok = lambda n: pl.BlockSpec((1, ts, n), lambda b, i: (b, i, 0))
    tab = lambda n: pl.BlockSpec((ts, n), lambda b, i: (i, 0))
    widths = (512, 512, 512, 512, 256, 256)
    return pl.pallas_call(
        _inproj_even_kernel,
        grid=(bsz, s // ts),
        in_specs=[tok(d), _mod_spec(d, batched), _mod_spec(d, batched), _const_spec(w.shape),
                  _const_spec((512, 512)), _const_spec((256, 256)),
                  tab(512), tab(512), tab(256), tab(256)],
        out_specs=[tok(n) for n in widths],
        out_shape=[jax.ShapeDtypeStruct((bsz, s, n), BF16) for n in widths],
        compiler_params=_cparams(2),
    )(x, sc, sh, w, _block_ones(512, 64), _block_ones(256, 64), qa, qb, ka, kb)


O_DQ, O_DQS, O_DK, O_DKS, O_DV, O_CQ, O_CKV, O_PE, O_END = (
    0, 512, 1024, 1536, 2048, 2560, 2816, 2944, 3072)


def _odd_weights(w_in, w_uq, w_ukv):
    dq, dk, dv, cq, ckv, kpe = jnp.split(w_in, [512, 1024, 1536, 1792, 1920], axis=-1)
    d = w_in.shape[0]
    pe_slot = jnp.concatenate([kpe, _swap_cols(kpe, 32), jnp.zeros((d, 64), F32)], axis=-1)
    w = jnp.concatenate([dq, _swap_cols(dq, 64), dk, _swap_cols(dk, 64), dv, cq, ckv, pe_slot],
                        axis=-1).astype(BF16)
    uq = w_uq.reshape(MLA_Q_RANK, MLA_HEADS, MLA_QK)
    z32 = jnp.zeros((MLA_Q_RANK, MLA_HEADS, 32), F32)
    z64 = jnp.zeros((MLA_Q_RANK, MLA_HEADS, 64), F32)
    uq_pad = jnp.concatenate([uq, z32], axis=-1).reshape(MLA_Q_RANK, MLA_HEADS * LANES)
    uq_sw = jnp.concatenate([z64, _swap_cols(uq[..., MLA_NOPE:], 32), z32],
                            axis=-1).reshape(MLA_Q_RANK, MLA_HEADS * LANES)
    wuq2 = jnp.concatenate([uq_pad, uq_sw], axis=-1).astype(BF16)
    ukv = w_ukv.reshape(MLA_KV_RANK, MLA_HEADS, MLA_NOPE + MLA_V)
    zk = jnp.zeros((MLA_KV_RANK, MLA_HEADS, 64), F32)
    wk_pad = jnp.concatenate([ukv[..., :MLA_NOPE], zk], axis=-1).reshape(MLA_KV_RANK, MLA_HEADS * LANES)
    place = np.zeros((LANES, MLA_HEADS, LANES), np.float32)
    for j in range(MLA_ROPE):
        place[j, :, MLA_NOPE + j] = 1.0
        place[MLA_ROPE + j, :, MLA_NOPE + j] = 1.0
    wk2 = jnp.concatenate([wk_pad, jnp.asarray(place.reshape(LANES, MLA_HEADS * LANES))],
                          axis=0).astype(BF16)
    wv = ukv[..., MLA_NOPE:].reshape(MLA_KV_RANK, MLA_HEADS * MLA_V).astype(BF16)
    return w, wuq2, wk2, wv


def _odd_tables(s, rope):
    m_scale = MLA_QK ** -0.5 * LOG2E
    if rope:
        cos64, sin64 = _rope_tables(s, 64)
        cos32, sin32 = _rope_tables(s, 32)
    else:
        cos64, sin64 = jnp.ones((s, 64), F32), jnp.zeros((s, 64), F32)
        cos32, sin32 = jnp.ones((s, 32), F32), jnp.zeros((s, 32), F32)
    one64, z32, z64 = jnp.ones((s, 64), F32), jnp.zeros((s, 32), F32), jnp.zeros((s, 64), F32)
    dcos, dsin = jnp.tile(cos64, (1, 8)), jnp.tile(sin64, (1, 8))
    qa = jnp.tile(jnp.concatenate([one64, cos32, z32], axis=-1) * m_scale, (1, MLA_HEADS))
    qb = jnp.tile(jnp.concatenate([z64, sin32, z32], axis=-1) * m_scale, (1, MLA_HEADS))
    pe = jnp.concatenate([cos32, sin32, z64], axis=-1)
    return dcos, dsin, qa, qb, pe


def _inproj_odd_kernel(x_ref, sc_ref, sh_ref, w_ref, wuq_ref, wk_ref, wv_ref, qg_ref, kvg_ref,
                       dcos_ref, dsin_ref, qa_ref, qb_ref, pe_ref,
                       dq_ref, dk_ref, dv_ref, mq_ref, mk_ref, mv_ref):
    h = (x_ref[0] * (1.0 + sc_ref[0]) + sh_ref[0]).astype(BF16)
    dcos, dsin = dcos_ref[...], dsin_ref[...]
    y = _dot(h, w_ref[:, O_DQ:O_DQS])
    ys = _dot(h, w_ref[:, O_DQS:O_DK])
    dq_ref[0] = ((y * dcos + ys * dsin) * (0.125 * LOG2E)).astype(BF16)
    y = _dot(h, w_ref[:, O_DK:O_DKS])
    ys = _dot(h, w_ref[:, O_DKS:O_DV])
    dk_ref[0] = (y * dcos + ys * dsin).astype(BF16)
    dv_ref[0] = _dot(h, w_ref[:, O_DV:O_CQ]).astype(BF16)
    cq = _dot(h, w_ref[:, O_CQ:O_CKV])
    nq = cq * lax.rsqrt(jnp.mean(cq * cq, axis=-1, keepdims=True) + RMS_EPS) * qg_ref[...]
    y2 = _dot(nq.astype(BF16), wuq_ref[...])
    half = MLA_HEADS * LANES
    mq_ref[0] = (y2[:, :half] * qa_ref[...] + y2[:, half:] * qb_ref[...]).astype(BF16)
    ckv = _dot(h, w_ref[:, O_CKV:O_PE])
    nk = ckv * lax.rsqrt(jnp.mean(ckv * ckv, axis=-1, keepdims=True) + RMS_EPS) * kvg_ref[...]
    pe = _dot(h, w_ref[:, O_PE:O_END]) * pe_ref[...]
    nkb = nk.astype(BF16)
    cat = jnp.concatenate([nkb, pe.astype(BF16)], axis=-1)
    mk_ref[0] = _dot(cat, wk_ref[...]).astype(BF16)
    mv_ref[0] = _dot(nkb, wv_ref[...]).astype(BF16)


def _inproj_odd(x, sc, sh, weights, gains, tables, ts):
    bsz, s, d = x.shape
    batched = sc.shape[0] > 1
    w, wuq2, wk2, wv = weights
    qg, kvg = gains
    dcos, dsin, qa, qb, pe = tables
    tok = lambda n: pl.BlockSpec((1, ts, n), lambda b, i: (b, i, 0))
    tab = lambda n: pl.BlockSpec((ts, n), lambda b, i: (i, 0))
    widths = (512, 512, 512, 1024, 1024, 512)
    return pl.pallas_call(
        _inproj_odd_kernel,
        grid=(bsz, s // ts),
        in_specs=[tok(d), _mod_spec(d, batched), _mod_spec(d, batched), _const_spec(w.shape),
                  _const_spec(wuq2.shape), _const_spec(wk2.shape), _const_spec(wv.shape),
                  _const_spec((1, MLA_Q_RANK)), _const_spec((1, MLA_KV_RANK)),
                  tab(512), tab(512), tab(1024), tab(1024), tab(128)],
        out_specs=[tok(n) for n in widths],
        out_shape=[jax.ShapeDtypeStruct((bsz, s, n), BF16) for n in widths],
        compiler_params=_cparams(2),
    )(x, sc, sh, w, wuq2, wk2, wv, qg.reshape(1, -1), kvg.reshape(1, -1), dcos, dsin, qa, qb, pe)


def _half_masks():
    lane = lax.broadcasted_iota(jnp.int32, (1, LANES), 1)
    return lane < HEAD_DIM, lane >= HEAD_DIM


def _softmax_values(ss, vs):
    m = functools.reduce(jnp.maximum, [jnp.max(s, axis=-1, keepdims=True) for s in ss])
    es = [jnp.exp2(s - m) for s in ss]
    l = functools.reduce(lambda a, b: a + b, [jnp.sum(e, axis=-1, keepdims=True) for e in es])
    o = functools.reduce(lambda a, b: a + b, [_dot(e.astype(BF16), v) for e, v in zip(es, vs)])
    return o * (1.0 / l)


def _attend(qm, ks, vs):
    return _softmax_values([_dot_nt(qm, k) for k in ks], vs)


NA_RB = 4
NA_WIN_ROWS = NA_RB + NA_WIN_H - 1
NA_SUB = 2


def _na_block_plan(rows):
    plan = []
    for blk in range(rows // NA_RB):
        r0 = blk * NA_RB
        rs = [int(np.clip(r0 + i - NA_WIN_H // 2, 0, rows - NA_WIN_H)) for i in range(NA_RB)]
        ws = int(np.clip(r0 - NA_WIN_H // 2, 0, rows - NA_WIN_ROWS))
        pat = tuple((rs[i] - ws, r0 + i - ws) for i in range(NA_RB))
        assert all(0 <= o and o + NA_WIN_H <= NA_WIN_ROWS for o, _ in pat)
        plan.append(pat)
    assert all(p == plan[1] for p in plan[1:-1])
    return (plan[0], plan[1], plan[-1])


def _na_bias_table(rpb, rows):
    cols = np.arange(GRID_W)
    col_start = np.clip(cols - NA_WIN_W // 2, 0, GRID_W - NA_WIN_W)
    col_mask = (cols[None, :] >= col_start[:, None]) & (cols[None, :] < col_start[:, None] + NA_WIN_W)
    col_idx = np.clip(cols[None, :] - cols[:, None] + NA_WIN_W - 1, 0, 2 * NA_WIN_W - 2)
    wr = np.arange(NA_WIN_ROWS)
    tables = []
    for pat in _na_block_plan(rows):
        off = np.array([o for o, _ in pat])[:, None]
        rq = np.array([r for _, r in pat])[:, None]
        row_ok = (wr[None, :] >= off) & (wr[None, :] < off + NA_WIN_H)
        ridx = np.clip(wr[None, :] - rq + NA_WIN_H - 1, 0, 2 * NA_WIN_H - 2)
        t = rpb.astype(F32)[:, ridx]
        t = t[..., col_idx]
        t = t.transpose(0, 1, 3, 2, 4)
        ok = row_ok[None, :, None, :, None] & col_mask[None, None, :, None, :]
        t = jnp.where(ok, t * LOG2E, NEG_INF)
        tables.append(t.reshape(NA_HEADS, NA_RB * GRID_W, NA_WIN_ROWS * GRID_W))
    return jnp.stack(tables)


def _na_kernel(rows, sub, q_ref, k_ref, v_ref, kc_ref, vc_ref, *refs):
    bt_refs, o_ref = refs[:sub], refs[sub]
    step = pl.program_id(1)
    tq = NA_RB * GRID_W
    m0, m1 = _half_masks()
    for u in range(sub):
        r0 = (step * sub + u) * NA_RB
        ws = jnp.clip(r0 - NA_WIN_H // 2, 0, rows - NA_WIN_ROWS)
        win = pl.ds(pl.multiple_of(ws * GRID_W, GRID_W), NA_WIN_ROWS * GRID_W)
        qrows = slice(u * tq, (u + 1) * tq)
        for j in range(NA_HEADS // 2):
            sl = slice(j * LANES, (j + 1) * LANES)
            qp = q_ref[0, qrows, sl]
            kp, vp = k_ref[0, win, sl], v_ref[0, win, sl]
            kcp, vcp = kc_ref[0, :, sl], vc_ref[0, :, sl]
            outs = []
            for par, msk in ((0, m0), (1, m1)):
                qm = jnp.where(msk, qp, jnp.zeros_like(qp))
                s_loc = _dot_nt(qm, kp) + bt_refs[u][0, 2 * j + par]
                outs.append(_softmax_values([s_loc, _dot_nt(qm, kcp)], [vp, vcp]))
            o_ref[0, qrows, sl] = jnp.where(m0, outs[0], outs[1]).astype(BF16)


def _na_attention(q, k, v, kc, vc, rpb):
    bsz, s, w = q.shape
    l = kc.shape[1]
    rows = s // GRID_W
    nblk = rows // NA_RB
    sub = NA_SUB
    nstep = nblk // sub
    assert rows % NA_RB == 0 and rows >= NA_WIN_ROWS and nblk % sub == 0 and nblk >= 3
    bt = _na_bias_table(rpb, rows)
    tq = NA_RB * GRID_W * sub
    full = lambda n: pl.BlockSpec((1, n, w), lambda b, r: (b, 0, 0))

    def bt_spec(u):
        def kind(b, r):
            blk = r * sub + u
            return ((blk > 0).astype(jnp.int32) + (blk == nblk - 1).astype(jnp.int32), 0, 0, 0)
        return pl.BlockSpec((1,) + bt.shape[1:], kind)

    return pl.pallas_call(
        functools.partial(_na_kernel, rows, sub),
        grid=(bsz, nstep),
        in_specs=[pl.BlockSpec((1, tq, w), lambda b, r: (b, r, 0)),
                  full(s), full(s), full(l), full(l)] + [bt_spec(u) for u in range(sub)],
        out_specs=pl.BlockSpec((1, tq, w), lambda b, r: (b, r, 0)),
        out_shape=jax.ShapeDtypeStruct((bsz, s, w), BF16),
        compiler_params=_cparams(2),
    )(q, k, v, kc, vc, *([bt] * sub))


def _slot_attn_kernel(n_heads, q_slot, has_lat, sub, *refs):
    if has_lat:
        q_ref, k_ref, v_ref, kc_ref, vc_ref, o_ref = refs
    else:
        q_ref, kc_ref, vc_ref, o_ref = refs
    m0, m1 = _half_masks()
    tq = q_ref.shape[1] // sub
    for u in range(sub):
        qrows = slice(u * tq, (u + 1) * tq)
        for j in range(n_heads // 2):
            vsl = slice(j * LANES, (j + 1) * LANES)
            outs = []
            for par, msk in ((0, m0), (1, m1)):
                h = 2 * j + par
                if q_slot:
                    ksl = slice(h * LANES, (h + 1) * LANES)
                    qm = q_ref[0, qrows, ksl]
                else:
                    ksl = vsl
                    qp = q_ref[0, qrows, vsl]
                    qm = jnp.where(msk, qp, jnp.zeros_like(qp))
                ks, vs = [kc_ref[0, :, ksl]], [vc_ref[0, :, vsl]]
                if has_lat:
                    ks.insert(0, k_ref[0, :, ksl])
                    vs.insert(0, v_ref[0, :, vsl])
                outs.append(_attend(qm, ks, vs))
            o_ref[0, qrows, vsl] = jnp.where(m0, outs[0], outs[1]).astype(BF16)


def _slot_attention(q, k, v, kc, vc, n_heads, q_slot, tq, sub=1):
    bsz, sq, wq = q.shape
    l, wk, wv = kc.shape[1], kc.shape[2], vc.shape[2]
    has_lat = k is not None
    qspec = pl.BlockSpec((1, tq, wq), lambda b, i: (b, i, 0))
    full = lambda n, w: pl.BlockSpec((1, n, w), lambda b, i: (b, 0, 0))
    in_specs, args = [qspec], [q]
    if has_lat:
        s = k.shape[1]
        in_specs += [full(s, wk), full(s, wv)]
        args += [k, v]
    in_specs += [full(l, wk), full(l, wv)]
    args += [kc, vc]
    return pl.pallas_call(
        functools.partial(_slot_attn_kernel, n_heads, q_slot, has_lat, sub),
        grid=(bsz, sq // tq),
        in_specs=in_specs,
        out_specs=pl.BlockSpec((1, tq, wv), lambda b, i: (b, i, 0)),
        out_shape=jax.ShapeDtypeStruct((bsz, sq, wv), BF16),
        compiler_params=_cparams(2),
    )(*args)


def _gqa_kernel(has_lat, tq, sub, *refs):
    if has_lat:
        q_ref, k_ref, v_ref, kc_ref, vc_ref, o_ref = refs
    else:
        q_ref, kc_ref, vc_ref, o_ref = refs
    masks = _half_masks()
    for u in range(sub):
        qrows = slice(u * tq, (u + 1) * tq)
        res = {}
        for g in range(GQA_KV_HEADS):
            for var in range(2):
                par = g if var == 0 else 1 - g
                heads = (4 * g + par, 4 * g + 2 + par)
                vsl = slice(var * LANES, (var + 1) * LANES)
                qs = []
                for h in heads:
                    qp = q_ref[0, qrows, (h // 2) * LANES:(h // 2 + 1) * LANES]
                    qs.append(jnp.where(masks[par], qp, jnp.zeros_like(qp)))
                qm = jnp.concatenate(qs, axis=0)
                ks, vs = [kc_ref[0, :, vsl]], [vc_ref[0, :, vsl]]
                if has_lat:
                    ks.insert(0, k_ref[0, :, vsl])
                    vs.insert(0, v_ref[0, :, vsl])
                o = _attend(qm, ks, vs)
                res[heads[0]] = o[:tq]
                res[heads[1]] = o[tq:]
        for j in range(GQA_HEADS // 2):
            o_ref[0, qrows, j * LANES:(j + 1) * LANES] = jnp.where(
                masks[0], res[2 * j], res[2 * j + 1]).astype(BF16)


def _gqa_attention(q, k2, v2, k2c, v2c, tq, sub=1):
    bsz, sq, wq = q.shape
    l = k2c.shape[1]
    has_lat = k2 is not None
    full = lambda n: pl.BlockSpec((1, n, 2 * LANES), lambda b, i: (b, 0, 0))
    in_specs, args = [pl.BlockSpec((1, tq * sub, wq), lambda b, i: (b, i, 0))], [q]
    if has_lat:
        in_specs += [full(k2.shape[1])] * 2
        args += [k2, v2]
    in_specs += [full(l)] * 2
    args += [k2c, v2c]
    return pl.pallas_call(
        functools.partial(_gqa_kernel, has_lat, tq, sub),
        grid=(bsz, sq // (tq * sub)),
        in_specs=in_specs,
        out_specs=pl.BlockSpec((1, tq * sub, wq), lambda b, i: (b, i, 0)),
        out_shape=jax.ShapeDtypeStruct((bsz, sq, wq), BF16),
        compiler_params=_cparams(2),
    )(*args)


def _diff_kernel(has_lat, tq, sub, lam_init, *refs):
    if has_lat:
        q_ref, k_ref, v_ref, kc_ref, vc_ref, lq1, lk1, lq2, lk2, sub_ref, o_ref = refs
    else:
        q_ref, kc_ref, vc_ref, lq1, lk1, lq2, lk2, sub_ref, o_ref = refs
    lam = (jnp.exp(jnp.sum(lq1[...] * lk1[...], axis=-1, keepdims=True))
           - jnp.exp(jnp.sum(lq2[...] * lk2[...], axis=-1, keepdims=True)) + lam_init)
    m0, m1 = _half_masks()
    for u in range(sub):
        qrows = slice(u * tq, (u + 1) * tq)
        for h in range(DIFF_HEADS):
            sl = slice(h * LANES, (h + 1) * LANES)
            qp = q_ref[0, qrows, sl]
            zero = jnp.zeros_like(qp)
            qm = jnp.concatenate([jnp.where(m0, qp, zero), jnp.where(m1, qp, zero)], axis=0)
            ks, vs = [kc_ref[0, :, sl]], [vc_ref[0, :, sl]]
            if has_lat:
                ks.insert(0, k_ref[0, :, sl])
                vs.insert(0, v_ref[0, :, sl])
            o2 = _attend(qm, ks, vs)
            o = o2[:tq] - lam * o2[tq:]
            o = o * lax.rsqrt(jnp.mean(o * o, axis=-1, keepdims=True) + RMS_EPS) * sub_ref[...]
            o_ref[0, qrows, sl] = (o * (1.0 - lam_init)).astype(BF16)


def _diff_attention(q, k, v, kc, vc, lams, subln, lam_init, tq, sub=1):
    bsz, sq, w = q.shape
    l = kc.shape[1]
    has_lat = k is not None
    full = lambda n: pl.BlockSpec((1, n, w), lambda b, i: (b, 0, 0))
    in_specs, args = [pl.BlockSpec((1, tq * sub, w), lambda b, i: (b, i, 0))], [q]
    if has_lat:
        in_specs += [full(k.shape[1])] * 2
        args += [k, v]
    in_specs += [full(l)] * 2 + [_const_spec((1, HEAD_DIM))] * 4 + [_const_spec((1, LANES))]
    args += [kc, vc] + [a.reshape(1, -1).astype(F32) for a in lams] + [subln.reshape(1, -1).astype(F32)]
    return pl.pallas_call(
        functools.partial(_diff_kernel, has_lat, tq, sub, lam_init),
        grid=(bsz, sq // (tq * sub)),
        in_specs=in_specs,
        out_specs=pl.BlockSpec((1, tq * sub, w), lambda b, i: (b, i, 0)),
        out_shape=jax.ShapeDtypeStruct((bsz, sq, w), BF16),
        compiler_params=_cparams(2),
    )(*args)


def _pack_bf16_pairs(v):
    w = v.shape[1] // 2
    hi = pltpu.bitcast(v[:, :w].astype(F32), jnp.int32)
    lo = pltpu.bitcast(v[:, w:].astype(F32), jnp.int32)
    return hi | lax.shift_right_logical(lo, 16)


def _unpack_bf16_pairs(u):
    hi = pltpu.bitcast(u & jnp.int32(-65536), F32)
    lo = pltpu.bitcast(lax.shift_left(u, 16), F32)
    return hi, lo


def _layer_norm(z, g, b):
    mu = jnp.mean(z, axis=-1, keepdims=True)
    zc = z - mu
    var = jnp.mean(zc * zc, axis=-1, keepdims=True)
    return zc * lax.rsqrt(var + LN_EPS) * g + b


def _outproj_kernel(alpha, o1_ref, o2_ref, w1_ref, w2_ref, x_ref, g1_ref, lng_ref, lnb_ref,
                    sc2_ref, sh2_ref, rwh_ref, rwl_ref, rb_ref, tri_ref, cnt0_ref,
                    xo_ref, h2a_ref, h2b_ref, idx_ref, rank_ref, gw_ref, cnt_ref, carry_ref):
    o = _dot(o1_ref[0], w1_ref[...]) + _dot(o2_ref[0], w2_ref[...])
    xn = _layer_norm(alpha * x_ref[0] + g1_ref[0] * o, lng_ref[...], lnb_ref[...])
    xo_ref[0] = xn
    h2 = xn * (1.0 + sc2_ref[0]) + sh2_ref[0]
    hi = h2.astype(BF16)
    packed = _pack_bf16_pairs(hi)
    q = packed.shape[1] // 2
    h2a_ref[0] = packed[:, :q]
    h2b_ref[0] = packed[:, q:]
    lo = (h2 - hi.astype(F32)).astype(BF16)
    logits = (_dot(hi, rwh_ref[...]) + _dot(lo, rwh_ref[...]) + _dot(hi, rwl_ref[...])) + rb_ref[...]
    lane = lax.broadcasted_iota(jnp.int32, logits.shape, 1).astype(F32)
    vals, idxs = [], []
    cur = logits
    for _ in range(TOP_K):
        m = jnp.max(cur, axis=-1, keepdims=True)
        ik = jnp.min(jnp.where(cur == m, lane, float(LANES)), axis=-1, keepdims=True)
        vals.append(m)
        idxs.append(ik)
        cur = jnp.where(lane == ik, -jnp.inf, cur)
    ws = [jnp.exp(v - vals[0]) for v in vals]
    inv = 1.0 / functools.reduce(lambda a, b: a + b, ws)

    @pl.when(jnp.logical_and(pl.program_id(0) == 0, pl.program_id(1) == 0))
    def _():
        carry_ref[...] = cnt0_ref[...]

    sel = [lane == ik for ik in idxs]
    onehot = functools.reduce(lambda a, b: a + b, [m.astype(F32) for m in sel])
    before = _dot(tri_ref[...], onehot.astype(BF16)) + carry_ref[...]
    idx_out = jnp.zeros_like(logits)
    rank_out = jnp.zeros_like(logits)
    w_out = jnp.zeros_like(logits)
    for k in range(TOP_K):
        rk = jnp.sum(jnp.where(sel[k], before, 0.0), axis=-1, keepdims=True)
        idx_out = jnp.where(lane == float(k), idxs[k], idx_out)
        rank_out = jnp.where(lane == float(k), rk, rank_out)
        w_out = jnp.where(lane == float(k), ws[k] * inv, w_out)
    idx_ref[0] = idx_out.astype(jnp.int32)
    rank_ref[0] = rank_out.astype(jnp.int32)
    gw_ref[0] = w_out
    carry_ref[...] += jnp.sum(onehot, axis=0, keepdims=True)
    cnt_ref[...] = carry_ref[...]


def _outproj_ln_router(o1, o2, w_out, x, g1, lng, lnb, sc2, sh2, router, cnt0, alpha, ts):
    bsz, s, d = x.shape
    batched = g1.shape[0] > 1
    rwh, rwl, rb = router
    w1, w2 = w_out[:512].astype(BF16), w_out[512:].astype(BF16)
    tri = jnp.asarray(np.tril(np.ones((ts, ts), np.float32), -1), dtype=BF16)
    tok = lambda n: pl.BlockSpec((1, ts, n), lambda b, i: (b, i, 0))
    ms = _mod_spec(d, batched)
    return pl.pallas_call(
        functools.partial(_outproj_kernel, alpha),
        grid=(bsz, s // ts),
        in_specs=[tok(512), tok(512), _const_spec((512, d)), _const_spec((512, d)), tok(d), ms,
                  _const_spec((1, d)), _const_spec((1, d)), ms, ms,
                  _const_spec((d, LANES)), _const_spec((d, LANES)), _const_spec((1, LANES)),
                  _const_spec((ts, ts)), _const_spec((1, LANES))],
        out_specs=[tok(d), tok(d // 4), tok(d // 4), tok(LANES), tok(LANES), tok(LANES),
                   _const_spec((1, LANES))],
        out_shape=[jax.ShapeDtypeStruct((bsz, s, d), F32),
                   jax.ShapeDtypeStruct((bsz, s, d // 4), jnp.int32),
                   jax.ShapeDtypeStruct((bsz, s, d // 4), jnp.int32),
                   jax.ShapeDtypeStruct((bsz, s, LANES), jnp.int32),
                   jax.ShapeDtypeStruct((bsz, s, LANES), jnp.int32),
                   jax.ShapeDtypeStruct((bsz, s, LANES), F32),
                   jax.ShapeDtypeStruct((1, LANES), F32)],
        scratch_shapes=[pltpu.VMEM((1, LANES), F32)],
        compiler_params=_cparams(2),
    )(o1, o2, w1, w2, x, g1, lng.reshape(1, d), lnb.reshape(1, d), sc2, sh2, rwh, rwl, rb, tri, cnt0)


def _router_weights(router_w, router_b):
    d, e = router_w.shape
    wp = jnp.zeros((d, LANES), F32).at[:, :e].set(router_w)
    hi = wp.astype(BF16)
    lo = (wp - hi.astype(F32)).astype(BF16)
    rb = jnp.full((1, LANES), -jnp.inf, F32).at[0, :e].set(router_b)
    return hi, lo, rb


def _deinterleave_perm():
    p = np.zeros((GU_BLOCK, GU_BLOCK), np.float32)
    m = np.arange(GU_BLOCK // 2)
    p[2 * m, m] = 1.0
    p[2 * m + 1, GU_BLOCK // 2 + m] = 1.0
    return jnp.asarray(p, dtype=BF16)


def _ffn_kernel(te_ref, tv_ref, xa_ref, xb_ref, wgu_ref, bgu_ref, wd_ref, bd_ref, perm_ref, y_ref,
                wgu_s, wd_s):
    j = pl.program_id(0)
    n_blk = wgu_s.shape[1] // GU_BLOCK
    half = GU_BLOCK // 2

    @pl.when(jnp.logical_or(j == 0, te_ref[j] != te_ref[jnp.maximum(j - 1, 0)]))
    def _():
        for b in range(n_blk):
            sl = slice(b * GU_BLOCK, (b + 1) * GU_BLOCK)
            wgu_s[:, sl] = _dot(wgu_ref[0, 0, :, sl].astype(BF16), perm_ref[...]).astype(BF16)
        wd_s[...] = wd_ref[0, 0].astype(BF16)

    @pl.when(tv_ref[j] > 0)
    def _():
        a_hi, a_lo = _unpack_bf16_pairs(xa_ref[...])
        b_hi, b_lo = _unpack_bf16_pairs(xb_ref[...])
        x = jnp.concatenate([a_hi, b_hi, a_lo, b_lo], axis=-1).astype(BF16)
        acts = []
        for b in range(n_blk):
            sl = slice(b * GU_BLOCK, (b + 1) * GU_BLOCK)
            gu = _dot(x, wgu_s[:, sl]) + bgu_ref[0, :, sl]
            glu = jnp.minimum(gu[:, :half], SWIGLU_LIMIT)
            lin = jnp.clip(gu[:, half:], -SWIGLU_LIMIT, SWIGLU_LIMIT)
            acts.append(((lin + 1.0) * (glu * jax.nn.sigmoid(SWIGLU_ALPHA * glu))).astype(BF16))
        a = jnp.concatenate(acts, axis=-1)
        y_ref[...] = (_dot(a, wd_s[...]) + bd_ref[0]).astype(BF16)

    @pl.when(tv_ref[j] == 0)
    def _():
        y_ref[...] = jnp.zeros_like(y_ref)


def _expert_ffn(xsa, xsb, tile_expert, tile_valid, layer, w_gu, b_gu, w_down, b_down, tm):
    p, q = xsa.shape
    d = 4 * q
    _, e, _, f2 = w_gu.shape
    f = f2 // 2
    half = GU_BLOCK // 2
    bgu = jnp.stack([b_gu[:, 0::2].reshape(e, f // half, half),
                     b_gu[:, 1::2].reshape(e, f // half, half)], axis=2).reshape(e, 1, f2)
    wspec = lambda a, b: pl.BlockSpec((1, 1, a, b), lambda j, te, tv: (layer, te[j], 0, 0))
    bspec = lambda b: pl.BlockSpec((1, 1, b), lambda j, te, tv: (te[j], 0, 0))
    return pl.pallas_call(
        _ffn_kernel,
        grid_spec=pltpu.PrefetchScalarGridSpec(
            num_scalar_prefetch=2,
            grid=(p // tm,),
            in_specs=[pl.BlockSpec((tm, q), lambda j, te, tv: (j, 0)),
                      pl.BlockSpec((tm, q), lambda j, te, tv: (j, 0)),
                      wspec(d, f2), bspec(f2), wspec(f, d), bspec(d),
                      pl.BlockSpec((GU_BLOCK, GU_BLOCK), lambda j, te, tv: (0, 0))],
            out_specs=pl.BlockSpec((tm, d), lambda j, te, tv: (j, 0)),
            scratch_shapes=[pltpu.VMEM((d, f2), BF16), pltpu.VMEM((f, d), BF16)]),
        out_shape=jax.ShapeDtypeStruct((p, d), BF16),
        compiler_params=_cparams(1),
    )(tile_expert, tile_valid, xsa, xsb, w_gu, bgu, w_down, b_down.reshape(e, 1, d),
      _deinterleave_perm())


def _moe_plan(idx4, rank4, counts, tm):
    t = idx4.shape[0]
    n_experts = counts.shape[0]
    pc = ((counts + tm - 1) // tm) * tm
    pend = jnp.cumsum(pc)
    pstart = pend - pc
    dpos_t = (pstart[idx4] + rank4).T.astype(jnp.int32)
    p = t * TOP_K + n_experts * tm
    tile_start = jnp.arange(p // tm, dtype=jnp.int32) * tm
    tile_expert = jnp.minimum(jnp.sum((tile_start[:, None] >= pend[None, :]).astype(jnp.int32), axis=1),
                              n_experts - 1).astype(jnp.int32)
    tile_valid = (tile_start < pend[-1]).astype(jnp.int32)
    return dpos_t, p, tile_expert, tile_valid


SC_WINDOW = 128


def _sc_dispatch(rows, dpos_t, p):
    t, w = rows.shape
    k = dpos_t.shape[0]
    mesh = plsc.VectorSubcoreMesh(core_axis_name="core", subcore_axis_name="subcore")

    @functools.partial(pl.kernel, out_type=jax.ShapeDtypeStruct((p, w), rows.dtype), mesh=mesh)
    def kern(x_hbm, *refs):
        i_hbms, o_hbm = refs[:k], refs[k]

        def body(x_vmem, *i_vmems):
            for iv in i_vmems:
                pltpu.sync_copy(x_vmem, o_hbm.at[iv.at[0]])

        pltpu.emit_pipeline(
            body,
            grid=(t // SC_WINDOW,),
            in_specs=[pl.BlockSpec((SC_WINDOW, w), index_map=lambda i: (i, 0))]
            + [pl.BlockSpec((1, SC_WINDOW), index_map=lambda i: (0, i)) for _ in range(k)],
            out_specs=[],
            core_axis_name=("core", "subcore"),
            dimension_semantics=(pltpu.PARALLEL,),
        )(x_hbm, *i_hbms)

    return kern(rows, *[dpos_t[kk:kk + 1] for kk in range(k)])


def _combine_kernel(alpha, yg_ref, gw_ref, x_ref, g2_ref, lng_ref, lnb_ref, o_ref):
    gw = gw_ref[...]
    y = yg_ref[0].astype(F32) * gw[:, 0:1]
    for k in range(1, TOP_K):
        y = y + yg_ref[k].astype(F32) * gw[:, k:k + 1]
    o_ref[0] = _layer_norm(alpha * x_ref[0] + g2_ref[0] * y, lng_ref[...], lnb_ref[...])


def _combine_ln(yg, gw, x, g2, lng, lnb, alpha, row_offset, ts):
    bsz, s, d = x.shape
    batched = g2.shape[0] > 1
    nblk = s // ts
    off = row_offset // ts
    return pl.pallas_call(
        functools.partial(_combine_kernel, alpha),
        grid=(bsz, nblk),
        in_specs=[pl.BlockSpec((TOP_K, ts, d), lambda b, i: (0, off + b * nblk + i, 0)),
                  pl.BlockSpec((ts, LANES), lambda b, i: (off + b * nblk + i, 0)),
                  pl.BlockSpec((1, ts, d), lambda b, i: (b, i, 0)),
                  _mod_spec(d, batched), _const_spec((1, d)), _const_spec((1, d))],
        out_specs=pl.BlockSpec((1, ts, d), lambda b, i: (b, i, 0)),
        out_shape=jax.ShapeDtypeStruct((bsz, s, d), F32),
        compiler_params=_cparams(2),
    )(yg, gw, x, g2, lng.reshape(1, d), lnb.reshape(1, d))


def kernel(x, c, ctx, c_ctx, mod_w, mod_b, ln1_g, ln1_b, ln2_g, ln2_b, even_w_in, even_w_out, na_rpb, gqa_q_gain, gqa_k_gain, odd_w_in, odd_w_out, diff_lq1, diff_lk1, diff_lq2, diff_lk2, diff_subln, mla_q_gain, mla_w_uq, mla_kv_gain, mla_w_ukv, router_w, router_b, exp_w_gu, exp_b_gu, exp_w_down, exp_b_down):
    bsz, s, d = x.shape
    l = ctx.shape[1]
    depth = mod_w.shape[0]
    n_experts = router_w.shape[-1]
    alpha = (2 * depth) ** 0.25
    ts = min(512, s)
    tm = MOE_TILE

    pad = (-(bsz + 1)) % 8
    c_all = jnp.concatenate([c, c_ctx[None, :], jnp.zeros((pad, d), F32)], axis=0)
    mod = _modulation_all(c_all, mod_w, mod_b)

    for i in range(depth):
        last = i == depth - 1
        j = i // 2
        ml = [mod[i, :bsz, k * d:(k + 1) * d].reshape(bsz, 1, d) for k in range(6)]
        mc = [mod[i, bsz:bsz + 1, k * d:(k + 1) * d].reshape(1, 1, d) for k in range(6)]
        sh1, sc1, g1, sh2, sc2, g2 = ml
        csh1, csc1, cg1, csh2, csc2, cg2 = mc

        if i % 2 == 0:
            w = _even_weights(even_w_in[j])
            naq, nak, nav, gq, gk, gv = _inproj_even(
                x, sc1, sh1, w, _even_tables(s, gqa_q_gain[j], gqa_k_gain[j], True), ts)
            cnaq, cnak, cnav, cgq, cgk, cgv = _inproj_even(
                ctx, csc1, csh1, w, _even_tables(l, gqa_q_gain[j], gqa_k_gain[j], False), l)
            o1 = _na_attention(naq, nak, nav, cnak, cnav, na_rpb[j])
            o2 = _gqa_attention(gq, gk, gv, cgk, cgv, 256, ATTN_SUB)
            if not last:
                co1 = _slot_attention(cnaq, None, None, cnak, cnav, NA_HEADS, False, l)
                co2 = _gqa_attention(cgq, None, None, cgk, cgv, l)
            w_out = even_w_out[j]
        else:
            lam_init = 0.8 - 0.6 * math.exp(-0.3 * i)
            weights = _odd_weights(odd_w_in[j], mla_w_uq[j], mla_w_ukv[j])
            gains = (mla_q_gain[j], mla_kv_gain[j])
            dq, dk, dv, mq, mk, mv = _inproj_odd(x, sc1, sh1, weights, gains, _odd_tables(s, True), ts)
            cdq, cdk, cdv, cmq, cmk, cmv = _inproj_odd(ctx, csc1, csh1, weights, gains,
                                                       _odd_tables(l, False), l)
            lams = (diff_lq1[j], diff_lk1[j], diff_lq2[j], diff_lk2[j])
            o1 = _diff_attention(dq, dk, dv, cdk, cdv, lams, diff_subln[j], lam_init, 256, ATTN_SUB)
            o2 = _slot_attention(mq, mk, mv, cmk, cmv, MLA_HEADS, True, 512)
            if not last:
                co1 = _diff_attention(cdq, None, None, cdk, cdv, lams, diff_subln[j], lam_init, l)
                co2 = _slot_attention(cmq, None, None, cmk, cmv, MLA_HEADS, True, l)
            w_out = odd_w_out[j]

        router = _router_weights(router_w[i], router_b[i])
        cnt0 = jnp.zeros((1, LANES), F32)
        x, h2a, h2b, ridx, rrank, rgw, cnt = _outproj_ln_router(
            o1, o2, w_out, x, g1, ln1_g[i], ln1_b[i], sc2, sh2, router, cnt0, alpha, ts)
        flat = lambda a, n: a.reshape(bsz * n, a.shape[-1])
        h2a, h2b, ridx, rrank, rgw = [flat(a, s) for a in (h2a, h2b, ridx, rrank, rgw)]
        if not last:
            ctx, ch2a, ch2b, cidx, crank, cgw, cnt = _outproj_ln_router(
                co1, co2, w_out, ctx, cg1, ln1_g[i], ln1_b[i], csc2, csh2, router, cnt, alpha, l)
            cat = lambda a, b: jnp.concatenate([a, flat(b, l)], axis=0)
            h2a, h2b = cat(h2a, ch2a), cat(h2b, ch2b)
            ridx, rrank, rgw = cat(ridx, cidx), cat(rrank, crank), cat(rgw, cgw)

        counts = cnt[0, :n_experts].astype(jnp.int32)
        dpos_t, p, tile_expert, tile_valid = _moe_plan(ridx[:, :TOP_K], rrank[:, :TOP_K], counts, tm)
        xsa = _sc_dispatch(h2a, dpos_t, p)
        xsb = _sc_dispatch(h2b, dpos_t, p)
        ys = _expert_ffn(xsa, xsb, tile_expert, tile_valid, i, exp_w_gu, exp_b_gu[i],
                         exp_w_down, exp_b_down[i], tm)
        yg = ys.at[dpos_t].get(mode="promise_in_bounds")

        x = _combine_ln(yg, rgw, x, g2, ln2_g[i], ln2_b[i], alpha, 0, ts)
        if not last:
            ctx = _combine_ln(yg, rgw, ctx, cg2, ln2_g[i], ln2_b[i], alpha, bsz * s, l)
    return x
```

```python
import functools
import math

import numpy as np
import jax
import jax.numpy as jnp
from jax import lax
from jax.experimental import pallas as pl
from jax.experimental.pallas import tpu as pltpu
from jax.experimental.pallas import tpu_sc as plsc

F32 = jnp.float32
BF16 = jnp.bfloat16

GRID_W = 64
HEAD_DIM = 64
ROPE_THETA = 10000.0
LN_EPS = 1e-6
RMS_EPS = 1e-6
NEG_INF = -1e30
NA_HEADS = 8
NA_WIN_H = 8
NA_WIN_W = 16
GQA_HEADS = 8
GQA_KV_HEADS = 2
DIFF_HEADS = 4
MLA_HEADS = 8
MLA_Q_RANK = 256
MLA_KV_RANK = 128
MLA_NOPE = 64
MLA_ROPE = 32
MLA_V = 64
MLA_QK = MLA_NOPE + MLA_ROPE
TOP_K = 4
SWIGLU_ALPHA = 1.702
SWIGLU_LIMIT = 7.0
LOG2E = 1.4426950408889634

LANES = 128
VMEM_LIMIT = 56 * 1024 * 1024
MOE_TILE = 1024
GU_BLOCK = 512
ATTN_SUB = 2


def _cparams(n_axes):
    return pltpu.CompilerParams(dimension_semantics=("arbitrary",) * n_axes,
                                vmem_limit_bytes=VMEM_LIMIT)


def _dot(a, b):
    return jnp.dot(a, b, preferred_element_type=F32)


def _dot_nt(a, b):
    return lax.dot_general(a, b, (((1,), (1,)), ((), ())), preferred_element_type=F32)


def _mod_kernel(c_ref, w_ref, b_ref, o_ref):
    cv = c_ref[...]
    a = (cv * jax.nn.sigmoid(cv)).astype(BF16)
    o_ref[0] = _dot(a, w_ref[0].astype(BF16)) + b_ref[0]


def _modulation_all(c_all, mod_w, mod_b):
    depth, d, n = mod_w.shape
    rows = c_all.shape[0]
    tn = 1536
    return pl.pallas_call(
        _mod_kernel,
        grid=(depth, n // tn),
        in_specs=[pl.BlockSpec((rows, d), lambda i, j: (0, 0)),
                  pl.BlockSpec((1, d, tn), lambda i, j: (i, 0, j)),
                  pl.BlockSpec((1, 1, tn), lambda i, j: (i, 0, j))],
        out_specs=pl.BlockSpec((1, rows, tn), lambda i, j: (i, 0, j)),
        out_shape=jax.ShapeDtypeStruct((depth, rows, n), F32),
        compiler_params=_cparams(2),
    )(c_all, mod_w, mod_b.reshape(depth, 1, n))


def _rope_tables(s, dim):
    pos = jnp.arange(s)
    row = (pos // GRID_W).astype(F32)[:, None]
    col = (pos % GRID_W).astype(F32)[:, None]
    quarter = dim // 4
    inv_freq = ROPE_THETA ** (-jnp.arange(quarter, dtype=F32) / quarter)
    ar, ac = row * inv_freq, col * inv_freq
    cos = jnp.concatenate([jnp.cos(ar), jnp.cos(ar), jnp.cos(ac), jnp.cos(ac)], axis=-1)
    sin = jnp.concatenate([-jnp.sin(ar), jnp.sin(ar), -jnp.sin(ac), jnp.sin(ac)], axis=-1)
    return cos, sin


def _swap_perm(dim):
    q = dim // 4
    idx = np.arange(dim)
    return np.where((idx % (2 * q)) < q, idx + q, idx - q)


def _rope_partner(y, dim):
    q = dim // 4
    lane = lax.broadcasted_iota(jnp.int32, (1, LANES), 1)
    first = (lane % (2 * q)) < q
    tiles = []
    for j in range(0, y.shape[1], LANES):
        t = y[:, j:j + LANES]
        tiles.append(jnp.where(first, pltpu.roll(t, LANES - q, 1), pltpu.roll(t, q, 1)))
    return tiles[0] if len(tiles) == 1 else jnp.concatenate(tiles, axis=-1)


def _swap_cols(w, dim):
    n = w.shape[-1] // dim
    perm = (np.arange(n)[:, None] * dim + _swap_perm(dim)[None, :]).reshape(-1)
    return w[..., perm]


def _block_ones(n, blk):
    i = np.arange(n) // blk
    return jnp.asarray((i[:, None] == i[None, :]).astype(np.float32), dtype=BF16)


E_NAQ, E_NAK, E_NAV, E_GQ, E_GK, E_GV, E_END = (0, 512, 1024, 1536, 2048, 2304, 2560)


def _even_weights(w_in):
    naq, nak, nav, gq, gk, gv = jnp.split(w_in, [512, 1024, 1536, 2048, 2176], axis=-1)
    k0, k1 = gk[:, :64], gk[:, 64:]
    v0, v1 = gv[:, :64], gv[:, 64:]
    gk2 = jnp.concatenate([k0, k1, k1, k0], axis=-1)
    gv2 = jnp.concatenate([v0, v1, v1, v0], axis=-1)
    w = jnp.concatenate([naq * (0.125 * LOG2E), nak, nav, gq, gk2, gv2], axis=-1).astype(BF16)
    return w


def _even_tables(s, q_gain, k_gain, rope):
    sw = _swap_perm(64)
    if rope:
        cos, sin = _rope_tables(s, 64)
    else:
        cos, sin = jnp.ones((s, 64), F32), jnp.zeros((s, 64), F32)
    qa = jnp.tile(cos * q_gain[None, :] * (0.125 * LOG2E), (1, 8))
    qb = jnp.tile(sin * q_gain[sw][None, :] * (0.125 * LOG2E), (1, 8))
    ka = jnp.tile(cos * k_gain[None, :], (1, 4))
    kb = jnp.tile(sin * k_gain[sw][None, :], (1, 4))
    return qa, qb, ka, kb


def _inproj_even_kernel(x_ref, sc_ref, sh_ref, w_ref, g512_ref, g256_ref,
                        qa_ref, qb_ref, ka_ref, kb_ref,
                        naq_ref, nak_ref, nav_ref, gq_ref, gk_ref, gv_ref):
    h = (x_ref[0] * (1.0 + sc_ref[0]) + sh_ref[0]).astype(BF16)
    naq_ref[0] = _dot(h, w_ref[:, E_NAQ:E_NAK]).astype(BF16)
    nak_ref[0] = _dot(h, w_ref[:, E_NAK:E_NAV]).astype(BF16)
    nav_ref[0] = _dot(h, w_ref[:, E_NAV:E_GQ]).astype(BF16)
    gv_ref[0] = _dot(h, w_ref[:, E_GV:E_END]).astype(BF16)
    y = _dot(h, w_ref[:, E_GQ:E_GK])
    ys = _rope_partner(y, HEAD_DIM)
    r = lax.rsqrt(_dot((y * y).astype(BF16), g512_ref[...]) * (1.0 / HEAD_DIM) + RMS_EPS)
    gq_ref[0] = (r * (y * qa_ref[...] + ys * qb_ref[...])).astype(BF16)
    y = _dot(h, w_ref[:, E_GK:E_GV])
    ys = _rope_partner(y, HEAD_DIM)
    r = lax.rsqrt(_dot((y * y).astype(BF16), g256_ref[...]) * (1.0 / HEAD_DIM) + RMS_EPS)
    gk_ref[0] = (r * (y * ka_ref[...] + ys * kb_ref[...])).astype(BF16)


def _mod_spec(d, batched):
    if batched:
        return pl.BlockSpec((1, 1, d), lambda b, s: (b, 0, 0))
    return pl.BlockSpec((1, 1, d), lambda b, s: (0, 0, 0))


def _const_spec(shape):
    nd = len(shape)
    return pl.BlockSpec(shape, lambda b, s: (0,) * nd)


def _inproj_even(x, sc, sh, w, tables, ts):
    bsz, s, d = x.shape
    batched = sc.shape[0] > 1
    qa, qb, ka, kb = tables
    tok = lambda n: pl.BlockSpec((1, ts, n), lambda b, i: (b, i, 0))
    tab = lambda n: pl.BlockSpec((ts, n), lambda b, i: (i, 0))
    widths = (512, 512, 512, 512, 256, 256)
    return pl.pallas_call(
        _inproj_even_kernel,
        grid=(bsz, s // ts),
        in_specs=[tok(d), _mod_spec(d, batched), _mod_spec(d, batched), _const_spec(w.shape),
                  _const_spec((512, 512)), _const_spec((256, 256)),
                  tab(512), tab(512), tab(256), tab(256)],
        out_specs=[tok(n) for n in widths],
        out_shape=[jax.ShapeDtypeStruct((bsz, s, n), BF16) for n in widths],
        compiler_params=_cparams(2),
    )(x, sc, sh, w, _block_ones(512, 64), _block_ones(256, 64), qa, qb, ka, kb)


O_DQ, O_DK, O_DV, O_CQ, O_CKV, O_PE, O_END = (0, 512, 1024, 1536, 1792, 1920, 2048)


def _odd_weights(w_in, w_uq, w_ukv):
    dq, dk, dv, cq, ckv, kpe = jnp.split(w_in, [512, 1024, 1536, 1792, 1920], axis=-1)
    d = w_in.shape[0]
    pe_slot = jnp.concatenate([kpe, _swap_cols(kpe, 32), jnp.zeros((d, 64), F32)], axis=-1)
    w = jnp.concatenate([dq, dk, dv, cq, ckv, pe_slot], axis=-1).astype(BF16)
    uq = w_uq.reshape(MLA_Q_RANK, MLA_HEADS, MLA_QK)
    z32 = jnp.zeros((MLA_Q_RANK, MLA_HEADS, 32), F32)
    z64 = jnp.zeros((MLA_Q_RANK, MLA_HEADS, 64), F32)
    uq_pad = jnp.concatenate([uq, z32], axis=-1).reshape(MLA_Q_RANK, MLA_HEADS * LANES)
    uq_sw = jnp.concatenate([z64, _swap_cols(uq[..., MLA_NOPE:], 32), z32],
                            axis=-1).reshape(MLA_Q_RANK, MLA_HEADS * LANES)
    wuq2 = jnp.concatenate([uq_pad, uq_sw], axis=-1).astype(BF16)
    ukv = w_ukv.reshape(MLA_KV_RANK, MLA_HEADS, MLA_NOPE + MLA_V)
    zk = jnp.zeros((MLA_KV_RANK, MLA_HEADS, 64), F32)
    wk_pad = jnp.concatenate([ukv[..., :MLA_NOPE], zk], axis=-1).reshape(MLA_KV_RANK, MLA_HEADS * LANES)
    place = np.zeros((LANES, MLA_HEADS, LANES), np.float32)
    for j in range(MLA_ROPE):
        place[j, :, MLA_NOPE + j] = 1.0
        place[MLA_ROPE + j, :, MLA_NOPE + j] = 1.0
    wk2 = jnp.concatenate([wk_pad, jnp.asarray(place.reshape(LANES, MLA_HEADS * LANES))],
                          axis=0).astype(BF16)
    wv = ukv[..., MLA_NOPE:].reshape(MLA_KV_RANK, MLA_HEADS * MLA_V).astype(BF16)
    return w, wuq2, wk2, wv


def _odd_tables(s, rope):
    m_scale = MLA_QK ** -0.5 * LOG2E
    if rope:
        cos64, sin64 = _rope_tables(s, 64)
        cos32, sin32 = _rope_tables(s, 32)
    else:
        cos64, sin64 = jnp.ones((s, 64), F32), jnp.zeros((s, 64), F32)
        cos32, sin32 = jnp.ones((s, 32), F32), jnp.zeros((s, 32), F32)
    one64, z32, z64 = jnp.ones((s, 64), F32), jnp.zeros((s, 32), F32), jnp.zeros((s, 64), F32)
    dcos, dsin = jnp.tile(cos64, (1, 8)), jnp.tile(sin64, (1, 8))
    qa = jnp.tile(jnp.concatenate([one64, cos32, z32], axis=-1) * m_scale, (1, MLA_HEADS))
    qb = jnp.tile(jnp.concatenate([z64, sin32, z32], axis=-1) * m_scale, (1, MLA_HEADS))
    pe = jnp.concatenate([cos32, sin32, z64], axis=-1)
    return dcos, dsin, qa, qb, pe


def _inproj_odd_kernel(x_ref, sc_ref, sh_ref, w_ref, wuq_ref, wk_ref, wv_ref, qg_ref, kvg_ref,
                       dcos_ref, dsin_ref, qa_ref, qb_ref, pe_ref,
                       dq_ref, dk_ref, dv_ref, mq_ref, mk_ref, mv_ref):
    h = (x_ref[0] * (1.0 + sc_ref[0]) + sh_ref[0]).astype(BF16)
    dcos, dsin = dcos_ref[...], dsin_ref[...]
    y = _dot(h, w_ref[:, O_DQ:O_DK])
    ys = _rope_partner(y, HEAD_DIM)
    dq_ref[0] = ((y * dcos + ys * dsin) * (0.125 * LOG2E)).astype(BF16)
    y = _dot(h, w_ref[:, O_DK:O_DV])
    ys = _rope_partner(y, HEAD_DIM)
    dk_ref[0] = (y * dcos + ys * dsin).astype(BF16)
    dv_ref[0] = _dot(h, w_ref[:, O_DV:O_CQ]).astype(BF16)
    cq = _dot(h, w_ref[:, O_CQ:O_CKV])
    nq = cq * lax.rsqrt(jnp.mean(cq * cq, axis=-1, keepdims=True) + RMS_EPS) * qg_ref[...]
    y2 = _dot(nq.astype(BF16), wuq_ref[...])
    half = MLA_HEADS * LANES
    mq_ref[0] = (y2[:, :half] * qa_ref[...] + y2[:, half:] * qb_ref[...]).astype(BF16)
    ckv = _dot(h, w_ref[:, O_CKV:O_PE])
    nk = ckv * lax.rsqrt(jnp.mean(ckv * ckv, axis=-1, keepdims=True) + RMS_EPS) * kvg_ref[...]
    pe = _dot(h, w_ref[:, O_PE:O_END]) * pe_ref[...]
    nkb = nk.astype(BF16)
    cat = jnp.concatenate([nkb, pe.astype(BF16)], axis=-1)
    mk_ref[0] = _dot(cat, wk_ref[...]).astype(BF16)
    mv_ref[0] = _dot(nkb, wv_ref[...]).astype(BF16)


def _inproj_odd(x, sc, sh, weights, gains, tables, ts):
    bsz, s, d = x.shape
    batched = sc.shape[0] > 1
    w, wuq2, wk2, wv = weights
    qg, kvg = gains
    dcos, dsin, qa, qb, pe = tables
    tok = lambda n: pl.BlockSpec((1, ts, n), lambda b, i: (b, i, 0))
    tab = lambda n: pl.BlockSpec((ts, n), lambda b, i: (i, 0))
    widths = (512, 512, 512, 1024, 1024, 512)
    return pl.pallas_call(
        _inproj_odd_kernel,
        grid=(bsz, s // ts),
        in_specs=[tok(d), _mod_spec(d, batched), _mod_spec(d, batched), _const_spec(w.shape),
                  _const_spec(wuq2.shape), _const_spec(wk2.shape), _const_spec(wv.shape),
                  _const_spec((1, MLA_Q_RANK)), _const_spec((1, MLA_KV_RANK)),
                  tab(512), tab(512), tab(1024), tab(1024), tab(128)],
        out_specs=[tok(n) for n in widths],
        out_shape=[jax.ShapeDtypeStruct((bsz, s, n), BF16) for n in widths],
        compiler_params=_cparams(2),
    )(x, sc, sh, w, wuq2, wk2, wv, qg.reshape(1, -1), kvg.reshape(1, -1), dcos, dsin, qa, qb, pe)


def _half_masks():
    lane = lax.broadcasted_iota(jnp.int32, (1, LANES), 1)
    return lane < HEAD_DIM, lane >= HEAD_DIM


def _softmax_values(ss, vs):
    m = functools.reduce(jnp.maximum, [jnp.max(s, axis=-1, keepdims=True) for s in ss])
    es = [jnp.exp2(s - m) for s in ss]
    l = functools.reduce(lambda a, b: a + b, [jnp.sum(e, axis=-1, keepdims=True) for e in es])
    o = functools.reduce(lambda a, b: a + b, [_dot(e.astype(BF16), v) for e, v in zip(es, vs)])
    return o * (1.0 / l)


def _attend(qm, ks, vs):
    return _softmax_values([_dot_nt(qm, k) for k in ks], vs)


NA_RB = 4
NA_WIN_ROWS = NA_RB + NA_WIN_H - 1
NA_SUB = 2


def _na_block_plan(rows):
    plan = []
    for blk in range(rows // NA_RB):
        r0 = blk * NA_RB
        rs = [int(np.clip(r0 + i - NA_WIN_H // 2, 0, rows - NA_WIN_H)) for i in range(NA_RB)]
        ws = int(np.clip(r0 - NA_WIN_H // 2, 0, rows - NA_WIN_ROWS))
        pat = tuple((rs[i] - ws, r0 + i - ws) for i in range(NA_RB))
        assert all(0 <= o and o + NA_WIN_H <= NA_WIN_ROWS for o, _ in pat)
        plan.append(pat)
    assert all(p == plan[1] for p in plan[1:-1])
    return (plan[0], plan[1], plan[-1])


def _na_bias_table(rpb, rows):
    cols = np.arange(GRID_W)
    col_start = np.clip(cols - NA_WIN_W // 2, 0, GRID_W - NA_WIN_W)
    col_mask = (cols[None, :] >= col_start[:, None]) & (cols[None, :] < col_start[:, None] + NA_WIN_W)
    col_idx = np.clip(cols[None, :] - cols[:, None] + NA_WIN_W - 1, 0, 2 * NA_WIN_W - 2)
    wr = np.arange(NA_WIN_ROWS)
    tables = []
    for pat in _na_block_plan(rows):
        off = np.array([o for o, _ in pat])[:, None]
        rq = np.array([r for _, r in pat])[:, None]
        row_ok = (wr[None, :] >= off) & (wr[None, :] < off + NA_WIN_H)
        ridx = np.clip(wr[None, :] - rq + NA_WIN_H - 1, 0, 2 * NA_WIN_H - 2)
        t = rpb.astype(F32)[:, ridx]
        t = t[..., col_idx]
        t = t.transpose(0, 1, 3, 2, 4)
        ok = row_ok[None, :, None, :, None] & col_mask[None, None, :, None, :]
        t = jnp.where(ok, t * LOG2E, NEG_INF)
        tables.append(t.reshape(NA_HEADS, NA_RB * GRID_W, NA_WIN_ROWS * GRID_W))
    return jnp.stack(tables)


def _na_kernel(rows, sub, q_ref, k_ref, v_ref, kc_ref, vc_ref, *refs):
    bt_refs, o_ref = refs[:sub], refs[sub]
    step = pl.program_id(1)
    tq = NA_RB * GRID_W
    m0, m1 = _half_masks()
    for u in range(sub):
        r0 = (step * sub + u) * NA_RB
        ws = jnp.clip(r0 - NA_WIN_H // 2, 0, rows - NA_WIN_ROWS)
        win = pl.ds(pl.multiple_of(ws * GRID_W, GRID_W), NA_WIN_ROWS * GRID_W)
        qrows = slice(u * tq, (u + 1) * tq)
        for j in range(NA_HEADS // 2):
            sl = slice(j * LANES, (j + 1) * LANES)
            qp = q_ref[0, qrows, sl]
            kp, vp = k_ref[0, win, sl], v_ref[0, win, sl]
            kcp, vcp = kc_ref[0, :, sl], vc_ref[0, :, sl]
            outs = []
            for par, msk in ((0, m0), (1, m1)):
                qm = jnp.where(msk, qp, jnp.zeros_like(qp))
                s_loc = _dot_nt(qm, kp) + bt_refs[u][0, 2 * j + par]
                outs.append(_softmax_values([s_loc, _dot_nt(qm, kcp)], [vp, vcp]))
            o_ref[0, qrows, sl] = jnp.where(m0, outs[0], outs[1]).astype(BF16)


def _na_attention(q, k, v, kc, vc, rpb):
    bsz, s, w = q.shape
    l = kc.shape[1]
    rows = s // GRID_W
    nblk = rows // NA_RB
    sub = NA_SUB
    nstep = nblk // sub
    assert rows % NA_RB == 0 and rows >= NA_WIN_ROWS and nblk % sub == 0 and nblk >= 3
    bt = _na_bias_table(rpb, rows)
    tq = NA_RB * GRID_W * sub
    full = lambda n: pl.BlockSpec((1, n, w), lambda b, r: (b, 0, 0))

    def bt_spec(u):
        def kind(b, r):
            blk = r * sub + u
            return ((blk > 0).astype(jnp.int32) + (blk == nblk - 1).astype(jnp.int32), 0, 0, 0)
        return pl.BlockSpec((1,) + bt.shape[1:], kind)

    return pl.pallas_call(
        functools.partial(_na_kernel, rows, sub),
        grid=(bsz, nstep),
        in_specs=[pl.BlockSpec((1, tq, w), lambda b, r: (b, r, 0)),
                  full(s), full(s), full(l), full(l)] + [bt_spec(u) for u in range(sub)],
        out_specs=pl.BlockSpec((1, tq, w), lambda b, r: (b, r, 0)),
        out_shape=jax.ShapeDtypeStruct((bsz, s, w), BF16),
        compiler_params=_cparams(2),
    )(q, k, v, kc, vc, *([bt] * sub))


def _slot_attn_kernel(n_heads, q_slot, has_lat, sub, *refs):
    if has_lat:
        q_ref, k_ref, v_ref, kc_ref, vc_ref, o_ref = refs
    else:
        q_ref, kc_ref, vc_ref, o_ref = refs
    m0, m1 = _half_masks()
    tq = q_ref.shape[1] // sub
    for u in range(sub):
        qrows = slice(u * tq, (u + 1) * tq)
        for j in range(n_heads // 2):
            vsl = slice(j * LANES, (j + 1) * LANES)
            outs = []
            for par, msk in ((0, m0), (1, m1)):
                h = 2 * j + par
                if q_slot:
                    ksl = slice(h * LANES, (h + 1) * LANES)
                    qm = q_ref[0, qrows, ksl]
                else:
                    ksl = vsl
                    qp = q_ref[0, qrows, vsl]
                    qm = jnp.where(msk, qp, jnp.zeros_like(qp))
                ks, vs = [kc_ref[0, :, ksl]], [vc_ref[0, :, vsl]]
                if has_lat:
                    ks.insert(0, k_ref[0, :, ksl])
                    vs.insert(0, v_ref[0, :, vsl])
                outs.append(_attend(qm, ks, vs))
            o_ref[0, qrows, vsl] = jnp.where(m0, outs[0], outs[1]).astype(BF16)


def _slot_attention(q, k, v, kc, vc, n_heads, q_slot, tq, sub=1):
    bsz, sq, wq = q.shape
    l, wk, wv = kc.shape[1], kc.shape[2], vc.shape[2]
    has_lat = k is not None
    qspec = pl.BlockSpec((1, tq, wq), lambda b, i: (b, i, 0))
    full = lambda n, w: pl.BlockSpec((1, n, w), lambda b, i: (b, 0, 0))
    in_specs, args = [qspec], [q]
    if has_lat:
        s = k.shape[1]
        in_specs += [full(s, wk), full(s, wv)]
        args += [k, v]
    in_specs += [full(l, wk), full(l, wv)]
    args += [kc, vc]
    return pl.pallas_call(
        functools.partial(_slot_attn_kernel, n_heads, q_slot, has_lat, sub),
        grid=(bsz, sq // tq),
        in_specs=in_specs,
        out_specs=pl.BlockSpec((1, tq, wv), lambda b, i: (b, i, 0)),
        out_shape=jax.ShapeDtypeStruct((bsz, sq, wv), BF16),
        compiler_params=_cparams(2),
    )(*args)


def _gqa_kernel(has_lat, tq, sub, *refs):
    if has_lat:
        q_ref, k_ref, v_ref, kc_ref, vc_ref, o_ref = refs
    else:
        q_ref, kc_ref, vc_ref, o_ref = refs
    masks = _half_masks()
    for u in range(sub):
        qrows = slice(u * tq, (u + 1) * tq)
        res = {}
        for g in range(GQA_KV_HEADS):
            for var in range(2):
                par = g if var == 0 else 1 - g
                heads = (4 * g + par, 4 * g + 2 + par)
                vsl = slice(var * LANES, (var + 1) * LANES)
                qs = []
                for h in heads:
                    qp = q_ref[0, qrows, (h // 2) * LANES:(h // 2 + 1) * LANES]
                    qs.append(jnp.where(masks[par], qp, jnp.zeros_like(qp)))
                qm = jnp.concatenate(qs, axis=0)
                ks, vs = [kc_ref[0, :, vsl]], [vc_ref[0, :, vsl]]
                if has_lat:
                    ks.insert(0, k_ref[0, :, vsl])
                    vs.insert(0, v_ref[0, :, vsl])
                o = _attend(qm, ks, vs)
                res[heads[0]] = o[:tq]
                res[heads[1]] = o[tq:]
        for j in range(GQA_HEADS // 2):
            o_ref[0, qrows, j * LANES:(j + 1) * LANES] = jnp.where(
                masks[0], res[2 * j], res[2 * j + 1]).astype(BF16)


def _gqa_attention(q, k2, v2, k2c, v2c, tq, sub=1):
    bsz, sq, wq = q.shape
    l = k2c.shape[1]
    has_lat = k2 is not None
    full = lambda n: pl.BlockSpec((1, n, 2 * LANES), lambda b, i: (b, 0, 0))
    in_specs, args = [pl.BlockSpec((1, tq * sub, wq), lambda b, i: (b, i, 0))], [q]
    if has_lat:
        in_specs += [full(k2.shape[1])] * 2
        args += [k2, v2]
    in_specs += [full(l)] * 2
    args += [k2c, v2c]
    return pl.pallas_call(
        functools.partial(_gqa_kernel, has_lat, tq, sub),
        grid=(bsz, sq // (tq * sub)),
        in_specs=in_specs,
        out_specs=pl.BlockSpec((1, tq * sub, wq), lambda b, i: (b, i, 0)),
        out_shape=jax.ShapeDtypeStruct((bsz, sq, wq), BF16),
        compiler_params=_cparams(2),
    )(*args)


def _diff_kernel(has_lat, tq, sub, lam_init, *refs):
    if has_lat:
        q_ref, k_ref, v_ref, kc_ref, vc_ref, lq1, lk1, lq2, lk2, sub_ref, o_ref = refs
    else:
        q_ref, kc_ref, vc_ref, lq1, lk1, lq2, lk2, sub_ref, o_ref = refs
    lam = (jnp.exp(jnp.sum(lq1[...] * lk1[...], axis=-1, keepdims=True))
           - jnp.exp(jnp.sum(lq2[...] * lk2[...], axis=-1, keepdims=True)) + lam_init)
    m0, m1 = _half_masks()
    for u in range(sub):
        qrows = slice(u * tq, (u + 1) * tq)
        for h in range(DIFF_HEADS):
            sl = slice(h * LANES, (h + 1) * LANES)
            qp = q_ref[0, qrows, sl]
            zero = jnp.zeros_like(qp)
            qm = jnp.concatenate([jnp.where(m0, qp, zero), jnp.where(m1, qp, zero)], axis=0)
            ks, vs = [kc_ref[0, :, sl]], [vc_ref[0, :, sl]]
            if has_lat:
                ks.insert(0, k_ref[0, :, sl])
                vs.insert(0, v_ref[0, :, sl])
            o2 = _attend(qm, ks, vs)
            o = o2[:tq] - lam * o2[tq:]
            o = o * lax.rsqrt(jnp.mean(o * o, axis=-1, keepdims=True) + RMS_EPS) * sub_ref[...]
            o_ref[0, qrows, sl] = (o * (1.0 - lam_init)).astype(BF16)


def _diff_attention(q, k, v, kc, vc, lams, subln, lam_init, tq, sub=1):
    bsz, sq, w = q.shape
    l = kc.shape[1]
    has_lat = k is not None
    full = lambda n: pl.BlockSpec((1, n, w), lambda b, i: (b, 0, 0))
    in_specs, args = [pl.BlockSpec((1, tq * sub, w), lambda b, i: (b, i, 0))], [q]
    if has_lat:
        in_specs += [full(k.shape[1])] * 2
        args += [k, v]
    in_specs += [full(l)] * 2 + [_const_spec((1, HEAD_DIM))] * 4 + [_const_spec((1, LANES))]
    args += [kc, vc] + [a.reshape(1, -1).astype(F32) for a in lams] + [subln.reshape(1, -1).astype(F32)]
    return pl.pallas_call(
        functools.partial(_diff_kernel, has_lat, tq, sub, lam_init),
        grid=(bsz, sq // (tq * sub)),
        in_specs=in_specs,
        out_specs=pl.BlockSpec((1, tq * sub, w), lambda b, i: (b, i, 0)),
        out_shape=jax.ShapeDtypeStruct((bsz, sq, w), BF16),
        compiler_params=_cparams(2),
    )(*args)


def _pack_bf16_pairs(v):
    w = v.shape[1] // 2
    hi = pltpu.bitcast(v[:, :w].astype(F32), jnp.int32)
    lo = pltpu.bitcast(v[:, w:].astype(F32), jnp.int32)
    return hi | lax.shift_right_logical(lo, 16)


def _unpack_bf16_pairs(u):
    hi = pltpu.bitcast(u & jnp.int32(-65536), F32)
    lo = pltpu.bitcast(lax.shift_left(u, 16), F32)
    return hi, lo


def _layer_norm(z, g, b):
    mu = jnp.mean(z, axis=-1, keepdims=True)
    zc = z - mu
    var = jnp.mean(zc * zc, axis=-1, keepdims=True)
    return zc * lax.rsqrt(var + LN_EPS) * g + b


def _outproj_kernel(alpha, sub, o1_ref, o2_ref, w1_ref, w2_ref, x_ref, g1_ref, lng_ref, lnb_ref,
                    sc2_ref, sh2_ref, rwh_ref, rwl_ref, rb_ref, tri_ref, cnt0_ref,
                    xo_ref, h2a_ref, h2b_ref, idx_ref, rank_ref, gw_ref, cnt_ref, carry_ref):
    @pl.when(jnp.logical_and(pl.program_id(0) == 0, pl.program_id(1) == 0))
    def _():
        carry_ref[...] = cnt0_ref[...]

    tr = x_ref.shape[1] // sub
    carry = carry_ref[...]
    for u in range(sub):
        rows = slice(u * tr, (u + 1) * tr)
        o = _dot(o1_ref[0, rows], w1_ref[...]) + _dot(o2_ref[0, rows], w2_ref[...])
        xn = _layer_norm(alpha * x_ref[0, rows] + g1_ref[0] * o, lng_ref[...], lnb_ref[...])
        xo_ref[0, rows] = xn
        h2 = xn * (1.0 + sc2_ref[0]) + sh2_ref[0]
        hi = h2.astype(BF16)
        packed = _pack_bf16_pairs(hi)
        q = packed.shape[1] // 2
        h2a_ref[0, rows] = packed[:, :q]
        h2b_ref[0, rows] = packed[:, q:]
        lo = (h2 - hi.astype(F32)).astype(BF16)
        logits = (_dot(hi, rwh_ref[...]) + _dot(lo, rwh_ref[...]) + _dot(hi, rwl_ref[...])) + rb_ref[...]
        lane = lax.broadcasted_iota(jnp.int32, logits.shape, 1).astype(F32)
        vals, idxs = [], []
        cur = logits
        for _ in range(TOP_K):
            m = jnp.max(cur, axis=-1, keepdims=True)
            ik = jnp.min(jnp.where(cur == m, lane, float(LANES)), axis=-1, keepdims=True)
            vals.append(m)
            idxs.append(ik)
            cur = jnp.where(lane == ik, -jnp.inf, cur)
        ws = [jnp.exp(v - vals[0]) for v in vals]
        inv = 1.0 / functools.reduce(lambda a, b: a + b, ws)
        sel = [lane == ik for ik in idxs]
        onehot = functools.reduce(lambda a, b: a + b, [m.astype(F32) for m in sel])
        before = _dot(tri_ref[...], onehot.astype(BF16)) + carry
        idx_out = jnp.zeros_like(logits)
        rank_out = jnp.zeros_like(logits)
        w_out = jnp.zeros_like(logits)
        for k in range(TOP_K):
            rk = jnp.sum(jnp.where(sel[k], before, 0.0), axis=-1, keepdims=True)
            idx_out = jnp.where(lane == float(k), idxs[k], idx_out)
            rank_out = jnp.where(lane == float(k), rk, rank_out)
            w_out = jnp.where(lane == float(k), ws[k] * inv, w_out)
        idx_ref[0, rows] = idx_out.astype(jnp.int32)
        rank_ref[0, rows] = rank_out.astype(jnp.int32)
        gw_ref[0, rows] = w_out
        carry = carry + jnp.sum(onehot, axis=0, keepdims=True)
    carry_ref[...] = carry
    cnt_ref[...] = carry


def _outproj_ln_router(o1, o2, w_out, x, g1, lng, lnb, sc2, sh2, router, cnt0, alpha, ts):
    bsz, s, d = x.shape
    batched = g1.shape[0] > 1
    rwh, rwl, rb = router
    w1, w2 = w_out[:512].astype(BF16), w_out[512:].astype(BF16)
    sub = 2 if ts % 512 == 0 else 1
    tr = ts // sub
    tri = jnp.asarray(np.tril(np.ones((tr, tr), np.float32), -1), dtype=BF16)
    tok = lambda n: pl.BlockSpec((1, ts, n), lambda b, i: (b, i, 0))
    ms = _mod_spec(d, batched)
    return pl.pallas_call(
        functools.partial(_outproj_kernel, alpha, sub),
        grid=(bsz, s // ts),
        in_specs=[tok(512), tok(512), _const_spec((512, d)), _const_spec((512, d)), tok(d), ms,
                  _const_spec((1, d)), _const_spec((1, d)), ms, ms,
                  _const_spec((d, LANES)), _const_spec((d, LANES)), _const_spec((1, LANES)),
                  _const_spec((tr, tr)), _const_spec((1, LANES))],
        out_specs=[tok(d), tok(d // 4), tok(d // 4), tok(LANES), tok(LANES), tok(LANES),
                   _const_spec((1, LANES))],
        out_shape=[jax.ShapeDtypeStruct((bsz, s, d), F32),
                   jax.ShapeDtypeStruct((bsz, s, d // 4), jnp.int32),
                   jax.ShapeDtypeStruct((bsz, s, d // 4), jnp.int32),
                   jax.ShapeDtypeStruct((bsz, s, LANES), jnp.int32),
                   jax.ShapeDtypeStruct((bsz, s, LANES), jnp.int32),
                   jax.ShapeDtypeStruct((bsz, s, LANES), F32),
                   jax.ShapeDtypeStruct((1, LANES), F32)],
        scratch_shapes=[pltpu.VMEM((1, LANES), F32)],
        compiler_params=_cparams(2),
    )(o1, o2, w1, w2, x, g1, lng.reshape(1, d), lnb.reshape(1, d), sc2, sh2, rwh, rwl, rb, tri, cnt0)


def _router_weights(router_w, router_b):
    d, e = router_w.shape
    wp = jnp.zeros((d, LANES), F32).at[:, :e].set(router_w)
    hi = wp.astype(BF16)
    lo = (wp - hi.astype(F32)).astype(BF16)
    rb = jnp.full((1, LANES), -jnp.inf, F32).at[0, :e].set(router_b)
    return hi, lo, rb


def _deinterleave_perm():
    p = np.zeros((GU_BLOCK, GU_BLOCK), np.float32)
    m = np.arange(GU_BLOCK // 2)
    p[2 * m, m] = 1.0
    p[2 * m + 1, GU_BLOCK // 2 + m] = 1.0
    return jnp.asarray(p, dtype=BF16)


def _ffn_kernel(te_ref, tv_ref, xa_ref, xb_ref, wgu_ref, bgu_ref, wd_ref, bd_ref, perm_ref, y_ref,
                wgu_s, wd_s):
    j = pl.program_id(0)
    n_blk = wgu_s.shape[1] // GU_BLOCK
    half = GU_BLOCK // 2

    @pl.when(jnp.logical_or(j == 0, te_ref[j] != te_ref[jnp.maximum(j - 1, 0)]))
    def _():
        for b in range(n_blk):
            sl = slice(b * GU_BLOCK, (b + 1) * GU_BLOCK)
            wgu_s[:, sl] = _dot(wgu_ref[0, 0, :, sl].astype(BF16), perm_ref[...]).astype(BF16)
        wd_s[...] = wd_ref[0, 0].astype(BF16)

    @pl.when(tv_ref[j] > 0)
    def _():
        a_hi, a_lo = _unpack_bf16_pairs(xa_ref[...])
        b_hi, b_lo = _unpack_bf16_pairs(xb_ref[...])
        x = jnp.concatenate([a_hi, b_hi, a_lo, b_lo], axis=-1).astype(BF16)
        acts = []
        for b in range(n_blk):
            sl = slice(b * GU_BLOCK, (b + 1) * GU_BLOCK)
            gu = _dot(x, wgu_s[:, sl]) + bgu_ref[0, :, sl]
            glu = jnp.minimum(gu[:, :half], SWIGLU_LIMIT)
            lin = jnp.clip(gu[:, half:], -SWIGLU_LIMIT, SWIGLU_LIMIT)
            acts.append(((lin + 1.0) * (glu * jax.nn.sigmoid(SWIGLU_ALPHA * glu))).astype(BF16))
        a = jnp.concatenate(acts, axis=-1)
        y_ref[...] = (_dot(a, wd_s[...]) + bd_ref[0]).astype(BF16)

    @pl.when(tv_ref[j] == 0)
    def _():
        y_ref[...] = jnp.zeros_like(y_ref)


def _expert_ffn(xsa, xsb, tile_expert, tile_valid, layer, w_gu, b_gu, w_down, b_down, tm):
    p, q = xsa.shape
    d = 4 * q
    _, e, _, f2 = w_gu.shape
    f = f2 // 2
    half = GU_BLOCK // 2
    bgu = jnp.stack([b_gu[:, 0::2].reshape(e, f // half, half),
                     b_gu[:, 1::2].reshape(e, f // half, half)], axis=2).reshape(e, 1, f2)
    wspec = lambda a, b: pl.BlockSpec((1, 1, a, b), lambda j, te, tv: (layer, te[j], 0, 0))
    bspec = lambda b: pl.BlockSpec((1, 1, b), lambda j, te, tv: (te[j], 0, 0))
    return pl.pallas_call(
        _ffn_kernel,
        grid_spec=pltpu.PrefetchScalarGridSpec(
            num_scalar_prefetch=2,
            grid=(p // tm,),
            in_specs=[pl.BlockSpec((tm, q), lambda j, te, tv: (j, 0)),
                      pl.BlockSpec((tm, q), lambda j, te, tv: (j, 0)),
                      wspec(d, f2), bspec(f2), wspec(f, d), bspec(d),
                      pl.BlockSpec((GU_BLOCK, GU_BLOCK), lambda j, te, tv: (0, 0))],
            out_specs=pl.BlockSpec((tm, d), lambda j, te, tv: (j, 0)),
            scratch_shapes=[pltpu.VMEM((d, f2), BF16), pltpu.VMEM((f, d), BF16)]),
        out_shape=jax.ShapeDtypeStruct((p, d), BF16),
        compiler_params=_cparams(1),
    )(tile_expert, tile_valid, xsa, xsb, w_gu, bgu, w_down, b_down.reshape(e, 1, d),
      _deinterleave_perm())


def _moe_plan(idx4, rank4, counts, tm):
    t = idx4.shape[0]
    n_experts = counts.shape[0]
    pc = ((counts + tm - 1) // tm) * tm
    pend = jnp.cumsum(pc)
    pstart = pend - pc
    dpos_t = (pstart[idx4] + rank4).T.astype(jnp.int32)
    p = t * TOP_K + n_experts * tm
    tile_start = jnp.arange(p // tm, dtype=jnp.int32) * tm
    tile_expert = jnp.minimum(jnp.sum((tile_start[:, None] >= pend[None, :]).astype(jnp.int32), axis=1),
                              n_experts - 1).astype(jnp.int32)
    tile_valid = (tile_start < pend[-1]).astype(jnp.int32)
    return dpos_t, p, tile_expert, tile_valid


SC_WINDOW = 128


def _sc_dispatch(rows, dpos_t, p):
    t, w = rows.shape
    k = dpos_t.shape[0]
    mesh = plsc.VectorSubcoreMesh(core_axis_name="core", subcore_axis_name="subcore")

    @functools.partial(pl.kernel, out_type=jax.ShapeDtypeStruct((p, w), rows.dtype), mesh=mesh)
    def kern(x_hbm, *refs):
        i_hbms, o_hbm = refs[:k], refs[k]

        def body(x_vmem, *i_vmems):
            for iv in i_vmems:
                pltpu.sync_copy(x_vmem, o_hbm.at[iv.at[0]])

        pltpu.emit_pipeline(
            body,
            grid=(t // SC_WINDOW,),
            in_specs=[pl.BlockSpec((SC_WINDOW, w), index_map=lambda i: (i, 0))]
            + [pl.BlockSpec((1, SC_WINDOW), index_map=lambda i: (0, i)) for _ in range(k)],
            out_specs=[],
            core_axis_name=("core", "subcore"),
            dimension_semantics=(pltpu.PARALLEL,),
        )(x_hbm, *i_hbms)

    return kern(rows, *[dpos_t[kk:kk + 1] for kk in range(k)])


def _combine_kernel(alpha, yg_ref, gw_ref, x_ref, g2_ref, lng_ref, lnb_ref, o_ref):
    gw = gw_ref[...]
    y = yg_ref[0].astype(F32) * gw[:, 0:1]
    for k in range(1, TOP_K):
        y = y + yg_ref[k].astype(F32) * gw[:, k:k + 1]
    o_ref[0] = _layer_norm(alpha * x_ref[0] + g2_ref[0] * y, lng_ref[...], lnb_ref[...])


def _combine_ln(yg, gw, x, g2, lng, lnb, alpha, row_offset, ts):
    bsz, s, d = x.shape
    batched = g2.shape[0] > 1
    nblk = s // ts
    off = row_offset // ts
    return pl.pallas_call(
        functools.partial(_combine_kernel, alpha),
        grid=(bsz, nblk),
        in_specs=[pl.BlockSpec((TOP_K, ts, d), lambda b, i: (0, off + b * nblk + i, 0)),
                  pl.BlockSpec((ts, LANES), lambda b, i: (off + b * nblk + i, 0)),
                  pl.BlockSpec((1, ts, d), lambda b, i: (b, i, 0)),
                  _mod_spec(d, batched), _const_spec((1, d)), _const_spec((1, d))],
        out_specs=pl.BlockSpec((1, ts, d), lambda b, i: (b, i, 0)),
        out_shape=jax.ShapeDtypeStruct((bsz, s, d), F32),
        compiler_params=_cparams(2),
    )(yg, gw, x, g2, lng.reshape(1, d), lnb.reshape(1, d))


def kernel(x, c, ctx, c_ctx, mod_w, mod_b, ln1_g, ln1_b, ln2_g, ln2_b, even_w_in, even_w_out, na_rpb, gqa_q_gain, gqa_k_gain, odd_w_in, odd_w_out, diff_lq1, diff_lk1, diff_lq2, diff_lk2, diff_subln, mla_q_gain, mla_w_uq, mla_kv_gain, mla_w_ukv, router_w, router_b, exp_w_gu, exp_b_gu, exp_w_down, exp_b_down):
    bsz, s, d = x.shape
    l = ctx.shape[1]
    depth = mod_w.shape[0]
    n_experts = router_w.shape[-1]
    alpha = (2 * depth) ** 0.25
    ts = min(512, s)
    tm = MOE_TILE

    pad = (-(bsz + 1)) % 8
    c_all = jnp.concatenate([c, c_ctx[None, :], jnp.zeros((pad, d), F32)], axis=0)
    mod = _modulation_all(c_all, mod_w, mod_b)

    for i in range(depth):
        last = i == depth - 1
        j = i // 2
        ml = [mod[i, :bsz, k * d:(k + 1) * d].reshape(bsz, 1, d) for k in range(6)]
        mc = [mod[i, bsz:bsz + 1, k * d:(k + 1) * d].reshape(1, 1, d) for k in range(6)]
        sh1, sc1, g1, sh2, sc2, g2 = ml
        csh1, csc1, cg1, csh2, csc2, cg2 = mc

        if i % 2 == 0:
            w = _even_weights(even_w_in[j])
            naq, nak, nav, gq, gk, gv = _inproj_even(
                x, sc1, sh1, w, _even_tables(s, gqa_q_gain[j], gqa_k_gain[j], True), ts)
            cnaq, cnak, cnav, cgq, cgk, cgv = _inproj_even(
                ctx, csc1, csh1, w, _even_tables(l, gqa_q_gain[j], gqa_k_gain[j], False), l)
            o1 = _na_attention(naq, nak, nav, cnak, cnav, na_rpb[j])
            o2 = _gqa_attention(gq, gk, gv, cgk, cgv, 256, ATTN_SUB)
            if not last:
                co1 = _slot_attention(cnaq, None, None, cnak, cnav, NA_HEADS, False, l)
                co2 = _gqa_attention(cgq, None, None, cgk, cgv, l)
            w_out = even_w_out[j]
        else:
            lam_init = 0.8 - 0.6 * math.exp(-0.3 * i)
            weights = _odd_weights(odd_w_in[j], mla_w_uq[j], mla_w_ukv[j])
            gains = (mla_q_gain[j], mla_kv_gain[j])
            dq, dk, dv, mq, mk, mv = _inproj_odd(x, sc1, sh1, weights, gains, _odd_tables(s, True), ts)
            cdq, cdk, cdv, cmq, cmk, cmv = _inproj_odd(ctx, csc1, csh1, weights, gains,
                                                       _odd_tables(l, False), l)
            lams = (diff_lq1[j], diff_lk1[j], diff_lq2[j], diff_lk2[j])
            o1 = _diff_attention(dq, dk, dv, cdk, cdv, lams, diff_subln[j], lam_init, 256, ATTN_SUB)
            o2 = _slot_attention(mq, mk, mv, cmk, cmv, MLA_HEADS, True, 512)
            if not last:
                co1 = _diff_attention(cdq, None, None, cdk, cdv, lams, diff_subln[j], lam_init, l)
                co2 = _slot_attention(cmq, None, None, cmk, cmv, MLA_HEADS, True, l)
            w_out = odd_w_out[j]

        router = _router_weights(router_w[i], router_b[i])
        cnt0 = jnp.zeros((1, LANES), F32)
        x, h2a, h2b, ridx, rrank, rgw, cnt = _outproj_ln_router(
            o1, o2, w_out, x, g1, ln1_g[i], ln1_b[i], sc2, sh2, router, cnt0, alpha, ts)
        flat = lambda a, n: a.reshape(bsz * n, a.shape[-1])
        h2a, h2b, ridx, rrank, rgw = [flat(a, s) for a in (h2a, h2b, ridx, rrank, rgw)]
        if not last:
            ctx, ch2a, ch2b, cidx, crank, cgw, cnt = _outproj_ln_router(
                co1, co2, w_out, ctx, cg1, ln1_g[i], ln1_b[i], csc2, csh2, router, cnt, alpha, l)
            cat = lambda a, b: jnp.concatenate([a, flat(b, l)], axis=0)
            h2a, h2b = cat(h2a, ch2a), cat(h2b, ch2b)
            ridx, rrank, rgw = cat(ridx, cidx), cat(rrank, crank), cat(rgw, cgw)

        counts = cnt[0, :n_experts].astype(jnp.int32)
        dpos_t, p, tile_expert, tile_valid = _moe_plan(ridx[:, :TOP_K], rrank[:, :TOP_K], counts, tm)
        xsa = _sc_dispatch(h2a, dpos_t, p)
        xsb = _sc_dispatch(h2b, dpos_t, p)
        ys = _expert_ffn(xsa, xsb, tile_expert, tile_valid, i, exp_w_gu, exp_b_gu[i],
                         exp_w_down, exp_b_down[i], tm)
        yg = ys.at[dpos_t].get(mode="promise_in_bounds")

        x = _combine_ln(yg, rgw, x, g2, ln2_g[i], ln2_b[i], alpha, 0, ts)
        if not last:
            ctx = _combine_ln(yg, rgw, ctx, cg2, ln2_g[i], ln2_b[i], alpha, bsz * s, l)
    return x
```

```python
import functools
import math

import numpy as np
import jax
import jax.numpy as jnp
from jax import lax
from jax.experimental import pallas as pl
from jax.experimental.pallas import tpu as pltpu
from jax.experimental.pallas import tpu_sc as plsc

F32 = jnp.float32
BF16 = jnp.bfloat16

GRID_W = 64
HEAD_DIM = 64
ROPE_THETA = 10000.0
LN_EPS = 1e-6
RMS_EPS = 1e-6
NEG_INF = -1e30
NA_HEADS = 8
NA_WIN_H = 8
NA_WIN_W = 16
GQA_HEADS = 8
GQA_KV_HEADS = 2
DIFF_HEADS = 4
MLA_HEADS = 8
MLA_Q_RANK = 256
MLA_KV_RANK = 128
MLA_NOPE = 64
MLA_ROPE = 32
MLA_V = 64
MLA_QK = MLA_NOPE + MLA_ROPE
TOP_K = 4
SWIGLU_ALPHA = 1.702
SWIGLU_LIMIT = 7.0
LOG2E = 1.4426950408889634

LANES = 128
VMEM_LIMIT = 56 * 1024 * 1024
MOE_TILE = 1024
GU_BLOCK = 512
ATTN_SUB = 2


def _cparams(n_axes):
    return pltpu.CompilerParams(dimension_semantics=("arbitrary",) * n_axes,
                                vmem_limit_bytes=VMEM_LIMIT)


def _dot(a, b):
    return jnp.dot(a, b, preferred_element_type=F32)


def _dot_nt(a, b):
    return lax.dot_general(a, b, (((1,), (1,)), ((), ())), preferred_element_type=F32)


def _mod_kernel(c_ref, w_ref, b_ref, o_ref):
    cv = c_ref[...]
    a = (cv * jax.nn.sigmoid(cv)).astype(BF16)
    o_ref[0] = _dot(a, w_ref[0].astype(BF16)) + b_ref[0]


def _modulation_all(c_all, mod_w, mod_b):
    depth, d, n = mod_w.shape
    rows = c_all.shape[0]
    tn = 1536
    return pl.pallas_call(
        _mod_kernel,
        grid=(depth, n // tn),
        in_specs=[pl.BlockSpec((rows, d), lambda i, j: (0, 0)),
                  pl.BlockSpec((1, d, tn), lambda i, j: (i, 0, j)),
                  pl.BlockSpec((1, 1, tn), lambda i, j: (i, 0, j))],
        out_specs=pl.BlockSpec((1, rows, tn), lambda i, j: (i, 0, j)),
        out_shape=jax.ShapeDtypeStruct((depth, rows, n), F32),
        compiler_params=_cparams(2),
    )(c_all, mod_w, mod_b.reshape(depth, 1, n))


def _rope_tables(s, dim):
    pos = jnp.arange(s)
    row = (pos // GRID_W).astype(F32)[:, None]
    col = (pos % GRID_W).astype(F32)[:, None]
    quarter = dim // 4
    inv_freq = ROPE_THETA ** (-jnp.arange(quarter, dtype=F32) / quarter)
    ar, ac = row * inv_freq, col * inv_freq
    cos = jnp.concatenate([jnp.cos(ar), jnp.cos(ar), jnp.cos(ac), jnp.cos(ac)], axis=-1)
    sin = jnp.concatenate([-jnp.sin(ar), jnp.sin(ar), -jnp.sin(ac), jnp.sin(ac)], axis=-1)
    return cos, sin


def _swap_perm(dim):
    q = dim // 4
    idx = np.arange(dim)
    return np.where((idx % (2 * q)) < q, idx + q, idx - q)


def _rope_partner(y, dim):
    q = dim // 4
    lane = lax.broadcasted_iota(jnp.int32, (1, LANES), 1)
    first = (lane % (2 * q)) < q
    tiles = []
    for j in range(0, y.shape[1], LANES):
        t = y[:, j:j + LANES]
        tiles.append(jnp.where(first, pltpu.roll(t, LANES - q, 1), pltpu.roll(t, q, 1)))
    return tiles[0] if len(tiles) == 1 else jnp.concatenate(tiles, axis=-1)


def _swap_cols(w, dim):
    n = w.shape[-1] // dim
    perm = (np.arange(n)[:, None] * dim + _swap_perm(dim)[None, :]).reshape(-1)
    return w[..., perm]


def _block_ones(n, blk):
    i = np.arange(n) // blk
    return jnp.asarray((i[:, None] == i[None, :]).astype(np.float32), dtype=BF16)


E_NAQ, E_NAK, E_NAV, E_GQ, E_GK, E_GV, E_END = (0, 512, 1024, 1536, 2048, 2304, 2560)


def _even_weights(w_in):
    naq, nak, nav, gq, gk, gv = jnp.split(w_in, [512, 1024, 1536, 2048, 2176], axis=-1)
    k0, k1 = gk[:, :64], gk[:, 64:]
    v0, v1 = gv[:, :64], gv[:, 64:]
    gk2 = jnp.concatenate([k0, k1, k1, k0], axis=-1)
    gv2 = jnp.concatenate([v0, v1, v1, v0], axis=-1)
    w = jnp.concatenate([naq * (0.125 * LOG2E), nak, nav, gq, gk2, gv2], axis=-1).astype(BF16)
    return w


def _even_tables(s, q_gain, k_gain, rope):
    sw = _swap_perm(64)
    if rope:
        cos, sin = _rope_tables(s, 64)
    else:
        cos, sin = jnp.ones((s, 64), F32), jnp.zeros((s, 64), F32)
    qa = jnp.tile(cos * q_gain[None, :] * (0.125 * LOG2E), (1, 8))
    qb = jnp.tile(sin * q_gain[sw][None, :] * (0.125 * LOG2E), (1, 8))
    ka = jnp.tile(cos * k_gain[None, :], (1, 4))
    kb = jnp.tile(sin * k_gain[sw][None, :], (1, 4))
    return qa, qb, ka, kb


def _inproj_even_kernel(x_ref, sc_ref, sh_ref, w_ref, g512_ref, g256_ref,
                        qa_ref, qb_ref, ka_ref, kb_ref,
                        naq_ref, nak_ref, nav_ref, gq_ref, gk_ref, gv_ref):
    h = (x_ref[0] * (1.0 + sc_ref[0]) + sh_ref[0]).astype(BF16)
    naq_ref[0] = _dot(h, w_ref[:, E_NAQ:E_NAK]).astype(BF16)
    nak_ref[0] = _dot(h, w_ref[:, E_NAK:E_NAV]).astype(BF16)
    nav_ref[0] = _dot(h, w_ref[:, E_NAV:E_GQ]).astype(BF16)
    gv_ref[0] = _dot(h, w_ref[:, E_GV:E_END]).astype(BF16)
    y = _dot(h, w_ref[:, E_GQ:E_GK])
    ys = _rope_partner(y, HEAD_DIM)
    r = lax.rsqrt(_dot((y * y).astype(BF16), g512_ref[...]) * (1.0 / HEAD_DIM) + RMS_EPS)
    gq_ref[0] = (r * (y * qa_ref[...] + ys * qb_ref[...])).astype(BF16)
    y = _dot(h, w_ref[:, E_GK:E_GV])
    ys = _rope_partner(y, HEAD_DIM)
    r = lax.rsqrt(_dot((y * y).astype(BF16), g256_ref[...]) * (1.0 / HEAD_DIM) + RMS_EPS)
    gk_ref[0] = (r * (y * ka_ref[...] + ys * kb_ref[...])).astype(BF16)


def _mod_spec(d, batched):
    if batched:
        return pl.BlockSpec((1, 1, d), lambda b, s: (b, 0, 0))
    return pl.BlockSpec((1, 1, d), lambda b, s: (0, 0, 0))


def _const_spec(shape):
    nd = len(shape)
    return pl.BlockSpec(shape, lambda b, s: (0,) * nd)


def _inproj_even(x, sc, sh, w, tables, ts):
    bsz, s, d = x.shape
    batched = sc.shape[0] > 1
    qa, qb, ka, kb = tables
    tok = lambda n: pl.BlockSpec((1, ts, n), lambda b, i: (b, i, 0))
    tab = lambda n: pl.BlockSpec((ts, n), lambda b, i: (i, 0))
    widths = (512, 512, 512, 512, 256, 256)
    return pl.pallas_call(
        _inproj_even_kernel,
        grid=(bsz, s // ts),
        in_specs=[tok(d), _mod_spec(d, batched), _mod_spec(d, batched), _const_spec(w.shape),
                  _const_spec((512, 512)), _const_spec((256, 256)),
                  tab(512), tab(512), tab(256), tab(256)],
        out_specs=[tok(n) for n in widths],
        out_shape=[jax.ShapeDtypeStruct((bsz, s, n), BF16) for n in widths],
        compiler_params=_cparams(2),
    )(x, sc, sh, w, _block_ones(512, 64), _block_ones(256, 64), qa, qb, ka, kb)


O_DQ, O_DK, O_DV, O_CQ, O_CKV, O_PE, O_END = (0, 512, 1024, 1536, 1792, 1920, 2048)


def _odd_weights(w_in, w_uq, w_ukv):
    dq, dk, dv, cq, ckv, kpe = jnp.split(w_in, [512, 1024, 1536, 1792, 1920], axis=-1)
    d = w_in.shape[0]
    pe_slot = jnp.concatenate([kpe, _swap_cols(kpe, 32), jnp.zeros((d, 64), F32)], axis=-1)
    w = jnp.concatenate([dq, dk, dv, cq, ckv, pe_slot], axis=-1).astype(BF16)
    uq = w_uq.reshape(MLA_Q_RANK, MLA_HEADS, MLA_QK)
    z32 = jnp.zeros((MLA_Q_RANK, MLA_HEADS, 32), F32)
    z64 = jnp.zeros((MLA_Q_RANK, MLA_HEADS, 64), F32)
    uq_pad = jnp.concatenate([uq, z32], axis=-1).reshape(MLA_Q_RANK, MLA_HEADS * LANES)
    uq_sw = jnp.concatenate([z64, _swap_cols(uq[..., MLA_NOPE:], 32), z32],
                            axis=-1).reshape(MLA_Q_RANK, MLA_HEADS * LANES)
    wuq2 = jnp.concatenate([uq_pad, uq_sw], axis=-1).astype(BF16)
    ukv = w_ukv.reshape(MLA_KV_RANK, MLA_HEADS, MLA_NOPE + MLA_V)
    zk = jnp.zeros((MLA_KV_RANK, MLA_HEADS, 64), F32)
    wk_pad = jnp.concatenate([ukv[..., :MLA_NOPE], zk], axis=-1).reshape(MLA_KV_RANK, MLA_HEADS * LANES)
    place = np.zeros((LANES, MLA_HEADS, LANES), np.float32)
    for j in range(MLA_ROPE):
        place[j, :, MLA_NOPE + j] = 1.0
        place[MLA_ROPE + j, :, MLA_NOPE + j] = 1.0
    wk2 = jnp.concatenate([wk_pad, jnp.asarray(place.reshape(LANES, MLA_HEADS * LANES))],
                          axis=0).astype(BF16)
    wv = ukv[..., MLA_NOPE:].reshape(MLA_KV_RANK, MLA_HEADS * MLA_V).astype(BF16)
    return w, wuq2, wk2, wv


def _odd_tables(s, rope):
    m_scale = MLA_QK ** -0.5 * LOG2E
    if rope:
        cos64, sin64 = _rope_tables(s, 64)
        cos32, sin32 = _rope_tables(s, 32)
    else:
        cos64, sin64 = jnp.ones((s, 64), F32), jnp.zeros((s, 64), F32)
        cos32, sin32 = jnp.ones((s, 32), F32), jnp.zeros((s, 32), F32)
    one64, z32, z64 = jnp.ones((s, 64), F32), jnp.zeros((s, 32), F32), jnp.zeros((s, 64), F32)
    dcos, dsin = jnp.tile(cos64, (1, 8)), jnp.tile(sin64, (1, 8))
    qa = jnp.tile(jnp.concatenate([one64, cos32, z32], axis=-1) * m_scale, (1, MLA_HEADS))
    qb = jnp.tile(jnp.concatenate([z64, sin32, z32], axis=-1) * m_scale, (1, MLA_HEADS))
    pe = jnp.concatenate([cos32, sin32, z64], axis=-1)
    return dcos, dsin, qa, qb, pe


def _inproj_odd_kernel(x_ref, sc_ref, sh_ref, w_ref, wuq_ref, wk_ref, wv_ref, qg_ref, kvg_ref,
                       dcos_ref, dsin_ref, qa_ref, qb_ref, pe_ref,
                       dq_ref, dk_ref, dv_ref, mq_ref, mk_ref, mv_ref):
    h = (x_ref[0] * (1.0 + sc_ref[0]) + sh_ref[0]).astype(BF16)
    dcos, dsin = dcos_ref[...], dsin_ref[...]
    y = _dot(h, w_ref[:, O_DQ:O_DK])
    ys = _rope_partner(y, HEAD_DIM)
    dq_ref[0] = ((y * dcos + ys * dsin) * (0.125 * LOG2E)).astype(BF16)
    y = _dot(h, w_ref[:, O_DK:O_DV])
    ys = _rope_partner(y, HEAD_DIM)
    dk_ref[0] = (y * dcos + ys * dsin).astype(BF16)
    dv_ref[0] = _dot(h, w_ref[:, O_DV:O_CQ]).astype(BF16)
    cq = _dot(h, w_ref[:, O_CQ:O_CKV])
    nq = cq * lax.rsqrt(jnp.mean(cq * cq, axis=-1, keepdims=True) + RMS_EPS) * qg_ref[...]
    y2 = _dot(nq.astype(BF16), wuq_ref[...])
    half = MLA_HEADS * LANES
    mq_ref[0] = (y2[:, :half] * qa_ref[...] + y2[:, half:] * qb_ref[...]).astype(BF16)
    ckv = _dot(h, w_ref[:, O_CKV:O_PE])
    nk = ckv * lax.rsqrt(jnp.mean(ckv * ckv, axis=-1, keepdims=True) + RMS_EPS) * kvg_ref[...]
    pe = _dot(h, w_ref[:, O_PE:O_END]) * pe_ref[...]
    nkb = nk.astype(BF16)
    cat = jnp.concatenate([nkb, pe.astype(BF16)], axis=-1)
    mk_ref[0] = _dot(cat, wk_ref[...]).astype(BF16)
    mv_ref[0] = _dot(nkb, wv_ref[...]).astype(BF16)


def _inproj_odd(x, sc, sh, weights, gains, tables, ts):
    bsz, s, d = x.shape
    batched = sc.shape[0] > 1
    w, wuq2, wk2, wv = weights
    qg, kvg = gains
    dcos, dsin, qa, qb, pe = tables
    tok = lambda n: pl.BlockSpec((1, ts, n), lambda b, i: (b, i, 0))
    tab = lambda n: pl.BlockSpec((ts, n), lambda b, i: (i, 0))
    widths = (512, 512, 512, 1024, 1024, 512)
    return pl.pallas_call(
        _inproj_odd_kernel,
        grid=(bsz, s // ts),
        in_specs=[tok(d), _mod_spec(d, batched), _mod_spec(d, batched), _const_spec(w.shape),
                  _const_spec(wuq2.shape), _const_spec(wk2.shape), _const_spec(wv.shape),
                  _const_spec((1, MLA_Q_RANK)), _const_spec((1, MLA_KV_RANK)),
                  tab(512), tab(512), tab(1024), tab(1024), tab(128)],
        out_specs=[tok(n) for n in widths],
        out_shape=[jax.ShapeDtypeStruct((bsz, s, n), BF16) for n in widths],
        compiler_params=_cparams(2),
    )(x, sc, sh, w, wuq2, wk2, wv, qg.reshape(1, -1), kvg.reshape(1, -1), dcos, dsin, qa, qb, pe)


def _half_masks():
    lane = lax.broadcasted_iota(jnp.int32, (1, LANES), 1)
    return lane < HEAD_DIM, lane >= HEAD_DIM


def _softmax_values(ss, vs):
    m = functools.reduce(jnp.maximum, [jnp.max(s, axis=-1, keepdims=True) for s in ss])
    es = [jnp.exp2(s - m) for s in ss]
    l = functools.reduce(lambda a, b: a + b, [jnp.sum(e, axis=-1, keepdims=True) for e in es])
    o = functools.reduce(lambda a, b: a + b, [_dot(e.astype(BF16), v) for e, v in zip(es, vs)])
    return o * (1.0 / l)


def _attend(qm, ks, vs):
    return _softmax_values([_dot_nt(qm, k) for k in ks], vs)


NA_RB = 4
NA_WIN_ROWS = NA_RB + NA_WIN_H - 1
NA_SUB = 2


def _na_block_plan(rows):
    plan = []
    for blk in range(rows // NA_RB):
        r0 = blk * NA_RB
        rs = [int(np.clip(r0 + i - NA_WIN_H // 2, 0, rows - NA_WIN_H)) for i in range(NA_RB)]
        ws = int(np.clip(r0 - NA_WIN_H // 2, 0, rows - NA_WIN_ROWS))
        pat = tuple((rs[i] - ws, r0 + i - ws) for i in range(NA_RB))
        assert all(0 <= o and o + NA_WIN_H <= NA_WIN_ROWS for o, _ in pat)
        plan.append(pat)
    assert all(p == plan[1] for p in plan[1:-1])
    return (plan[0], plan[1], plan[-1])


def _na_bias_table(rpb, rows):
    cols = np.arange(GRID_W)
    col_start = np.clip(cols - NA_WIN_W // 2, 0, GRID_W - NA_WIN_W)
    col_mask = (cols[None, :] >= col_start[:, None]) & (cols[None, :] < col_start[:, None] + NA_WIN_W)
    col_idx = np.clip(cols[None, :] - cols[:, None] + NA_WIN_W - 1, 0, 2 * NA_WIN_W - 2)
    wr = np.arange(NA_WIN_ROWS)
    tables = []
    for pat in _na_block_plan(rows):
        off = np.array([o for o, _ in pat])[:, None]
        rq = np.array([r for _, r in pat])[:, None]
        row_ok = (wr[None, :] >= off) & (wr[None, :] < off + NA_WIN_H)
        ridx = np.clip(wr[None, :] - rq + NA_WIN_H - 1, 0, 2 * NA_WIN_H - 2)
        t = rpb.astype(F32)[:, ridx]
        t = t[..., col_idx]
        t = t.transpose(0, 1, 3, 2, 4)
        ok = row_ok[None, :, None, :, None] & col_mask[None, None, :, None, :]
        t = jnp.where(ok, t * LOG2E, NEG_INF)
        tables.append(t.reshape(NA_HEADS, NA_RB * GRID_W, NA_WIN_ROWS * GRID_W))
    return jnp.stack(tables)


def _na_kernel(rows, sub, q_ref, k_ref, v_ref, kc_ref, vc_ref, *refs):
    bt_refs, o_ref = refs[:sub], refs[sub]
    step = pl.program_id(1)
    tq = NA_RB * GRID_W
    m0, m1 = _half_masks()
    for u in range(sub):
        r0 = (step * sub + u) * NA_RB
        ws = jnp.clip(r0 - NA_WIN_H // 2, 0, rows - NA_WIN_ROWS)
        win = pl.ds(pl.multiple_of(ws * GRID_W, GRID_W), NA_WIN_ROWS * GRID_W)
        qrows = slice(u * tq, (u + 1) * tq)
        for j in range(NA_HEADS // 2):
            sl = slice(j * LANES, (j + 1) * LANES)
            qp = q_ref[0, qrows, sl]
            kp, vp = k_ref[0, win, sl], v_ref[0, win, sl]
            kcp, vcp = kc_ref[0, :, sl], vc_ref[0, :, sl]
            outs = []
            for par, msk in ((0, m0), (1, m1)):
                qm = jnp.where(msk, qp, jnp.zeros_like(qp))
                s_loc = _dot_nt(qm, kp) + bt_refs[u][0, 2 * j + par]
                outs.append(_softmax_values([s_loc, _dot_nt(qm, kcp)], [vp, vcp]))
            o_ref[0, qrows, sl] = jnp.where(m0, outs[0], outs[1]).astype(BF16)


def _na_attention(q, k, v, kc, vc, rpb):
    bsz, s, w = q.shape
    l = kc.shape[1]
    rows = s // GRID_W
    nblk = rows // NA_RB
    sub = NA_SUB
    nstep = nblk // sub
    assert rows % NA_RB == 0 and rows >= NA_WIN_ROWS and nblk % sub == 0 and nblk >= 3
    bt = _na_bias_table(rpb, rows)
    tq = NA_RB * GRID_W * sub
    full = lambda n: pl.BlockSpec((1, n, w), lambda b, r: (b, 0, 0))

    def bt_spec(u):
        def kind(b, r):
            blk = r * sub + u
            return ((blk > 0).astype(jnp.int32) + (blk == nblk - 1).astype(jnp.int32), 0, 0, 0)
        return pl.BlockSpec((1,) + bt.shape[1:], kind)

    return pl.pallas_call(
        functools.partial(_na_kernel, rows, sub),
        grid=(bsz, nstep),
        in_specs=[pl.BlockSpec((1, tq, w), lambda b, r: (b, r, 0)),
                  full(s), full(s), full(l), full(l)] + [bt_spec(u) for u in range(sub)],
        out_specs=pl.BlockSpec((1, tq, w), lambda b, r: (b, r, 0)),
        out_shape=jax.ShapeDtypeStruct((bsz, s, w), BF16),
        compiler_params=_cparams(2),
    )(q, k, v, kc, vc, *([bt] * sub))


def _slot_attn_kernel(n_heads, q_slot, has_lat, sub, *refs):
    if has_lat:
        q_ref, k_ref, v_ref, kc_ref, vc_ref, o_ref = refs
    else:
        q_ref, kc_ref, vc_ref, o_ref = refs
    m0, m1 = _half_masks()
    tq = q_ref.shape[1] // sub
    for u in range(sub):
        qrows = slice(u * tq, (u + 1) * tq)
        for j in range(n_heads // 2):
            vsl = slice(j * LANES, (j + 1) * LANES)
            outs = []
            for par, msk in ((0, m0), (1, m1)):
                h = 2 * j + par
                if q_slot:
                    ksl = slice(h * LANES, (h + 1) * LANES)
                    qm = q_ref[0, qrows, ksl]
                else:
                    ksl = vsl
                    qp = q_ref[0, qrows, vsl]
                    qm = jnp.where(msk, qp, jnp.zeros_like(qp))
                ks, vs = [kc_ref[0, :, ksl]], [vc_ref[0, :, vsl]]
                if has_lat:
                    ks.insert(0, k_ref[0, :, ksl])
                    vs.insert(0, v_ref[0, :, vsl])
                outs.append(_attend(qm, ks, vs))
            o_ref[0, qrows, vsl] = jnp.where(m0, outs[0], outs[1]).astype(BF16)


def _slot_attention(q, k, v, kc, vc, n_heads, q_slot, tq, sub=1):
    bsz, sq, wq = q.shape
    l, wk, wv = kc.shape[1], kc.shape[2], vc.shape[2]
    has_lat = k is not None
    qspec = pl.BlockSpec((1, tq, wq), lambda b, i: (b, i, 0))
    full = lambda n, w: pl.BlockSpec((1, n, w), lambda b, i: (b, 0, 0))
    in_specs, args = [qspec], [q]
    if has_lat:
        s = k.shape[1]
        in_specs += [full(s, wk), full(s, wv)]
        args += [k, v]
    in_specs += [full(l, wk), full(l, wv)]
    args += [kc, vc]
    return pl.pallas_call(
        functools.partial(_slot_attn_kernel, n_heads, q_slot, has_lat, sub),
        grid=(bsz, sq // tq),
        in_specs=in_specs,
        out_specs=pl.BlockSpec((1, tq, wv), lambda b, i: (b, i, 0)),
        out_shape=jax.ShapeDtypeStruct((bsz, sq, wv), BF16),
        compiler_params=_cparams(2),
    )(*args)


def _gqa_kernel(has_lat, tq, sub, *refs):
    if has_lat:
        q_ref, k_ref, v_ref, kc_ref, vc_ref, o_ref = refs
    else:
        q_ref, kc_ref, vc_ref, o_ref = refs
    masks = _half_masks()
    for u in range(sub):
        qrows = slice(u * tq, (u + 1) * tq)
        res = {}
        for g in range(GQA_KV_HEADS):
            for var in range(2):
                par = g if var == 0 else 1 - g
                heads = (4 * g + par, 4 * g + 2 + par)
                vsl = slice(var * LANES, (var + 1) * LANES)
                qs = []
                for h in heads:
                    qp = q_ref[0, qrows, (h // 2) * LANES:(h // 2 + 1) * LANES]
                    qs.append(jnp.where(masks[par], qp, jnp.zeros_like(qp)))
                qm = jnp.concatenate(qs, axis=0)
                ks, vs = [kc_ref[0, :, vsl]], [vc_ref[0, :, vsl]]
                if has_lat:
                    ks.insert(0, k_ref[0, :, vsl])
                    vs.insert(0, v_ref[0, :, vsl])
                o = _attend(qm, ks, vs)
                res[heads[0]] = o[:tq]
                res[heads[1]] = o[tq:]
        for j in range(GQA_HEADS // 2):
            o_ref[0, qrows, j * LANES:(j + 1) * LANES] = jnp.where(
                masks[0], res[2 * j], res[2 * j + 1]).astype(BF16)


def _gqa_attention(q, k2, v2, k2c, v2c, tq, sub=1):
    bsz, sq, wq = q.shape
    l = k2c.shape[1]
    has_lat = k2 is not None
    full = lambda n: pl.BlockSpec((1, n, 2 * LANES), lambda b, i: (b, 0, 0))
    in_specs, args = [pl.BlockSpec((1, tq * sub, wq), lambda b, i: (b, i, 0))], [q]
    if has_lat:
        in_specs += [full(k2.shape[1])] * 2
        args += [k2, v2]
    in_specs += [full(l)] * 2
    args += [k2c, v2c]
    return pl.pallas_call(
        functools.partial(_gqa_kernel, has_lat, tq, sub),
        grid=(bsz, sq // (tq * sub)),
        in_specs=in_specs,
        out_specs=pl.BlockSpec((1, tq * sub, wq), lambda b, i: (b, i, 0)),
        out_shape=jax.ShapeDtypeStruct((bsz, sq, wq), BF16),
        compiler_params=_cparams(2),
    )(*args)


def _diff_kernel(has_lat, tq, sub, lam_init, *refs):
    if has_lat:
        q_ref, k_ref, v_ref, kc_ref, vc_ref, lq1, lk1, lq2, lk2, sub_ref, o_ref = refs
    else:
        q_ref, kc_ref, vc_ref, lq1, lk1, lq2, lk2, sub_ref, o_ref = refs
    lam = (jnp.exp(jnp.sum(lq1[...] * lk1[...], axis=-1, keepdims=True))
           - jnp.exp(jnp.sum(lq2[...] * lk2[...], axis=-1, keepdims=True)) + lam_init)
    m0, m1 = _half_masks()
    for u in range(sub):
        qrows = slice(u * tq, (u + 1) * tq)
        for h in range(DIFF_HEADS):
            sl = slice(h * LANES, (h + 1) * LANES)
            qp = q_ref[0, qrows, sl]
            zero = jnp.zeros_like(qp)
            qm = jnp.concatenate([jnp.where(m0, qp, zero), jnp.where(m1, qp, zero)], axis=0)
            ks, vs = [kc_ref[0, :, sl]], [vc_ref[0, :, sl]]
            if has_lat:
                ks.insert(0, k_ref[0, :, sl])
                vs.insert(0, v_ref[0, :, sl])
            o2 = _attend(qm, ks, vs)
            o = o2[:tq] - lam * o2[tq:]
            o = o * lax.rsqrt(jnp.mean(o * o, axis=-1, keepdims=True) + RMS_EPS) * sub_ref[...]
            o_ref[0, qrows, sl] = (o * (1.0 - lam_init)).astype(BF16)


def _diff_attention(q, k, v, kc, vc, lams, subln, lam_init, tq, sub=1):
    bsz, sq, w = q.shape
    l = kc.shape[1]
    has_lat = k is not None
    full = lambda n: pl.BlockSpec((1, n, w), lambda b, i: (b, 0, 0))
    in_specs, args = [pl.BlockSpec((1, tq * sub, w), lambda b, i: (b, i, 0))], [q]
    if has_lat:
        in_specs += [full(k.shape[1])] * 2
        args += [k, v]
    in_specs += [full(l)] * 2 + [_const_spec((1, HEAD_DIM))] * 4 + [_const_spec((1, LANES))]
    args += [kc, vc] + [a.reshape(1, -1).astype(F32) for a in lams] + [subln.reshape(1, -1).astype(F32)]
    return pl.pallas_call(
        functools.partial(_diff_kernel, has_lat, tq, sub, lam_init),
        grid=(bsz, sq // (tq * sub)),
        in_specs=in_specs,
        out_specs=pl.BlockSpec((1, tq * sub, w), lambda b, i: (b, i, 0)),
        out_shape=jax.ShapeDtypeStruct((bsz, sq, w), BF16),
        compiler_params=_cparams(2),
    )(*args)


def _pack_bf16_pairs(v):
    w = v.shape[1] // 2
    hi = pltpu.bitcast(v[:, :w].astype(F32), jnp.int32)
    lo = pltpu.bitcast(v[:, w:].astype(F32), jnp.int32)
    return hi | lax.shift_right_logical(lo, 16)


def _unpack_bf16_pairs(u):
    hi = pltpu.bitcast(u & jnp.int32(-65536), F32)
    lo = pltpu.bitcast(lax.shift_left(u, 16), F32)
    return hi, lo


def _layer_norm(z, g, b):
    mu = jnp.mean(z, axis=-1, keepdims=True)
    zc = z - mu
    var = jnp.mean(zc * zc, axis=-1, keepdims=True)
    return zc * lax.rsqrt(var + LN_EPS) * g + b


def _outproj_kernel(alpha, sub, o1_ref, o2_ref, w1_ref, w2_ref, x_ref, g1_ref, lng_ref, lnb_ref,
                    sc2_ref, sh2_ref, rwh_ref, rwl_ref, rb_ref, tri_ref, cnt0_ref,
                    xo_ref, h2a_ref, h2b_ref, idx_ref, rank_ref, gw_ref, cnt_ref, carry_ref):
    @pl.when(jnp.logical_and(pl.program_id(0) == 0, pl.program_id(1) == 0))
    def _():
        carry_ref[...] = cnt0_ref[...]

    tr = x_ref.shape[1] // sub
    carry = carry_ref[...]
    for u in range(sub):
        rows = slice(u * tr, (u + 1) * tr)
        o = _dot(o1_ref[0, rows], w1_ref[...]) + _dot(o2_ref[0, rows], w2_ref[...])
        xn = _layer_norm(alpha * x_ref[0, rows] + g1_ref[0] * o, lng_ref[...], lnb_ref[...])
        xo_ref[0, rows] = xn
        h2 = xn * (1.0 + sc2_ref[0]) + sh2_ref[0]
        hi = h2.astype(BF16)
        packed = _pack_bf16_pairs(hi)
        q = packed.shape[1] // 2
        h2a_ref[0, rows] = packed[:, :q]
        h2b_ref[0, rows] = packed[:, q:]
        lo = (h2 - hi.astype(F32)).astype(BF16)
        logits = (_dot(hi, rwh_ref[...]) + _dot(lo, rwh_ref[...]) + _dot(hi, rwl_ref[...])) + rb_ref[...]
        lane = lax.broadcasted_iota(jnp.int32, logits.shape, 1).astype(F32)
        vals, idxs = [], []
        cur = logits
        for _ in range(TOP_K):
            m = jnp.max(cur, axis=-1, keepdims=True)
            ik = jnp.min(jnp.where(cur == m, lane, float(LANES)), axis=-1, keepdims=True)
            vals.append(m)
            idxs.append(ik)
            cur = jnp.where(lane == ik, -jnp.inf, cur)
        ws = [jnp.exp(v - vals[0]) for v in vals]
        inv = 1.0 / functools.reduce(lambda a, b: a + b, ws)
        sel = [lane == ik for ik in idxs]
        onehot = functools.reduce(lambda a, b: a + b, [m.astype(F32) for m in sel])
        before = _dot(tri_ref[...], onehot.astype(BF16)) + carry
        idx_out = jnp.zeros_like(logits)
        rank_out = jnp.zeros_like(logits)
        w_out = jnp.zeros_like(logits)
        for k in range(TOP_K):
            rk = jnp.sum(jnp.where(sel[k], before, 0.0), axis=-1, keepdims=True)
            idx_out = jnp.where(lane == float(k), idxs[k], idx_out)
            rank_out = jnp.where(lane == float(k), rk, rank_out)
            w_out = jnp.where(lane == float(k), ws[k] * inv, w_out)
        idx_ref[0, rows] = idx_out.astype(jnp.int32)
        rank_ref[0, rows] = rank_out.astype(jnp.int32)
        gw_ref[0, rows] = w_out
        carry = carry + jnp.sum(onehot, axis=0, keepdims=True)
    carry_ref[...] = carry
    cnt_ref[...] = carry


def _outproj_ln_router(o1, o2, w_out, x, g1, lng, lnb, sc2, sh2, router, cnt0, alpha, ts):
    bsz, s, d = x.shape
    batched = g1.shape[0] > 1
    rwh, rwl, rb = router
    w1, w2 = w_out[:512].astype(BF16), w_out[512:].astype(BF16)
    sub = 1
    tr = ts // sub
    tri = jnp.asarray(np.tril(np.ones((tr, tr), np.float32), -1), dtype=BF16)
    tok = lambda n: pl.BlockSpec((1, ts, n), lambda b, i: (b, i, 0))
    ms = _mod_spec(d, batched)
    return pl.pallas_call(
        functools.partial(_outproj_kernel, alpha, sub),
        grid=(bsz, s // ts),
        in_specs=[tok(512), tok(512), _const_spec((512, d)), _const_spec((512, d)), tok(d), ms,
                  _const_spec((1, d)), _const_spec((1, d)), ms, ms,
                  _const_spec((d, LANES)), _const_spec((d, LANES)), _const_spec((1, LANES)),
                  _const_spec((tr, tr)), _const_spec((1, LANES))],
        out_specs=[tok(d), tok(d // 4), tok(d // 4), tok(LANES), tok(LANES), tok(LANES),
                   _const_spec((1, LANES))],
        out_shape=[jax.ShapeDtypeStruct((bsz, s, d), F32),
                   jax.ShapeDtypeStruct((bsz, s, d // 4), jnp.int32),
                   jax.ShapeDtypeStruct((bsz, s, d // 4), jnp.int32),
                   jax.ShapeDtypeStruct((bsz, s, LANES), jnp.int32),
                   jax.ShapeDtypeStruct((bsz, s, LANES), jnp.int32),
                   jax.ShapeDtypeStruct((bsz, s, LANES), F32),
                   jax.ShapeDtypeStruct((1, LANES), F32)],
        scratch_shapes=[pltpu.VMEM((1, LANES), F32)],
        compiler_params=_cparams(2),
    )(o1, o2, w1, w2, x, g1, lng.reshape(1, d), lnb.reshape(1, d), sc2, sh2, rwh, rwl, rb, tri, cnt0)


def _router_weights(router_w, router_b):
    d, e = router_w.shape
    wp = jnp.zeros((d, LANES), F32).at[:, :e].set(router_w)
    hi = wp.astype(BF16)
    lo = (wp - hi.astype(F32)).astype(BF16)
    rb = jnp.full((1, LANES), -jnp.inf, F32).at[0, :e].set(router_b)
    return hi, lo, rb


def _deinterleave_perm():
    p = np.zeros((GU_BLOCK, GU_BLOCK), np.float32)
    m = np.arange(GU_BLOCK // 2)
    p[2 * m, m] = 1.0
    p[2 * m + 1, GU_BLOCK // 2 + m] = 1.0
    return jnp.asarray(p, dtype=BF16)


def _ffn_kernel(te_ref, tv_ref, xa_ref, xb_ref, wgu_ref, bgu_ref, wd_ref, bd_ref, perm_ref, y_ref,
                wgu_s, wd_s):
    j = pl.program_id(0)
    n_blk = wgu_s.shape[1] // GU_BLOCK
    half = GU_BLOCK // 2

    @pl.when(jnp.logical_or(j == 0, te_ref[j] != te_ref[jnp.maximum(j - 1, 0)]))
    def _():
        for b in range(n_blk):
            sl = slice(b * GU_BLOCK, (b + 1) * GU_BLOCK)
            wgu_s[:, sl] = _dot(wgu_ref[0, 0, :, sl].astype(BF16), perm_ref[...]).astype(BF16)
        wd_s[...] = wd_ref[0, 0].astype(BF16)

    @pl.when(tv_ref[j] > 0)
    def _():
        a_hi, a_lo = _unpack_bf16_pairs(xa_ref[...])
        b_hi, b_lo = _unpack_bf16_pairs(xb_ref[...])
        x = jnp.concatenate([a_hi, b_hi, a_lo, b_lo], axis=-1).astype(BF16)
        acts = []
        for b in range(n_blk):
            sl = slice(b * GU_BLOCK, (b + 1) * GU_BLOCK)
            gu = _dot(x, wgu_s[:, sl]) + bgu_ref[0, :, sl]
            glu = jnp.minimum(gu[:, :half], SWIGLU_LIMIT)
            lin = jnp.clip(gu[:, half:], -SWIGLU_LIMIT, SWIGLU_LIMIT)
            acts.append(((lin + 1.0) * (glu * jax.nn.sigmoid(SWIGLU_ALPHA * glu))).astype(BF16))
        a = jnp.concatenate(acts, axis=-1)
        y_ref[...] = (_dot(a, wd_s[...]) + bd_ref[0]).astype(BF16)

    @pl.when(tv_ref[j] == 0)
    def _():
        y_ref[...] = jnp.zeros_like(y_ref)


def _expert_ffn(xsa, xsb, tile_expert, tile_valid, layer, w_gu, b_gu, w_down, b_down, tm):
    p, q = xsa.shape
    d = 4 * q
    _, e, _, f2 = w_gu.shape
    f = f2 // 2
    half = GU_BLOCK // 2
    bgu = jnp.stack([b_gu[:, 0::2].reshape(e, f // half, half),
                     b_gu[:, 1::2].reshape(e, f // half, half)], axis=2).reshape(e, 1, f2)
    wspec = lambda a, b: pl.BlockSpec((1, 1, a, b), lambda j, te, tv: (layer, te[j], 0, 0))
    bspec = lambda b: pl.BlockSpec((1, 1, b), lambda j, te, tv: (te[j], 0, 0))
    return pl.pallas_call(
        _ffn_kernel,
        grid_spec=pltpu.PrefetchScalarGridSpec(
            num_scalar_prefetch=2,
            grid=(p // tm,),
            in_specs=[pl.BlockSpec((tm, q), lambda j, te, tv: (j, 0)),
                      pl.BlockSpec((tm, q), lambda j, te, tv: (j, 0)),
                      wspec(d, f2), bspec(f2), wspec(f, d), bspec(d),
                      pl.BlockSpec((GU_BLOCK, GU_BLOCK), lambda j, te, tv: (0, 0))],
            out_specs=pl.BlockSpec((tm, d), lambda j, te, tv: (j, 0)),
            scratch_shapes=[pltpu.VMEM((d, f2), BF16), pltpu.VMEM((f, d), BF16)]),
        out_shape=jax.ShapeDtypeStruct((p, d), BF16),
        compiler_params=_cparams(1),
    )(tile_expert, tile_valid, xsa, xsb, w_gu, bgu, w_down, b_down.reshape(e, 1, d),
      _deinterleave_perm())


def _moe_plan(idx4, rank4, counts, tm):
    t = idx4.shape[0]
    n_experts = counts.shape[0]
    pc = ((counts + tm - 1) // tm) * tm
    pend = jnp.cumsum(pc)
    pstart = pend - pc
    dpos_t = (pstart[idx4] + rank4).T.astype(jnp.int32)
    p = t * TOP_K + n_experts * tm
    tile_start = jnp.arange(p // tm, dtype=jnp.int32) * tm
    tile_expert = jnp.minimum(jnp.sum((tile_start[:, None] >= pend[None, :]).astype(jnp.int32), axis=1),
                              n_experts - 1).astype(jnp.int32)
    tile_valid = (tile_start < pend[-1]).astype(jnp.int32)
    return dpos_t, p, tile_expert, tile_valid


SC_WINDOW = 128


def _sc_dispatch(rows, dpos_t, p):
    t, w = rows.shape
    k = dpos_t.shape[0]
    mesh = plsc.VectorSubcoreMesh(core_axis_name="core", subcore_axis_name="subcore")

    @functools.partial(pl.kernel, out_type=jax.ShapeDtypeStruct((p, w), rows.dtype), mesh=mesh)
    def kern(x_hbm, *refs):
        i_hbms, o_hbm = refs[:k], refs[k]

        def body(x_vmem, *i_vmems):
            for iv in i_vmems:
                pltpu.sync_copy(x_vmem, o_hbm.at[iv.at[0]])

        pltpu.emit_pipeline(
            body,
            grid=(t // SC_WINDOW,),
            in_specs=[pl.BlockSpec((SC_WINDOW, w), index_map=lambda i: (i, 0))]
            + [pl.BlockSpec((1, SC_WINDOW), index_map=lambda i: (0, i)) for _ in range(k)],
            out_specs=[],
            core_axis_name=("core", "subcore"),
            dimension_semantics=(pltpu.PARALLEL,),
        )(x_hbm, *i_hbms)

    return kern(rows, *[dpos_t[kk:kk + 1] for kk in range(k)])


def _combine_kernel(alpha, yg_ref, gw_ref, x_ref, g2_ref, lng_ref, lnb_ref, *rest):
    o_ref = rest[-1]
    gw = gw_ref[...]
    y = yg_ref[0].astype(F32) * gw[:, 0:1]
    for k in range(1, TOP_K):
        y = y + yg_ref[k].astype(F32) * gw[:, k:k + 1]
    o_ref[0] = _layer_norm(alpha * x_ref[0] + g2_ref[0] * y, lng_ref[...], lnb_ref[...])


def _combine_ln(yg, gw, x, g2, lng, lnb, alpha, ts, yg_row0, gw_row0, b0, nb, prev=None):
    bsz, s, d = x.shape
    nblk = s // ts
    yo, go = yg_row0 // ts, gw_row0 // ts
    if g2.shape[0] > 1:
        mod = pl.BlockSpec((1, 1, d), lambda b, i: (b + b0, 0, 0))
    else:
        mod = pl.BlockSpec((1, 1, d), lambda b, i: (0, 0, 0))
    in_specs = [pl.BlockSpec((TOP_K, ts, d), lambda b, i: (0, yo + b * nblk + i, 0)),
                pl.BlockSpec((ts, LANES), lambda b, i: (go + b * nblk + i, 0)),
                pl.BlockSpec((1, ts, d), lambda b, i: (b + b0, i, 0)),
                mod, _const_spec((1, d)), _const_spec((1, d))]
    args = [yg, gw, x, g2, lng.reshape(1, d), lnb.reshape(1, d)]
    aliases = {}
    if prev is not None:
        in_specs.append(pl.BlockSpec(memory_space=pl.ANY))
        args.append(prev)
        aliases = {len(args) - 1: 0}
    return pl.pallas_call(
        functools.partial(_combine_kernel, alpha),
        grid=(nb, nblk),
        in_specs=in_specs,
        out_specs=pl.BlockSpec((1, ts, d), lambda b, i: (b + b0, i, 0)),
        out_shape=jax.ShapeDtypeStruct((bsz, s, d), F32),
        input_output_aliases=aliases,
        compiler_params=_cparams(2),
    )(*args)


def kernel(x, c, ctx, c_ctx, mod_w, mod_b, ln1_g, ln1_b, ln2_g, ln2_b, even_w_in, even_w_out, na_rpb, gqa_q_gain, gqa_k_gain, odd_w_in, odd_w_out, diff_lq1, diff_lk1, diff_lq2, diff_lk2, diff_subln, mla_q_gain, mla_w_uq, mla_kv_gain, mla_w_ukv, router_w, router_b, exp_w_gu, exp_b_gu, exp_w_down, exp_b_down):
    bsz, s, d = x.shape
    l = ctx.shape[1]
    depth = mod_w.shape[0]
    n_experts = router_w.shape[-1]
    alpha = (2 * depth) ** 0.25
    ts = min(512, s)
    tm = MOE_TILE

    pad = (-(bsz + 1)) % 8
    c_all = jnp.concatenate([c, c_ctx[None, :], jnp.zeros((pad, d), F32)], axis=0)
    mod = _modulation_all(c_all, mod_w, mod_b)

    for i in range(depth):
        last = i == depth - 1
        j = i // 2
        ml = [mod[i, :bsz, k * d:(k + 1) * d].reshape(bsz, 1, d) for k in range(6)]
        mc = [mod[i, bsz:bsz + 1, k * d:(k + 1) * d].reshape(1, 1, d) for k in range(6)]
        sh1, sc1, g1, sh2, sc2, g2 = ml
        csh1, csc1, cg1, csh2, csc2, cg2 = mc

        if i % 2 == 0:
            w = _even_weights(even_w_in[j])
            naq, nak, nav, gq, gk, gv = _inproj_even(
                x, sc1, sh1, w, _even_tables(s, gqa_q_gain[j], gqa_k_gain[j], True), ts)
            cnaq, cnak, cnav, cgq, cgk, cgv = _inproj_even(
                ctx, csc1, csh1, w, _even_tables(l, gqa_q_gain[j], gqa_k_gain[j], False), l)
            o1 = _na_attention(naq, nak, nav, cnak, cnav, na_rpb[j])
            o2 = _gqa_attention(gq, gk, gv, cgk, cgv, 256, ATTN_SUB)
            if not last:
                co1 = _slot_attention(cnaq, None, None, cnak, cnav, NA_HEADS, False, l)
                co2 = _gqa_attention(cgq, None, None, cgk, cgv, l)
            w_out = even_w_out[j]
        else:
            lam_init = 0.8 - 0.6 * math.exp(-0.3 * i)
            weights = _odd_weights(odd_w_in[j], mla_w_uq[j], mla_w_ukv[j])
            gains = (mla_q_gain[j], mla_kv_gain[j])
            dq, dk, dv, mq, mk, mv = _inproj_odd(x, sc1, sh1, weights, gains, _odd_tables(s, True), ts)
            cdq, cdk, cdv, cmq, cmk, cmv = _inproj_odd(ctx, csc1, csh1, weights, gains,
                                                       _odd_tables(l, False), l)
            lams = (diff_lq1[j], diff_lk1[j], diff_lq2[j], diff_lk2[j])
            o1 = _diff_attention(dq, dk, dv, cdk, cdv, lams, diff_subln[j], lam_init, 256, ATTN_SUB)
            o2 = _slot_attention(mq, mk, mv, cmk, cmv, MLA_HEADS, True, 512)
            if not last:
                co1 = _diff_attention(cdq, None, None, cdk, cdv, lams, diff_subln[j], lam_init, l)
                co2 = _slot_attention(cmq, None, None, cmk, cmv, MLA_HEADS, True, l)
            w_out = odd_w_out[j]

        router = _router_weights(router_w[i], router_b[i])
        cnt0 = jnp.zeros((1, LANES), F32)
        x, h2a, h2b, ridx, rrank, rgw, cnt = _outproj_ln_router(
            o1, o2, w_out, x, g1, ln1_g[i], ln1_b[i], sc2, sh2, router, cnt0, alpha, ts)
        flat = lambda a, n: a.reshape(bsz * n, a.shape[-1])
        h2a, h2b, ridx, rrank, rgw = [flat(a, s) for a in (h2a, h2b, ridx, rrank, rgw)]
        if not last:
            ctx, ch2a, ch2b, cidx, crank, cgw, cnt = _outproj_ln_router(
                co1, co2, w_out, ctx, cg1, ln1_g[i], ln1_b[i], csc2, csh2, router, cnt, alpha, l)
            cat = lambda a, b: jnp.concatenate([a, flat(b, l)], axis=0)
            h2a, h2b = cat(h2a, ch2a), cat(h2b, ch2b)
            ridx, rrank, rgw = cat(ridx, cidx), cat(rrank, crank), cat(rgw, cgw)

        counts = cnt[0, :n_experts].astype(jnp.int32)
        dpos_t, p, tile_expert, tile_valid = _moe_plan(ridx[:, :TOP_K], rrank[:, :TOP_K], counts, tm)
        xsa = _sc_dispatch(h2a, dpos_t, p)
        xsb = _sc_dispatch(h2b, dpos_t, p)
        ys = _expert_ffn(xsa, xsb, tile_expert, tile_valid, i, exp_w_gu, exp_b_gu[i],
                         exp_w_down, exp_b_down[i], tm)
        t_lat, nb_a = bsz * s, bsz // 2
        t_a = nb_a * s
        yg_a = ys.at[dpos_t[:, :t_a]].get(mode="promise_in_bounds")
        yg_b = ys.at[dpos_t[:, t_a:]].get(mode="promise_in_bounds")
        x_a = _combine_ln(yg_a, rgw, x, g2, ln2_g[i], ln2_b[i], alpha, ts, 0, 0, 0, nb_a)
        x = _combine_ln(yg_b, rgw, x, g2, ln2_g[i], ln2_b[i], alpha, ts, 0, t_a, nb_a, bsz - nb_a,
                        prev=x_a)
        if not last:
            ctx = _combine_ln(yg_b, rgw, ctx, cg2, ln2_g[i], ln2_b[i], alpha, l, t_lat - t_a, t_lat,
                              0, bsz)
    return x
```

```python
import functools
import math

import numpy as np
import jax
import jax.numpy as jnp
from jax import lax
from jax.experimental import pallas as pl
from jax.experimental.pallas import tpu as pltpu
from jax.experimental.pallas import tpu_sc as plsc

F32 = jnp.float32
BF16 = jnp.bfloat16

GRID_W = 64
HEAD_DIM = 64
ROPE_THETA = 10000.0
LN_EPS = 1e-6
RMS_EPS = 1e-6
NEG_INF = -1e30
NA_HEADS = 8
NA_WIN_H = 8
NA_WIN_W = 16
GQA_HEADS = 8
GQA_KV_HEADS = 2
DIFF_HEADS = 4
MLA_HEADS = 8
MLA_Q_RANK = 256
MLA_KV_RANK = 128
MLA_NOPE = 64
MLA_ROPE = 32
MLA_V = 64
MLA_QK = MLA_NOPE + MLA_ROPE
TOP_K = 4
SWIGLU_ALPHA = 1.702
SWIGLU_LIMIT = 7.0
LOG2E = 1.4426950408889634

LANES = 128
VMEM_LIMIT = 56 * 1024 * 1024
MOE_TILE = 1024
GU_BLOCK = 512
ATTN_SUB = 2


def _cparams(n_axes):
    return pltpu.CompilerParams(dimension_semantics=("arbitrary",) * n_axes,
                                vmem_limit_bytes=VMEM_LIMIT)


def _dot(a, b):
    return jnp.dot(a, b, preferred_element_type=F32)


def _dot_nt(a, b):
    return lax.dot_general(a, b, (((1,), (1,)), ((), ())), preferred_element_type=F32)


def _mod_kernel(c_ref, w_ref, b_ref, o_ref):
    cv = c_ref[...]
    a = (cv * jax.nn.sigmoid(cv)).astype(BF16)
    o_ref[0] = _dot(a, w_ref[0].astype(BF16)) + b_ref[0]


def _modulation_all(c_all, mod_w, mod_b):
    depth, d, n = mod_w.shape
    rows = c_all.shape[0]
    tn = 1536
    return pl.pallas_call(
        _mod_kernel,
        grid=(depth, n // tn),
        in_specs=[pl.BlockSpec((rows, d), lambda i, j: (0, 0)),
                  pl.BlockSpec((1, d, tn), lambda i, j: (i, 0, j)),
                  pl.BlockSpec((1, 1, tn), lambda i, j: (i, 0, j))],
        out_specs=pl.BlockSpec((1, rows, tn), lambda i, j: (i, 0, j)),
        out_shape=jax.ShapeDtypeStruct((depth, rows, n), F32),
        compiler_params=_cparams(2),
    )(c_all, mod_w, mod_b.reshape(depth, 1, n))


def _rope_tables(s, dim):
    pos = jnp.arange(s)
    row = (pos // GRID_W).astype(F32)[:, None]
    col = (pos % GRID_W).astype(F32)[:, None]
    quarter = dim // 4
    inv_freq = ROPE_THETA ** (-jnp.arange(quarter, dtype=F32) / quarter)
    ar, ac = row * inv_freq, col * inv_freq
    cos = jnp.concatenate([jnp.cos(ar), jnp.cos(ar), jnp.cos(ac), jnp.cos(ac)], axis=-1)
    sin = jnp.concatenate([-jnp.sin(ar), jnp.sin(ar), -jnp.sin(ac), jnp.sin(ac)], axis=-1)
    return cos, sin


def _swap_perm(dim):
    q = dim // 4
    idx = np.arange(dim)
    return np.where((idx % (2 * q)) < q, idx + q, idx - q)


def _rope_partner(y, dim):
    q = dim // 4
    lane = lax.broadcasted_iota(jnp.int32, (1, LANES), 1)
    first = (lane % (2 * q)) < q
    tiles = []
    for j in range(0, y.shape[1], LANES):
        t = y[:, j:j + LANES]
        tiles.append(jnp.where(first, pltpu.roll(t, LANES - q, 1), pltpu.roll(t, q, 1)))
    return tiles[0] if len(tiles) == 1 else jnp.concatenate(tiles, axis=-1)


def _swap_cols(w, dim):
    n = w.shape[-1] // dim
    perm = (np.arange(n)[:, None] * dim + _swap_perm(dim)[None, :]).reshape(-1)
    return w[..., perm]


def _block_ones(n, blk):
    i = np.arange(n) // blk
    return jnp.asarray((i[:, None] == i[None, :]).astype(np.float32), dtype=BF16)


E_NAQ, E_NAK, E_NAV, E_GQ, E_GK, E_GV, E_END = (0, 512, 1024, 1536, 2048, 2304, 2560)


def _even_weights(w_in):
    naq, nak, nav, gq, gk, gv = jnp.split(w_in, [512, 1024, 1536, 2048, 2176], axis=-1)
    k0, k1 = gk[:, :64], gk[:, 64:]
    v0, v1 = gv[:, :64], gv[:, 64:]
    gk2 = jnp.concatenate([k0, k1, k1, k0], axis=-1)
    gv2 = jnp.concatenate([v0, v1, v1, v0], axis=-1)
    w = jnp.concatenate([naq * (0.125 * LOG2E), nak, nav, gq, gk2, gv2], axis=-1).astype(BF16)
    return w


def _even_tables(s, q_gain, k_gain, rope):
    sw = _swap_perm(64)
    if rope:
        cos, sin = _rope_tables(s, 64)
    else:
        cos, sin = jnp.ones((s, 64), F32), jnp.zeros((s, 64), F32)
    qa = jnp.tile(cos * q_gain[None, :] * (0.125 * LOG2E), (1, 8))
    qb = jnp.tile(sin * q_gain[sw][None, :] * (0.125 * LOG2E), (1, 8))
    ka = jnp.tile(cos * k_gain[None, :], (1, 4))
    kb = jnp.tile(sin * k_gain[sw][None, :], (1, 4))
    return qa, qb, ka, kb


def _inproj_even_kernel(x_ref, sc_ref, sh_ref, w_ref, g512_ref, g256_ref,
                        qa_ref, qb_ref, ka_ref, kb_ref,
                        naq_ref, nak_ref, nav_ref, gq_ref, gk_ref, gv_ref):
    h = (x_ref[0] * (1.0 + sc_ref[0]) + sh_ref[0]).astype(BF16)
    naq_ref[0] = _dot(h, w_ref[:, E_NAQ:E_NAK]).astype(BF16)
    nak_ref[0] = _dot(h, w_ref[:, E_NAK:E_NAV]).astype(BF16)
    nav_ref[0] = _dot(h, w_ref[:, E_NAV:E_GQ]).astype(BF16)
    gv_ref[0] = _dot(h, w_ref[:, E_GV:E_END]).astype(BF16)
    y = _dot(h, w_ref[:, E_GQ:E_GK])
    ys = _rope_partner(y, HEAD_DIM)
    r = lax.rsqrt(_dot((y * y).astype(BF16), g512_ref[...]) * (1.0 / HEAD_DIM) + RMS_EPS)
    gq_ref[0] = (r * (y * qa_ref[...] + ys * qb_ref[...])).astype(BF16)
    y = _dot(h, w_ref[:, E_GK:E_GV])
    ys = _rope_partner(y, HEAD_DIM)
    r = lax.rsqrt(_dot((y * y).astype(BF16), g256_ref[...]) * (1.0 / HEAD_DIM) + RMS_EPS)
    gk_ref[0] = (r * (y * ka_ref[...] + ys * kb_ref[...])).astype(BF16)


def _mod_spec(d, batched):
    if batched:
        return pl.BlockSpec((1, 1, d), lambda b, s: (b, 0, 0))
    return pl.BlockSpec((1, 1, d), lambda b, s: (0, 0, 0))


def _const_spec(shape):
    nd = len(shape)
    return pl.BlockSpec(shape, lambda b, s: (0,) * nd)


def _inproj_even(x, sc, sh, w, tables, ts):
    bsz, s, d = x.shape
    batched = sc.shape[0] > 1
    qa, qb, ka, kb = tables
    tok = lambda n: pl.BlockSpec((1, ts, n), lambda b, i: (b, i, 0))
    tab = lambda n: pl.BlockSpec((ts, n), lambda b, i: (i, 0))
    widths = (512, 512, 512, 512, 256, 256)
    return pl.pallas_call(
        _inproj_even_kernel,
        grid=(bsz, s // ts),
        in_specs=[tok(d), _mod_spec(d, batched), _mod_spec(d, batched), _const_spec(w.shape),
                  _const_spec((512, 512)), _const_spec((256, 256)),
                  tab(512), tab(512), tab(256), tab(256)],
        out_specs=[tok(n) for n in widths],
        out_shape=[jax.ShapeDtypeStruct((bsz, s, n), BF16) for n in widths],
        compiler_params=_cparams(2),
    )(x, sc, sh, w, _block_ones(512, 64), _block_ones(256, 64), qa, qb, ka, kb)


O_DQ, O_DK, O_DV, O_CQ, O_CKV, O_PE, O_END = (0, 512, 1024, 1536, 1792, 1920, 2048)


def _odd_weights(w_in, w_uq, w_ukv):
    dq, dk, dv, cq, ckv, kpe = jnp.split(w_in, [512, 1024, 1536, 1792, 1920], axis=-1)
    d = w_in.shape[0]
    pe_slot = jnp.concatenate([kpe, _swap_cols(kpe, 32), jnp.zeros((d, 64), F32)], axis=-1)
    w = jnp.concatenate([dq, dk, dv, cq, ckv, pe_slot], axis=-1).astype(BF16)
    uq = w_uq.reshape(MLA_Q_RANK, MLA_HEADS, MLA_QK)
    z32 = jnp.zeros((MLA_Q_RANK, MLA_HEADS, 32), F32)
    z64 = jnp.zeros((MLA_Q_RANK, MLA_HEADS, 64), F32)
    uq_pad = jnp.concatenate([uq, z32], axis=-1).reshape(MLA_Q_RANK, MLA_HEADS * LANES)
    uq_sw = jnp.concatenate([z64, _swap_cols(uq[..., MLA_NOPE:], 32), z32],
                            axis=-1).reshape(MLA_Q_RANK, MLA_HEADS * LANES)
    wuq2 = jnp.concatenate([uq_pad, uq_sw], axis=-1).astype(BF16)
    ukv = w_ukv.reshape(MLA_KV_RANK, MLA_HEADS, MLA_NOPE + MLA_V)
    zk = jnp.zeros((MLA_KV_RANK, MLA_HEADS, 64), F32)
    wk_pad = jnp.concatenate([ukv[..., :MLA_NOPE], zk], axis=-1).reshape(MLA_KV_RANK, MLA_HEADS * LANES)
    place = np.zeros((LANES, MLA_HEADS, LANES), np.float32)
    for j in range(MLA_ROPE):
        place[j, :, MLA_NOPE + j] = 1.0
        place[MLA_ROPE + j, :, MLA_NOPE + j] = 1.0
    wk2 = jnp.concatenate([wk_pad, jnp.asarray(place.reshape(LANES, MLA_HEADS * LANES))],
                          axis=0).astype(BF16)
    wv = ukv[..., MLA_NOPE:].reshape(MLA_KV_RANK, MLA_HEADS * MLA_V).astype(BF16)
    return w, wuq2, wk2, wv


def _odd_tables(s, rope):
    m_scale = MLA_QK ** -0.5 * LOG2E
    if rope:
        cos64, sin64 = _rope_tables(s, 64)
        cos32, sin32 = _rope_tables(s, 32)
    else:
        cos64, sin64 = jnp.ones((s, 64), F32), jnp.zeros((s, 64), F32)
        cos32, sin32 = jnp.ones((s, 32), F32), jnp.zeros((s, 32), F32)
    one64, z32, z64 = jnp.ones((s, 64), F32), jnp.zeros((s, 32), F32), jnp.zeros((s, 64), F32)
    dcos, dsin = jnp.tile(cos64, (1, 8)), jnp.tile(sin64, (1, 8))
    qa = jnp.tile(jnp.concatenate([one64, cos32, z32], axis=-1) * m_scale, (1, MLA_HEADS))
    qb = jnp.tile(jnp.concatenate([z64, sin32, z32], axis=-1) * m_scale, (1, MLA_HEADS))
    pe = jnp.concatenate([cos32, sin32, z64], axis=-1)
    return dcos, dsin, qa, qb, pe


def _inproj_odd_kernel(x_ref, sc_ref, sh_ref, w_ref, wuq_ref, wk_ref, wv_ref, qg_ref, kvg_ref,
                       dcos_ref, dsin_ref, qa_ref, qb_ref, pe_ref,
                       dq_ref, dk_ref, dv_ref, mq_ref, mk_ref, mv_ref):
    h = (x_ref[0] * (1.0 + sc_ref[0]) + sh_ref[0]).astype(BF16)
    dcos, dsin = dcos_ref[...], dsin_ref[...]
    y = _dot(h, w_ref[:, O_DQ:O_DK])
    ys = _rope_partner(y, HEAD_DIM)
    dq_ref[0] = ((y * dcos + ys * dsin) * (0.125 * LOG2E)).astype(BF16)
    y = _dot(h, w_ref[:, O_DK:O_DV])
    ys = _rope_partner(y, HEAD_DIM)
    dk_ref[0] = (y * dcos + ys * dsin).astype(BF16)
    dv_ref[0] = _dot(h, w_ref[:, O_DV:O_CQ]).astype(BF16)
    cq = _dot(h, w_ref[:, O_CQ:O_CKV])
    nq = cq * lax.rsqrt(jnp.mean(cq * cq, axis=-1, keepdims=True) + RMS_EPS) * qg_ref[...]
    y2 = _dot(nq.astype(BF16), wuq_ref[...])
    half = MLA_HEADS * LANES
    mq_ref[0] = (y2[:, :half] * qa_ref[...] + y2[:, half:] * qb_ref[...]).astype(BF16)
    ckv = _dot(h, w_ref[:, O_CKV:O_PE])
    nk = ckv * lax.rsqrt(jnp.mean(ckv * ckv, axis=-1, keepdims=True) + RMS_EPS) * kvg_ref[...]
    pe = _dot(h, w_ref[:, O_PE:O_END]) * pe_ref[...]
    nkb = nk.astype(BF16)
    cat = jnp.concatenate([nkb, pe.astype(BF16)], axis=-1)
    mk_ref[0] = _dot(cat, wk_ref[...]).astype(BF16)
    mv_ref[0] = _dot(nkb, wv_ref[...]).astype(BF16)


def _inproj_odd(x, sc, sh, weights, gains, tables, ts):
    bsz, s, d = x.shape
    batched = sc.shape[0] > 1
    w, wuq2, wk2, wv = weights
    qg, kvg = gains
    dcos, dsin, qa, qb, pe = tables
    tok = lambda n: pl.BlockSpec((1, ts, n), lambda b, i: (b, i, 0))
    tab = lambda n: pl.BlockSpec((ts, n), lambda b, i: (i, 0))
    widths = (512, 512, 512, 1024, 1024, 512)
    return pl.pallas_call(
        _inproj_odd_kernel,
        grid=(bsz, s // ts),
        in_specs=[tok(d), _mod_spec(d, batched), _mod_spec(d, batched), _const_spec(w.shape),
                  _const_spec(wuq2.shape), _const_spec(wk2.shape), _const_spec(wv.shape),
                  _const_spec((1, MLA_Q_RANK)), _const_spec((1, MLA_KV_RANK)),
                  tab(512), tab(512), tab(1024), tab(1024), tab(128)],
        out_specs=[tok(n) for n in widths],
        out_shape=[jax.ShapeDtypeStruct((bsz, s, n), BF16) for n in widths],
        compiler_params=_cparams(2),
    )(x, sc, sh, w, wuq2, wk2, wv, qg.reshape(1, -1), kvg.reshape(1, -1), dcos, dsin, qa, qb, pe)


def _half_masks():
    lane = lax.broadcasted_iota(jnp.int32, (1, LANES), 1)
    return lane < HEAD_DIM, lane >= HEAD_DIM


def _softmax_values(ss, vs):
    m = functools.reduce(jnp.maximum, [jnp.max(s, axis=-1, keepdims=True) for s in ss])
    es = [jnp.exp2(s - m) for s in ss]
    l = functools.reduce(lambda a, b: a + b, [jnp.sum(e, axis=-1, keepdims=True) for e in es])
    o = functools.reduce(lambda a, b: a + b, [_dot(e.astype(BF16), v) for e, v in zip(es, vs)])
    return o * (1.0 / l)


def _attend(qm, ks, vs):
    return _softmax_values([_dot_nt(qm, k) for k in ks], vs)


NA_RB = 4
NA_WIN_ROWS = NA_RB + NA_WIN_H - 1
NA_SUB = 2


def _na_block_plan(rows):
    plan = []
    for blk in range(rows // NA_RB):
        r0 = blk * NA_RB
        rs = [int(np.clip(r0 + i - NA_WIN_H // 2, 0, rows - NA_WIN_H)) for i in range(NA_RB)]
        ws = int(np.clip(r0 - NA_WIN_H // 2, 0, rows - NA_WIN_ROWS))
        pat = tuple((rs[i] - ws, r0 + i - ws) for i in range(NA_RB))
        assert all(0 <= o and o + NA_WIN_H <= NA_WIN_ROWS for o, _ in pat)
        plan.append(pat)
    assert all(p == plan[1] for p in plan[1:-1])
    return (plan[0], plan[1], plan[-1])


def _na_bias_table(rpb, rows):
    cols = np.arange(GRID_W)
    col_start = np.clip(cols - NA_WIN_W // 2, 0, GRID_W - NA_WIN_W)
    col_mask = (cols[None, :] >= col_start[:, None]) & (cols[None, :] < col_start[:, None] + NA_WIN_W)
    col_idx = np.clip(cols[None, :] - cols[:, None] + NA_WIN_W - 1, 0, 2 * NA_WIN_W - 2)
    wr = np.arange(NA_WIN_ROWS)
    tables = []
    for pat in _na_block_plan(rows):
        off = np.array([o for o, _ in pat])[:, None]
        rq = np.array([r for _, r in pat])[:, None]
        row_ok = (wr[None, :] >= off) & (wr[None, :] < off + NA_WIN_H)
        ridx = np.clip(wr[None, :] - rq + NA_WIN_H - 1, 0, 2 * NA_WIN_H - 2)
        t = rpb.astype(F32)[:, ridx]
        t = t[..., col_idx]
        t = t.transpose(0, 1, 3, 2, 4)
        ok = row_ok[None, :, None, :, None] & col_mask[None, None, :, None, :]
        t = jnp.where(ok, t * LOG2E, NEG_INF)
        tables.append(t.reshape(NA_HEADS, NA_RB * GRID_W, NA_WIN_ROWS * GRID_W))
    return jnp.stack(tables)


def _na_kernel(rows, sub, q_ref, k_ref, v_ref, kc_ref, vc_ref, *refs):
    bt_refs, o_ref = refs[:sub], refs[sub]
    step = pl.program_id(1)
    tq = NA_RB * GRID_W
    m0, m1 = _half_masks()
    for u in range(sub):
        r0 = (step * sub + u) * NA_RB
        ws = jnp.clip(r0 - NA_WIN_H // 2, 0, rows - NA_WIN_ROWS)
        win = pl.ds(pl.multiple_of(ws * GRID_W, GRID_W), NA_WIN_ROWS * GRID_W)
        qrows = slice(u * tq, (u + 1) * tq)
        for j in range(NA_HEADS // 2):
            sl = slice(j * LANES, (j + 1) * LANES)
            qp = q_ref[0, qrows, sl]
            kp, vp = k_ref[0, win, sl], v_ref[0, win, sl]
            kcp, vcp = kc_ref[0, :, sl], vc_ref[0, :, sl]
            outs = []
            for par, msk in ((0, m0), (1, m1)):
                qm = jnp.where(msk, qp, jnp.zeros_like(qp))
                s_loc = _dot_nt(qm, kp) + bt_refs[u][0, 2 * j + par]
                outs.append(_softmax_values([s_loc, _dot_nt(qm, kcp)], [vp, vcp]))
            o_ref[0, qrows, sl] = jnp.where(m0, outs[0], outs[1]).astype(BF16)


def _na_attention(q, k, v, kc, vc, rpb):
    bsz, s, w = q.shape
    l = kc.shape[1]
    rows = s // GRID_W
    nblk = rows // NA_RB
    sub = NA_SUB
    nstep = nblk // sub
    assert rows % NA_RB == 0 and rows >= NA_WIN_ROWS and nblk % sub == 0 and nblk >= 3
    bt = _na_bias_table(rpb, rows)
    tq = NA_RB * GRID_W * sub
    full = lambda n: pl.BlockSpec((1, n, w), lambda b, r: (b, 0, 0))

    def bt_spec(u):
        def kind(b, r):
            blk = r * sub + u
            return ((blk > 0).astype(jnp.int32) + (blk == nblk - 1).astype(jnp.int32), 0, 0, 0)
        return pl.BlockSpec((1,) + bt.shape[1:], kind)

    return pl.pallas_call(
        functools.partial(_na_kernel, rows, sub),
        grid=(bsz, nstep),
        in_specs=[pl.BlockSpec((1, tq, w), lambda b, r: (b, r, 0)),
                  full(s), full(s), full(l), full(l)] + [bt_spec(u) for u in range(sub)],
        out_specs=pl.BlockSpec((1, tq, w), lambda b, r: (b, r, 0)),
        out_shape=jax.ShapeDtypeStruct((bsz, s, w), BF16),
        compiler_params=_cparams(2),
    )(q, k, v, kc, vc, *([bt] * sub))


def _slot_attn_kernel(n_heads, q_slot, has_lat, sub, *refs):
    if has_lat:
        q_ref, k_ref, v_ref, kc_ref, vc_ref, o_ref = refs
    else:
        q_ref, kc_ref, vc_ref, o_ref = refs
    m0, m1 = _half_masks()
    tq = q_ref.shape[1] // sub
    for u in range(sub):
        qrows = slice(u * tq, (u + 1) * tq)
        for j in range(n_heads // 2):
            vsl = slice(j * LANES, (j + 1) * LANES)
            outs = []
            for par, msk in ((0, m0), (1, m1)):
                h = 2 * j + par
                if q_slot:
                    ksl = slice(h * LANES, (h + 1) * LANES)
                    qm = q_ref[0, qrows, ksl]
                else:
                    ksl = vsl
                    qp = q_ref[0, qrows, vsl]
                    qm = jnp.where(msk, qp, jnp.zeros_like(qp))
                ks, vs = [kc_ref[0, :, ksl]], [vc_ref[0, :, vsl]]
                if has_lat:
                    ks.insert(0, k_ref[0, :, ksl])
                    vs.insert(0, v_ref[0, :, vsl])
                outs.append(_attend(qm, ks, vs))
            o_ref[0, qrows, vsl] = jnp.where(m0, outs[0], outs[1]).astype(BF16)


def _slot_attention(q, k, v, kc, vc, n_heads, q_slot, tq, sub=1):
    bsz, sq, wq = q.shape
    l, wk, wv = kc.shape[1], kc.shape[2], vc.shape[2]
    has_lat = k is not None
    qspec = pl.BlockSpec((1, tq, wq), lambda b, i: (b, i, 0))
    full = lambda n, w: pl.BlockSpec((1, n, w), lambda b, i: (b, 0, 0))
    in_specs, args = [qspec], [q]
    if has_lat:
        s = k.shape[1]
        in_specs += [full(s, wk), full(s, wv)]
        args += [k, v]
    in_specs += [full(l, wk), full(l, wv)]
    args += [kc, vc]
    return pl.pallas_call(
        functools.partial(_slot_attn_kernel, n_heads, q_slot, has_lat, sub),
        grid=(bsz, sq // tq),
        in_specs=in_specs,
        out_specs=pl.BlockSpec((1, tq, wv), lambda b, i: (b, i, 0)),
        out_shape=jax.ShapeDtypeStruct((bsz, sq, wv), BF16),
        compiler_params=_cparams(2),
    )(*args)


def _gqa_kernel(has_lat, tq, sub, *refs):
    if has_lat:
        q_ref, k_ref, v_ref, kc_ref, vc_ref, o_ref = refs
    else:
        q_ref, kc_ref, vc_ref, o_ref = refs
    masks = _half_masks()
    for u in range(sub):
        qrows = slice(u * tq, (u + 1) * tq)
        res = {}
        for g in range(GQA_KV_HEADS):
            for var in range(2):
                par = g if var == 0 else 1 - g
                heads = (4 * g + par, 4 * g + 2 + par)
                vsl = slice(var * LANES, (var + 1) * LANES)
                qs = []
                for h in heads:
                    qp = q_ref[0, qrows, (h // 2) * LANES:(h // 2 + 1) * LANES]
                    qs.append(jnp.where(masks[par], qp, jnp.zeros_like(qp)))
                qm = jnp.concatenate(qs, axis=0)
                ks, vs = [kc_ref[0, :, vsl]], [vc_ref[0, :, vsl]]
                if has_lat:
                    ks.insert(0, k_ref[0, :, vsl])
                    vs.insert(0, v_ref[0, :, vsl])
                o = _attend(qm, ks, vs)
                res[heads[0]] = o[:tq]
                res[heads[1]] = o[tq:]
        for j in range(GQA_HEADS // 2):
            o_ref[0, qrows, j * LANES:(j + 1) * LANES] = jnp.where(
                masks[0], res[2 * j], res[2 * j + 1]).astype(BF16)


def _gqa_attention(q, k2, v2, k2c, v2c, tq, sub=1):
    bsz, sq, wq = q.shape
    l = k2c.shape[1]
    has_lat = k2 is not None
    full = lambda n: pl.BlockSpec((1, n, 2 * LANES), lambda b, i: (b, 0, 0))
    in_specs, args = [pl.BlockSpec((1, tq * sub, wq), lambda b, i: (b, i, 0))], [q]
    if has_lat:
        in_specs += [full(k2.shape[1])] * 2
        args += [k2, v2]
    in_specs += [full(l)] * 2
    args += [k2c, v2c]
    return pl.pallas_call(
        functools.partial(_gqa_kernel, has_lat, tq, sub),
        grid=(bsz, sq // (tq * sub)),
        in_specs=in_specs,
        out_specs=pl.BlockSpec((1, tq * sub, wq), lambda b, i: (b, i, 0)),
        out_shape=jax.ShapeDtypeStruct((bsz, sq, wq), BF16),
        compiler_params=_cparams(2),
    )(*args)


def _diff_kernel(has_lat, tq, sub, lam_init, *refs):
    if has_lat:
        q_ref, k_ref, v_ref, kc_ref, vc_ref, lq1, lk1, lq2, lk2, sub_ref, o_ref = refs
    else:
        q_ref, kc_ref, vc_ref, lq1, lk1, lq2, lk2, sub_ref, o_ref = refs
    lam = (jnp.exp(jnp.sum(lq1[...] * lk1[...], axis=-1, keepdims=True))
           - jnp.exp(jnp.sum(lq2[...] * lk2[...], axis=-1, keepdims=True)) + lam_init)
    m0, m1 = _half_masks()
    for u in range(sub):
        qrows = slice(u * tq, (u + 1) * tq)
        for h in range(DIFF_HEADS):
            sl = slice(h * LANES, (h + 1) * LANES)
            qp = q_ref[0, qrows, sl]
            zero = jnp.zeros_like(qp)
            qm = jnp.concatenate([jnp.where(m0, qp, zero), jnp.where(m1, qp, zero)], axis=0)
            ks, vs = [kc_ref[0, :, sl]], [vc_ref[0, :, sl]]
            if has_lat:
                ks.insert(0, k_ref[0, :, sl])
                vs.insert(0, v_ref[0, :, sl])
            o2 = _attend(qm, ks, vs)
            o = o2[:tq] - lam * o2[tq:]
            o = o * lax.rsqrt(jnp.mean(o * o, axis=-1, keepdims=True) + RMS_EPS) * sub_ref[...]
            o_ref[0, qrows, sl] = (o * (1.0 - lam_init)).astype(BF16)


def _diff_attention(q, k, v, kc, vc, lams, subln, lam_init, tq, sub=1):
    bsz, sq, w = q.shape
    l = kc.shape[1]
    has_lat = k is not None
    full = lambda n: pl.BlockSpec((1, n, w), lambda b, i: (b, 0, 0))
    in_specs, args = [pl.BlockSpec((1, tq * sub, w), lambda b, i: (b, i, 0))], [q]
    if has_lat:
        in_specs += [full(k.shape[1])] * 2
        args += [k, v]
    in_specs += [full(l)] * 2 + [_const_spec((1, HEAD_DIM))] * 4 + [_const_spec((1, LANES))]
    args += [kc, vc] + [a.reshape(1, -1).astype(F32) for a in lams] + [subln.reshape(1, -1).astype(F32)]
    return pl.pallas_call(
        functools.partial(_diff_kernel, has_lat, tq, sub, lam_init),
        grid=(bsz, sq // (tq * sub)),
        in_specs=in_specs,
        out_specs=pl.BlockSpec((1, tq * sub, w), lambda b, i: (b, i, 0)),
        out_shape=jax.ShapeDtypeStruct((bsz, sq, w), BF16),
        compiler_params=_cparams(2),
    )(*args)


def _pack_bf16_pairs(v):
    w = v.shape[1] // 2
    hi = pltpu.bitcast(v[:, :w].astype(F32), jnp.int32)
    lo = pltpu.bitcast(v[:, w:].astype(F32), jnp.int32)
    return hi | lax.shift_right_logical(lo, 16)


def _unpack_bf16_pairs(u):
    hi = pltpu.bitcast(u & jnp.int32(-65536), F32)
    lo = pltpu.bitcast(lax.shift_left(u, 16), F32)
    return hi, lo


def _layer_norm(z, g, b):
    mu = jnp.mean(z, axis=-1, keepdims=True)
    zc = z - mu
    var = jnp.mean(zc * zc, axis=-1, keepdims=True)
    return zc * lax.rsqrt(var + LN_EPS) * g + b


def _outproj_kernel(alpha, sub, o1_ref, o2_ref, w1_ref, w2_ref, x_ref, g1_ref, lng_ref, lnb_ref,
                    sc2_ref, sh2_ref, rwh_ref, rwl_ref, rb_ref, tri_ref, cnt0_ref,
                    xo_ref, h2a_ref, h2b_ref, idx_ref, rank_ref, gw_ref, cnt_ref, carry_ref):
    @pl.when(jnp.logical_and(pl.program_id(0) == 0, pl.program_id(1) == 0))
    def _():
        carry_ref[...] = cnt0_ref[...]

    tr = x_ref.shape[1] // sub
    carry = carry_ref[...]
    for u in range(sub):
        rows = slice(u * tr, (u + 1) * tr)
        o = _dot(o1_ref[0, rows], w1_ref[...]) + _dot(o2_ref[0, rows], w2_ref[...])
        xn = _layer_norm(alpha * x_ref[0, rows] + g1_ref[0] * o, lng_ref[...], lnb_ref[...])
        xo_ref[0, rows] = xn
        h2 = xn * (1.0 + sc2_ref[0]) + sh2_ref[0]
        hi = h2.astype(BF16)
        packed = _pack_bf16_pairs(hi)
        q = packed.shape[1] // 2
        h2a_ref[0, rows] = packed[:, :q]
        h2b_ref[0, rows] = packed[:, q:]
        lo = (h2 - hi.astype(F32)).astype(BF16)
        logits = (_dot(hi, rwh_ref[...]) + _dot(lo, rwh_ref[...]) + _dot(hi, rwl_ref[...])) + rb_ref[...]
        lane = lax.broadcasted_iota(jnp.int32, logits.shape, 1).astype(F32)
        vals, idxs = [], []
        cur = logits
        for _ in range(TOP_K):
            m = jnp.max(cur, axis=-1, keepdims=True)
            ik = jnp.min(jnp.where(cur == m, lane, float(LANES)), axis=-1, keepdims=True)
            vals.append(m)
            idxs.append(ik)
            cur = jnp.where(lane == ik, -jnp.inf, cur)
        ws = [jnp.exp(v - vals[0]) for v in vals]
        inv = 1.0 / functools.reduce(lambda a, b: a + b, ws)
        sel = [lane == ik for ik in idxs]
        onehot = functools.reduce(lambda a, b: a + b, [m.astype(F32) for m in sel])
        before = _dot(tri_ref[...], onehot.astype(BF16)) + carry
        idx_out = jnp.zeros_like(logits)
        rank_out = jnp.zeros_like(logits)
        w_out = jnp.zeros_like(logits)
        for k in range(TOP_K):
            rk = jnp.sum(jnp.where(sel[k], before, 0.0), axis=-1, keepdims=True)
            idx_out = jnp.where(lane == float(k), idxs[k], idx_out)
            rank_out = jnp.where(lane == float(k), rk, rank_out)
            w_out = jnp.where(lane == float(k), ws[k] * inv, w_out)
        idx_ref[0, rows] = idx_out.astype(jnp.int32)
        rank_ref[0, rows] = rank_out.astype(jnp.int32)
        gw_ref[0, rows] = w_out
        carry = carry + jnp.sum(onehot, axis=0, keepdims=True)
    carry_ref[...] = carry
    cnt_ref[...] = carry


def _outproj_ln_router(o1, o2, w_out, x, g1, lng, lnb, sc2, sh2, router, cnt0, alpha, ts):
    bsz, s, d = x.shape
    batched = g1.shape[0] > 1
    rwh, rwl, rb = router
    w1, w2 = w_out[:512].astype(BF16), w_out[512:].astype(BF16)
    sub = 1
    tr = ts // sub
    tri = jnp.asarray(np.tril(np.ones((tr, tr), np.float32), -1), dtype=BF16)
    tok = lambda n: pl.BlockSpec((1, ts, n), lambda b, i: (b, i, 0))
    ms = _mod_spec(d, batched)
    return pl.pallas_call(
        functools.partial(_outproj_kernel, alpha, sub),
        grid=(bsz, s // ts),
        in_specs=[tok(512), tok(512), _const_spec((512, d)), _const_spec((512, d)), tok(d), ms,
                  _const_spec((1, d)), _const_spec((1, d)), ms, ms,
                  _const_spec((d, LANES)), _const_spec((d, LANES)), _const_spec((1, LANES)),
                  _const_spec((tr, tr)), _const_spec((1, LANES))],
        out_specs=[tok(d), tok(d // 4), tok(d // 4), tok(LANES), tok(LANES), tok(LANES),
                   _const_spec((1, LANES))],
        out_shape=[jax.ShapeDtypeStruct((bsz, s, d), F32),
                   jax.ShapeDtypeStruct((bsz, s, d // 4), jnp.int32),
                   jax.ShapeDtypeStruct((bsz, s, d // 4), jnp.int32),
                   jax.ShapeDtypeStruct((bsz, s, LANES), jnp.int32),
                   jax.ShapeDtypeStruct((bsz, s, LANES), jnp.int32),
                   jax.ShapeDtypeStruct((bsz, s, LANES), F32),
                   jax.ShapeDtypeStruct((1, LANES), F32)],
        scratch_shapes=[pltpu.VMEM((1, LANES), F32)],
        compiler_params=_cparams(2),
    )(o1, o2, w1, w2, x, g1, lng.reshape(1, d), lnb.reshape(1, d), sc2, sh2, rwh, rwl, rb, tri, cnt0)


def _router_weights(router_w, router_b):
    d, e = router_w.shape
    wp = jnp.zeros((d, LANES), F32).at[:, :e].set(router_w)
    hi = wp.astype(BF16)
    lo = (wp - hi.astype(F32)).astype(BF16)
    rb = jnp.full((1, LANES), -jnp.inf, F32).at[0, :e].set(router_b)
    return hi, lo, rb


def _deinterleave_perm():
    p = np.zeros((GU_BLOCK, GU_BLOCK), np.float32)
    m = np.arange(GU_BLOCK // 2)
    p[2 * m, m] = 1.0
    p[2 * m + 1, GU_BLOCK // 2 + m] = 1.0
    return jnp.asarray(p, dtype=BF16)


def _ffn_kernel(te_ref, tv_ref, xa_ref, xb_ref, wgu_ref, bgu_ref, wd_ref, bd_ref, perm_ref, y_ref,
                wgu_s, wd_s):
    j = pl.program_id(0)
    n_blk = wgu_s.shape[1] // GU_BLOCK
    half = GU_BLOCK // 2

    @pl.when(jnp.logical_or(j == 0, te_ref[j] != te_ref[jnp.maximum(j - 1, 0)]))
    def _():
        for b in range(n_blk):
            sl = slice(b * GU_BLOCK, (b + 1) * GU_BLOCK)
            wgu_s[:, sl] = _dot(wgu_ref[0, 0, :, sl].astype(BF16), perm_ref[...]).astype(BF16)
        wd_s[...] = wd_ref[0, 0].astype(BF16)

    @pl.when(tv_ref[j] > 0)
    def _():
        a_hi, a_lo = _unpack_bf16_pairs(xa_ref[...])
        b_hi, b_lo = _unpack_bf16_pairs(xb_ref[...])
        x = jnp.concatenate([a_hi, b_hi, a_lo, b_lo], axis=-1).astype(BF16)
        acts = []
        for b in range(n_blk):
            sl = slice(b * GU_BLOCK, (b + 1) * GU_BLOCK)
            gu = _dot(x, wgu_s[:, sl]) + bgu_ref[0, :, sl]
            glu = jnp.minimum(gu[:, :half], SWIGLU_LIMIT)
            lin = jnp.clip(gu[:, half:], -SWIGLU_LIMIT, SWIGLU_LIMIT)
            acts.append(((lin + 1.0) * (glu * jax.nn.sigmoid(SWIGLU_ALPHA * glu))).astype(BF16))
        a = jnp.concatenate(acts, axis=-1)
        y_ref[...] = (_dot(a, wd_s[...]) + bd_ref[0]).astype(BF16)

    @pl.when(tv_ref[j] == 0)
    def _():
        y_ref[...] = jnp.zeros_like(y_ref)


def _expert_ffn(xsa, xsb, tile_expert, tile_valid, layer, w_gu, b_gu, w_down, b_down, tm):
    p, q = xsa.shape
    d = 4 * q
    _, e, _, f2 = w_gu.shape
    f = f2 // 2
    half = GU_BLOCK // 2
    bgu = jnp.stack([b_gu[:, 0::2].reshape(e, f // half, half),
                     b_gu[:, 1::2].reshape(e, f // half, half)], axis=2).reshape(e, 1, f2)
    wspec = lambda a, b: pl.BlockSpec((1, 1, a, b), lambda j, te, tv: (layer, te[j], 0, 0))
    bspec = lambda b: pl.BlockSpec((1, 1, b), lambda j, te, tv: (te[j], 0, 0))
    return pl.pallas_call(
        _ffn_kernel,
        grid_spec=pltpu.PrefetchScalarGridSpec(
            num_scalar_prefetch=2,
            grid=(p // tm,),
            in_specs=[pl.BlockSpec((tm, q), lambda j, te, tv: (j, 0)),
                      pl.BlockSpec((tm, q), lambda j, te, tv: (j, 0)),
                      wspec(d, f2), bspec(f2), wspec(f, d), bspec(d),
                      pl.BlockSpec((GU_BLOCK, GU_BLOCK), lambda j, te, tv: (0, 0))],
            out_specs=pl.BlockSpec((tm, d), lambda j, te, tv: (j, 0)),
            scratch_shapes=[pltpu.VMEM((d, f2), BF16), pltpu.VMEM((f, d), BF16)]),
        out_shape=jax.ShapeDtypeStruct((p, d), BF16),
        compiler_params=_cparams(1),
    )(tile_expert, tile_valid, xsa, xsb, w_gu, bgu, w_down, b_down.reshape(e, 1, d),
      _deinterleave_perm())


def _moe_plan(idx4, rank4, counts, tm):
    t = idx4.shape[0]
    n_experts = counts.shape[0]
    pc = ((counts + tm - 1) // tm) * tm
    pend = jnp.cumsum(pc)
    pstart = pend - pc
    dpos_t = (pstart[idx4] + rank4).T.astype(jnp.int32)
    p = t * TOP_K + n_experts * tm
    tile_start = jnp.arange(p // tm, dtype=jnp.int32) * tm
    tile_expert = jnp.minimum(jnp.sum((tile_start[:, None] >= pend[None, :]).astype(jnp.int32), axis=1),
                              n_experts - 1).astype(jnp.int32)
    tile_valid = (tile_start < pend[-1]).astype(jnp.int32)
    return dpos_t, p, tile_expert, tile_valid


SC_WINDOW = 128


def _sc_dispatch(rows, dpos_t, p):
    t, w = rows.shape
    k = dpos_t.shape[0]
    mesh = plsc.VectorSubcoreMesh(core_axis_name="core", subcore_axis_name="subcore")

    @functools.partial(pl.kernel, out_type=jax.ShapeDtypeStruct((p, w), rows.dtype), mesh=mesh)
    def kern(x_hbm, *refs):
        i_hbms, o_hbm = refs[:k], refs[k]

        def body(x_vmem, *i_vmems):
            for iv in i_vmems:
                pltpu.sync_copy(x_vmem, o_hbm.at[iv.at[0]])

        pltpu.emit_pipeline(
            body,
            grid=(t // SC_WINDOW,),
            in_specs=[pl.BlockSpec((SC_WINDOW, w), index_map=lambda i: (i, 0))]
            + [pl.BlockSpec((1, SC_WINDOW), index_map=lambda i: (0, i)) for _ in range(k)],
            out_specs=[],
            core_axis_name=("core", "subcore"),
            dimension_semantics=(pltpu.PARALLEL,),
        )(x_hbm, *i_hbms)

    return kern(rows, *[dpos_t[kk:kk + 1] for kk in range(k)])


def _combine_kernel(alpha, yg_ref, gw_ref, x_ref, g2_ref, lng_ref, lnb_ref, *rest):
    o_ref = rest[-1]
    gw = gw_ref[...]
    y = yg_ref[0].astype(F32) * gw[:, 0:1]
    for k in range(1, TOP_K):
        y = y + yg_ref[k].astype(F32) * gw[:, k:k + 1]
    o_ref[0] = _layer_norm(alpha * x_ref[0] + g2_ref[0] * y, lng_ref[...], lnb_ref[...])


def _combine_ln(yg, gw, x, g2, lng, lnb, alpha, ts, yg_row0, gw_row0, b0, nb, prev=None):
    bsz, s, d = x.shape
    nblk = s // ts
    yo, go = yg_row0 // ts, gw_row0 // ts
    if g2.shape[0] > 1:
        mod = pl.BlockSpec((1, 1, d), lambda b, i: (b + b0, 0, 0))
    else:
        mod = pl.BlockSpec((1, 1, d), lambda b, i: (0, 0, 0))
    in_specs = [pl.BlockSpec((TOP_K, ts, d), lambda b, i: (0, yo + b * nblk + i, 0)),
                pl.BlockSpec((ts, LANES), lambda b, i: (go + b * nblk + i, 0)),
                pl.BlockSpec((1, ts, d), lambda b, i: (b + b0, i, 0)),
                mod, _const_spec((1, d)), _const_spec((1, d))]
    args = [yg, gw, x, g2, lng.reshape(1, d), lnb.reshape(1, d)]
    aliases = {}
    if prev is not None:
        in_specs.append(pl.BlockSpec(memory_space=pl.ANY))
        args.append(prev)
        aliases = {len(args) - 1: 0}
    return pl.pallas_call(
        functools.partial(_combine_kernel, alpha),
        grid=(nb, nblk),
        in_specs=in_specs,
        out_specs=pl.BlockSpec((1, ts, d), lambda b, i: (b + b0, i, 0)),
        out_shape=jax.ShapeDtypeStruct((bsz, s, d), F32),
        input_output_aliases=aliases,
        compiler_params=_cparams(2),
    )(*args)


def _forward(x, c, ctx, c_ctx, mod_w, mod_b, ln1_g, ln1_b, ln2_g, ln2_b, even_w_in, even_w_out, na_rpb, gqa_q_gain, gqa_k_gain, odd_w_in, odd_w_out, diff_lq1, diff_lk1, diff_lq2, diff_lk2, diff_subln, mla_q_gain, mla_w_uq, mla_kv_gain, mla_w_ukv, router_w, router_b, exp_w_gu, exp_b_gu, exp_w_down, exp_b_down):
    bsz, s, d = x.shape
    l = ctx.shape[1]
    depth = mod_w.shape[0]
    n_experts = router_w.shape[-1]
    alpha = (2 * depth) ** 0.25
    ts = min(512, s)
    tm = MOE_TILE

    pad = (-(bsz + 1)) % 8
    c_all = jnp.concatenate([c, c_ctx[None, :], jnp.zeros((pad, d), F32)], axis=0)
    mod = _modulation_all(c_all, mod_w, mod_b)

    for i in range(depth):
        last = i == depth - 1
        j = i // 2
        ml = [mod[i, :bsz, k * d:(k + 1) * d].reshape(bsz, 1, d) for k in range(6)]
        mc = [mod[i, bsz:bsz + 1, k * d:(k + 1) * d].reshape(1, 1, d) for k in range(6)]
        sh1, sc1, g1, sh2, sc2, g2 = ml
        csh1, csc1, cg1, csh2, csc2, cg2 = mc

        if i % 2 == 0:
            w = _even_weights(even_w_in[j])
            naq, nak, nav, gq, gk, gv = _inproj_even(
                x, sc1, sh1, w, _even_tables(s, gqa_q_gain[j], gqa_k_gain[j], True), ts)
            cnaq, cnak, cnav, cgq, cgk, cgv = _inproj_even(
                ctx, csc1, csh1, w, _even_tables(l, gqa_q_gain[j], gqa_k_gain[j], False), l)
            o1 = _na_attention(naq, nak, nav, cnak, cnav, na_rpb[j])
            o2 = _gqa_attention(gq, gk, gv, cgk, cgv, 256, ATTN_SUB)
            if not last:
                co1 = _slot_attention(cnaq, None, None, cnak, cnav, NA_HEADS, False, l)
                co2 = _gqa_attention(cgq, None, None, cgk, cgv, l)
            w_out = even_w_out[j]
        else:
            lam_init = 0.8 - 0.6 * math.exp(-0.3 * i)
            weights = _odd_weights(odd_w_in[j], mla_w_uq[j], mla_w_ukv[j])
            gains = (mla_q_gain[j], mla_kv_gain[j])
            dq, dk, dv, mq, mk, mv = _inproj_odd(x, sc1, sh1, weights, gains, _odd_tables(s, True), ts)
            cdq, cdk, cdv, cmq, cmk, cmv = _inproj_odd(ctx, csc1, csh1, weights, gains,
                                                       _odd_tables(l, False), l)
            lams = (diff_lq1[j], diff_lk1[j], diff_lq2[j], diff_lk2[j])
            o1 = _diff_attention(dq, dk, dv, cdk, cdv, lams, diff_subln[j], lam_init, 256, ATTN_SUB)
            o2 = _slot_attention(mq, mk, mv, cmk, cmv, MLA_HEADS, True, 512)
            if not last:
                co1 = _diff_attention(cdq, None, None, cdk, cdv, lams, diff_subln[j], lam_init, l)
                co2 = _slot_attention(cmq, None, None, cmk, cmv, MLA_HEADS, True, l)
            w_out = odd_w_out[j]

        router = _router_weights(router_w[i], router_b[i])
        cnt0 = jnp.zeros((1, LANES), F32)
        x, h2a, h2b, ridx, rrank, rgw, cnt = _outproj_ln_router(
            o1, o2, w_out, x, g1, ln1_g[i], ln1_b[i], sc2, sh2, router, cnt0, alpha, ts)
        flat = lambda a, n: a.reshape(bsz * n, a.shape[-1])
        h2a, h2b, ridx, rrank, rgw = [flat(a, s) for a in (h2a, h2b, ridx, rrank, rgw)]
        if not last:
            ctx, ch2a, ch2b, cidx, crank, cgw, cnt = _outproj_ln_router(
                co1, co2, w_out, ctx, cg1, ln1_g[i], ln1_b[i], csc2, csh2, router, cnt, alpha, l)
            cat = lambda a, b: jnp.concatenate([a, flat(b, l)], axis=0)
            h2a, h2b = cat(h2a, ch2a), cat(h2b, ch2b)
            ridx, rrank, rgw = cat(ridx, cidx), cat(rrank, crank), cat(rgw, cgw)

        counts = cnt[0, :n_experts].astype(jnp.int32)
        dpos_t, p, tile_expert, tile_valid = _moe_plan(ridx[:, :TOP_K], rrank[:, :TOP_K], counts, tm)
        xsa = _sc_dispatch(h2a, dpos_t, p)
        xsb = _sc_dispatch(h2b, dpos_t, p)
        ys = _expert_ffn(xsa, xsb, tile_expert, tile_valid, i, exp_w_gu, exp_b_gu[i],
                         exp_w_down, exp_b_down[i], tm)
        t_lat, nb_a = bsz * s, bsz // 2
        t_a = nb_a * s
        yg_a = ys.at[dpos_t[:, :t_a]].get(mode="promise_in_bounds")
        yg_b = ys.at[dpos_t[:, t_a:]].get(mode="promise_in_bounds")
        x_a = _combine_ln(yg_a, rgw, x, g2, ln2_g[i], ln2_b[i], alpha, ts, 0, 0, 0, nb_a)
        x = _combine_ln(yg_b, rgw, x, g2, ln2_g[i], ln2_b[i], alpha, ts, 0, t_a, nb_a, bsz - nb_a,
                        prev=x_a)
        if not last:
            ctx = _combine_ln(yg_b, rgw, ctx, cg2, ln2_g[i], ln2_b[i], alpha, l, t_lat - t_a, t_lat,
                              0, bsz)
    return x


BATCH_GROUPS = 2


def kernel(x, c, ctx, c_ctx, mod_w, mod_b, ln1_g, ln1_b, ln2_g, ln2_b, even_w_in, even_w_out, na_rpb, gqa_q_gain, gqa_k_gain, odd_w_in, odd_w_out, diff_lq1, diff_lk1, diff_lq2, diff_lk2, diff_subln, mla_q_gain, mla_w_uq, mla_kv_gain, mla_w_ukv, router_w, router_b, exp_w_gu, exp_b_gu, exp_w_down, exp_b_down):
    bsz = x.shape[0]
    if bsz % (2 * BATCH_GROUPS) != 0:
        return _forward(x, c, ctx, c_ctx, mod_w, mod_b, ln1_g, ln1_b, ln2_g, ln2_b, even_w_in, even_w_out, na_rpb, gqa_q_gain, gqa_k_gain, odd_w_in, odd_w_out, diff_lq1, diff_lk1, diff_lq2, diff_lk2, diff_subln, mla_q_gain, mla_w_uq, mla_kv_gain, mla_w_ukv, router_w, router_b, exp_w_gu, exp_b_gu, exp_w_down, exp_b_down)
    g = bsz // BATCH_GROUPS
    outs = [_forward(x[k * g:(k + 1) * g], c[k * g:(k + 1) * g], ctx[k * g:(k + 1) * g], c_ctx, mod_w, mod_b, ln1_g, ln1_b, ln2_g, ln2_b, even_w_in, even_w_out, na_rpb, gqa_q_gain, gqa_k_gain, odd_w_in, odd_w_out, diff_lq1, diff_lk1, diff_lq2, diff_lk2, diff_subln, mla_q_gain, mla_w_uq, mla_kv_gain, mla_w_ukv, router_w, router_b, exp_w_gu, exp_b_gu, exp_w_down, exp_b_down)
            for k in range(BATCH_GROUPS)]
    return jnp.concatenate(outs, axis=0)
```

```python
import functools
import math

import numpy as np
import jax
import jax.numpy as jnp
from jax import lax
from jax.experimental import pallas as pl
from jax.experimental.pallas import tpu as pltpu
from jax.experimental.pallas import tpu_sc as plsc

F32 = jnp.float32
BF16 = jnp.bfloat16

GRID_W = 64
HEAD_DIM = 64
ROPE_THETA = 10000.0
LN_EPS = 1e-6
RMS_EPS = 1e-6
NEG_INF = -1e30
NA_HEADS = 8
NA_WIN_H = 8
NA_WIN_W = 16
GQA_HEADS = 8
GQA_KV_HEADS = 2
DIFF_HEADS = 4
MLA_HEADS = 8
MLA_Q_RANK = 256
MLA_KV_RANK = 128
MLA_NOPE = 64
MLA_ROPE = 32
MLA_V = 64
MLA_QK = MLA_NOPE + MLA_ROPE
TOP_K = 4
SWIGLU_ALPHA = 1.702
SWIGLU_LIMIT = 7.0
LOG2E = 1.4426950408889634

LANES = 128
VMEM_LIMIT = 56 * 1024 * 1024
MOE_TILE = 1024
GU_BLOCK = 512
ATTN_SUB = 2


def _cparams(n_axes):
    return pltpu.CompilerParams(dimension_semantics=("arbitrary",) * n_axes,
                                vmem_limit_bytes=VMEM_LIMIT)


def _dot(a, b):
    return jnp.dot(a, b, preferred_element_type=F32)


def _dot_nt(a, b):
    return lax.dot_general(a, b, (((1,), (1,)), ((), ())), preferred_element_type=F32)


def _mod_kernel(c_ref, w_ref, b_ref, o_ref):
    cv = c_ref[...]
    a = (cv * jax.nn.sigmoid(cv)).astype(BF16)
    o_ref[0] = _dot(a, w_ref[0].astype(BF16)) + b_ref[0]


def _modulation_all(c_all, mod_w, mod_b):
    depth, d, n = mod_w.shape
    rows = c_all.shape[0]
    tn = 1536
    return pl.pallas_call(
        _mod_kernel,
        grid=(depth, n // tn),
        in_specs=[pl.BlockSpec((rows, d), lambda i, j: (0, 0)),
                  pl.BlockSpec((1, d, tn), lambda i, j: (i, 0, j)),
                  pl.BlockSpec((1, 1, tn), lambda i, j: (i, 0, j))],
        out_specs=pl.BlockSpec((1, rows, tn), lambda i, j: (i, 0, j)),
        out_shape=jax.ShapeDtypeStruct((depth, rows, n), F32),
        compiler_params=_cparams(2),
    )(c_all, mod_w, mod_b.reshape(depth, 1, n))


def _rope_tables(s, dim):
    pos = jnp.arange(s)
    row = (pos // GRID_W).astype(F32)[:, None]
    col = (pos % GRID_W).astype(F32)[:, None]
    quarter = dim // 4
    inv_freq = ROPE_THETA ** (-jnp.arange(quarter, dtype=F32) / quarter)
    ar, ac = row * inv_freq, col * inv_freq
    cos = jnp.concatenate([jnp.cos(ar), jnp.cos(ar), jnp.cos(ac), jnp.cos(ac)], axis=-1)
    sin = jnp.concatenate([-jnp.sin(ar), jnp.sin(ar), -jnp.sin(ac), jnp.sin(ac)], axis=-1)
    return cos, sin


def _swap_perm(dim):
    q = dim // 4
    idx = np.arange(dim)
    return np.where((idx % (2 * q)) < q, idx + q, idx - q)


def _rope_partner(y, dim):
    q = dim // 4
    lane = lax.broadcasted_iota(jnp.int32, (1, LANES), 1)
    first = (lane % (2 * q)) < q
    tiles = []
    for j in range(0, y.shape[1], LANES):
        t = y[:, j:j + LANES]
        tiles.append(jnp.where(first, pltpu.roll(t, LANES - q, 1), pltpu.roll(t, q, 1)))
    return tiles[0] if len(tiles) == 1 else jnp.concatenate(tiles, axis=-1)


def _swap_cols(w, dim):
    n = w.shape[-1] // dim
    perm = (np.arange(n)[:, None] * dim + _swap_perm(dim)[None, :]).reshape(-1)
    return w[..., perm]


def _block_ones(n, blk):
    i = np.arange(n) // blk
    return jnp.asarray((i[:, None] == i[None, :]).astype(np.float32), dtype=BF16)


E_NAQ, E_NAK, E_NAV, E_GQ, E_GK, E_GV, E_END = (0, 512, 1024, 1536, 2048, 2304, 2560)


def _even_weights(w_in):
    naq, nak, nav, gq, gk, gv = jnp.split(w_in, [512, 1024, 1536, 2048, 2176], axis=-1)
    k0, k1 = gk[:, :64], gk[:, 64:]
    v0, v1 = gv[:, :64], gv[:, 64:]
    gk2 = jnp.concatenate([k0, k1, k1, k0], axis=-1)
    gv2 = jnp.concatenate([v0, v1, v1, v0], axis=-1)
    w = jnp.concatenate([naq * (0.125 * LOG2E), nak, nav, gq, gk2, gv2], axis=-1).astype(BF16)
    return w


def _even_tables(s, q_gain, k_gain, rope):
    sw = _swap_perm(64)
    if rope:
        cos, sin = _rope_tables(s, 64)
    else:
        cos, sin = jnp.ones((s, 64), F32), jnp.zeros((s, 64), F32)
    qa = jnp.tile(cos * q_gain[None, :] * (0.125 * LOG2E), (1, 8))
    qb = jnp.tile(sin * q_gain[sw][None, :] * (0.125 * LOG2E), (1, 8))
    ka = jnp.tile(cos * k_gain[None, :], (1, 4))
    kb = jnp.tile(sin * k_gain[sw][None, :], (1, 4))
    return qa, qb, ka, kb


def _inproj_even_kernel(x_ref, sc_ref, sh_ref, w_ref, g512_ref, g256_ref,
                        qa_ref, qb_ref, ka_ref, kb_ref,
                        naq_ref, nak_ref, nav_ref, gq_ref, gk_ref, gv_ref):
    h = (x_ref[0] * (1.0 + sc_ref[0]) + sh_ref[0]).astype(BF16)
    naq_ref[0] = _dot(h, w_ref[:, E_NAQ:E_NAK]).astype(BF16)
    nak_ref[0] = _dot(h, w_ref[:, E_NAK:E_NAV]).astype(BF16)
    nav_ref[0] = _dot(h, w_ref[:, E_NAV:E_GQ]).astype(BF16)
    gv_ref[0] = _dot(h, w_ref[:, E_GV:E_END]).astype(BF16)
    y = _dot(h, w_ref[:, E_GQ:E_GK])
    ys = _rope_partner(y, HEAD_DIM)
    r = lax.rsqrt(_dot((y * y).astype(BF16), g512_ref[...]) * (1.0 / HEAD_DIM) + RMS_EPS)
    gq_ref[0] = (r * (y * qa_ref[...] + ys * qb_ref[...])).astype(BF16)
    y = _dot(h, w_ref[:, E_GK:E_GV])
    ys = _rope_partner(y, HEAD_DIM)
    r = lax.rsqrt(_dot((y * y).astype(BF16), g256_ref[...]) * (1.0 / HEAD_DIM) + RMS_EPS)
    gk_ref[0] = (r * (y * ka_ref[...] + ys * kb_ref[...])).astype(BF16)


def _mod_spec(d, batched):
    if batched:
        return pl.BlockSpec((1, 1, d), lambda b, s: (b, 0, 0))
    return pl.BlockSpec((1, 1, d), lambda b, s: (0, 0, 0))


def _const_spec(shape):
    nd = len(shape)
    return pl.BlockSpec(shape, lambda b, s: (0,) * nd)


def _inproj_even(x, sc, sh, w, tables, ts):
    bsz, s, d = x.shape
    batched = sc.shape[0] > 1
    qa, qb, ka, kb = tables
    tok = lambda n: pl.BlockSpec((1, ts, n), lambda b, i: (b, i, 0))
    tab = lambda n: pl.BlockSpec((ts, n), lambda b, i: (i, 0))
    widths = (512, 512, 512, 512, 256, 256)
    return pl.pallas_call(
        _inproj_even_kernel,
        grid=(bsz, s // ts),
        in_specs=[tok(d), _mod_spec(d, batched), _mod_spec(d, batched), _const_spec(w.shape),
                  _const_spec((512, 512)), _const_spec((256, 256)),
                  tab(512), tab(512), tab(256), tab(256)],
        out_specs=[tok(n) for n in widths],
        out_shape=[jax.ShapeDtypeStruct((bsz, s, n), BF16) for n in widths],
        compiler_params=_cparams(2),
    )(x, sc, sh, w, _block_ones(512, 64), _block_ones(256, 64), qa, qb, ka, kb)


O_DQ, O_DK, O_DV, O_CQ, O_CKV, O_PE, O_END = (0, 512, 1024, 1536, 1792, 1920, 2048)


def _odd_weights(w_in, w_uq, w_ukv):
    dq, dk, dv, cq, ckv, kpe = jnp.split(w_in, [512, 1024, 1536, 1792, 1920], axis=-1)
    d = w_in.shape[0]
    pe_slot = jnp.concatenate([kpe, _swap_cols(kpe, 32), jnp.zeros((d, 64), F32)], axis=-1)
    w = jnp.concatenate([dq, dk, dv, cq, ckv, pe_slot], axis=-1).astype(BF16)
    uq = w_uq.reshape(MLA_Q_RANK, MLA_HEADS, MLA_QK)
    z32 = jnp.zeros((MLA_Q_RANK, MLA_HEADS, 32), F32)
    z64 = jnp.zeros((MLA_Q_RANK, MLA_HEADS, 64), F32)
    uq_pad = jnp.concatenate([uq, z32], axis=-1).reshape(MLA_Q_RANK, MLA_HEADS * LANES)
    uq_sw = jnp.concatenate([z64, _swap_cols(uq[..., MLA_NOPE:], 32), z32],
                            axis=-1).reshape(MLA_Q_RANK, MLA_HEADS * LANES)
    wuq2 = jnp.concatenate([uq_pad, uq_sw], axis=-1).astype(BF16)
    ukv = w_ukv.reshape(MLA_KV_RANK, MLA_HEADS, MLA_NOPE + MLA_V)
    zk = jnp.zeros((MLA_KV_RANK, MLA_HEADS, 64), F32)
    wk_pad = jnp.concatenate([ukv[..., :MLA_NOPE], zk], axis=-1).reshape(MLA_KV_RANK, MLA_HEADS * LANES)
    place = np.zeros((LANES, MLA_HEADS, LANES), np.float32)
    for j in range(MLA_ROPE):
        place[j, :, MLA_NOPE + j] = 1.0
        place[MLA_ROPE + j, :, MLA_NOPE + j] = 1.0
    wk2 = jnp.concatenate([wk_pad, jnp.asarray(place.reshape(LANES, MLA_HEADS * LANES))],
                          axis=0).astype(BF16)
    wv = ukv[..., MLA_NOPE:].reshape(MLA_KV_RANK, MLA_HEADS * MLA_V).astype(BF16)
    return w, wuq2, wk2, wv


def _odd_tables(s, rope):
    m_scale = MLA_QK ** -0.5 * LOG2E
    if rope:
        cos64, sin64 = _rope_tables(s, 64)
        cos32, sin32 = _rope_tables(s, 32)
    else:
        cos64, sin64 = jnp.ones((s, 64), F32), jnp.zeros((s, 64), F32)
        cos32, sin32 = jnp.ones((s, 32), F32), jnp.zeros((s, 32), F32)
    one64, z32, z64 = jnp.ones((s, 64), F32), jnp.zeros((s, 32), F32), jnp.zeros((s, 64), F32)
    dcos, dsin = jnp.tile(cos64, (1, 8)), jnp.tile(sin64, (1, 8))
    qa = jnp.tile(jnp.concatenate([one64, cos32, z32], axis=-1) * m_scale, (1, MLA_HEADS))
    qb = jnp.tile(jnp.concatenate([z64, sin32, z32], axis=-1) * m_scale, (1, MLA_HEADS))
    pe = jnp.concatenate([cos32, sin32, z64], axis=-1)
    return dcos, dsin, qa, qb, pe


def _inproj_odd_kernel(x_ref, sc_ref, sh_ref, w_ref, wuq_ref, wk_ref, wv_ref, qg_ref, kvg_ref,
                       dcos_ref, dsin_ref, qa_ref, qb_ref, pe_ref,
                       dq_ref, dk_ref, dv_ref, mq_ref, mk_ref, mv_ref):
    h = (x_ref[0] * (1.0 + sc_ref[0]) + sh_ref[0]).astype(BF16)
    dcos, dsin = dcos_ref[...], dsin_ref[...]
    y = _dot(h, w_ref[:, O_DQ:O_DK])
    ys = _rope_partner(y, HEAD_DIM)
    dq_ref[0] = ((y * dcos + ys * dsin) * (0.125 * LOG2E)).astype(BF16)
    y = _dot(h, w_ref[:, O_DK:O_DV])
    ys = _rope_partner(y, HEAD_DIM)
    dk_ref[0] = (y * dcos + ys * dsin).astype(BF16)
    dv_ref[0] = _dot(h, w_ref[:, O_DV:O_CQ]).astype(BF16)
    cq = _dot(h, w_ref[:, O_CQ:O_CKV])
    nq = cq * lax.rsqrt(jnp.mean(cq * cq, axis=-1, keepdims=True) + RMS_EPS) * qg_ref[...]
    y2 = _dot(nq.astype(BF16), wuq_ref[...])
    half = MLA_HEADS * LANES
    mq_ref[0] = (y2[:, :half] * qa_ref[...] + y2[:, half:] * qb_ref[...]).astype(BF16)
    ckv = _dot(h, w_ref[:, O_CKV:O_PE])
    nk = ckv * lax.rsqrt(jnp.mean(ckv * ckv, axis=-1, keepdims=True) + RMS_EPS) * kvg_ref[...]
    pe = _dot(h, w_ref[:, O_PE:O_END]) * pe_ref[...]
    nkb = nk.astype(BF16)
    cat = jnp.concatenate([nkb, pe.astype(BF16)], axis=-1)
    mk_ref[0] = _dot(cat, wk_ref[...]).astype(BF16)
    mv_ref[0] = _dot(nkb, wv_ref[...]).astype(BF16)


def _inproj_odd(x, sc, sh, weights, gains, tables, ts):
    bsz, s, d = x.shape
    batched = sc.shape[0] > 1
    w, wuq2, wk2, wv = weights
    qg, kvg = gains
    dcos, dsin, qa, qb, pe = tables
    tok = lambda n: pl.BlockSpec((1, ts, n), lambda b, i: (b, i, 0))
    tab = lambda n: pl.BlockSpec((ts, n), lambda b, i: (i, 0))
    widths = (512, 512, 512, 1024, 1024, 512)
    return pl.pallas_call(
        _inproj_odd_kernel,
        grid=(bsz, s // ts),
        in_specs=[tok(d), _mod_spec(d, batched), _mod_spec(d, batched), _const_spec(w.shape),
                  _const_spec(wuq2.shape), _const_spec(wk2.shape), _const_spec(wv.shape),
                  _const_spec((1, MLA_Q_RANK)), _const_spec((1, MLA_KV_RANK)),
                  tab(512), tab(512), tab(1024), tab(1024), tab(128)],
        out_specs=[tok(n) for n in widths],
        out_shape=[jax.ShapeDtypeStruct((bsz, s, n), BF16) for n in widths],
        compiler_params=_cparams(2),
    )(x, sc, sh, w, wuq2, wk2, wv, qg.reshape(1, -1), kvg.reshape(1, -1), dcos, dsin, qa, qb, pe)


def _half_masks():
    lane = lax.broadcasted_iota(jnp.int32, (1, LANES), 1)
    return lane < HEAD_DIM, lane >= HEAD_DIM


def _softmax_values(ss, vs):
    m = functools.reduce(jnp.maximum, [jnp.max(s, axis=-1, keepdims=True) for s in ss])
    es = [jnp.exp2(s - m) for s in ss]
    l = functools.reduce(lambda a, b: a + b, [jnp.sum(e, axis=-1, keepdims=True) for e in es])
    o = functools.reduce(lambda a, b: a + b, [_dot(e.astype(BF16), v) for e, v in zip(es, vs)])
    return o * (1.0 / l)


def _attend(qm, ks, vs):
    return _softmax_values([_dot_nt(qm, k) for k in ks], vs)


NA_RB = 4
NA_WIN_ROWS = NA_RB + NA_WIN_H - 1
NA_SUB = 2


def _na_block_plan(rows):
    plan = []
    for blk in range(rows // NA_RB):
        r0 = blk * NA_RB
        rs = [int(np.clip(r0 + i - NA_WIN_H // 2, 0, rows - NA_WIN_H)) for i in range(NA_RB)]
        ws = int(np.clip(r0 - NA_WIN_H // 2, 0, rows - NA_WIN_ROWS))
        pat = tuple((rs[i] - ws, r0 + i - ws) for i in range(NA_RB))
        assert all(0 <= o and o + NA_WIN_H <= NA_WIN_ROWS for o, _ in pat)
        plan.append(pat)
    assert all(p == plan[1] for p in plan[1:-1])
    return (plan[0], plan[1], plan[-1])


def _na_bias_table(rpb, rows):
    cols = np.arange(GRID_W)
    col_start = np.clip(cols - NA_WIN_W // 2, 0, GRID_W - NA_WIN_W)
    col_mask = (cols[None, :] >= col_start[:, None]) & (cols[None, :] < col_start[:, None] + NA_WIN_W)
    col_idx = np.clip(cols[None, :] - cols[:, None] + NA_WIN_W - 1, 0, 2 * NA_WIN_W - 2)
    wr = np.arange(NA_WIN_ROWS)
    tables = []
    for pat in _na_block_plan(rows):
        off = np.array([o for o, _ in pat])[:, None]
        rq = np.array([r for _, r in pat])[:, None]
        row_ok = (wr[None, :] >= off) & (wr[None, :] < off + NA_WIN_H)
        ridx = np.clip(wr[None, :] - rq + NA_WIN_H - 1, 0, 2 * NA_WIN_H - 2)
        t = rpb.astype(F32)[:, ridx]
        t = t[..., col_idx]
        t = t.transpose(0, 1, 3, 2, 4)
        ok = row_ok[None, :, None, :, None] & col_mask[None, None, :, None, :]
        t = jnp.where(ok, t * LOG2E, NEG_INF)
        tables.append(t.reshape(NA_HEADS, NA_RB * GRID_W, NA_WIN_ROWS * GRID_W))
    return jnp.stack(tables)


def _na_kernel(rows, sub, q_ref, k_ref, v_ref, kc_ref, vc_ref, *refs):
    bt_refs, o_ref = refs[:sub], refs[sub]
    step = pl.program_id(1)
    tq = NA_RB * GRID_W
    m0, m1 = _half_masks()
    for u in range(sub):
        r0 = (step * sub + u) * NA_RB
        ws = jnp.clip(r0 - NA_WIN_H // 2, 0, rows - NA_WIN_ROWS)
        win = pl.ds(pl.multiple_of(ws * GRID_W, GRID_W), NA_WIN_ROWS * GRID_W)
        qrows = slice(u * tq, (u + 1) * tq)
        for j in range(NA_HEADS // 2):
            sl = slice(j * LANES, (j + 1) * LANES)
            qp = q_ref[0, qrows, sl]
            kp, vp = k_ref[0, win, sl], v_ref[0, win, sl]
            kcp, vcp = kc_ref[0, :, sl], vc_ref[0, :, sl]
            outs = []
            for par, msk in ((0, m0), (1, m1)):
                qm = jnp.where(msk, qp, jnp.zeros_like(qp))
                s_loc = _dot_nt(qm, kp) + bt_refs[u][0, 2 * j + par]
                outs.append(_softmax_values([s_loc, _dot_nt(qm, kcp)], [vp, vcp]))
            o_ref[0, qrows, sl] = jnp.where(m0, outs[0], outs[1]).astype(BF16)


def _na_attention(q, k, v, kc, vc, rpb):
    bsz, s, w = q.shape
    l = kc.shape[1]
    rows = s // GRID_W
    nblk = rows // NA_RB
    sub = NA_SUB
    nstep = nblk // sub
    assert rows % NA_RB == 0 and rows >= NA_WIN_ROWS and nblk % sub == 0 and nblk >= 3
    bt = _na_bias_table(rpb, rows)
    tq = NA_RB * GRID_W * sub
    full = lambda n: pl.BlockSpec((1, n, w), lambda b, r: (b, 0, 0))

    def bt_spec(u):
        def kind(b, r):
            blk = r * sub + u
            return ((blk > 0).astype(jnp.int32) + (blk == nblk - 1).astype(jnp.int32), 0, 0, 0)
        return pl.BlockSpec((1,) + bt.shape[1:], kind)

    return pl.pallas_call(
        functools.partial(_na_kernel, rows, sub),
        grid=(bsz, nstep),
        in_specs=[pl.BlockSpec((1, tq, w), lambda b, r: (b, r, 0)),
                  full(s), full(s), full(l), full(l)] + [bt_spec(u) for u in range(sub)],
        out_specs=pl.BlockSpec((1, tq, w), lambda b, r: (b, r, 0)),
        out_shape=jax.ShapeDtypeStruct((bsz, s, w), BF16),
        compiler_params=_cparams(2),
    )(q, k, v, kc, vc, *([bt] * sub))


def _slot_attn_kernel(n_heads, q_slot, has_lat, sub, *refs):
    if has_lat:
        q_ref, k_ref, v_ref, kc_ref, vc_ref, o_ref = refs
    else:
        q_ref, kc_ref, vc_ref, o_ref = refs
    m0, m1 = _half_masks()
    tq = q_ref.shape[1] // sub
    for u in range(sub):
        qrows = slice(u * tq, (u + 1) * tq)
        for j in range(n_heads // 2):
            vsl = slice(j * LANES, (j + 1) * LANES)
            outs = []
            for par, msk in ((0, m0), (1, m1)):
                h = 2 * j + par
                if q_slot:
                    ksl = slice(h * LANES, (h + 1) * LANES)
                    qm = q_ref[0, qrows, ksl]
                else:
                    ksl = vsl
                    qp = q_ref[0, qrows, vsl]
                    qm = jnp.where(msk, qp, jnp.zeros_like(qp))
                ks, vs = [kc_ref[0, :, ksl]], [vc_ref[0, :, vsl]]
                if has_lat:
                    ks.insert(0, k_ref[0, :, ksl])
                    vs.insert(0, v_ref[0, :, vsl])
                outs.append(_attend(qm, ks, vs))
            o_ref[0, qrows, vsl] = jnp.where(m0, outs[0], outs[1]).astype(BF16)


def _slot_attention(q, k, v, kc, vc, n_heads, q_slot, tq, sub=1):
    bsz, sq, wq = q.shape
    l, wk, wv = kc.shape[1], kc.shape[2], vc.shape[2]
    has_lat = k is not None
    qspec = pl.BlockSpec((1, tq, wq), lambda b, i: (b, i, 0))
    full = lambda n, w: pl.BlockSpec((1, n, w), lambda b, i: (b, 0, 0))
    in_specs, args = [qspec], [q]
    if has_lat:
        s = k.shape[1]
        in_specs += [full(s, wk), full(s, wv)]
        args += [k, v]
    in_specs += [full(l, wk), full(l, wv)]
    args += [kc, vc]
    return pl.pallas_call(
        functools.partial(_slot_attn_kernel, n_heads, q_slot, has_lat, sub),
        grid=(bsz, sq // tq),
        in_specs=in_specs,
        out_specs=pl.BlockSpec((1, tq, wv), lambda b, i: (b, i, 0)),
        out_shape=jax.ShapeDtypeStruct((bsz, sq, wv), BF16),
        compiler_params=_cparams(2),
    )(*args)


def _gqa_kernel(has_lat, tq, sub, *refs):
    if has_lat:
        q_ref, k_ref, v_ref, kc_ref, vc_ref, o_ref = refs
    else:
        q_ref, kc_ref, vc_ref, o_ref = refs
    masks = _half_masks()
    for u in range(sub):
        qrows = slice(u * tq, (u + 1) * tq)
        res = {}
        for g in range(GQA_KV_HEADS):
            for var in range(2):
                par = g if var == 0 else 1 - g
                heads = (4 * g + par, 4 * g + 2 + par)
                vsl = slice(var * LANES, (var + 1) * LANES)
                qs = []
                for h in heads:
                    qp = q_ref[0, qrows, (h // 2) * LANES:(h // 2 + 1) * LANES]
                    qs.append(jnp.where(masks[par], qp, jnp.zeros_like(qp)))
                qm = jnp.concatenate(qs, axis=0)
                ks, vs = [kc_ref[0, :, vsl]], [vc_ref[0, :, vsl]]
                if has_lat:
                    ks.insert(0, k_ref[0, :, vsl])
                    vs.insert(0, v_ref[0, :, vsl])
                o = _attend(qm, ks, vs)
                res[heads[0]] = o[:tq]
                res[heads[1]] = o[tq:]
        for j in range(GQA_HEADS // 2):
            o_ref[0, qrows, j * LANES:(j + 1) * LANES] = jnp.where(
                masks[0], res[2 * j], res[2 * j + 1]).astype(BF16)


def _gqa_attention(q, k2, v2, k2c, v2c, tq, sub=1):
    bsz, sq, wq = q.shape
    l = k2c.shape[1]
    has_lat = k2 is not None
    full = lambda n: pl.BlockSpec((1, n, 2 * LANES), lambda b, i: (b, 0, 0))
    in_specs, args = [pl.BlockSpec((1, tq * sub, wq), lambda b, i: (b, i, 0))], [q]
    if has_lat:
        in_specs += [full(k2.shape[1])] * 2
        args += [k2, v2]
    in_specs += [full(l)] * 2
    args += [k2c, v2c]
    return pl.pallas_call(
        functools.partial(_gqa_kernel, has_lat, tq, sub),
        grid=(bsz, sq // (tq * sub)),
        in_specs=in_specs,
        out_specs=pl.BlockSpec((1, tq * sub, wq), lambda b, i: (b, i, 0)),
        out_shape=jax.ShapeDtypeStruct((bsz, sq, wq), BF16),
        compiler_params=_cparams(2),
    )(*args)


def _diff_kernel(has_lat, tq, sub, lam_init, *refs):
    if has_lat:
        q_ref, k_ref, v_ref, kc_ref, vc_ref, lq1, lk1, lq2, lk2, sub_ref, o_ref = refs
    else:
        q_ref, kc_ref, vc_ref, lq1, lk1, lq2, lk2, sub_ref, o_ref = refs
    lam = (jnp.exp(jnp.sum(lq1[...] * lk1[...], axis=-1, keepdims=True))
           - jnp.exp(jnp.sum(lq2[...] * lk2[...], axis=-1, keepdims=True)) + lam_init)
    m0, m1 = _half_masks()
    for u in range(sub):
        qrows = slice(u * tq, (u + 1) * tq)
        for h in range(DIFF_HEADS):
            sl = slice(h * LANES, (h + 1) * LANES)
            qp = q_ref[0, qrows, sl]
            zero = jnp.zeros_like(qp)
            qm = jnp.concatenate([jnp.where(m0, qp, zero), jnp.where(m1, qp, zero)], axis=0)
            ks, vs = [kc_ref[0, :, sl]], [vc_ref[0, :, sl]]
            if has_lat:
                ks.insert(0, k_ref[0, :, sl])
                vs.insert(0, v_ref[0, :, sl])
            o2 = _attend(qm, ks, vs)
            o = o2[:tq] - lam * o2[tq:]
            o = o * lax.rsqrt(jnp.mean(o * o, axis=-1, keepdims=True) + RMS_EPS) * sub_ref[...]
            o_ref[0, qrows, sl] = (o * (1.0 - lam_init)).astype(BF16)


def _diff_attention(q, k, v, kc, vc, lams, subln, lam_init, tq, sub=1):
    bsz, sq, w = q.shape
    l = kc.shape[1]
    has_lat = k is not None
    full = lambda n: pl.BlockSpec((1, n, w), lambda b, i: (b, 0, 0))
    in_specs, args = [pl.BlockSpec((1, tq * sub, w), lambda b, i: (b, i, 0))], [q]
    if has_lat:
        in_specs += [full(k.shape[1])] * 2
        args += [k, v]
    in_specs += [full(l)] * 2 + [_const_spec((1, HEAD_DIM))] * 4 + [_const_spec((1, LANES))]
    args += [kc, vc] + [a.reshape(1, -1).astype(F32) for a in lams] + [subln.reshape(1, -1).astype(F32)]
    return pl.pallas_call(
        functools.partial(_diff_kernel, has_lat, tq, sub, lam_init),
        grid=(bsz, sq // (tq * sub)),
        in_specs=in_specs,
        out_specs=pl.BlockSpec((1, tq * sub, w), lambda b, i: (b, i, 0)),
        out_shape=jax.ShapeDtypeStruct((bsz, sq, w), BF16),
        compiler_params=_cparams(2),
    )(*args)


def _pack_bf16_pairs(v):
    w = v.shape[1] // 2
    hi = pltpu.bitcast(v[:, :w].astype(F32), jnp.int32)
    lo = pltpu.bitcast(v[:, w:].astype(F32), jnp.int32)
    return hi | lax.shift_right_logical(lo, 16)


def _unpack_bf16_pairs(u):
    hi = pltpu.bitcast(u & jnp.int32(-65536), F32)
    lo = pltpu.bitcast(lax.shift_left(u, 16), F32)
    return hi, lo


def _layer_norm(z, g, b):
    mu = jnp.mean(z, axis=-1, keepdims=True)
    zc = z - mu
    var = jnp.mean(zc * zc, axis=-1, keepdims=True)
    return zc * lax.rsqrt(var + LN_EPS) * g + b


def _outproj_kernel(alpha, sub, o1_ref, o2_ref, w1_ref, w2_ref, x_ref, g1_ref, lng_ref, lnb_ref,
                    sc2_ref, sh2_ref, rwh_ref, rwl_ref, rb_ref, tri_ref, cnt0_ref,
                    xo_ref, h2a_ref, h2b_ref, idx_ref, rank_ref, gw_ref, cnt_ref, carry_ref):
    @pl.when(jnp.logical_and(pl.program_id(0) == 0, pl.program_id(1) == 0))
    def _():
        carry_ref[...] = cnt0_ref[...]

    tr = x_ref.shape[1] // sub
    carry = carry_ref[...]
    for u in range(sub):
        rows = slice(u * tr, (u + 1) * tr)
        o = _dot(o1_ref[0, rows], w1_ref[...]) + _dot(o2_ref[0, rows], w2_ref[...])
        xn = _layer_norm(alpha * x_ref[0, rows] + g1_ref[0] * o, lng_ref[...], lnb_ref[...])
        xo_ref[0, rows] = xn
        h2 = xn * (1.0 + sc2_ref[0]) + sh2_ref[0]
        hi = h2.astype(BF16)
        packed = _pack_bf16_pairs(hi)
        q = packed.shape[1] // 2
        h2a_ref[0, rows] = packed[:, :q]
        h2b_ref[0, rows] = packed[:, q:]
        lo = (h2 - hi.astype(F32)).astype(BF16)
        logits = (_dot(hi, rwh_ref[...]) + _dot(lo, rwh_ref[...]) + _dot(hi, rwl_ref[...])) + rb_ref[...]
        lane = lax.broadcasted_iota(jnp.int32, logits.shape, 1).astype(F32)
        vals, idxs = [], []
        cur = logits
        for _ in range(TOP_K):
            m = jnp.max(cur, axis=-1, keepdims=True)
            ik = jnp.min(jnp.where(cur == m, lane, float(LANES)), axis=-1, keepdims=True)
            vals.append(m)
            idxs.append(ik)
            cur = jnp.where(lane == ik, -jnp.inf, cur)
        ws = [jnp.exp(v - vals[0]) for v in vals]
        inv = 1.0 / functools.reduce(lambda a, b: a + b, ws)
        sel = [lane == ik for ik in idxs]
        onehot = functools.reduce(lambda a, b: a + b, [m.astype(F32) for m in sel])
        before = _dot(tri_ref[...], onehot.astype(BF16)) + carry
        idx_out = jnp.zeros_like(logits)
        rank_out = jnp.zeros_like(logits)
        w_out = jnp.zeros_like(logits)
        for k in range(TOP_K):
            rk = jnp.sum(jnp.where(sel[k], before, 0.0), axis=-1, keepdims=True)
            idx_out = jnp.where(lane == float(k), idxs[k], idx_out)
            rank_out = jnp.where(lane == float(k), rk, rank_out)
            w_out = jnp.where(lane == float(k), ws[k] * inv, w_out)
        idx_ref[0, rows] = idx_out.astype(jnp.int32)
        rank_ref[0, rows] = rank_out.astype(jnp.int32)
        gw_ref[0, rows] = w_out
        carry = carry + jnp.sum(onehot, axis=0, keepdims=True)
    carry_ref[...] = carry
    cnt_ref[...] = carry


def _outproj_ln_router(o1, o2, w_out, x, g1, lng, lnb, sc2, sh2, router, cnt0, alpha, ts):
    bsz, s, d = x.shape
    batched = g1.shape[0] > 1
    rwh, rwl, rb = router
    w1, w2 = w_out[:512].astype(BF16), w_out[512:].astype(BF16)
    sub = 1
    tr = ts // sub
    tri = jnp.asarray(np.tril(np.ones((tr, tr), np.float32), -1), dtype=BF16)
    tok = lambda n: pl.BlockSpec((1, ts, n), lambda b, i: (b, i, 0))
    ms = _mod_spec(d, batched)
    return pl.pallas_call(
        functools.partial(_outproj_kernel, alpha, sub),
        grid=(bsz, s // ts),
        in_specs=[tok(512), tok(512), _const_spec((512, d)), _const_spec((512, d)), tok(d), ms,
                  _const_spec((1, d)), _const_spec((1, d)), ms, ms,
                  _const_spec((d, LANES)), _const_spec((d, LANES)), _const_spec((1, LANES)),
                  _const_spec((tr, tr)), _const_spec((1, LANES))],
        out_specs=[tok(d), tok(d // 4), tok(d // 4), tok(LANES), tok(LANES), tok(LANES),
                   _const_spec((1, LANES))],
        out_shape=[jax.ShapeDtypeStruct((bsz, s, d), F32),
                   jax.ShapeDtypeStruct((bsz, s, d // 4), jnp.int32),
                   jax.ShapeDtypeStruct((bsz, s, d // 4), jnp.int32),
                   jax.ShapeDtypeStruct((bsz, s, LANES), jnp.int32),
                   jax.ShapeDtypeStruct((bsz, s, LANES), jnp.int32),
                   jax.ShapeDtypeStruct((bsz, s, LANES), F32),
                   jax.ShapeDtypeStruct((1, LANES), F32)],
        scratch_shapes=[pltpu.VMEM((1, LANES), F32)],
        compiler_params=_cparams(2),
    )(o1, o2, w1, w2, x, g1, lng.reshape(1, d), lnb.reshape(1, d), sc2, sh2, rwh, rwl, rb, tri, cnt0)


def _router_weights(router_w, router_b):
    d, e = router_w.shape
    wp = jnp.zeros((d, LANES), F32).at[:, :e].set(router_w)
    hi = wp.astype(BF16)
    lo = (wp - hi.astype(F32)).astype(BF16)
    rb = jnp.full((1, LANES), -jnp.inf, F32).at[0, :e].set(router_b)
    return hi, lo, rb


def _deinterleave_perm():
    p = np.zeros((GU_BLOCK, GU_BLOCK), np.float32)
    m = np.arange(GU_BLOCK // 2)
    p[2 * m, m] = 1.0
    p[2 * m + 1, GU_BLOCK // 2 + m] = 1.0
    return jnp.asarray(p, dtype=BF16)


def _ffn_kernel(te_ref, tv_ref, xa_ref, xb_ref, wgu_ref, bgu_ref, wd_ref, bd_ref, perm_ref, y_ref,
                wgu_s, wd_s):
    j = pl.program_id(0)
    n_blk = wgu_s.shape[1] // GU_BLOCK
    half = GU_BLOCK // 2

    @pl.when(jnp.logical_or(j == 0, te_ref[j] != te_ref[jnp.maximum(j - 1, 0)]))
    def _():
        for b in range(n_blk):
            sl = slice(b * GU_BLOCK, (b + 1) * GU_BLOCK)
            wgu_s[:, sl] = _dot(wgu_ref[0, 0, :, sl].astype(BF16), perm_ref[...]).astype(BF16)
        wd_s[...] = wd_ref[0, 0].astype(BF16)

    @pl.when(tv_ref[j] > 0)
    def _():
        a_hi, a_lo = _unpack_bf16_pairs(xa_ref[...])
        b_hi, b_lo = _unpack_bf16_pairs(xb_ref[...])
        x = jnp.concatenate([a_hi, b_hi, a_lo, b_lo], axis=-1).astype(BF16)
        acts = []
        for b in range(n_blk):
            sl = slice(b * GU_BLOCK, (b + 1) * GU_BLOCK)
            gu = _dot(x, wgu_s[:, sl]) + bgu_ref[0, :, sl]
            glu = jnp.minimum(gu[:, :half], SWIGLU_LIMIT)
            lin = jnp.clip(gu[:, half:], -SWIGLU_LIMIT, SWIGLU_LIMIT)
            acts.append(((lin + 1.0) * (glu * jax.nn.sigmoid(SWIGLU_ALPHA * glu))).astype(BF16))
        a = jnp.concatenate(acts, axis=-1)
        y_ref[...] = (_dot(a, wd_s[...]) + bd_ref[0]).astype(BF16)

    @pl.when(tv_ref[j] == 0)
    def _():
        y_ref[...] = jnp.zeros_like(y_ref)


def _expert_ffn(xsa, xsb, tile_expert, tile_valid, layer, w_gu, b_gu, w_down, b_down, tm):
    p, q = xsa.shape
    d = 4 * q
    _, e, _, f2 = w_gu.shape
    f = f2 // 2
    half = GU_BLOCK // 2
    bgu = jnp.stack([b_gu[:, 0::2].reshape(e, f // half, half),
                     b_gu[:, 1::2].reshape(e, f // half, half)], axis=2).reshape(e, 1, f2)
    wspec = lambda a, b: pl.BlockSpec((1, 1, a, b), lambda j, te, tv: (layer, te[j], 0, 0))
    bspec = lambda b: pl.BlockSpec((1, 1, b), lambda j, te, tv: (te[j], 0, 0))
    return pl.pallas_call(
        _ffn_kernel,
        grid_spec=pltpu.PrefetchScalarGridSpec(
            num_scalar_prefetch=2,
            grid=(p // tm,),
            in_specs=[pl.BlockSpec((tm, q), lambda j, te, tv: (j, 0)),
                      pl.BlockSpec((tm, q), lambda j, te, tv: (j, 0)),
                      wspec(d, f2), bspec(f2), wspec(f, d), bspec(d),
                      pl.BlockSpec((GU_BLOCK, GU_BLOCK), lambda j, te, tv: (0, 0))],
            out_specs=pl.BlockSpec((tm, d), lambda j, te, tv: (j, 0)),
            scratch_shapes=[pltpu.VMEM((d, f2), BF16), pltpu.VMEM((f, d), BF16)]),
        out_shape=jax.ShapeDtypeStruct((p, d), BF16),
        compiler_params=_cparams(1),
    )(tile_expert, tile_valid, xsa, xsb, w_gu, bgu, w_down, b_down.reshape(e, 1, d),
      _deinterleave_perm())


def _moe_plan(idx4, rank4, counts, tm):
    t = idx4.shape[0]
    n_experts = counts.shape[0]
    pc = ((counts + tm - 1) // tm) * tm
    pend = jnp.cumsum(pc)
    pstart = pend - pc
    dpos_t = (pstart[idx4] + rank4).T.astype(jnp.int32)
    p = t * TOP_K + n_experts * tm
    tile_start = jnp.arange(p // tm, dtype=jnp.int32) * tm
    tile_expert = jnp.minimum(jnp.sum((tile_start[:, None] >= pend[None, :]).astype(jnp.int32), axis=1),
                              n_experts - 1).astype(jnp.int32)
    tile_valid = (tile_start < pend[-1]).astype(jnp.int32)
    return dpos_t, p, tile_expert, tile_valid


SC_WINDOW = 128


def _sc_dispatch(rows, dpos_t, p):
    t, w = rows.shape
    k = dpos_t.shape[0]
    mesh = plsc.VectorSubcoreMesh(core_axis_name="core", subcore_axis_name="subcore")

    @functools.partial(pl.kernel, out_type=jax.ShapeDtypeStruct((p, w), rows.dtype), mesh=mesh)
    def kern(x_hbm, *refs):
        i_hbms, o_hbm = refs[:k], refs[k]

        def body(x_vmem, *i_vmems):
            for iv in i_vmems:
                pltpu.sync_copy(x_vmem, o_hbm.at[iv.at[0]])

        pltpu.emit_pipeline(
            body,
            grid=(t // SC_WINDOW,),
            in_specs=[pl.BlockSpec((SC_WINDOW, w), index_map=lambda i: (i, 0))]
            + [pl.BlockSpec((1, SC_WINDOW), index_map=lambda i: (0, i)) for _ in range(k)],
            out_specs=[],
            core_axis_name=("core", "subcore"),
            dimension_semantics=(pltpu.PARALLEL,),
        )(x_hbm, *i_hbms)

    return kern(rows, *[dpos_t[kk:kk + 1] for kk in range(k)])


def _combine_kernel(alpha, yg_ref, gw_ref, x_ref, g2_ref, lng_ref, lnb_ref, o_ref):
    gw = gw_ref[...]
    y = yg_ref[0].astype(F32) * gw[:, 0:1]
    for k in range(1, TOP_K):
        y = y + yg_ref[k].astype(F32) * gw[:, k:k + 1]
    o_ref[0] = _layer_norm(alpha * x_ref[0] + g2_ref[0] * y, lng_ref[...], lnb_ref[...])


def _combine_ln(yg, gw, x, g2, lng, lnb, alpha, row_offset, ts):
    bsz, s, d = x.shape
    batched = g2.shape[0] > 1
    nblk = s // ts
    off = row_offset // ts
    return pl.pallas_call(
        functools.partial(_combine_kernel, alpha),
        grid=(bsz, nblk),
        in_specs=[pl.BlockSpec((TOP_K, ts, d), lambda b, i: (0, off + b * nblk + i, 0)),
                  pl.BlockSpec((ts, LANES), lambda b, i: (off + b * nblk + i, 0)),
                  pl.BlockSpec((1, ts, d), lambda b, i: (b, i, 0)),
                  _mod_spec(d, batched), _const_spec((1, d)), _const_spec((1, d))],
        out_specs=pl.BlockSpec((1, ts, d), lambda b, i: (b, i, 0)),
        out_shape=jax.ShapeDtypeStruct((bsz, s, d), F32),
        compiler_params=_cparams(2),
    )(yg, gw, x, g2, lng.reshape(1, d), lnb.reshape(1, d))


def kernel(x, c, ctx, c_ctx, mod_w, mod_b, ln1_g, ln1_b, ln2_g, ln2_b, even_w_in, even_w_out, na_rpb, gqa_q_gain, gqa_k_gain, odd_w_in, odd_w_out, diff_lq1, diff_lk1, diff_lq2, diff_lk2, diff_subln, mla_q_gain, mla_w_uq, mla_kv_gain, mla_w_ukv, router_w, router_b, exp_w_gu, exp_b_gu, exp_w_down, exp_b_down):
    bsz, s, d = x.shape
    l = ctx.shape[1]
    depth = mod_w.shape[0]
    n_experts = router_w.shape[-1]
    alpha = (2 * depth) ** 0.25
    ts = min(512, s)
    tm = MOE_TILE

    pad = (-(bsz + 1)) % 8
    c_all = jnp.concatenate([c, c_ctx[None, :], jnp.zeros((pad, d), F32)], axis=0)
    mod = _modulation_all(c_all, mod_w, mod_b)

    for i in range(depth):
        last = i == depth - 1
        j = i // 2
        ml = [mod[i, :bsz, k * d:(k + 1) * d].reshape(bsz, 1, d) for k in range(6)]
        mc = [mod[i, bsz:bsz + 1, k * d:(k + 1) * d].reshape(1, 1, d) for k in range(6)]
        sh1, sc1, g1, sh2, sc2, g2 = ml
        csh1, csc1, cg1, csh2, csc2, cg2 = mc

        if i % 2 == 0:
            w = _even_weights(even_w_in[j])
            naq, nak, nav, gq, gk, gv = _inproj_even(
                x, sc1, sh1, w, _even_tables(s, gqa_q_gain[j], gqa_k_gain[j], True), ts)
            cnaq, cnak, cnav, cgq, cgk, cgv = _inproj_even(
                ctx, csc1, csh1, w, _even_tables(l, gqa_q_gain[j], gqa_k_gain[j], False), l)
            o1 = _na_attention(naq, nak, nav, cnak, cnav, na_rpb[j])
            o2 = _gqa_attention(gq, gk, gv, cgk, cgv, 256, ATTN_SUB)
            if not last:
                co1 = _slot_attention(cnaq, None, None, cnak, cnav, NA_HEADS, False, l)
                co2 = _gqa_attention(cgq, None, None, cgk, cgv, l)
            w_out = even_w_out[j]
        else:
            lam_init = 0.8 - 0.6 * math.exp(-0.3 * i)
            weights = _odd_weights(odd_w_in[j], mla_w_uq[j], mla_w_ukv[j])
            gains = (mla_q_gain[j], mla_kv_gain[j])
            dq, dk, dv, mq, mk, mv = _inproj_odd(x, sc1, sh1, weights, gains, _odd_tables(s, True), ts)
            cdq, cdk, cdv, cmq, cmk, cmv = _inproj_odd(ctx, csc1, csh1, weights, gains,
                                                       _odd_tables(l, False), l)
            lams = (diff_lq1[j], diff_lk1[j], diff_lq2[j], diff_lk2[j])
            o1 = _diff_attention(dq, dk, dv, cdk, cdv, lams, diff_subln[j], lam_init, 256, ATTN_SUB)
            o2 = _slot_attention(mq, mk, mv, cmk, cmv, MLA_HEADS, True, 512)
            if not last:
                co1 = _diff_attention(cdq, None, None, cdk, cdv, lams, diff_subln[j], lam_init, l)
                co2 = _slot_attention(cmq, None, None, cmk, cmv, MLA_HEADS, True, l)
            w_out = odd_w_out[j]

        router = _router_weights(router_w[i], router_b[i])
        cnt0 = jnp.zeros((1, LANES), F32)
        x, h2a, h2b, ridx, rrank, rgw, cnt = _outproj_ln_router(
            o1, o2, w_out, x, g1, ln1_g[i], ln1_b[i], sc2, sh2, router, cnt0, alpha, ts)
        flat = lambda a, n: a.reshape(bsz * n, a.shape[-1])
        h2a, h2b, ridx, rrank, rgw = [flat(a, s) for a in (h2a, h2b, ridx, rrank, rgw)]
        if not last:
            ctx, ch2a, ch2b, cidx, crank, cgw, cnt = _outproj_ln_router(
                co1, co2, w_out, ctx, cg1, ln1_g[i], ln1_b[i], csc2, csh2, router, cnt, alpha, l)
            cat = lambda a, b: jnp.concatenate([a, flat(b, l)], axis=0)
            h2a, h2b = cat(h2a, ch2a), cat(h2b, ch2b)
            ridx, rrank, rgw = cat(ridx, cidx), cat(rrank, crank), cat(rgw, cgw)

        counts = cnt[0, :n_experts].astype(jnp.int32)
        dpos_t, p, tile_expert, tile_valid = _moe_plan(ridx[:, :TOP_K], rrank[:, :TOP_K], counts, tm)
        xsa = _sc_dispatch(h2a, dpos_t, p)
        xsb = _sc_dispatch(h2b, dpos_t, p)
        ys = _expert_ffn(xsa, xsb, tile_expert, tile_valid, i, exp_w_gu, exp_b_gu[i],
                         exp_w_down, exp_b_down[i], tm)
        yg = ys.at[dpos_t].get(mode="promise_in_bounds")

        x = _combine_ln(yg, rgw, x, g2, ln2_g[i], ln2_b[i], alpha, 0, ts)
        if not last:
            ctx = _combine_ln(yg, rgw, ctx, cg2, ln2_g[i], ln2_b[i], alpha, bsz * s, l)
    return x
```

```python
import functools
import math

import numpy as np
import jax
import jax.numpy as jnp
from jax import lax
from jax.experimental import pallas as pl
from jax.experimental.pallas import tpu as pltpu
from jax.experimental.pallas import tpu_sc as plsc

F32 = jnp.float32
BF16 = jnp.bfloat16

GRID_W = 64
HEAD_DIM = 64
ROPE_THETA = 10000.0
LN_EPS = 1e-6
RMS_EPS = 1e-6
NEG_INF = -1e30
NA_HEADS = 8
NA_WIN_H = 8
NA_WIN_W = 16
GQA_HEADS = 8
GQA_KV_HEADS = 2
DIFF_HEADS = 4
MLA_HEADS = 8
MLA_Q_RANK = 256
MLA_KV_RANK = 128
MLA_NOPE = 64
MLA_ROPE = 32
MLA_V = 64
MLA_QK = MLA_NOPE + MLA_ROPE
TOP_K = 4
SWIGLU_ALPHA = 1.702
SWIGLU_LIMIT = 7.0
LOG2E = 1.4426950408889634

LANES = 128
VMEM_LIMIT = 56 * 1024 * 1024
MOE_TILE = 1024
GU_BLOCK = 512
ATTN_SUB = 2


def _cparams(n_axes):
    return pltpu.CompilerParams(dimension_semantics=("arbitrary",) * n_axes,
                                vmem_limit_bytes=VMEM_LIMIT)


def _dot(a, b):
    return jnp.dot(a, b, preferred_element_type=F32)


def _dot_nt(a, b):
    return lax.dot_general(a, b, (((1,), (1,)), ((), ())), preferred_element_type=F32)


def _mod_kernel(c_ref, w_ref, b_ref, o_ref):
    cv = c_ref[...]
    a = (cv * jax.nn.sigmoid(cv)).astype(BF16)
    o_ref[0] = _dot(a, w_ref[0].astype(BF16)) + b_ref[0]


def _modulation_all(c_all, mod_w, mod_b):
    depth, d, n = mod_w.shape
    rows = c_all.shape[0]
    tn = 1536
    return pl.pallas_call(
        _mod_kernel,
        grid=(depth, n // tn),
        in_specs=[pl.BlockSpec((rows, d), lambda i, j: (0, 0)),
                  pl.BlockSpec((1, d, tn), lambda i, j: (i, 0, j)),
                  pl.BlockSpec((1, 1, tn), lambda i, j: (i, 0, j))],
        out_specs=pl.BlockSpec((1, rows, tn), lambda i, j: (i, 0, j)),
        out_shape=jax.ShapeDtypeStruct((depth, rows, n), F32),
        compiler_params=_cparams(2),
    )(c_all, mod_w, mod_b.reshape(depth, 1, n))


def _rope_tables(s, dim):
    pos = jnp.arange(s)
    row = (pos // GRID_W).astype(F32)[:, None]
    col = (pos % GRID_W).astype(F32)[:, None]
    quarter = dim // 4
    inv_freq = ROPE_THETA ** (-jnp.arange(quarter, dtype=F32) / quarter)
    ar, ac = row * inv_freq, col * inv_freq
    cos = jnp.concatenate([jnp.cos(ar), jnp.cos(ar), jnp.cos(ac), jnp.cos(ac)], axis=-1)
    sin = jnp.concatenate([-jnp.sin(ar), jnp.sin(ar), -jnp.sin(ac), jnp.sin(ac)], axis=-1)
    return cos, sin


def _swap_perm(dim):
    q = dim // 4
    idx = np.arange(dim)
    return np.where((idx % (2 * q)) < q, idx + q, idx - q)


def _rope_partner(y, dim):
    q = dim // 4
    lane = lax.broadcasted_iota(jnp.int32, (1, LANES), 1)
    first = (lane % (2 * q)) < q
    tiles = []
    for j in range(0, y.shape[1], LANES):
        t = y[:, j:j + LANES]
        tiles.append(jnp.where(first, pltpu.roll(t, LANES - q, 1), pltpu.roll(t, q, 1)))
    return tiles[0] if len(tiles) == 1 else jnp.concatenate(tiles, axis=-1)


def _swap_cols(w, dim):
    n = w.shape[-1] // dim
    perm = (np.arange(n)[:, None] * dim + _swap_perm(dim)[None, :]).reshape(-1)
    return w[..., perm]


def _block_ones(n, blk):
    i = np.arange(n) // blk
    return jnp.asarray((i[:, None] == i[None, :]).astype(np.float32), dtype=BF16)


E_NAQ, E_NAK, E_NAV, E_GQ, E_GK, E_GV, E_END = (0, 512, 1024, 1536, 2048, 2304, 2560)


def _even_weights(w_in):
    naq, nak, nav, gq, gk, gv = jnp.split(w_in, [512, 1024, 1536, 2048, 2176], axis=-1)
    k0, k1 = gk[:, :64], gk[:, 64:]
    v0, v1 = gv[:, :64], gv[:, 64:]
    gk2 = jnp.concatenate([k0, k1, k1, k0], axis=-1)
    gv2 = jnp.concatenate([v0, v1, v1, v0], axis=-1)
    w = jnp.concatenate([naq * (0.125 * LOG2E), nak, nav, gq, gk2, gv2], axis=-1).astype(BF16)
    return w


def _even_tables(s, q_gain, k_gain, rope):
    sw = _swap_perm(64)
    if rope:
        cos, sin = _rope_tables(s, 64)
    else:
        cos, sin = jnp.ones((s, 64), F32), jnp.zeros((s, 64), F32)
    qa = jnp.tile(cos * q_gain[None, :] * (0.125 * LOG2E), (1, 8))
    qb = jnp.tile(sin * q_gain[sw][None, :] * (0.125 * LOG2E), (1, 8))
    ka = jnp.tile(cos * k_gain[None, :], (1, 4))
    kb = jnp.tile(sin * k_gain[sw][None, :], (1, 4))
    return qa, qb, ka, kb


def _inproj_even_kernel(x_ref, sc_ref, sh_ref, w_ref, g512_ref, g256_ref,
                        qa_ref, qb_ref, ka_ref, kb_ref,
                        naq_ref, nak_ref, nav_ref, gq_ref, gk_ref, gv_ref):
    h = (x_ref[0] * (1.0 + sc_ref[0]) + sh_ref[0]).astype(BF16)
    naq_ref[0] = _dot(h, w_ref[:, E_NAQ:E_NAK]).astype(BF16)
    nak_ref[0] = _dot(h, w_ref[:, E_NAK:E_NAV]).astype(BF16)
    nav_ref[0] = _dot(h, w_ref[:, E_NAV:E_GQ]).astype(BF16)
    gv_ref[0] = _dot(h, w_ref[:, E_GV:E_END]).astype(BF16)
    y = _dot(h, w_ref[:, E_GQ:E_GK])
    ys = _rope_partner(y, HEAD_DIM)
    r = lax.rsqrt(_dot((y * y).astype(BF16), g512_ref[...]) * (1.0 / HEAD_DIM) + RMS_EPS)
    gq_ref[0] = (r * (y * qa_ref[...] + ys * qb_ref[...])).astype(BF16)
    y = _dot(h, w_ref[:, E_GK:E_GV])
    ys = _rope_partner(y, HEAD_DIM)
    r = lax.rsqrt(_dot((y * y).astype(BF16), g256_ref[...]) * (1.0 / HEAD_DIM) + RMS_EPS)
    gk_ref[0] = (r * (y * ka_ref[...] + ys * kb_ref[...])).astype(BF16)


def _mod_spec(d, batched):
    if batched:
        return pl.BlockSpec((1, 1, d), lambda b, s: (b, 0, 0))
    return pl.BlockSpec((1, 1, d), lambda b, s: (0, 0, 0))


def _const_spec(shape):
    nd = len(shape)
    return pl.BlockSpec(shape, lambda b, s: (0,) * nd)


def _inproj_even(x, sc, sh, w, tables, ts):
    bsz, s, d = x.shape
    batched = sc.shape[0] > 1
    qa, qb, ka, kb = tables
    tok = lambda n: pl.BlockSpec((1, ts, n), lambda b, i: (b, i, 0))
    tab = lambda n: pl.BlockSpec((ts, n), lambda b, i: (i, 0))
    widths = (512, 512, 512, 512, 256, 256)
    return pl.pallas_call(
        _inproj_even_kernel,
        grid=(bsz, s // ts),
        in_specs=[tok(d), _mod_spec(d, batched), _mod_spec(d, batched), _const_spec(w.shape),
                  _const_spec((512, 512)), _const_spec((256, 256)),
                  tab(512), tab(512), tab(256), tab(256)],
        out_specs=[tok(n) for n in widths],
        out_shape=[jax.ShapeDtypeStruct((bsz, s, n), BF16) for n in widths],
        compiler_params=_cparams(2),
    )(x, sc, sh, w, _block_ones(512, 64), _block_ones(256, 64), qa, qb, ka, kb)


O_DQ, O_DK, O_DV, O_CQ, O_CKV, O_PE, O_END = (0, 512, 1024, 1536, 1792, 1920, 2048)


def _odd_weights(w_in, w_uq, w_ukv):
    dq, dk, dv, cq, ckv, kpe = jnp.split(w_in, [512, 1024, 1536, 1792, 1920], axis=-1)
    d = w_in.shape[0]
    pe_slot = jnp.concatenate([kpe, _swap_cols(kpe, 32), jnp.zeros((d, 64), F32)], axis=-1)
    w = jnp.concatenate([dq, dk, dv, cq, ckv, pe_slot], axis=-1).astype(BF16)
    uq = w_uq.reshape(MLA_Q_RANK, MLA_HEADS, MLA_QK)
    z32 = jnp.zeros((MLA_Q_RANK, MLA_HEADS, 32), F32)
    z64 = jnp.zeros((MLA_Q_RANK, MLA_HEADS, 64), F32)
    uq_pad = jnp.concatenate([uq, z32], axis=-1).reshape(MLA_Q_RANK, MLA_HEADS * LANES)
    uq_sw = jnp.concatenate([z64, _swap_cols(uq[..., MLA_NOPE:], 32), z32],
                            axis=-1).reshape(MLA_Q_RANK, MLA_HEADS * LANES)
    wuq2 = jnp.concatenate([uq_pad, uq_sw], axis=-1).astype(BF16)
    ukv = w_ukv.reshape(MLA_KV_RANK, MLA_HEADS, MLA_NOPE + MLA_V)
    zk = jnp.zeros((MLA_KV_RANK, MLA_HEADS, 64), F32)
    wk_pad = jnp.concatenate([ukv[..., :MLA_NOPE], zk], axis=-1).reshape(MLA_KV_RANK, MLA_HEADS * LANES)
    place = np.zeros((LANES, MLA_HEADS, LANES), np.float32)
    for j in range(MLA_ROPE):
        place[j, :, MLA_NOPE + j] = 1.0
        place[MLA_ROPE + j, :, MLA_NOPE + j] = 1.0
    wk2 = jnp.concatenate([wk_pad, jnp.asarray(place.reshape(LANES, MLA_HEADS * LANES))],
                          axis=0).astype(BF16)
    wv = ukv[..., MLA_NOPE:].reshape(MLA_KV_RANK, MLA_HEADS * MLA_V).astype(BF16)
    return w, wuq2, wk2, wv


def _odd_tables(s, rope):
    m_scale = MLA_QK ** -0.5 * LOG2E
    if rope:
        cos64, sin64 = _rope_tables(s, 64)
        cos32, sin32 = _rope_tables(s, 32)
    else:
        cos64, sin64 = jnp.ones((s, 64), F32), jnp.zeros((s, 64), F32)
        cos32, sin32 = jnp.ones((s, 32), F32), jnp.zeros((s, 32), F32)
    one64, z32, z64 = jnp.ones((s, 64), F32), jnp.zeros((s, 32), F32), jnp.zeros((s, 64), F32)
    dcos, dsin = jnp.tile(cos64, (1, 8)), jnp.tile(sin64, (1, 8))
    qa = jnp.tile(jnp.concatenate([one64, cos32, z32], axis=-1) * m_scale, (1, MLA_HEADS))
    qb = jnp.tile(jnp.concatenate([z64, sin32, z32], axis=-1) * m_scale, (1, MLA_HEADS))
    pe = jnp.concatenate([cos32, sin32, z64], axis=-1)
    return dcos, dsin, qa, qb, pe


def _inproj_odd_kernel(x_ref, sc_ref, sh_ref, w_ref, wuq_ref, wk_ref, wv_ref, qg_ref, kvg_ref,
                       dcos_ref, dsin_ref, qa_ref, qb_ref, pe_ref,
                       dq_ref, dk_ref, dv_ref, mq_ref, mk_ref, mv_ref):
    h = (x_ref[0] * (1.0 + sc_ref[0]) + sh_ref[0]).astype(BF16)
    dcos, dsin = dcos_ref[...], dsin_ref[...]
    y = _dot(h, w_ref[:, O_DQ:O_DK])
    ys = _rope_partner(y, HEAD_DIM)
    dq_ref[0] = ((y * dcos + ys * dsin) * (0.125 * LOG2E)).astype(BF16)
    y = _dot(h, w_ref[:, O_DK:O_DV])
    ys = _rope_partner(y, HEAD_DIM)
    dk_ref[0] = (y * dcos + ys * dsin).astype(BF16)
    dv_ref[0] = _dot(h, w_ref[:, O_DV:O_CQ]).astype(BF16)
    cq = _dot(h, w_ref[:, O_CQ:O_CKV])
    nq = cq * lax.rsqrt(jnp.mean(cq * cq, axis=-1, keepdims=True) + RMS_EPS) * qg_ref[...]
    y2 = _dot(nq.astype(BF16), wuq_ref[...])
    half = MLA_HEADS * LANES
    mq_ref[0] = (y2[:, :half] * qa_ref[...] + y2[:, half:] * qb_ref[...]).astype(BF16)
    ckv = _dot(h, w_ref[:, O_CKV:O_PE])
    nk = ckv * lax.rsqrt(jnp.mean(ckv * ckv, axis=-1, keepdims=True) + RMS_EPS) * kvg_ref[...]
    pe = _dot(h, w_ref[:, O_PE:O_END]) * pe_ref[...]
    nkb = nk.astype(BF16)
    cat = jnp.concatenate([nkb, pe.astype(BF16)], axis=-1)
    mk_ref[0] = _dot(cat, wk_ref[...]).astype(BF16)
    mv_ref[0] = _dot(nkb, wv_ref[...]).astype(BF16)


def _inproj_odd(x, sc, sh, weights, gains, tables, ts):
    bsz, s, d = x.shape
    batched = sc.shape[0] > 1
    w, wuq2, wk2, wv = weights
    qg, kvg = gains
    dcos, dsin, qa, qb, pe = tables
    tok = lambda n: pl.BlockSpec((1, ts, n), lambda b, i: (b, i, 0))
    tab = lambda n: pl.BlockSpec((ts, n), lambda b, i: (i, 0))
    widths = (512, 512, 512, 1024, 1024, 512)
    return pl.pallas_call(
        _inproj_odd_kernel,
        grid=(bsz, s // ts),
        in_specs=[tok(d), _mod_spec(d, batched), _mod_spec(d, batched), _const_spec(w.shape),
                  _const_spec(wuq2.shape), _const_spec(wk2.shape), _const_spec(wv.shape),
                  _const_spec((1, MLA_Q_RANK)), _const_spec((1, MLA_KV_RANK)),
                  tab(512), tab(512), tab(1024), tab(1024), tab(128)],
        out_specs=[tok(n) for n in widths],
        out_shape=[jax.ShapeDtypeStruct((bsz, s, n), BF16) for n in widths],
        compiler_params=_cparams(2),
    )(x, sc, sh, w, wuq2, wk2, wv, qg.reshape(1, -1), kvg.reshape(1, -1), dcos, dsin, qa, qb, pe)


def _half_masks():
    lane = lax.broadcasted_iota(jnp.int32, (1, LANES), 1)
    return lane < HEAD_DIM, lane >= HEAD_DIM


def _softmax_values(ss, vs):
    m = functools.reduce(jnp.maximum, [jnp.max(s, axis=-1, keepdims=True) for s in ss])
    es = [jnp.exp2(s - m) for s in ss]
    l = functools.reduce(lambda a, b: a + b, [jnp.sum(e, axis=-1, keepdims=True) for e in es])
    o = functools.reduce(lambda a, b: a + b, [_dot(e.astype(BF16), v) for e, v in zip(es, vs)])
    return o * (1.0 / l)


def _attend(qm, ks, vs):
    return _softmax_values([_dot_nt(qm, k) for k in ks], vs)


NA_RB = 4
NA_WIN_ROWS = NA_RB + NA_WIN_H - 1
NA_SUB = 2


def _na_block_plan(rows):
    plan = []
    for blk in range(rows // NA_RB):
        r0 = blk * NA_RB
        rs = [int(np.clip(r0 + i - NA_WIN_H // 2, 0, rows - NA_WIN_H)) for i in range(NA_RB)]
        ws = int(np.clip(r0 - NA_WIN_H // 2, 0, rows - NA_WIN_ROWS))
        pat = tuple((rs[i] - ws, r0 + i - ws) for i in range(NA_RB))
        assert all(0 <= o and o + NA_WIN_H <= NA_WIN_ROWS for o, _ in pat)
        plan.append(pat)
    assert all(p == plan[1] for p in plan[1:-1])
    return (plan[0], plan[1], plan[-1])


def _na_bias_table(rpb, rows):
    cols = np.arange(GRID_W)
    col_start = np.clip(cols - NA_WIN_W // 2, 0, GRID_W - NA_WIN_W)
    col_mask = (cols[None, :] >= col_start[:, None]) & (cols[None, :] < col_start[:, None] + NA_WIN_W)
    col_idx = np.clip(cols[None, :] - cols[:, None] + NA_WIN_W - 1, 0, 2 * NA_WIN_W - 2)
    wr = np.arange(NA_WIN_ROWS)
    tables = []
    for pat in _na_block_plan(rows):
        off = np.array([o for o, _ in pat])[:, None]
        rq = np.array([r for _, r in pat])[:, None]
        row_ok = (wr[None, :] >= off) & (wr[None, :] < off + NA_WIN_H)
        ridx = np.clip(wr[None, :] - rq + NA_WIN_H - 1, 0, 2 * NA_WIN_H - 2)
        t = rpb.astype(F32)[:, ridx]
        t = t[..., col_idx]
        t = t.transpose(0, 1, 3, 2, 4)
        ok = row_ok[None, :, None, :, None] & col_mask[None, None, :, None, :]
        t = jnp.where(ok, t * LOG2E, NEG_INF)
        tables.append(t.reshape(NA_HEADS, NA_RB * GRID_W, NA_WIN_ROWS * GRID_W))
    return jnp.stack(tables)


def _na_kernel(rows, sub, q_ref, k_ref, v_ref, kc_ref, vc_ref, *refs):
    bt_refs, o_ref = refs[:sub], refs[sub]
    step = pl.program_id(1)
    tq = NA_RB * GRID_W
    m0, m1 = _half_masks()
    for u in range(sub):
        r0 = (step * sub + u) * NA_RB
        ws = jnp.clip(r0 - NA_WIN_H // 2, 0, rows - NA_WIN_ROWS)
        win = pl.ds(pl.multiple_of(ws * GRID_W, GRID_W), NA_WIN_ROWS * GRID_W)
        qrows = slice(u * tq, (u + 1) * tq)
        for j in range(NA_HEADS // 2):
            sl = slice(j * LANES, (j + 1) * LANES)
            qp = q_ref[0, qrows, sl]
            kp, vp = k_ref[0, win, sl], v_ref[0, win, sl]
            kcp, vcp = kc_ref[0, :, sl], vc_ref[0, :, sl]
            outs = []
            for par, msk in ((0, m0), (1, m1)):
                qm = jnp.where(msk, qp, jnp.zeros_like(qp))
                s_loc = _dot_nt(qm, kp) + bt_refs[u][0, 2 * j + par]
                outs.append(_softmax_values([s_loc, _dot_nt(qm, kcp)], [vp, vcp]))
            o_ref[0, qrows, sl] = jnp.where(m0, outs[0], outs[1]).astype(BF16)


def _na_attention(q, k, v, kc, vc, rpb):
    bsz, s, w = q.shape
    l = kc.shape[1]
    rows = s // GRID_W
    nblk = rows // NA_RB
    sub = NA_SUB
    nstep = nblk // sub
    assert rows % NA_RB == 0 and rows >= NA_WIN_ROWS and nblk % sub == 0 and nblk >= 3
    bt = _na_bias_table(rpb, rows)
    tq = NA_RB * GRID_W * sub
    full = lambda n: pl.BlockSpec((1, n, w), lambda b, r: (b, 0, 0))

    def bt_spec(u):
        def kind(b, r):
            blk = r * sub + u
            return ((blk > 0).astype(jnp.int32) + (blk == nblk - 1).astype(jnp.int32), 0, 0, 0)
        return pl.BlockSpec((1,) + bt.shape[1:], kind)

    return pl.pallas_call(
        functools.partial(_na_kernel, rows, sub),
        grid=(bsz, nstep),
        in_specs=[pl.BlockSpec((1, tq, w), lambda b, r: (b, r, 0)),
                  full(s), full(s), full(l), full(l)] + [bt_spec(u) for u in range(sub)],
        out_specs=pl.BlockSpec((1, tq, w), lambda b, r: (b, r, 0)),
        out_shape=jax.ShapeDtypeStruct((bsz, s, w), BF16),
        compiler_params=_cparams(2),
    )(q, k, v, kc, vc, *([bt] * sub))


def _slot_attn_kernel(n_heads, q_slot, has_lat, sub, *refs):
    if has_lat:
        q_ref, k_ref, v_ref, kc_ref, vc_ref, o_ref = refs
    else:
        q_ref, kc_ref, vc_ref, o_ref = refs
    m0, m1 = _half_masks()
    tq = q_ref.shape[1] // sub
    for u in range(sub):
        qrows = slice(u * tq, (u + 1) * tq)
        for j in range(n_heads // 2):
            vsl = slice(j * LANES, (j + 1) * LANES)
            outs = []
            for par, msk in ((0, m0), (1, m1)):
                h = 2 * j + par
                if q_slot:
                    ksl = slice(h * LANES, (h + 1) * LANES)
                    qm = q_ref[0, qrows, ksl]
                else:
                    ksl = vsl
                    qp = q_ref[0, qrows, vsl]
                    qm = jnp.where(msk, qp, jnp.zeros_like(qp))
                ks, vs = [kc_ref[0, :, ksl]], [vc_ref[0, :, vsl]]
                if has_lat:
                    ks.insert(0, k_ref[0, :, ksl])
                    vs.insert(0, v_ref[0, :, vsl])
                outs.append(_attend(qm, ks, vs))
            o_ref[0, qrows, vsl] = jnp.where(m0, outs[0], outs[1]).astype(BF16)


def _slot_attention(q, k, v, kc, vc, n_heads, q_slot, tq, sub=1):
    bsz, sq, wq = q.shape
    l, wk, wv = kc.shape[1], kc.shape[2], vc.shape[2]
    has_lat = k is not None
    qspec = pl.BlockSpec((1, tq, wq), lambda b, i: (b, i, 0))
    full = lambda n, w: pl.BlockSpec((1, n, w), lambda b, i: (b, 0, 0))
    in_specs, args = [qspec], [q]
    if has_lat:
        s = k.shape[1]
        in_specs += [full(s, wk), full(s, wv)]
        args += [k, v]
    in_specs += [full(l, wk), full(l, wv)]
    args += [kc, vc]
    return pl.pallas_call(
        functools.partial(_slot_attn_kernel, n_heads, q_slot, has_lat, sub),
        grid=(bsz, sq // tq),
        in_specs=in_specs,
        out_specs=pl.BlockSpec((1, tq, wv), lambda b, i: (b, i, 0)),
        out_shape=jax.ShapeDtypeStruct((bsz, sq, wv), BF16),
        compiler_params=_cparams(2),
    )(*args)


def _gqa_kernel(has_lat, tq, sub, *refs):
    if has_lat:
        q_ref, k_ref, v_ref, kc_ref, vc_ref, o_ref = refs
    else:
        q_ref, kc_ref, vc_ref, o_ref = refs
    masks = _half_masks()
    for u in range(sub):
        qrows = slice(u * tq, (u + 1) * tq)
        res = {}
        for g in range(GQA_KV_HEADS):
            for var in range(2):
                par = g if var == 0 else 1 - g
                heads = (4 * g + par, 4 * g + 2 + par)
                vsl = slice(var * LANES, (var + 1) * LANES)
                qs = []
                for h in heads:
                    qp = q_ref[0, qrows, (h // 2) * LANES:(h // 2 + 1) * LANES]
                    qs.append(jnp.where(masks[par], qp, jnp.zeros_like(qp)))
                qm = jnp.concatenate(qs, axis=0)
                ks, vs = [kc_ref[0, :, vsl]], [vc_ref[0, :, vsl]]
                if has_lat:
                    ks.insert(0, k_ref[0, :, vsl])
                    vs.insert(0, v_ref[0, :, vsl])
                o = _attend(qm, ks, vs)
                res[heads[0]] = o[:tq]
                res[heads[1]] = o[tq:]
        for j in range(GQA_HEADS // 2):
            o_ref[0, qrows, j * LANES:(j + 1) * LANES] = jnp.where(
                masks[0], res[2 * j], res[2 * j + 1]).astype(BF16)


def _gqa_attention(q, k2, v2, k2c, v2c, tq, sub=1):
    bsz, sq, wq = q.shape
    l = k2c.shape[1]
    has_lat = k2 is not None
    full = lambda n: pl.BlockSpec((1, n, 2 * LANES), lambda b, i: (b, 0, 0))
    in_specs, args = [pl.BlockSpec((1, tq * sub, wq), lambda b, i: (b, i, 0))], [q]
    if has_lat:
        in_specs += [full(k2.shape[1])] * 2
        args += [k2, v2]
    in_specs += [full(l)] * 2
    args += [k2c, v2c]
    return pl.pallas_call(
        functools.partial(_gqa_kernel, has_lat, tq, sub),
        grid=(bsz, sq // (tq * sub)),
        in_specs=in_specs,
        out_specs=pl.BlockSpec((1, tq * sub, wq), lambda b, i: (b, i, 0)),
        out_shape=jax.ShapeDtypeStruct((bsz, sq, wq), BF16),
        compiler_params=_cparams(2),
    )(*args)


def _diff_kernel(has_lat, tq, sub, lam_init, *refs):
    if has_lat:
        q_ref, k_ref, v_ref, kc_ref, vc_ref, lq1, lk1, lq2, lk2, sub_ref, o_ref = refs
    else:
        q_ref, kc_ref, vc_ref, lq1, lk1, lq2, lk2, sub_ref, o_ref = refs
    lam = (jnp.exp(jnp.sum(lq1[...] * lk1[...], axis=-1, keepdims=True))
           - jnp.exp(jnp.sum(lq2[...] * lk2[...], axis=-1, keepdims=True)) + lam_init)
    m0, m1 = _half_masks()
    for u in range(sub):
        qrows = slice(u * tq, (u + 1) * tq)
        for h in range(DIFF_HEADS):
            sl = slice(h * LANES, (h + 1) * LANES)
            qp = q_ref[0, qrows, sl]
            zero = jnp.zeros_like(qp)
            qm = jnp.concatenate([jnp.where(m0, qp, zero), jnp.where(m1, qp, zero)], axis=0)
            ks, vs = [kc_ref[0, :, sl]], [vc_ref[0, :, sl]]
            if has_lat:
                ks.insert(0, k_ref[0, :, sl])
                vs.insert(0, v_ref[0, :, sl])
            o2 = _attend(qm, ks, vs)
            o = o2[:tq] - lam * o2[tq:]
            o = o * lax.rsqrt(jnp.mean(o * o, axis=-1, keepdims=True) + RMS_EPS) * sub_ref[...]
            o_ref[0, qrows, sl] = (o * (1.0 - lam_init)).astype(BF16)


def _diff_attention(q, k, v, kc, vc, lams, subln, lam_init, tq, sub=1):
    bsz, sq, w = q.shape
    l = kc.shape[1]
    has_lat = k is not None
    full = lambda n: pl.BlockSpec((1, n, w), lambda b, i: (b, 0, 0))
    in_specs, args = [pl.BlockSpec((1, tq * sub, w), lambda b, i: (b, i, 0))], [q]
    if has_lat:
        in_specs += [full(k.shape[1])] * 2
        args += [k, v]
    in_specs += [full(l)] * 2 + [_const_spec((1, HEAD_DIM))] * 4 + [_const_spec((1, LANES))]
    args += [kc, vc] + [a.reshape(1, -1).astype(F32) for a in lams] + [subln.reshape(1, -1).astype(F32)]
    return pl.pallas_call(
        functools.partial(_diff_kernel, has_lat, tq, sub, lam_init),
        grid=(bsz, sq // (tq * sub)),
        in_specs=in_specs,
        out_specs=pl.BlockSpec((1, tq * sub, w), lambda b, i: (b, i, 0)),
        out_shape=jax.ShapeDtypeStruct((bsz, sq, w), BF16),
        compiler_params=_cparams(2),
    )(*args)


def _pack_bf16_pairs(v):
    w = v.shape[1] // 2
    hi = pltpu.bitcast(v[:, :w].astype(F32), jnp.int32)
    lo = pltpu.bitcast(v[:, w:].astype(F32), jnp.int32)
    return hi | lax.shift_right_logical(lo, 16)


def _unpack_bf16_pairs(u):
    hi = pltpu.bitcast(u & jnp.int32(-65536), F32)
    lo = pltpu.bitcast(lax.shift_left(u, 16), F32)
    return hi, lo


def _layer_norm(z, g, b):
    mu = jnp.mean(z, axis=-1, keepdims=True)
    zc = z - mu
    var = jnp.mean(zc * zc, axis=-1, keepdims=True)
    return zc * lax.rsqrt(var + LN_EPS) * g + b


def _outproj_kernel(alpha, sub, o1_ref, o2_ref, w1_ref, w2_ref, x_ref, g1_ref, lng_ref, lnb_ref,
                    sc2_ref, sh2_ref, rwh_ref, rwl_ref, rb_ref, tri_ref, cnt0_ref,
                    xo_ref, h2a_ref, h2b_ref, idx_ref, rank_ref, gw_ref, cnt_ref, carry_ref):
    @pl.when(jnp.logical_and(pl.program_id(0) == 0, pl.program_id(1) == 0))
    def _():
        carry_ref[...] = cnt0_ref[...]

    tr = x_ref.shape[1] // sub
    carry = carry_ref[...]
    for u in range(sub):
        rows = slice(u * tr, (u + 1) * tr)
        o = _dot(o1_ref[0, rows], w1_ref[...]) + _dot(o2_ref[0, rows], w2_ref[...])
        xn = _layer_norm(alpha * x_ref[0, rows] + g1_ref[0] * o, lng_ref[...], lnb_ref[...])
        xo_ref[0, rows] = xn
        h2 = xn * (1.0 + sc2_ref[0]) + sh2_ref[0]
        hi = h2.astype(BF16)
        packed = _pack_bf16_pairs(hi)
        q = packed.shape[1] // 2
        h2a_ref[0, rows] = packed[:, :q]
        h2b_ref[0, rows] = packed[:, q:]
        lo = (h2 - hi.astype(F32)).astype(BF16)
        logits = (_dot(hi, rwh_ref[...]) + _dot(lo, rwh_ref[...]) + _dot(hi, rwl_ref[...])) + rb_ref[...]
        lane = lax.broadcasted_iota(jnp.int32, logits.shape, 1).astype(F32)
        vals, idxs = [], []
        cur = logits
        for _ in range(TOP_K):
            m = jnp.max(cur, axis=-1, keepdims=True)
            ik = jnp.min(jnp.where(cur == m, lane, float(LANES)), axis=-1, keepdims=True)
            vals.append(m)
            idxs.append(ik)
            cur = jnp.where(lane == ik, -jnp.inf, cur)
        ws = [jnp.exp(v - vals[0]) for v in vals]
        inv = 1.0 / functools.reduce(lambda a, b: a + b, ws)
        sel = [lane == ik for ik in idxs]
        onehot = functools.reduce(lambda a, b: a + b, [m.astype(F32) for m in sel])
        before = _dot(tri_ref[...], onehot.astype(BF16)) + carry
        idx_out = jnp.zeros_like(logits)
        rank_out = jnp.zeros_like(logits)
        w_out = jnp.zeros_like(logits)
        for k in range(TOP_K):
            rk = jnp.sum(jnp.where(sel[k], before, 0.0), axis=-1, keepdims=True)
            idx_out = jnp.where(lane == float(k), idxs[k], idx_out)
            rank_out = jnp.where(lane == float(k), rk, rank_out)
            w_out = jnp.where(lane == float(k), ws[k] * inv, w_out)
        idx_ref[0, rows] = idx_out.astype(jnp.int32)
        rank_ref[0, rows] = rank_out.astype(jnp.int32)
        gw_ref[0, rows] = w_out
        carry = carry + jnp.sum(onehot, axis=0, keepdims=True)
    carry_ref[...] = carry
    cnt_ref[...] = carry


def _outproj_ln_router(o1, o2, w_out, x, g1, lng, lnb, sc2, sh2, router, cnt0, alpha, ts):
    bsz, s, d = x.shape
    batched = g1.shape[0] > 1
    rwh, rwl, rb = router
    w1, w2 = w_out[:512].astype(BF16), w_out[512:].astype(BF16)
    sub = 1
    tr = ts // sub
    tri = jnp.asarray(np.tril(np.ones((tr, tr), np.float32), -1), dtype=BF16)
    tok = lambda n: pl.BlockSpec((1, ts, n), lambda b, i: (b, i, 0))
    ms = _mod_spec(d, batched)
    return pl.pallas_call(
        functools.partial(_outproj_kernel, alpha, sub),
        grid=(bsz, s // ts),
        in_specs=[tok(512), tok(512), _const_spec((512, d)), _const_spec((512, d)), tok(d), ms,
                  _const_spec((1, d)), _const_spec((1, d)), ms, ms,
                  _const_spec((d, LANES)), _const_spec((d, LANES)), _const_spec((1, LANES)),
                  _const_spec((tr, tr)), _const_spec((1, LANES))],
        out_specs=[tok(d), tok(d // 4), tok(d // 4), tok(LANES), tok(LANES), tok(LANES),
                   _const_spec((1, LANES))],
        out_shape=[jax.ShapeDtypeStruct((bsz, s, d), F32),
                   jax.ShapeDtypeStruct((bsz, s, d // 4), jnp.int32),
                   jax.ShapeDtypeStruct((bsz, s, d // 4), jnp.int32),
                   jax.ShapeDtypeStruct((bsz, s, LANES), jnp.int32),
                   jax.ShapeDtypeStruct((bsz, s, LANES), jnp.int32),
                   jax.ShapeDtypeStruct((bsz, s, LANES), F32),
                   jax.ShapeDtypeStruct((1, LANES), F32)],
        scratch_shapes=[pltpu.VMEM((1, LANES), F32)],
        compiler_params=_cparams(2),
    )(o1, o2, w1, w2, x, g1, lng.reshape(1, d), lnb.reshape(1, d), sc2, sh2, rwh, rwl, rb, tri, cnt0)


def _router_weights(router_w, router_b):
    d, e = router_w.shape
    wp = jnp.zeros((d, LANES), F32).at[:, :e].set(router_w)
    hi = wp.astype(BF16)
    lo = (wp - hi.astype(F32)).astype(BF16)
    rb = jnp.full((1, LANES), -jnp.inf, F32).at[0, :e].set(router_b)
    return hi, lo, rb


def _deinterleave_perm():
    p = np.zeros((GU_BLOCK, GU_BLOCK), np.float32)
    m = np.arange(GU_BLOCK // 2)
    p[2 * m, m] = 1.0
    p[2 * m + 1, GU_BLOCK // 2 + m] = 1.0
    return jnp.asarray(p, dtype=BF16)


def _ffn_kernel(te_ref, tv_ref, xa_ref, xb_ref, wgu_ref, bgu_ref, wd_ref, bd_ref, perm_ref, y_ref,
                wgu_s, wd_s):
    j = pl.program_id(0)
    n_blk = wgu_s.shape[1] // GU_BLOCK
    half = GU_BLOCK // 2

    @pl.when(jnp.logical_or(j == 0, te_ref[j] != te_ref[jnp.maximum(j - 1, 0)]))
    def _():
        for b in range(n_blk):
            sl = slice(b * GU_BLOCK, (b + 1) * GU_BLOCK)
            wgu_s[:, sl] = _dot(wgu_ref[0, 0, :, sl].astype(BF16), perm_ref[...]).astype(BF16)
        wd_s[...] = wd_ref[0, 0].astype(BF16)

    @pl.when(tv_ref[j] > 0)
    def _():
        a_hi, a_lo = _unpack_bf16_pairs(xa_ref[...])
        b_hi, b_lo = _unpack_bf16_pairs(xb_ref[...])
        x = jnp.concatenate([a_hi, b_hi, a_lo, b_lo], axis=-1).astype(BF16)
        acts = []
        for b in range(n_blk):
            sl = slice(b * GU_BLOCK, (b + 1) * GU_BLOCK)
            gu = _dot(x, wgu_s[:, sl]) + bgu_ref[0, :, sl]
            glu = jnp.minimum(gu[:, :half], SWIGLU_LIMIT)
            lin = jnp.clip(gu[:, half:], -SWIGLU_LIMIT, SWIGLU_LIMIT)
            acts.append(((lin + 1.0) * (glu * jax.nn.sigmoid(SWIGLU_ALPHA * glu))).astype(BF16))
        a = jnp.concatenate(acts, axis=-1)
        y_ref[...] = (_dot(a, wd_s[...]) + bd_ref[0]).astype(BF16)

    @pl.when(tv_ref[j] == 0)
    def _():
        y_ref[...] = jnp.zeros_like(y_ref)


def _expert_ffn(xsa, xsb, tile_expert, tile_valid, layer, w_gu, b_gu, w_down, b_down, tm):
    p, q = xsa.shape
    d = 4 * q
    _, e, _, f2 = w_gu.shape
    f = f2 // 2
    half = GU_BLOCK // 2
    bgu = jnp.stack([b_gu[:, 0::2].reshape(e, f // half, half),
                     b_gu[:, 1::2].reshape(e, f // half, half)], axis=2).reshape(e, 1, f2)
    wspec = lambda a, b: pl.BlockSpec((1, 1, a, b), lambda j, te, tv: (layer, te[j], 0, 0))
    bspec = lambda b: pl.BlockSpec((1, 1, b), lambda j, te, tv: (te[j], 0, 0))
    return pl.pallas_call(
        _ffn_kernel,
        grid_spec=pltpu.PrefetchScalarGridSpec(
            num_scalar_prefetch=2,
            grid=(p // tm,),
            in_specs=[pl.BlockSpec((tm, q), lambda j, te, tv: (j, 0)),
                      pl.BlockSpec((tm, q), lambda j, te, tv: (j, 0)),
                      wspec(d, f2), bspec(f2), wspec(f, d), bspec(d),
                      pl.BlockSpec((GU_BLOCK, GU_BLOCK), lambda j, te, tv: (0, 0))],
            out_specs=pl.BlockSpec((tm, d), lambda j, te, tv: (j, 0)),
            scratch_shapes=[pltpu.VMEM((d, f2), BF16), pltpu.VMEM((f, d), BF16)]),
        out_shape=jax.ShapeDtypeStruct((p, d), BF16),
        compiler_params=_cparams(1),
    )(tile_expert, tile_valid, xsa, xsb, w_gu, bgu, w_down, b_down.reshape(e, 1, d),
      _deinterleave_perm())


def _moe_layout(counts, t_total, tm):
    n_experts = counts.shape[0]
    pc = ((counts + tm - 1) // tm) * tm
    pend = jnp.cumsum(pc)
    pstart = pend - pc
    p = t_total * TOP_K + n_experts * tm
    tile_start = jnp.arange(p // tm, dtype=jnp.int32) * tm
    tile_expert = jnp.minimum(jnp.sum((tile_start[:, None] >= pend[None, :]).astype(jnp.int32), axis=1),
                              n_experts - 1).astype(jnp.int32)
    tile_valid = (tile_start < pend[-1]).astype(jnp.int32)
    return pstart, p, tile_expert, tile_valid


SC_WINDOW = 128


def _sc_dispatch(parts, p):
    w = parts[0][0].shape[1]
    k = parts[0][1].shape[0]
    mesh = plsc.VectorSubcoreMesh(core_axis_name="core", subcore_axis_name="subcore")

    @functools.partial(pl.kernel, out_type=jax.ShapeDtypeStruct((p, w), parts[0][0].dtype), mesh=mesh)
    def kern(*refs):
        o_hbm = refs[-1]

        def body(x_vmem, *i_vmems):
            for iv in i_vmems:
                pltpu.sync_copy(x_vmem, o_hbm.at[iv.at[0]])

        for n, (rows, _) in enumerate(parts):
            x_hbm, i_hbms = refs[n * (k + 1)], refs[n * (k + 1) + 1:(n + 1) * (k + 1)]
            pltpu.emit_pipeline(
                body,
                grid=(rows.shape[0] // SC_WINDOW,),
                in_specs=[pl.BlockSpec((SC_WINDOW, w), index_map=lambda i: (i, 0))]
                + [pl.BlockSpec((1, SC_WINDOW), index_map=lambda i: (0, i)) for _ in range(k)],
                out_specs=[],
                core_axis_name=("core", "subcore"),
                dimension_semantics=(pltpu.PARALLEL,),
            )(x_hbm, *i_hbms)

    args = []
    for rows, dpos_t in parts:
        args += [rows] + [dpos_t[kk:kk + 1] for kk in range(k)]
    return kern(*args)


def _combine_kernel(alpha, yg_ref, gw_ref, x_ref, g2_ref, lng_ref, lnb_ref, o_ref):
    gw = gw_ref[...]
    y = yg_ref[0].astype(F32) * gw[:, 0:1]
    for k in range(1, TOP_K):
        y = y + yg_ref[k].astype(F32) * gw[:, k:k + 1]
    o_ref[0] = _layer_norm(alpha * x_ref[0] + g2_ref[0] * y, lng_ref[...], lnb_ref[...])


def _combine_ln(yg, gw, x, g2, lng, lnb, alpha, yg_row0, ts):
    bsz, s, d = x.shape
    batched = g2.shape[0] > 1
    nblk = s // ts
    off = yg_row0 // ts
    return pl.pallas_call(
        functools.partial(_combine_kernel, alpha),
        grid=(bsz, nblk),
        in_specs=[pl.BlockSpec((TOP_K, ts, d), lambda b, i: (0, off + b * nblk + i, 0)),
                  pl.BlockSpec((ts, LANES), lambda b, i: (b * nblk + i, 0)),
                  pl.BlockSpec((1, ts, d), lambda b, i: (b, i, 0)),
                  _mod_spec(d, batched), _const_spec((1, d)), _const_spec((1, d))],
        out_specs=pl.BlockSpec((1, ts, d), lambda b, i: (b, i, 0)),
        out_shape=jax.ShapeDtypeStruct((bsz, s, d), F32),
        compiler_params=_cparams(2),
    )(yg, gw, x, g2, lng.reshape(1, d), lnb.reshape(1, d))


def kernel(x, c, ctx, c_ctx, mod_w, mod_b, ln1_g, ln1_b, ln2_g, ln2_b, even_w_in, even_w_out, na_rpb, gqa_q_gain, gqa_k_gain, odd_w_in, odd_w_out, diff_lq1, diff_lk1, diff_lq2, diff_lk2, diff_subln, mla_q_gain, mla_w_uq, mla_kv_gain, mla_w_ukv, router_w, router_b, exp_w_gu, exp_b_gu, exp_w_down, exp_b_down):
    bsz, s, d = x.shape
    l = ctx.shape[1]
    depth = mod_w.shape[0]
    n_experts = router_w.shape[-1]
    alpha = (2 * depth) ** 0.25
    ts = min(512, s)
    tm = MOE_TILE

    pad = (-(bsz + 1)) % 8
    c_all = jnp.concatenate([c, c_ctx[None, :], jnp.zeros((pad, d), F32)], axis=0)
    mod = _modulation_all(c_all, mod_w, mod_b)

    for i in range(depth):
        last = i == depth - 1
        j = i // 2
        ml = [mod[i, :bsz, k * d:(k + 1) * d].reshape(bsz, 1, d) for k in range(6)]
        mc = [mod[i, bsz:bsz + 1, k * d:(k + 1) * d].reshape(1, 1, d) for k in range(6)]
        sh1, sc1, g1, sh2, sc2, g2 = ml
        csh1, csc1, cg1, csh2, csc2, cg2 = mc

        if i % 2 == 0:
            w = _even_weights(even_w_in[j])
            naq, nak, nav, gq, gk, gv = _inproj_even(
                x, sc1, sh1, w, _even_tables(s, gqa_q_gain[j], gqa_k_gain[j], True), ts)
            cnaq, cnak, cnav, cgq, cgk, cgv = _inproj_even(
                ctx, csc1, csh1, w, _even_tables(l, gqa_q_gain[j], gqa_k_gain[j], False), l)
            o1 = _na_attention(naq, nak, nav, cnak, cnav, na_rpb[j])
            o2 = _gqa_attention(gq, gk, gv, cgk, cgv, 256, ATTN_SUB)
            if not last:
                co1 = _slot_attention(cnaq, None, None, cnak, cnav, NA_HEADS, False, l)
                co2 = _gqa_attention(cgq, None, None, cgk, cgv, l)
            w_out = even_w_out[j]
        else:
            lam_init = 0.8 - 0.6 * math.exp(-0.3 * i)
            weights = _odd_weights(odd_w_in[j], mla_w_uq[j], mla_w_ukv[j])
            gains = (mla_q_gain[j], mla_kv_gain[j])
            dq, dk, dv, mq, mk, mv = _inproj_odd(x, sc1, sh1, weights, gains, _odd_tables(s, True), ts)
            cdq, cdk, cdv, cmq, cmk, cmv = _inproj_odd(ctx, csc1, csh1, weights, gains,
                                                       _odd_tables(l, False), l)
            lams = (diff_lq1[j], diff_lk1[j], diff_lq2[j], diff_lk2[j])
            o1 = _diff_attention(dq, dk, dv, cdk, cdv, lams, diff_subln[j], lam_init, 256, ATTN_SUB)
            o2 = _slot_attention(mq, mk, mv, cmk, cmv, MLA_HEADS, True, 512, 2)
            if not last:
                co1 = _diff_attention(cdq, None, None, cdk, cdv, lams, diff_subln[j], lam_init, l)
                co2 = _slot_attention(cmq, None, None, cmk, cmv, MLA_HEADS, True, l)
            w_out = odd_w_out[j]

        router = _router_weights(router_w[i], router_b[i])
        cnt0 = jnp.zeros((1, LANES), F32)
        x, h2a, h2b, ridx, rrank, rgw, cnt = _outproj_ln_router(
            o1, o2, w_out, x, g1, ln1_g[i], ln1_b[i], sc2, sh2, router, cnt0, alpha, ts)
        flat = lambda a: a.reshape(-1, a.shape[-1])
        routed = [[flat(a) for a in (h2a, h2b, ridx, rrank, rgw)]]
        if not last:
            ctx, ch2a, ch2b, cidx, crank, cgw, cnt = _outproj_ln_router(
                co1, co2, w_out, ctx, cg1, ln1_g[i], ln1_b[i], csc2, csh2, router, cnt, alpha, l)
            routed.append([flat(a) for a in (ch2a, ch2b, cidx, crank, cgw)])

        counts = cnt[0, :n_experts].astype(jnp.int32)
        t_total = sum(r[0].shape[0] for r in routed)
        pstart, p, tile_expert, tile_valid = _moe_layout(counts, t_total, tm)
        dpos = [(pstart[r[2][:, :TOP_K]] + r[3][:, :TOP_K]).T.astype(jnp.int32) for r in routed]
        xsa = _sc_dispatch([(r[0], dp) for r, dp in zip(routed, dpos)], p)
        xsb = _sc_dispatch([(r[1], dp) for r, dp in zip(routed, dpos)], p)
        ys = _expert_ffn(xsa, xsb, tile_expert, tile_valid, i, exp_w_gu, exp_b_gu[i],
                         exp_w_down, exp_b_down[i], tm)
        dpos_t = dpos[0] if last else jnp.concatenate(dpos, axis=1)
        yg = ys.at[dpos_t].get(mode="promise_in_bounds")

        x = _combine_ln(yg, routed[0][4], x, g2, ln2_g[i], ln2_b[i], alpha, 0, ts)
        if not last:
            ctx = _combine_ln(yg, routed[1][4], ctx, cg2, ln2_g[i], ln2_b[i], alpha, bsz * s, l)
    return x
```

```python
import functools
import math

import numpy as np
import jax
import jax.numpy as jnp
from jax import lax
from jax.experimental import pallas as pl
from jax.experimental.pallas import tpu as pltpu
from jax.experimental.pallas import tpu_sc as plsc

F32 = jnp.float32
BF16 = jnp.bfloat16

GRID_W = 64
HEAD_DIM = 64
ROPE_THETA = 10000.0
LN_EPS = 1e-6
RMS_EPS = 1e-6
NEG_INF = -1e30
NA_HEADS = 8
NA_WIN_H = 8
NA_WIN_W = 16
GQA_HEADS = 8
GQA_KV_HEADS = 2
DIFF_HEADS = 4
MLA_HEADS = 8
MLA_Q_RANK = 256
MLA_KV_RANK = 128
MLA_NOPE = 64
MLA_ROPE = 32
MLA_V = 64
MLA_QK = MLA_NOPE + MLA_ROPE
TOP_K = 4
SWIGLU_ALPHA = 1.702
SWIGLU_LIMIT = 7.0
LOG2E = 1.4426950408889634

LANES = 128
VMEM_LIMIT = 56 * 1024 * 1024
MOE_TILE = 1024
GU_BLOCK = 512
ATTN_SUB = 2


def _cparams(n_axes):
    return pltpu.CompilerParams(dimension_semantics=("arbitrary",) * n_axes,
                                vmem_limit_bytes=VMEM_LIMIT)


def _dot(a, b):
    return jnp.dot(a, b, preferred_element_type=F32)


def _dot_nt(a, b):
    return lax.dot_general(a, b, (((1,), (1,)), ((), ())), preferred_element_type=F32)


def _mod_kernel(c_ref, w_ref, b_ref, o_ref):
    cv = c_ref[...]
    a = (cv * jax.nn.sigmoid(cv)).astype(BF16)
    o_ref[0] = _dot(a, w_ref[0].astype(BF16)) + b_ref[0]


def _modulation_all(c_all, mod_w, mod_b):
    depth, d, n = mod_w.shape
    rows = c_all.shape[0]
    tn = 1536
    return pl.pallas_call(
        _mod_kernel,
        grid=(depth, n // tn),
        in_specs=[pl.BlockSpec((rows, d), lambda i, j: (0, 0)),
                  pl.BlockSpec((1, d, tn), lambda i, j: (i, 0, j)),
                  pl.BlockSpec((1, 1, tn), lambda i, j: (i, 0, j))],
        out_specs=pl.BlockSpec((1, rows, tn), lambda i, j: (i, 0, j)),
        out_shape=jax.ShapeDtypeStruct((depth, rows, n), F32),
        compiler_params=_cparams(2),
    )(c_all, mod_w, mod_b.reshape(depth, 1, n))


def _rope_tables(s, dim):
    pos = jnp.arange(s)
    row = (pos // GRID_W).astype(F32)[:, None]
    col = (pos % GRID_W).astype(F32)[:, None]
    quarter = dim // 4
    inv_freq = ROPE_THETA ** (-jnp.arange(quarter, dtype=F32) / quarter)
    ar, ac = row * inv_freq, col * inv_freq
    cos = jnp.concatenate([jnp.cos(ar), jnp.cos(ar), jnp.cos(ac), jnp.cos(ac)], axis=-1)
    sin = jnp.concatenate([-jnp.sin(ar), jnp.sin(ar), -jnp.sin(ac), jnp.sin(ac)], axis=-1)
    return cos, sin


def _swap_perm(dim):
    q = dim // 4
    idx = np.arange(dim)
    return np.where((idx % (2 * q)) < q, idx + q, idx - q)


def _rope_partner(y, dim):
    q = dim // 4
    lane = lax.broadcasted_iota(jnp.int32, (1, LANES), 1)
    first = (lane % (2 * q)) < q
    tiles = []
    for j in range(0, y.shape[1], LANES):
        t = y[:, j:j + LANES]
        tiles.append(jnp.where(first, pltpu.roll(t, LANES - q, 1), pltpu.roll(t, q, 1)))
    return tiles[0] if len(tiles) == 1 else jnp.concatenate(tiles, axis=-1)


def _swap_cols(w, dim):
    n = w.shape[-1] // dim
    perm = (np.arange(n)[:, None] * dim + _swap_perm(dim)[None, :]).reshape(-1)
    return w[..., perm]


def _block_ones(n, blk):
    i = np.arange(n) // blk
    return jnp.asarray((i[:, None] == i[None, :]).astype(np.float32), dtype=BF16)


E_NAQ, E_NAK, E_NAV, E_GQ, E_GK, E_GV, E_END = (0, 512, 1024, 1536, 2048, 2304, 2560)


def _even_weights(w_in):
    naq, nak, nav, gq, gk, gv = jnp.split(w_in, [512, 1024, 1536, 2048, 2176], axis=-1)
    k0, k1 = gk[:, :64], gk[:, 64:]
    v0, v1 = gv[:, :64], gv[:, 64:]
    gk2 = jnp.concatenate([k0, k1, k1, k0], axis=-1)
    gv2 = jnp.concatenate([v0, v1, v1, v0], axis=-1)
    w = jnp.concatenate([naq * (0.125 * LOG2E), nak, nav, gq, gk2, gv2], axis=-1).astype(BF16)
    return w


def _even_tables(s, q_gain, k_gain, rope):
    sw = _swap_perm(64)
    if rope:
        cos, sin = _rope_tables(s, 64)
    else:
        cos, sin = jnp.ones((s, 64), F32), jnp.zeros((s, 64), F32)
    qa = jnp.tile(cos * q_gain[None, :] * (0.125 * LOG2E), (1, 8))
    qb = jnp.tile(sin * q_gain[sw][None, :] * (0.125 * LOG2E), (1, 8))
    ka = jnp.tile(cos * k_gain[None, :], (1, 4))
    kb = jnp.tile(sin * k_gain[sw][None, :], (1, 4))
    return qa, qb, ka, kb


def _inproj_even_kernel(x_ref, sc_ref, sh_ref, w_ref, g512_ref, g256_ref,
                        qa_ref, qb_ref, ka_ref, kb_ref,
                        naq_ref, nak_ref, nav_ref, gq_ref, gk_ref, gv_ref):
    h = (x_ref[0] * (1.0 + sc_ref[0]) + sh_ref[0]).astype(BF16)
    naq_ref[0] = _dot(h, w_ref[:, E_NAQ:E_NAK]).astype(BF16)
    nak_ref[0] = _dot(h, w_ref[:, E_NAK:E_NAV]).astype(BF16)
    nav_ref[0] = _dot(h, w_ref[:, E_NAV:E_GQ]).astype(BF16)
    gv_ref[0] = _dot(h, w_ref[:, E_GV:E_END]).astype(BF16)
    y = _dot(h, w_ref[:, E_GQ:E_GK])
    ys = _rope_partner(y, HEAD_DIM)
    r = lax.rsqrt(_dot((y * y).astype(BF16), g512_ref[...]) * (1.0 / HEAD_DIM) + RMS_EPS)
    gq_ref[0] = (r * (y * qa_ref[...] + ys * qb_ref[...])).astype(BF16)
    y = _dot(h, w_ref[:, E_GK:E_GV])
    ys = _rope_partner(y, HEAD_DIM)
    r = lax.rsqrt(_dot((y * y).astype(BF16), g256_ref[...]) * (1.0 / HEAD_DIM) + RMS_EPS)
    gk_ref[0] = (r * (y * ka_ref[...] + ys * kb_ref[...])).astype(BF16)


def _mod_spec(d, batched):
    if batched:
        return pl.BlockSpec((1, 1, d), lambda b, s: (b, 0, 0))
    return pl.BlockSpec((1, 1, d), lambda b, s: (0, 0, 0))


def _const_spec(shape):
    nd = len(shape)
    return pl.BlockSpec(shape, lambda b, s: (0,) * nd)


def _inproj_even(x, sc, sh, w, tables, ts):
    bsz, s, d = x.shape
    batched = sc.shape[0] > 1
    qa, qb, ka, kb = tables
    tok = lambda n: pl.BlockSpec((1, ts, n), lambda b, i: (b, i, 0))
    tab = lambda n: pl.BlockSpec((ts, n), lambda b, i: (i, 0))
    widths = (512, 512, 512, 512, 256, 256)
    return pl.pallas_call(
        _inproj_even_kernel,
        grid=(bsz, s // ts),
        in_specs=[tok(d), _mod_spec(d, batched), _mod_spec(d, batched), _const_spec(w.shape),
                  _const_spec((512, 512)), _const_spec((256, 256)),
                  tab(512), tab(512), tab(256), tab(256)],
        out_specs=[tok(n) for n in widths],
        out_shape=[jax.ShapeDtypeStruct((bsz, s, n), BF16) for n in widths],
        compiler_params=_cparams(2),
    )(x, sc, sh, w, _block_ones(512, 64), _block_ones(256, 64), qa, qb, ka, kb)


O_DQ, O_DK, O_DV, O_CQ, O_CKV, O_PE, O_END = (0, 512, 1024, 1536, 1792, 1920, 2048)


def _odd_weights(w_in, w_uq, w_ukv):
    dq, dk, dv, cq, ckv, kpe = jnp.split(w_in, [512, 1024, 1536, 1792, 1920], axis=-1)
    d = w_in.shape[0]
    pe_slot = jnp.concatenate([kpe, _swap_cols(kpe, 32), jnp.zeros((d, 64), F32)], axis=-1)
    w = jnp.concatenate([dq, dk, dv, cq, ckv, pe_slot], axis=-1).astype(BF16)
    uq = w_uq.reshape(MLA_Q_RANK, MLA_HEADS, MLA_QK)
    z32 = jnp.zeros((MLA_Q_RANK, MLA_HEADS, 32), F32)
    z64 = jnp.zeros((MLA_Q_RANK, MLA_HEADS, 64), F32)
    uq_pad = jnp.concatenate([uq, z32], axis=-1).reshape(MLA_Q_RANK, MLA_HEADS * LANES)
    uq_sw = jnp.concatenate([z64, _swap_cols(uq[..., MLA_NOPE:], 32), z32],
                            axis=-1).reshape(MLA_Q_RANK, MLA_HEADS * LANES)
    wuq2 = jnp.concatenate([uq_pad, uq_sw], axis=-1).astype(BF16)
    ukv = w_ukv.reshape(MLA_KV_RANK, MLA_HEADS, MLA_NOPE + MLA_V)
    zk = jnp.zeros((MLA_KV_RANK, MLA_HEADS, 64), F32)
    wk_pad = jnp.concatenate([ukv[..., :MLA_NOPE], zk], axis=-1).reshape(MLA_KV_RANK, MLA_HEADS * LANES)
    place = np.zeros((LANES, MLA_HEADS, LANES), np.float32)
    for j in range(MLA_ROPE):
        place[j, :, MLA_NOPE + j] = 1.0
        place[MLA_ROPE + j, :, MLA_NOPE + j] = 1.0
    wk2 = jnp.concatenate([wk_pad, jnp.asarray(place.reshape(LANES, MLA_HEADS * LANES))],
                          axis=0).astype(BF16)
    wv = ukv[..., MLA_NOPE:].reshape(MLA_KV_RANK, MLA_HEADS * MLA_V).astype(BF16)
    return w, wuq2, wk2, wv


def _odd_tables(s, rope):
    m_scale = MLA_QK ** -0.5 * LOG2E
    if rope:
        cos64, sin64 = _rope_tables(s, 64)
        cos32, sin32 = _rope_tables(s, 32)
    else:
        cos64, sin64 = jnp.ones((s, 64), F32), jnp.zeros((s, 64), F32)
        cos32, sin32 = jnp.ones((s, 32), F32), jnp.zeros((s, 32), F32)
    one64, z32, z64 = jnp.ones((s, 64), F32), jnp.zeros((s, 32), F32), jnp.zeros((s, 64), F32)
    dcos, dsin = jnp.tile(cos64, (1, 8)), jnp.tile(sin64, (1, 8))
    qa = jnp.tile(jnp.concatenate([one64, cos32, z32], axis=-1) * m_scale, (1, MLA_HEADS))
    qb = jnp.tile(jnp.concatenate([z64, sin32, z32], axis=-1) * m_scale, (1, MLA_HEADS))
    pe = jnp.concatenate([cos32, sin32, z64], axis=-1)
    return dcos, dsin, qa, qb, pe


def _inproj_odd_kernel(x_ref, sc_ref, sh_ref, w_ref, wuq_ref, wk_ref, wv_ref, qg_ref, kvg_ref,
                       dcos_ref, dsin_ref, qa_ref, qb_ref, pe_ref,
                       dq_ref, dk_ref, dv_ref, mq_ref, mk_ref, mv_ref):
    h = (x_ref[0] * (1.0 + sc_ref[0]) + sh_ref[0]).astype(BF16)
    dcos, dsin = dcos_ref[...], dsin_ref[...]
    y = _dot(h, w_ref[:, O_DQ:O_DK])
    ys = _rope_partner(y, HEAD_DIM)
    dq_ref[0] = ((y * dcos + ys * dsin) * (0.125 * LOG2E)).astype(BF16)
    y = _dot(h, w_ref[:, O_DK:O_DV])
    ys = _rope_partner(y, HEAD_DIM)
    dk_ref[0] = (y * dcos + ys * dsin).astype(BF16)
    dv_ref[0] = _dot(h, w_ref[:, O_DV:O_CQ]).astype(BF16)
    cq = _dot(h, w_ref[:, O_CQ:O_CKV])
    nq = cq * lax.rsqrt(jnp.mean(cq * cq, axis=-1, keepdims=True) + RMS_EPS) * qg_ref[...]
    y2 = _dot(nq.astype(BF16), wuq_ref[...])
    half = MLA_HEADS * LANES
    mq_ref[0] = (y2[:, :half] * qa_ref[...] + y2[:, half:] * qb_ref[...]).astype(BF16)
    ckv = _dot(h, w_ref[:, O_CKV:O_PE])
    nk = ckv * lax.rsqrt(jnp.mean(ckv * ckv, axis=-1, keepdims=True) + RMS_EPS) * kvg_ref[...]
    pe = _dot(h, w_ref[:, O_PE:O_END]) * pe_ref[...]
    nkb = nk.astype(BF16)
    cat = jnp.concatenate([nkb, pe.astype(BF16)], axis=-1)
    mk_ref[0] = _dot(cat, wk_ref[...]).astype(BF16)
    mv_ref[0] = _dot(nkb, wv_ref[...]).astype(BF16)


def _inproj_odd(x, sc, sh, weights, gains, tables, ts):
    bsz, s, d = x.shape
    batched = sc.shape[0] > 1
    w, wuq2, wk2, wv = weights
    qg, kvg = gains
    dcos, dsin, qa, qb, pe = tables
    tok = lambda n: pl.BlockSpec((1, ts, n), lambda b, i: (b, i, 0))
    tab = lambda n: pl.BlockSpec((ts, n), lambda b, i: (i, 0))
    widths = (512, 512, 512, 1024, 1024, 512)
    return pl.pallas_call(
        _inproj_odd_kernel,
        grid=(bsz, s // ts),
        in_specs=[tok(d), _mod_spec(d, batched), _mod_spec(d, batched), _const_spec(w.shape),
                  _const_spec(wuq2.shape), _const_spec(wk2.shape), _const_spec(wv.shape),
                  _const_spec((1, MLA_Q_RANK)), _const_spec((1, MLA_KV_RANK)),
                  tab(512), tab(512), tab(1024), tab(1024), tab(128)],
        out_specs=[tok(n) for n in widths],
        out_shape=[jax.ShapeDtypeStruct((bsz, s, n), BF16) for n in widths],
        compiler_params=_cparams(2),
    )(x, sc, sh, w, wuq2, wk2, wv, qg.reshape(1, -1), kvg.reshape(1, -1), dcos, dsin, qa, qb, pe)


def _half_masks():
    lane = lax.broadcasted_iota(jnp.int32, (1, LANES), 1)
    return lane < HEAD_DIM, lane >= HEAD_DIM


def _softmax_values(ss, vs):
    m = functools.reduce(jnp.maximum, [jnp.max(s, axis=-1, keepdims=True) for s in ss])
    es = [jnp.exp2(s - m) for s in ss]
    l = functools.reduce(lambda a, b: a + b, [jnp.sum(e, axis=-1, keepdims=True) for e in es])
    o = functools.reduce(lambda a, b: a + b, [_dot(e.astype(BF16), v) for e, v in zip(es, vs)])
    return o * (1.0 / l)


def _attend(qm, ks, vs):
    return _softmax_values([_dot_nt(qm, k) for k in ks], vs)


NA_RB = 4
NA_WIN_ROWS = NA_RB + NA_WIN_H - 1
NA_SUB = 2


def _na_block_plan(rows):
    plan = []
    for blk in range(rows // NA_RB):
        r0 = blk * NA_RB
        rs = [int(np.clip(r0 + i - NA_WIN_H // 2, 0, rows - NA_WIN_H)) for i in range(NA_RB)]
        ws = int(np.clip(r0 - NA_WIN_H // 2, 0, rows - NA_WIN_ROWS))
        pat = tuple((rs[i] - ws, r0 + i - ws) for i in range(NA_RB))
        assert all(0 <= o and o + NA_WIN_H <= NA_WIN_ROWS for o, _ in pat)
        plan.append(pat)
    assert all(p == plan[1] for p in plan[1:-1])
    return (plan[0], plan[1], plan[-1])


def _na_bias_table(rpb, rows):
    cols = np.arange(GRID_W)
    col_start = np.clip(cols - NA_WIN_W // 2, 0, GRID_W - NA_WIN_W)
    col_mask = (cols[None, :] >= col_start[:, None]) & (cols[None, :] < col_start[:, None] + NA_WIN_W)
    col_idx = np.clip(cols[None, :] - cols[:, None] + NA_WIN_W - 1, 0, 2 * NA_WIN_W - 2)
    wr = np.arange(NA_WIN_ROWS)
    tables = []
    for pat in _na_block_plan(rows):
        off = np.array([o for o, _ in pat])[:, None]
        rq = np.array([r for _, r in pat])[:, None]
        row_ok = (wr[None, :] >= off) & (wr[None, :] < off + NA_WIN_H)
        ridx = np.clip(wr[None, :] - rq + NA_WIN_H - 1, 0, 2 * NA_WIN_H - 2)
        t = rpb.astype(F32)[:, ridx]
        t = t[..., col_idx]
        t = t.transpose(0, 1, 3, 2, 4)
        ok = row_ok[None, :, None, :, None] & col_mask[None, None, :, None, :]
        t = jnp.where(ok, t * LOG2E, NEG_INF)
        tables.append(t.reshape(NA_HEADS, NA_RB * GRID_W, NA_WIN_ROWS * GRID_W))
    return jnp.stack(tables)


def _na_kernel(rows, sub, q_ref, k_ref, v_ref, kc_ref, vc_ref, *refs):
    bt_refs, o_ref = refs[:sub], refs[sub]
    step = pl.program_id(1)
    tq = NA_RB * GRID_W
    m0, m1 = _half_masks()
    for u in range(sub):
        r0 = (step * sub + u) * NA_RB
        ws = jnp.clip(r0 - NA_WIN_H // 2, 0, rows - NA_WIN_ROWS)
        win = pl.ds(pl.multiple_of(ws * GRID_W, GRID_W), NA_WIN_ROWS * GRID_W)
        qrows = slice(u * tq, (u + 1) * tq)
        for j in range(NA_HEADS // 2):
            sl = slice(j * LANES, (j + 1) * LANES)
            qp = q_ref[0, qrows, sl]
            kp, vp = k_ref[0, win, sl], v_ref[0, win, sl]
            kcp, vcp = kc_ref[0, :, sl], vc_ref[0, :, sl]
            outs = []
            for par, msk in ((0, m0), (1, m1)):
                qm = jnp.where(msk, qp, jnp.zeros_like(qp))
                s_loc = _dot_nt(qm, kp) + bt_refs[u][0, 2 * j + par]
                outs.append(_softmax_values([s_loc, _dot_nt(qm, kcp)], [vp, vcp]))
            o_ref[0, qrows, sl] = jnp.where(m0, outs[0], outs[1]).astype(BF16)


def _na_attention(q, k, v, kc, vc, rpb):
    bsz, s, w = q.shape
    l = kc.shape[1]
    rows = s // GRID_W
    nblk = rows // NA_RB
    sub = NA_SUB
    nstep = nblk // sub
    assert rows % NA_RB == 0 and rows >= NA_WIN_ROWS and nblk % sub == 0 and nblk >= 3
    bt = _na_bias_table(rpb, rows)
    tq = NA_RB * GRID_W * sub
    full = lambda n: pl.BlockSpec((1, n, w), lambda b, r: (b, 0, 0))

    def bt_spec(u):
        def kind(b, r):
            blk = r * sub + u
            return ((blk > 0).astype(jnp.int32) + (blk == nblk - 1).astype(jnp.int32), 0, 0, 0)
        return pl.BlockSpec((1,) + bt.shape[1:], kind)

    return pl.pallas_call(
        functools.partial(_na_kernel, rows, sub),
        grid=(bsz, nstep),
        in_specs=[pl.BlockSpec((1, tq, w), lambda b, r: (b, r, 0)),
                  full(s), full(s), full(l), full(l)] + [bt_spec(u) for u in range(sub)],
        out_specs=pl.BlockSpec((1, tq, w), lambda b, r: (b, r, 0)),
        out_shape=jax.ShapeDtypeStruct((bsz, s, w), BF16),
        compiler_params=_cparams(2),
    )(q, k, v, kc, vc, *([bt] * sub))


def _slot_attn_kernel(n_heads, q_slot, has_lat, sub, *refs):
    if has_lat:
        q_ref, k_ref, v_ref, kc_ref, vc_ref, o_ref = refs
    else:
        q_ref, kc_ref, vc_ref, o_ref = refs
    m0, m1 = _half_masks()
    tq = q_ref.shape[1] // sub
    for u in range(sub):
        qrows = slice(u * tq, (u + 1) * tq)
        for j in range(n_heads // 2):
            vsl = slice(j * LANES, (j + 1) * LANES)
            outs = []
            for par, msk in ((0, m0), (1, m1)):
                h = 2 * j + par
                if q_slot:
                    ksl = slice(h * LANES, (h + 1) * LANES)
                    qm = q_ref[0, qrows, ksl]
                else:
                    ksl = vsl
                    qp = q_ref[0, qrows, vsl]
                    qm = jnp.where(msk, qp, jnp.zeros_like(qp))
                ks, vs = [kc_ref[0, :, ksl]], [vc_ref[0, :, vsl]]
                if has_lat:
                    ks.insert(0, k_ref[0, :, ksl])
                    vs.insert(0, v_ref[0, :, vsl])
                outs.append(_attend(qm, ks, vs))
            o_ref[0, qrows, vsl] = jnp.where(m0, outs[0], outs[1]).astype(BF16)


def _slot_attention(q, k, v, kc, vc, n_heads, q_slot, tq, sub=1):
    bsz, sq, wq = q.shape
    l, wk, wv = kc.shape[1], kc.shape[2], vc.shape[2]
    has_lat = k is not None
    qspec = pl.BlockSpec((1, tq, wq), lambda b, i: (b, i, 0))
    full = lambda n, w: pl.BlockSpec((1, n, w), lambda b, i: (b, 0, 0))
    in_specs, args = [qspec], [q]
    if has_lat:
        s = k.shape[1]
        in_specs += [full(s, wk), full(s, wv)]
        args += [k, v]
    in_specs += [full(l, wk), full(l, wv)]
    args += [kc, vc]
    return pl.pallas_call(
        functools.partial(_slot_attn_kernel, n_heads, q_slot, has_lat, sub),
        grid=(bsz, sq // tq),
        in_specs=in_specs,
        out_specs=pl.BlockSpec((1, tq, wv), lambda b, i: (b, i, 0)),
        out_shape=jax.ShapeDtypeStruct((bsz, sq, wv), BF16),
        compiler_params=_cparams(2),
    )(*args)


def _gqa_kernel(has_lat, tq, sub, *refs):
    if has_lat:
        q_ref, k_ref, v_ref, kc_ref, vc_ref, o_ref = refs
    else:
        q_ref, kc_ref, vc_ref, o_ref = refs
    masks = _half_masks()
    for u in range(sub):
        qrows = slice(u * tq, (u + 1) * tq)
        res = {}
        for g in range(GQA_KV_HEADS):
            for var in range(2):
                par = g if var == 0 else 1 - g
                heads = (4 * g + par, 4 * g + 2 + par)
                vsl = slice(var * LANES, (var + 1) * LANES)
                qs = []
                for h in heads:
                    qp = q_ref[0, qrows, (h // 2) * LANES:(h // 2 + 1) * LANES]
                    qs.append(jnp.where(masks[par], qp, jnp.zeros_like(qp)))
                qm = jnp.concatenate(qs, axis=0)
                ks, vs = [kc_ref[0, :, vsl]], [vc_ref[0, :, vsl]]
                if has_lat:
                    ks.insert(0, k_ref[0, :, vsl])
                    vs.insert(0, v_ref[0, :, vsl])
                o = _attend(qm, ks, vs)
                res[heads[0]] = o[:tq]
                res[heads[1]] = o[tq:]
        for j in range(GQA_HEADS // 2):
            o_ref[0, qrows, j * LANES:(j + 1) * LANES] = jnp.where(
                masks[0], res[2 * j], res[2 * j + 1]).astype(BF16)


def _gqa_attention(q, k2, v2, k2c, v2c, tq, sub=1):
    bsz, sq, wq = q.shape
    l = k2c.shape[1]
    has_lat = k2 is not None
    full = lambda n: pl.BlockSpec((1, n, 2 * LANES), lambda b, i: (b, 0, 0))
    in_specs, args = [pl.BlockSpec((1, tq * sub, wq), lambda b, i: (b, i, 0))], [q]
    if has_lat:
        in_specs += [full(k2.shape[1])] * 2
        args += [k2, v2]
    in_specs += [full(l)] * 2
    args += [k2c, v2c]
    return pl.pallas_call(
        functools.partial(_gqa_kernel, has_lat, tq, sub),
        grid=(bsz, sq // (tq * sub)),
        in_specs=in_specs,
        out_specs=pl.BlockSpec((1, tq * sub, wq), lambda b, i: (b, i, 0)),
        out_shape=jax.ShapeDtypeStruct((bsz, sq, wq), BF16),
        compiler_params=_cparams(2),
    )(*args)


def _diff_kernel(has_lat, tq, sub, lam_init, *refs):
    if has_lat:
        q_ref, k_ref, v_ref, kc_ref, vc_ref, lq1, lk1, lq2, lk2, sub_ref, o_ref = refs
    else:
        q_ref, kc_ref, vc_ref, lq1, lk1, lq2, lk2, sub_ref, o_ref = refs
    lam = (jnp.exp(jnp.sum(lq1[...] * lk1[...], axis=-1, keepdims=True))
           - jnp.exp(jnp.sum(lq2[...] * lk2[...], axis=-1, keepdims=True)) + lam_init)
    m0, m1 = _half_masks()
    for u in range(sub):
        qrows = slice(u * tq, (u + 1) * tq)
        for h in range(DIFF_HEADS):
            sl = slice(h * LANES, (h + 1) * LANES)
            qp = q_ref[0, qrows, sl]
            zero = jnp.zeros_like(qp)
            qm = jnp.concatenate([jnp.where(m0, qp, zero), jnp.where(m1, qp, zero)], axis=0)
            ks, vs = [kc_ref[0, :, sl]], [vc_ref[0, :, sl]]
            if has_lat:
                ks.insert(0, k_ref[0, :, sl])
                vs.insert(0, v_ref[0, :, sl])
            o2 = _attend(qm, ks, vs)
            o = o2[:tq] - lam * o2[tq:]
            o = o * lax.rsqrt(jnp.mean(o * o, axis=-1, keepdims=True) + RMS_EPS) * sub_ref[...]
            o_ref[0, qrows, sl] = (o * (1.0 - lam_init)).astype(BF16)


def _diff_attention(q, k, v, kc, vc, lams, subln, lam_init, tq, sub=1):
    bsz, sq, w = q.shape
    l = kc.shape[1]
    has_lat = k is not None
    full = lambda n: pl.BlockSpec((1, n, w), lambda b, i: (b, 0, 0))
    in_specs, args = [pl.BlockSpec((1, tq * sub, w), lambda b, i: (b, i, 0))], [q]
    if has_lat:
        in_specs += [full(k.shape[1])] * 2
        args += [k, v]
    in_specs += [full(l)] * 2 + [_const_spec((1, HEAD_DIM))] * 4 + [_const_spec((1, LANES))]
    args += [kc, vc] + [a.reshape(1, -1).astype(F32) for a in lams] + [subln.reshape(1, -1).astype(F32)]
    return pl.pallas_call(
        functools.partial(_diff_kernel, has_lat, tq, sub, lam_init),
        grid=(bsz, sq // (tq * sub)),
        in_specs=in_specs,
        out_specs=pl.BlockSpec((1, tq * sub, w), lambda b, i: (b, i, 0)),
        out_shape=jax.ShapeDtypeStruct((bsz, sq, w), BF16),
        compiler_params=_cparams(2),
    )(*args)


def _pack_bf16_pairs(v):
    w = v.shape[1] // 2
    hi = pltpu.bitcast(v[:, :w].astype(F32), jnp.int32)
    lo = pltpu.bitcast(v[:, w:].astype(F32), jnp.int32)
    return hi | lax.shift_right_logical(lo, 16)


def _unpack_bf16_pairs(u):
    hi = pltpu.bitcast(u & jnp.int32(-65536), F32)
    lo = pltpu.bitcast(lax.shift_left(u, 16), F32)
    return hi, lo


def _layer_norm(z, g, b):
    mu = jnp.mean(z, axis=-1, keepdims=True)
    zc = z - mu
    var = jnp.mean(zc * zc, axis=-1, keepdims=True)
    return zc * lax.rsqrt(var + LN_EPS) * g + b


def _outproj_kernel(alpha, sub, o1_ref, o2_ref, w1_ref, w2_ref, x_ref, g1_ref, lng_ref, lnb_ref,
                    sc2_ref, sh2_ref, rwh_ref, rwl_ref, rb_ref, tri_ref, cnt0_ref,
                    xo_ref, h2a_ref, h2b_ref, idx_ref, rank_ref, gw_ref, cnt_ref, carry_ref):
    @pl.when(jnp.logical_and(pl.program_id(0) == 0, pl.program_id(1) == 0))
    def _():
        carry_ref[...] = cnt0_ref[...]

    tr = x_ref.shape[1] // sub
    carry = carry_ref[...]
    for u in range(sub):
        rows = slice(u * tr, (u + 1) * tr)
        o = _dot(o1_ref[0, rows], w1_ref[...]) + _dot(o2_ref[0, rows], w2_ref[...])
        xn = _layer_norm(alpha * x_ref[0, rows] + g1_ref[0] * o, lng_ref[...], lnb_ref[...])
        xo_ref[0, rows] = xn
        h2 = xn * (1.0 + sc2_ref[0]) + sh2_ref[0]
        hi = h2.astype(BF16)
        packed = _pack_bf16_pairs(hi)
        q = packed.shape[1] // 2
        h2a_ref[0, rows] = packed[:, :q]
        h2b_ref[0, rows] = packed[:, q:]
        lo = (h2 - hi.astype(F32)).astype(BF16)
        logits = (_dot(hi, rwh_ref[...]) + _dot(lo, rwh_ref[...]) + _dot(hi, rwl_ref[...])) + rb_ref[...]
        lane = lax.broadcasted_iota(jnp.int32, logits.shape, 1).astype(F32)
        vals, idxs = [], []
        cur = logits
        for _ in range(TOP_K):
            m = jnp.max(cur, axis=-1, keepdims=True)
            ik = jnp.min(jnp.where(cur == m, lane, float(LANES)), axis=-1, keepdims=True)
            vals.append(m)
            idxs.append(ik)
            cur = jnp.where(lane == ik, -jnp.inf, cur)
        ws = [jnp.exp(v - vals[0]) for v in vals]
        inv = 1.0 / functools.reduce(lambda a, b: a + b, ws)
        sel = [lane == ik for ik in idxs]
        onehot = functools.reduce(lambda a, b: a + b, [m.astype(F32) for m in sel])
        before = _dot(tri_ref[...], onehot.astype(BF16)) + carry
        idx_out = jnp.zeros_like(logits)
        rank_out = jnp.zeros_like(logits)
        w_out = jnp.zeros_like(logits)
        for k in range(TOP_K):
            rk = jnp.sum(jnp.where(sel[k], before, 0.0), axis=-1, keepdims=True)
            idx_out = jnp.where(lane == float(k), idxs[k], idx_out)
            rank_out = jnp.where(lane == float(k), rk, rank_out)
            w_out = jnp.where(lane == float(k), ws[k] * inv, w_out)
        idx_ref[0, rows] = idx_out.astype(jnp.int32)
        rank_ref[0, rows] = rank_out.astype(jnp.int32)
        gw_ref[0, rows] = w_out
        carry = carry + jnp.sum(onehot, axis=0, keepdims=True)
    carry_ref[...] = carry
    cnt_ref[...] = carry


def _outproj_ln_router(o1, o2, w_out, x, g1, lng, lnb, sc2, sh2, router, cnt0, alpha, ts):
    bsz, s, d = x.shape
    batched = g1.shape[0] > 1
    rwh, rwl, rb = router
    w1, w2 = w_out[:512].astype(BF16), w_out[512:].astype(BF16)
    sub = 1
    tr = ts // sub
    tri = jnp.asarray(np.tril(np.ones((tr, tr), np.float32), -1), dtype=BF16)
    tok = lambda n: pl.BlockSpec((1, ts, n), lambda b, i: (b, i, 0))
    ms = _mod_spec(d, batched)
    return pl.pallas_call(
        functools.partial(_outproj_kernel, alpha, sub),
        grid=(bsz, s // ts),
        in_specs=[tok(512), tok(512), _const_spec((512, d)), _const_spec((512, d)), tok(d), ms,
                  _const_spec((1, d)), _const_spec((1, d)), ms, ms,
                  _const_spec((d, LANES)), _const_spec((d, LANES)), _const_spec((1, LANES)),
                  _const_spec((tr, tr)), _const_spec((1, LANES))],
        out_specs=[tok(d), tok(d // 4), tok(d // 4), tok(LANES), tok(LANES), tok(LANES),
                   _const_spec((1, LANES))],
        out_shape=[jax.ShapeDtypeStruct((bsz, s, d), F32),
                   jax.ShapeDtypeStruct((bsz, s, d // 4), jnp.int32),
                   jax.ShapeDtypeStruct((bsz, s, d // 4), jnp.int32),
                   jax.ShapeDtypeStruct((bsz, s, LANES), jnp.int32),
                   jax.ShapeDtypeStruct((bsz, s, LANES), jnp.int32),
                   jax.ShapeDtypeStruct((bsz, s, LANES), F32),
                   jax.ShapeDtypeStruct((1, LANES), F32)],
        scratch_shapes=[pltpu.VMEM((1, LANES), F32)],
        compiler_params=_cparams(2),
    )(o1, o2, w1, w2, x, g1, lng.reshape(1, d), lnb.reshape(1, d), sc2, sh2, rwh, rwl, rb, tri, cnt0)


def _router_weights(router_w, router_b):
    d, e = router_w.shape
    wp = jnp.zeros((d, LANES), F32).at[:, :e].set(router_w)
    hi = wp.astype(BF16)
    lo = (wp - hi.astype(F32)).astype(BF16)
    rb = jnp.full((1, LANES), -jnp.inf, F32).at[0, :e].set(router_b)
    return hi, lo, rb


def _deinterleave_perm():
    p = np.zeros((GU_BLOCK, GU_BLOCK), np.float32)
    m = np.arange(GU_BLOCK // 2)
    p[2 * m, m] = 1.0
    p[2 * m + 1, GU_BLOCK // 2 + m] = 1.0
    return jnp.asarray(p, dtype=BF16)


def _ffn_kernel(te_ref, tv_ref, xa_ref, xb_ref, wgu_ref, bgu_ref, wd_ref, bd_ref, perm_ref, y_ref,
                wgu_s, wd_s):
    j = pl.program_id(0)
    n_blk = wgu_s.shape[1] // GU_BLOCK
    half = GU_BLOCK // 2

    @pl.when(jnp.logical_or(j == 0, te_ref[j] != te_ref[jnp.maximum(j - 1, 0)]))
    def _():
        for b in range(n_blk):
            sl = slice(b * GU_BLOCK, (b + 1) * GU_BLOCK)
            wgu_s[:, sl] = _dot(wgu_ref[0, 0, :, sl].astype(BF16), perm_ref[...]).astype(BF16)
        wd_s[...] = wd_ref[0, 0].astype(BF16)

    @pl.when(tv_ref[j] > 0)
    def _():
        a_hi, a_lo = _unpack_bf16_pairs(xa_ref[...])
        b_hi, b_lo = _unpack_bf16_pairs(xb_ref[...])
        x = jnp.concatenate([a_hi, b_hi, a_lo, b_lo], axis=-1).astype(BF16)
        acts = []
        for b in range(n_blk):
            sl = slice(b * GU_BLOCK, (b + 1) * GU_BLOCK)
            gu = _dot(x, wgu_s[:, sl]) + bgu_ref[0, :, sl]
            glu = jnp.minimum(gu[:, :half], SWIGLU_LIMIT)
            lin = jnp.clip(gu[:, half:], -SWIGLU_LIMIT, SWIGLU_LIMIT)
            acts.append(((lin + 1.0) * (glu * jax.nn.sigmoid(SWIGLU_ALPHA * glu))).astype(BF16))
        a = jnp.concatenate(acts, axis=-1)
        y_ref[...] = (_dot(a, wd_s[...]) + bd_ref[0]).astype(BF16)

    @pl.when(tv_ref[j] == 0)
    def _():
        y_ref[...] = jnp.zeros_like(y_ref)


def _expert_ffn(xsa, xsb, tile_expert, tile_valid, layer, w_gu, b_gu, w_down, b_down, tm):
    p, q = xsa.shape
    d = 4 * q
    _, e, _, f2 = w_gu.shape
    f = f2 // 2
    half = GU_BLOCK // 2
    bgu = jnp.stack([b_gu[:, 0::2].reshape(e, f // half, half),
                     b_gu[:, 1::2].reshape(e, f // half, half)], axis=2).reshape(e, 1, f2)
    wspec = lambda a, b: pl.BlockSpec((1, 1, a, b), lambda j, te, tv: (layer, te[j], 0, 0))
    bspec = lambda b: pl.BlockSpec((1, 1, b), lambda j, te, tv: (te[j], 0, 0))
    return pl.pallas_call(
        _ffn_kernel,
        grid_spec=pltpu.PrefetchScalarGridSpec(
            num_scalar_prefetch=2,
            grid=(p // tm,),
            in_specs=[pl.BlockSpec((tm, q), lambda j, te, tv: (j, 0)),
                      pl.BlockSpec((tm, q), lambda j, te, tv: (j, 0)),
                      wspec(d, f2), bspec(f2), wspec(f, d), bspec(d),
                      pl.BlockSpec((GU_BLOCK, GU_BLOCK), lambda j, te, tv: (0, 0))],
            out_specs=pl.BlockSpec((tm, d), lambda j, te, tv: (j, 0)),
            scratch_shapes=[pltpu.VMEM((d, f2), BF16), pltpu.VMEM((f, d), BF16)]),
        out_shape=jax.ShapeDtypeStruct((p, d), BF16),
        compiler_params=_cparams(1),
    )(tile_expert, tile_valid, xsa, xsb, w_gu, bgu, w_down, b_down.reshape(e, 1, d),
      _deinterleave_perm())


def _moe_layout(counts, t_total, tm):
    n_experts = counts.shape[0]
    pc = ((counts + tm - 1) // tm) * tm
    pend = jnp.cumsum(pc)
    pstart = pend - pc
    p = t_total * TOP_K + n_experts * tm
    tile_start = jnp.arange(p // tm, dtype=jnp.int32) * tm
    tile_expert = jnp.minimum(jnp.sum((tile_start[:, None] >= pend[None, :]).astype(jnp.int32), axis=1),
                              n_experts - 1).astype(jnp.int32)
    tile_valid = (tile_start < pend[-1]).astype(jnp.int32)
    return pstart, p, tile_expert, tile_valid


SC_WINDOW = 128


def _sc_dispatch(parts, p):
    w = parts[0][0].shape[1]
    k = parts[0][1].shape[0]
    mesh = plsc.VectorSubcoreMesh(core_axis_name="core", subcore_axis_name="subcore")

    @functools.partial(pl.kernel, out_type=jax.ShapeDtypeStruct((p, w), parts[0][0].dtype), mesh=mesh)
    def kern(*refs):
        o_hbm = refs[-1]

        def body(x_vmem, *i_vmems):
            for iv in i_vmems:
                pltpu.sync_copy(x_vmem, o_hbm.at[iv.at[0]])

        for n, (rows, _) in enumerate(parts):
            x_hbm, i_hbms = refs[n * (k + 1)], refs[n * (k + 1) + 1:(n + 1) * (k + 1)]
            pltpu.emit_pipeline(
                body,
                grid=(rows.shape[0] // SC_WINDOW,),
                in_specs=[pl.BlockSpec((SC_WINDOW, w), index_map=lambda i: (i, 0))]
                + [pl.BlockSpec((1, SC_WINDOW), index_map=lambda i: (0, i)) for _ in range(k)],
                out_specs=[],
                core_axis_name=("core", "subcore"),
                dimension_semantics=(pltpu.PARALLEL,),
            )(x_hbm, *i_hbms)

    args = []
    for rows, dpos_t in parts:
        args += [rows] + [dpos_t[kk:kk + 1] for kk in range(k)]
    return kern(*args)


def _combine_kernel(alpha, yg_ref, gw_ref, x_ref, g2_ref, lng_ref, lnb_ref, o_ref):
    gw = gw_ref[...]
    y = yg_ref[0].astype(F32) * gw[:, 0:1]
    for k in range(1, TOP_K):
        y = y + yg_ref[k].astype(F32) * gw[:, k:k + 1]
    o_ref[0] = _layer_norm(alpha * x_ref[0] + g2_ref[0] * y, lng_ref[...], lnb_ref[...])


def _combine_ln(yg, gw, x, g2, lng, lnb, alpha, yg_row0, ts):
    bsz, s, d = x.shape
    batched = g2.shape[0] > 1
    nblk = s // ts
    off = yg_row0 // ts
    return pl.pallas_call(
        functools.partial(_combine_kernel, alpha),
        grid=(bsz, nblk),
        in_specs=[pl.BlockSpec((TOP_K, ts, d), lambda b, i: (0, off + b * nblk + i, 0)),
                  pl.BlockSpec((ts, LANES), lambda b, i: (b * nblk + i, 0)),
                  pl.BlockSpec((1, ts, d), lambda b, i: (b, i, 0)),
                  _mod_spec(d, batched), _const_spec((1, d)), _const_spec((1, d))],
        out_specs=pl.BlockSpec((1, ts, d), lambda b, i: (b, i, 0)),
        out_shape=jax.ShapeDtypeStruct((bsz, s, d), F32),
        compiler_params=_cparams(2),
    )(yg, gw, x, g2, lng.reshape(1, d), lnb.reshape(1, d))


def kernel(x, c, ctx, c_ctx, mod_w, mod_b, ln1_g, ln1_b, ln2_g, ln2_b, even_w_in, even_w_out, na_rpb, gqa_q_gain, gqa_k_gain, odd_w_in, odd_w_out, diff_lq1, diff_lk1, diff_lq2, diff_lk2, diff_subln, mla_q_gain, mla_w_uq, mla_kv_gain, mla_w_ukv, router_w, router_b, exp_w_gu, exp_b_gu, exp_w_down, exp_b_down):
    bsz, s, d = x.shape
    l = ctx.shape[1]
    depth = mod_w.shape[0]
    n_experts = router_w.shape[-1]
    alpha = (2 * depth) ** 0.25
    ts = min(512, s)
    tm = MOE_TILE

    pad = (-(bsz + 1)) % 8
    c_all = jnp.concatenate([c, c_ctx[None, :], jnp.zeros((pad, d), F32)], axis=0)
    mod = _modulation_all(c_all, mod_w, mod_b)

    for i in range(depth):
        last = i == depth - 1
        j = i // 2
        ml = [mod[i, :bsz, k * d:(k + 1) * d].reshape(bsz, 1, d) for k in range(6)]
        mc = [mod[i, bsz:bsz + 1, k * d:(k + 1) * d].reshape(1, 1, d) for k in range(6)]
        sh1, sc1, g1, sh2, sc2, g2 = ml
        csh1, csc1, cg1, csh2, csc2, cg2 = mc

        if i % 2 == 0:
            w = _even_weights(even_w_in[j])
            naq, nak, nav, gq, gk, gv = _inproj_even(
                x, sc1, sh1, w, _even_tables(s, gqa_q_gain[j], gqa_k_gain[j], True), ts)
            cnaq, cnak, cnav, cgq, cgk, cgv = _inproj_even(
                ctx, csc1, csh1, w, _even_tables(l, gqa_q_gain[j], gqa_k_gain[j], False), l)
            o1 = _na_attention(naq, nak, nav, cnak, cnav, na_rpb[j])
            o2 = _gqa_attention(gq, gk, gv, cgk, cgv, 256, ATTN_SUB)
            if not last:
                co1 = _slot_attention(cnaq, None, None, cnak, cnav, NA_HEADS, False, l)
                co2 = _gqa_attention(cgq, None, None, cgk, cgv, l)
            w_out = even_w_out[j]
        else:
            lam_init = 0.8 - 0.6 * math.exp(-0.3 * i)
            weights = _odd_weights(odd_w_in[j], mla_w_uq[j], mla_w_ukv[j])
            gains = (mla_q_gain[j], mla_kv_gain[j])
            dq, dk, dv, mq, mk, mv = _inproj_odd(x, sc1, sh1, weights, gains, _odd_tables(s, True), ts)
            cdq, cdk, cdv, cmq, cmk, cmv = _inproj_odd(ctx, csc1, csh1, weights, gains,
                                                       _odd_tables(l, False), l)
            lams = (diff_lq1[j], diff_lk1[j], diff_lq2[j], diff_lk2[j])
            o1 = _diff_attention(dq, dk, dv, cdk, cdv, lams, diff_subln[j], lam_init, 256, ATTN_SUB)
            o2 = _slot_attention(mq, mk, mv, cmk, cmv, MLA_HEADS, True, 512)
            if not last:
                co1 = _diff_attention(cdq, None, None, cdk, cdv, lams, diff_subln[j], lam_init, l)
                co2 = _slot_attention(cmq, None, None, cmk, cmv, MLA_HEADS, True, l)
            w_out = odd_w_out[j]

        router = _router_weights(router_w[i], router_b[i])
        cnt0 = jnp.zeros((1, LANES), F32)
        x, h2a, h2b, ridx, rrank, rgw, cnt = _outproj_ln_router(
            o1, o2, w_out, x, g1, ln1_g[i], ln1_b[i], sc2, sh2, router, cnt0, alpha, ts)
        flat = lambda a: a.reshape(-1, a.shape[-1])
        routed = [[flat(a) for a in (h2a, h2b, ridx, rrank, rgw)]]
        if not last:
            ctx, ch2a, ch2b, cidx, crank, cgw, cnt = _outproj_ln_router(
                co1, co2, w_out, ctx, cg1, ln1_g[i], ln1_b[i], csc2, csh2, router, cnt, alpha, l)
            routed.append([flat(a) for a in (ch2a, ch2b, cidx, crank, cgw)])

        counts = cnt[0, :n_experts].astype(jnp.int32)
        t_total = sum(r[0].shape[0] for r in routed)
        pstart, p, tile_expert, tile_valid = _moe_layout(counts, t_total, tm)
        dpos = [(pstart[r[2][:, :TOP_K]] + r[3][:, :TOP_K]).T.astype(jnp.int32) for r in routed]
        xsa = _sc_dispatch([(r[0], dp) for r, dp in zip(routed, dpos)], p)
        xsb = _sc_dispatch([(r[1], dp) for r, dp in zip(routed, dpos)], p)
        ys = _expert_ffn(xsa, xsb, tile_expert, tile_valid, i, exp_w_gu, exp_b_gu[i],
                         exp_w_down, exp_b_down[i], tm)
        dpos_t = dpos[0] if last else jnp.concatenate(dpos, axis=1)
        yg = ys.at[dpos_t].get(mode="promise_in_bounds")

        x = _combine_ln(yg, routed[0][4], x, g2, ln2_g[i], ln2_b[i], alpha, 0, ts)
        if not last:
            ctx = _combine_ln(yg, routed[1][4], ctx, cg2, ln2_g[i], ln2_b[i], alpha, bsz * s, l)
    return x
```

```python
import functools
import math

import numpy as np
import jax
import jax.numpy as jnp
from jax import lax
from jax.experimental import pallas as pl
from jax.experimental.pallas import tpu as pltpu
from jax.experimental.pallas import tpu_sc as plsc

F32 = jnp.float32
BF16 = jnp.bfloat16

GRID_W = 64
HEAD_DIM = 64
ROPE_THETA = 10000.0
LN_EPS = 1e-6
RMS_EPS = 1e-6
NEG_INF = -1e30
NA_HEADS = 8
NA_WIN_H = 8
NA_WIN_W = 16
GQA_HEADS = 8
GQA_KV_HEADS = 2
DIFF_HEADS = 4
MLA_HEADS = 8
MLA_Q_RANK = 256
MLA_KV_RANK = 128
MLA_NOPE = 64
MLA_ROPE = 32
MLA_V = 64
MLA_QK = MLA_NOPE + MLA_ROPE
TOP_K = 4
SWIGLU_ALPHA = 1.702
SWIGLU_LIMIT = 7.0
LOG2E = 1.4426950408889634

LANES = 128
VMEM_LIMIT = 56 * 1024 * 1024
MOE_TILE = 1024
GU_BLOCK = 512
ATTN_SUB = 2


def _cparams(n_axes):
    return pltpu.CompilerParams(dimension_semantics=("arbitrary",) * n_axes,
                                vmem_limit_bytes=VMEM_LIMIT)


def _dot(a, b):
    return jnp.dot(a, b, preferred_element_type=F32)


def _dot_nt(a, b):
    return lax.dot_general(a, b, (((1,), (1,)), ((), ())), preferred_element_type=F32)


def _mod_kernel(c_ref, w_ref, b_ref, o_ref):
    cv = c_ref[...]
    a = (cv * jax.nn.sigmoid(cv)).astype(BF16)
    o_ref[0] = _dot(a, w_ref[0].astype(BF16)) + b_ref[0]


def _modulation_all(c_all, mod_w, mod_b):
    depth, d, n = mod_w.shape
    rows = c_all.shape[0]
    tn = 1536
    return pl.pallas_call(
        _mod_kernel,
        grid=(depth, n // tn),
        in_specs=[pl.BlockSpec((rows, d), lambda i, j: (0, 0)),
                  pl.BlockSpec((1, d, tn), lambda i, j: (i, 0, j)),
                  pl.BlockSpec((1, 1, tn), lambda i, j: (i, 0, j))],
        out_specs=pl.BlockSpec((1, rows, tn), lambda i, j: (i, 0, j)),
        out_shape=jax.ShapeDtypeStruct((depth, rows, n), F32),
        compiler_params=_cparams(2),
    )(c_all, mod_w, mod_b.reshape(depth, 1, n))


def _rope_tables(s, dim):
    pos = jnp.arange(s)
    row = (pos // GRID_W).astype(F32)[:, None]
    col = (pos % GRID_W).astype(F32)[:, None]
    quarter = dim // 4
    inv_freq = ROPE_THETA ** (-jnp.arange(quarter, dtype=F32) / quarter)
    ar, ac = row * inv_freq, col * inv_freq
    cos = jnp.concatenate([jnp.cos(ar), jnp.cos(ar), jnp.cos(ac), jnp.cos(ac)], axis=-1)
    sin = jnp.concatenate([-jnp.sin(ar), jnp.sin(ar), -jnp.sin(ac), jnp.sin(ac)], axis=-1)
    return cos, sin


def _swap_perm(dim):
    q = dim // 4
    idx = np.arange(dim)
    return np.where((idx % (2 * q)) < q, idx + q, idx - q)


def _rope_partner(y, dim):
    q = dim // 4
    lane = lax.broadcasted_iota(jnp.int32, (1, LANES), 1)
    first = (lane % (2 * q)) < q
    tiles = []
    for j in range(0, y.shape[1], LANES):
        t = y[:, j:j + LANES]
        tiles.append(jnp.where(first, pltpu.roll(t, LANES - q, 1), pltpu.roll(t, q, 1)))
    return tiles[0] if len(tiles) == 1 else jnp.concatenate(tiles, axis=-1)


def _swap_cols(w, dim):
    n = w.shape[-1] // dim
    perm = (np.arange(n)[:, None] * dim + _swap_perm(dim)[None, :]).reshape(-1)
    return w[..., perm]


def _block_ones(n, blk):
    i = np.arange(n) // blk
    return jnp.asarray((i[:, None] == i[None, :]).astype(np.float32), dtype=BF16)


E_NAQ, E_NAK, E_NAV, E_GQ, E_GK, E_GV, E_END = (0, 512, 1024, 1536, 2048, 2304, 2560)


def _even_weights(w_in):
    naq, nak, nav, gq, gk, gv = jnp.split(w_in, [512, 1024, 1536, 2048, 2176], axis=-1)
    k0, k1 = gk[:, :64], gk[:, 64:]
    v0, v1 = gv[:, :64], gv[:, 64:]
    gk2 = jnp.concatenate([k0, k1, k1, k0], axis=-1)
    gv2 = jnp.concatenate([v0, v1, v1, v0], axis=-1)
    w = jnp.concatenate([naq * (0.125 * LOG2E), nak, nav, gq, gk2, gv2], axis=-1).astype(BF16)
    return w


def _even_tables(s, q_gain, k_gain, rope):
    sw = _swap_perm(64)
    if rope:
        cos, sin = _rope_tables(s, 64)
    else:
        cos, sin = jnp.ones((s, 64), F32), jnp.zeros((s, 64), F32)
    qa = jnp.tile(cos * q_gain[None, :] * (0.125 * LOG2E), (1, 8))
    qb = jnp.tile(sin * q_gain[sw][None, :] * (0.125 * LOG2E), (1, 8))
    ka = jnp.tile(cos * k_gain[None, :], (1, 4))
    kb = jnp.tile(sin * k_gain[sw][None, :], (1, 4))
    return qa, qb, ka, kb


def _inproj_even_kernel(x_ref, sc_ref, sh_ref, w_ref, g512_ref, g256_ref,
                        qa_ref, qb_ref, ka_ref, kb_ref,
                        naq_ref, nak_ref, nav_ref, gq_ref, gk_ref, gv_ref):
    h = (x_ref[0] * (1.0 + sc_ref[0]) + sh_ref[0]).astype(BF16)
    naq_ref[0] = _dot(h, w_ref[:, E_NAQ:E_NAK]).astype(BF16)
    nak_ref[0] = _dot(h, w_ref[:, E_NAK:E_NAV]).astype(BF16)
    nav_ref[0] = _dot(h, w_ref[:, E_NAV:E_GQ]).astype(BF16)
    gv_ref[0] = _dot(h, w_ref[:, E_GV:E_END]).astype(BF16)
    y = _dot(h, w_ref[:, E_GQ:E_GK])
    ys = _rope_partner(y, HEAD_DIM)
    r = lax.rsqrt(_dot((y * y).astype(BF16), g512_ref[...]) * (1.0 / HEAD_DIM) + RMS_EPS)
    gq_ref[0] = (r * (y * qa_ref[...] + ys * qb_ref[...])).astype(BF16)
    y = _dot(h, w_ref[:, E_GK:E_GV])
    ys = _rope_partner(y, HEAD_DIM)
    r = lax.rsqrt(_dot((y * y).astype(BF16), g256_ref[...]) * (1.0 / HEAD_DIM) + RMS_EPS)
    gk_ref[0] = (r * (y * ka_ref[...] + ys * kb_ref[...])).astype(BF16)


def _mod_spec(d, batched):
    if batched:
        return pl.BlockSpec((1, 1, d), lambda b, s: (b, 0, 0))
    return pl.BlockSpec((1, 1, d), lambda b, s: (0, 0, 0))


def _const_spec(shape):
    nd = len(shape)
    return pl.BlockSpec(shape, lambda b, s: (0,) * nd)


def _inproj_even(x, sc, sh, w, tables, ts):
    bsz, s, d = x.shape
    batched = sc.shape[0] > 1
    qa, qb, ka, kb = tables
    tok = lambda n: pl.BlockSpec((1, ts, n), lambda b, i: (b, i, 0))
    nt = qa.shape[0] // ts
    tab = lambda n: pl.BlockSpec((ts, n), lambda b, i: (i % nt, 0))
    widths = (512, 512, 512, 512, 256, 256)
    return pl.pallas_call(
        _inproj_even_kernel,
        grid=(bsz, s // ts),
        in_specs=[tok(d), _mod_spec(d, batched), _mod_spec(d, batched), _const_spec(w.shape),
                  _const_spec((512, 512)), _const_spec((256, 256)),
                  tab(512), tab(512), tab(256), tab(256)],
        out_specs=[tok(n) for n in widths],
        out_shape=[jax.ShapeDtypeStruct((bsz, s, n), BF16) for n in widths],
        compiler_params=_cparams(2),
    )(x, sc, sh, w, _block_ones(512, 64), _block_ones(256, 64), qa, qb, ka, kb)


O_DQ, O_DK, O_DV, O_CQ, O_CKV, O_PE, O_END = (0, 512, 1024, 1536, 1792, 1920, 2048)


def _odd_weights(w_in, w_uq, w_ukv):
    dq, dk, dv, cq, ckv, kpe = jnp.split(w_in, [512, 1024, 1536, 1792, 1920], axis=-1)
    d = w_in.shape[0]
    pe_slot = jnp.concatenate([kpe, _swap_cols(kpe, 32), jnp.zeros((d, 64), F32)], axis=-1)
    w = jnp.concatenate([dq, dk, dv, cq, ckv, pe_slot], axis=-1).astype(BF16)
    uq = w_uq.reshape(MLA_Q_RANK, MLA_HEADS, MLA_QK)
    z32 = jnp.zeros((MLA_Q_RANK, MLA_HEADS, 32), F32)
    z64 = jnp.zeros((MLA_Q_RANK, MLA_HEADS, 64), F32)
    uq_pad = jnp.concatenate([uq, z32], axis=-1).reshape(MLA_Q_RANK, MLA_HEADS * LANES)
    uq_sw = jnp.concatenate([z64, _swap_cols(uq[..., MLA_NOPE:], 32), z32],
                            axis=-1).reshape(MLA_Q_RANK, MLA_HEADS * LANES)
    wuq2 = jnp.concatenate([uq_pad, uq_sw], axis=-1).astype(BF16)
    ukv = w_ukv.reshape(MLA_KV_RANK, MLA_HEADS, MLA_NOPE + MLA_V)
    zk = jnp.zeros((MLA_KV_RANK, MLA_HEADS, 64), F32)
    wk_pad = jnp.concatenate([ukv[..., :MLA_NOPE], zk], axis=-1).reshape(MLA_KV_RANK, MLA_HEADS * LANES)
    place = np.zeros((LANES, MLA_HEADS, LANES), np.float32)
    for j in range(MLA_ROPE):
        place[j, :, MLA_NOPE + j] = 1.0
        place[MLA_ROPE + j, :, MLA_NOPE + j] = 1.0
    wk2 = jnp.concatenate([wk_pad, jnp.asarray(place.reshape(LANES, MLA_HEADS * LANES))],
                          axis=0).astype(BF16)
    wv = ukv[..., MLA_NOPE:].reshape(MLA_KV_RANK, MLA_HEADS * MLA_V).astype(BF16)
    return w, wuq2, wk2, wv


def _odd_tables(s, rope):
    m_scale = MLA_QK ** -0.5 * LOG2E
    if rope:
        cos64, sin64 = _rope_tables(s, 64)
        cos32, sin32 = _rope_tables(s, 32)
    else:
        cos64, sin64 = jnp.ones((s, 64), F32), jnp.zeros((s, 64), F32)
        cos32, sin32 = jnp.ones((s, 32), F32), jnp.zeros((s, 32), F32)
    one64, z32, z64 = jnp.ones((s, 64), F32), jnp.zeros((s, 32), F32), jnp.zeros((s, 64), F32)
    dcos, dsin = jnp.tile(cos64, (1, 8)), jnp.tile(sin64, (1, 8))
    qa = jnp.tile(jnp.concatenate([one64, cos32, z32], axis=-1) * m_scale, (1, MLA_HEADS))
    qb = jnp.tile(jnp.concatenate([z64, sin32, z32], axis=-1) * m_scale, (1, MLA_HEADS))
    pe = jnp.concatenate([cos32, sin32, z64], axis=-1)
    return dcos, dsin, qa, qb, pe


def _inproj_odd_kernel(x_ref, sc_ref, sh_ref, w_ref, wuq_ref, wk_ref, wv_ref, qg_ref, kvg_ref,
                       dcos_ref, dsin_ref, qa_ref, qb_ref, pe_ref,
                       dq_ref, dk_ref, dv_ref, mq_ref, mk_ref, mv_ref):
    h = (x_ref[0] * (1.0 + sc_ref[0]) + sh_ref[0]).astype(BF16)
    dcos, dsin = dcos_ref[...], dsin_ref[...]
    y = _dot(h, w_ref[:, O_DQ:O_DK])
    ys = _rope_partner(y, HEAD_DIM)
    dq_ref[0] = ((y * dcos + ys * dsin) * (0.125 * LOG2E)).astype(BF16)
    y = _dot(h, w_ref[:, O_DK:O_DV])
    ys = _rope_partner(y, HEAD_DIM)
    dk_ref[0] = (y * dcos + ys * dsin).astype(BF16)
    dv_ref[0] = _dot(h, w_ref[:, O_DV:O_CQ]).astype(BF16)
    cq = _dot(h, w_ref[:, O_CQ:O_CKV])
    nq = cq * lax.rsqrt(jnp.mean(cq * cq, axis=-1, keepdims=True) + RMS_EPS) * qg_ref[...]
    y2 = _dot(nq.astype(BF16), wuq_ref[...])
    half = MLA_HEADS * LANES
    mq_ref[0] = (y2[:, :half] * qa_ref[...] + y2[:, half:] * qb_ref[...]).astype(BF16)
    ckv = _dot(h, w_ref[:, O_CKV:O_PE])
    nk = ckv * lax.rsqrt(jnp.mean(ckv * ckv, axis=-1, keepdims=True) + RMS_EPS) * kvg_ref[...]
    pe = _dot(h, w_ref[:, O_PE:O_END]) * pe_ref[...]
    nkb = nk.astype(BF16)
    cat = jnp.concatenate([nkb, pe.astype(BF16)], axis=-1)
    mk_ref[0] = _dot(cat, wk_ref[...]).astype(BF16)
    mv_ref[0] = _dot(nkb, wv_ref[...]).astype(BF16)


def _inproj_odd(x, sc, sh, weights, gains, tables, ts):
    bsz, s, d = x.shape
    batched = sc.shape[0] > 1
    w, wuq2, wk2, wv = weights
    qg, kvg = gains
    dcos, dsin, qa, qb, pe = tables
    tok = lambda n: pl.BlockSpec((1, ts, n), lambda b, i: (b, i, 0))
    nt = dcos.shape[0] // ts
    tab = lambda n: pl.BlockSpec((ts, n), lambda b, i: (i % nt, 0))
    widths = (512, 512, 512, 1024, 1024, 512)
    return pl.pallas_call(
        _inproj_odd_kernel,
        grid=(bsz, s // ts),
        in_specs=[tok(d), _mod_spec(d, batched), _mod_spec(d, batched), _const_spec(w.shape),
                  _const_spec(wuq2.shape), _const_spec(wk2.shape), _const_spec(wv.shape),
                  _const_spec((1, MLA_Q_RANK)), _const_spec((1, MLA_KV_RANK)),
                  tab(512), tab(512), tab(1024), tab(1024), tab(128)],
        out_specs=[tok(n) for n in widths],
        out_shape=[jax.ShapeDtypeStruct((bsz, s, n), BF16) for n in widths],
        compiler_params=_cparams(2),
    )(x, sc, sh, w, wuq2, wk2, wv, qg.reshape(1, -1), kvg.reshape(1, -1), dcos, dsin, qa, qb, pe)


def _half_masks():
    lane = lax.broadcasted_iota(jnp.int32, (1, LANES), 1)
    return lane < HEAD_DIM, lane >= HEAD_DIM


def _softmax_values(ss, vs):
    m = functools.reduce(jnp.maximum, [jnp.max(s, axis=-1, keepdims=True) for s in ss])
    es = [jnp.exp2(s - m) for s in ss]
    l = functools.reduce(lambda a, b: a + b, [jnp.sum(e, axis=-1, keepdims=True) for e in es])
    o = functools.reduce(lambda a, b: a + b, [_dot(e.astype(BF16), v) for e, v in zip(es, vs)])
    return o * (1.0 / l)


def _attend(qm, ks, vs):
    return _softmax_values([_dot_nt(qm, k) for k in ks], vs)


NA_RB = 4
NA_WIN_ROWS = NA_RB + NA_WIN_H - 1
NA_SUB = 2


def _na_block_plan(rows):
    plan = []
    for blk in range(rows // NA_RB):
        r0 = blk * NA_RB
        rs = [int(np.clip(r0 + i - NA_WIN_H // 2, 0, rows - NA_WIN_H)) for i in range(NA_RB)]
        ws = int(np.clip(r0 - NA_WIN_H // 2, 0, rows - NA_WIN_ROWS))
        pat = tuple((rs[i] - ws, r0 + i - ws) for i in range(NA_RB))
        assert all(0 <= o and o + NA_WIN_H <= NA_WIN_ROWS for o, _ in pat)
        plan.append(pat)
    assert all(p == plan[1] for p in plan[1:-1])
    return (plan[0], plan[1], plan[-1])


def _na_bias_table(rpb, rows):
    cols = np.arange(GRID_W)
    col_start = np.clip(cols - NA_WIN_W // 2, 0, GRID_W - NA_WIN_W)
    col_mask = (cols[None, :] >= col_start[:, None]) & (cols[None, :] < col_start[:, None] + NA_WIN_W)
    col_idx = np.clip(cols[None, :] - cols[:, None] + NA_WIN_W - 1, 0, 2 * NA_WIN_W - 2)
    wr = np.arange(NA_WIN_ROWS)
    tables = []
    for pat in _na_block_plan(rows):
        off = np.array([o for o, _ in pat])[:, None]
        rq = np.array([r for _, r in pat])[:, None]
        row_ok = (wr[None, :] >= off) & (wr[None, :] < off + NA_WIN_H)
        ridx = np.clip(wr[None, :] - rq + NA_WIN_H - 1, 0, 2 * NA_WIN_H - 2)
        t = rpb.astype(F32)[:, ridx]
        t = t[..., col_idx]
        t = t.transpose(0, 1, 3, 2, 4)
        ok = row_ok[None, :, None, :, None] & col_mask[None, None, :, None, :]
        t = jnp.where(ok, t * LOG2E, NEG_INF)
        tables.append(t.reshape(NA_HEADS, NA_RB * GRID_W, NA_WIN_ROWS * GRID_W))
    return jnp.stack(tables)


def _na_kernel(rows, sub, q_ref, k_ref, v_ref, kc_ref, vc_ref, *refs):
    bt_refs, o_ref = refs[:sub], refs[sub]
    step = pl.program_id(1)
    tq = NA_RB * GRID_W
    m0, m1 = _half_masks()
    for u in range(sub):
        r0 = (step * sub + u) * NA_RB
        ws = jnp.clip(r0 - NA_WIN_H // 2, 0, rows - NA_WIN_ROWS)
        win = pl.ds(pl.multiple_of(ws * GRID_W, GRID_W), NA_WIN_ROWS * GRID_W)
        qrows = slice(u * tq, (u + 1) * tq)
        for j in range(NA_HEADS // 2):
            sl = slice(j * LANES, (j + 1) * LANES)
            qp = q_ref[0, qrows, sl]
            kp, vp = k_ref[0, win, sl], v_ref[0, win, sl]
            kcp, vcp = kc_ref[0, :, sl], vc_ref[0, :, sl]
            outs = []
            for par, msk in ((0, m0), (1, m1)):
                qm = jnp.where(msk, qp, jnp.zeros_like(qp))
                s_loc = _dot_nt(qm, kp) + bt_refs[u][0, 2 * j + par]
                outs.append(_softmax_values([s_loc, _dot_nt(qm, kcp)], [vp, vcp]))
            o_ref[0, qrows, sl] = jnp.where(m0, outs[0], outs[1]).astype(BF16)


def _na_attention(q, k, v, kc, vc, rpb):
    bsz, s, w = q.shape
    l = kc.shape[1]
    rows = s // GRID_W
    nblk = rows // NA_RB
    sub = NA_SUB
    nstep = nblk // sub
    assert rows % NA_RB == 0 and rows >= NA_WIN_ROWS and nblk % sub == 0 and nblk >= 3
    bt = _na_bias_table(rpb, rows)
    tq = NA_RB * GRID_W * sub
    full = lambda n: pl.BlockSpec((1, n, w), lambda b, r: (b, 0, 0))

    def bt_spec(u):
        def kind(b, r):
            blk = r * sub + u
            return ((blk > 0).astype(jnp.int32) + (blk == nblk - 1).astype(jnp.int32), 0, 0, 0)
        return pl.BlockSpec((1,) + bt.shape[1:], kind)

    return pl.pallas_call(
        functools.partial(_na_kernel, rows, sub),
        grid=(bsz, nstep),
        in_specs=[pl.BlockSpec((1, tq, w), lambda b, r: (b, r, 0)),
                  full(s), full(s), full(l), full(l)] + [bt_spec(u) for u in range(sub)],
        out_specs=pl.BlockSpec((1, tq, w), lambda b, r: (b, r, 0)),
        out_shape=jax.ShapeDtypeStruct((bsz, s, w), BF16),
        compiler_params=_cparams(2),
    )(q, k, v, kc, vc, *([bt] * sub))


def _slot_attn_kernel(n_heads, q_slot, has_lat, sub, *refs):
    if has_lat:
        q_ref, k_ref, v_ref, kc_ref, vc_ref, o_ref = refs
    else:
        q_ref, kc_ref, vc_ref, o_ref = refs
    m0, m1 = _half_masks()
    tq = q_ref.shape[1] // sub
    for u in range(sub):
        qrows = slice(u * tq, (u + 1) * tq)
        for j in range(n_heads // 2):
            vsl = slice(j * LANES, (j + 1) * LANES)
            outs = []
            for par, msk in ((0, m0), (1, m1)):
                h = 2 * j + par
                if q_slot:
                    ksl = slice(h * LANES, (h + 1) * LANES)
                    qm = q_ref[0, qrows, ksl]
                else:
                    ksl = vsl
                    qp = q_ref[0, qrows, vsl]
                    qm = jnp.where(msk, qp, jnp.zeros_like(qp))
                ks, vs = [kc_ref[0, :, ksl]], [vc_ref[0, :, vsl]]
                if has_lat:
                    ks.insert(0, k_ref[0, :, ksl])
                    vs.insert(0, v_ref[0, :, vsl])
                outs.append(_attend(qm, ks, vs))
            o_ref[0, qrows, vsl] = jnp.where(m0, outs[0], outs[1]).astype(BF16)


def _slot_attention(q, k, v, kc, vc, n_heads, q_slot, tq, sub=1):
    bsz, sq, wq = q.shape
    l, wk, wv = kc.shape[1], kc.shape[2], vc.shape[2]
    has_lat = k is not None
    qspec = pl.BlockSpec((1, tq, wq), lambda b, i: (b, i, 0))
    full = lambda n, w: pl.BlockSpec((1, n, w), lambda b, i: (b, 0, 0))
    in_specs, args = [qspec], [q]
    if has_lat:
        s = k.shape[1]
        in_specs += [full(s, wk), full(s, wv)]
        args += [k, v]
    in_specs += [full(l, wk), full(l, wv)]
    args += [kc, vc]
    return pl.pallas_call(
        functools.partial(_slot_attn_kernel, n_heads, q_slot, has_lat, sub),
        grid=(bsz, sq // tq),
        in_specs=in_specs,
        out_specs=pl.BlockSpec((1, tq, wv), lambda b, i: (b, i, 0)),
        out_shape=jax.ShapeDtypeStruct((bsz, sq, wv), BF16),
        compiler_params=_cparams(2),
    )(*args)


def _gqa_kernel(has_lat, tq, sub, *refs):
    if has_lat:
        q_ref, k_ref, v_ref, kc_ref, vc_ref, o_ref = refs
    else:
        q_ref, kc_ref, vc_ref, o_ref = refs
    masks = _half_masks()
    for u in range(sub):
        qrows = slice(u * tq, (u + 1) * tq)
        res = {}
        for g in range(GQA_KV_HEADS):
            for var in range(2):
                par = g if var == 0 else 1 - g
                heads = (4 * g + par, 4 * g + 2 + par)
                vsl = slice(var * LANES, (var + 1) * LANES)
                qs = []
                for h in heads:
                    qp = q_ref[0, qrows, (h // 2) * LANES:(h // 2 + 1) * LANES]
                    qs.append(jnp.where(masks[par], qp, jnp.zeros_like(qp)))
                qm = jnp.concatenate(qs, axis=0)
                ks, vs = [kc_ref[0, :, vsl]], [vc_ref[0, :, vsl]]
                if has_lat:
                    ks.insert(0, k_ref[0, :, vsl])
                    vs.insert(0, v_ref[0, :, vsl])
                o = _attend(qm, ks, vs)
                res[heads[0]] = o[:tq]
                res[heads[1]] = o[tq:]
        for j in range(GQA_HEADS // 2):
            o_ref[0, qrows, j * LANES:(j + 1) * LANES] = jnp.where(
                masks[0], res[2 * j], res[2 * j + 1]).astype(BF16)


def _gqa_attention(q, k2, v2, k2c, v2c, tq, sub=1):
    bsz, sq, wq = q.shape
    l = k2c.shape[1]
    has_lat = k2 is not None
    full = lambda n: pl.BlockSpec((1, n, 2 * LANES), lambda b, i: (b, 0, 0))
    in_specs, args = [pl.BlockSpec((1, tq * sub, wq), lambda b, i: (b, i, 0))], [q]
    if has_lat:
        in_specs += [full(k2.shape[1])] * 2
        args += [k2, v2]
    in_specs += [full(l)] * 2
    args += [k2c, v2c]
    return pl.pallas_call(
        functools.partial(_gqa_kernel, has_lat, tq, sub),
        grid=(bsz, sq // (tq * sub)),
        in_specs=in_specs,
        out_specs=pl.BlockSpec((1, tq * sub, wq), lambda b, i: (b, i, 0)),
        out_shape=jax.ShapeDtypeStruct((bsz, sq, wq), BF16),
        compiler_params=_cparams(2),
    )(*args)


def _diff_kernel(has_lat, tq, sub, lam_init, *refs):
    if has_lat:
        q_ref, k_ref, v_ref, kc_ref, vc_ref, lq1, lk1, lq2, lk2, sub_ref, o_ref = refs
    else:
        q_ref, kc_ref, vc_ref, lq1, lk1, lq2, lk2, sub_ref, o_ref = refs
    lam = (jnp.exp(jnp.sum(lq1[...] * lk1[...], axis=-1, keepdims=True))
           - jnp.exp(jnp.sum(lq2[...] * lk2[...], axis=-1, keepdims=True)) + lam_init)
    m0, m1 = _half_masks()
    for u in range(sub):
        qrows = slice(u * tq, (u + 1) * tq)
        for h in range(DIFF_HEADS):
            sl = slice(h * LANES, (h + 1) * LANES)
            qp = q_ref[0, qrows, sl]
            zero = jnp.zeros_like(qp)
            qm = jnp.concatenate([jnp.where(m0, qp, zero), jnp.where(m1, qp, zero)], axis=0)
            ks, vs = [kc_ref[0, :, sl]], [vc_ref[0, :, sl]]
            if has_lat:
                ks.insert(0, k_ref[0, :, sl])
                vs.insert(0, v_ref[0, :, sl])
            o2 = _attend(qm, ks, vs)
            o = o2[:tq] - lam * o2[tq:]
            o = o * lax.rsqrt(jnp.mean(o * o, axis=-1, keepdims=True) + RMS_EPS) * sub_ref[...]
            o_ref[0, qrows, sl] = (o * (1.0 - lam_init)).astype(BF16)


def _diff_attention(q, k, v, kc, vc, lams, subln, lam_init, tq, sub=1):
    bsz, sq, w = q.shape
    l = kc.shape[1]
    has_lat = k is not None
    full = lambda n: pl.BlockSpec((1, n, w), lambda b, i: (b, 0, 0))
    in_specs, args = [pl.BlockSpec((1, tq * sub, w), lambda b, i: (b, i, 0))], [q]
    if has_lat:
        in_specs += [full(k.shape[1])] * 2
        args += [k, v]
    in_specs += [full(l)] * 2 + [_const_spec((1, HEAD_DIM))] * 4 + [_const_spec((1, LANES))]
    args += [kc, vc] + [a.reshape(1, -1).astype(F32) for a in lams] + [subln.reshape(1, -1).astype(F32)]
    return pl.pallas_call(
        functools.partial(_diff_kernel, has_lat, tq, sub, lam_init),
        grid=(bsz, sq // (tq * sub)),
        in_specs=in_specs,
        out_specs=pl.BlockSpec((1, tq * sub, w), lambda b, i: (b, i, 0)),
        out_shape=jax.ShapeDtypeStruct((bsz, sq, w), BF16),
        compiler_params=_cparams(2),
    )(*args)


def _pack_bf16_pairs(v):
    w = v.shape[1] // 2
    hi = pltpu.bitcast(v[:, :w].astype(F32), jnp.int32)
    lo = pltpu.bitcast(v[:, w:].astype(F32), jnp.int32)
    return hi | lax.shift_right_logical(lo, 16)


def _unpack_bf16_pairs(u):
    hi = pltpu.bitcast(u & jnp.int32(-65536), F32)
    lo = pltpu.bitcast(lax.shift_left(u, 16), F32)
    return hi, lo


def _layer_norm(z, g, b):
    mu = jnp.mean(z, axis=-1, keepdims=True)
    zc = z - mu
    var = jnp.mean(zc * zc, axis=-1, keepdims=True)
    return zc * lax.rsqrt(var + LN_EPS) * g + b


def _outproj_kernel(alpha, sub, o1_ref, o2_ref, w1_ref, w2_ref, x_ref, g1_ref, lng_ref, lnb_ref,
                    sc2_ref, sh2_ref, rwh_ref, rwl_ref, rb_ref, tri_ref, cnt0_ref,
                    xo_ref, h2a_ref, h2b_ref, idx_ref, rank_ref, gw_ref, cnt_ref, carry_ref):
    @pl.when(jnp.logical_and(pl.program_id(0) == 0, pl.program_id(1) == 0))
    def _():
        carry_ref[...] = cnt0_ref[...]

    tr = x_ref.shape[1] // sub
    carry = carry_ref[...]
    for u in range(sub):
        rows = slice(u * tr, (u + 1) * tr)
        o = _dot(o1_ref[0, rows], w1_ref[...]) + _dot(o2_ref[0, rows], w2_ref[...])
        xn = _layer_norm(alpha * x_ref[0, rows] + g1_ref[0] * o, lng_ref[...], lnb_ref[...])
        xo_ref[0, rows] = xn
        h2 = xn * (1.0 + sc2_ref[0]) + sh2_ref[0]
        hi = h2.astype(BF16)
        packed = _pack_bf16_pairs(hi)
        q = packed.shape[1] // 2
        h2a_ref[0, rows] = packed[:, :q]
        h2b_ref[0, rows] = packed[:, q:]
        lo = (h2 - hi.astype(F32)).astype(BF16)
        logits = (_dot(hi, rwh_ref[...]) + _dot(lo, rwh_ref[...]) + _dot(hi, rwl_ref[...])) + rb_ref[...]
        lane = lax.broadcasted_iota(jnp.int32, logits.shape, 1).astype(F32)
        vals, idxs = [], []
        cur = logits
        for _ in range(TOP_K):
            m = jnp.max(cur, axis=-1, keepdims=True)
            ik = jnp.min(jnp.where(cur == m, lane, float(LANES)), axis=-1, keepdims=True)
            vals.append(m)
            idxs.append(ik)
            cur = jnp.where(lane == ik, -jnp.inf, cur)
        ws = [jnp.exp(v - vals[0]) for v in vals]
        inv = 1.0 / functools.reduce(lambda a, b: a + b, ws)
        sel = [lane == ik for ik in idxs]
        onehot = functools.reduce(lambda a, b: a + b, [m.astype(F32) for m in sel])
        before = _dot(tri_ref[...], onehot.astype(BF16)) + carry
        idx_out = jnp.zeros_like(logits)
        rank_out = jnp.zeros_like(logits)
        w_out = jnp.zeros_like(logits)
        for k in range(TOP_K):
            rk = jnp.sum(jnp.where(sel[k], before, 0.0), axis=-1, keepdims=True)
            idx_out = jnp.where(lane == float(k), idxs[k], idx_out)
            rank_out = jnp.where(lane == float(k), rk, rank_out)
            w_out = jnp.where(lane == float(k), ws[k] * inv, w_out)
        idx_ref[0, rows] = idx_out.astype(jnp.int32)
        rank_ref[0, rows] = rank_out.astype(jnp.int32)
        gw_ref[0, rows] = w_out
        carry = carry + jnp.sum(onehot, axis=0, keepdims=True)
    carry_ref[...] = carry
    cnt_ref[...] = carry


def _outproj_ln_router(o1, o2, w_out, x, g1, lng, lnb, sc2, sh2, router, cnt0, alpha, ts):
    bsz, s, d = x.shape
    batched = g1.shape[0] > 1
    rwh, rwl, rb = router
    w1, w2 = w_out[:512].astype(BF16), w_out[512:].astype(BF16)
    sub = 1
    tr = ts // sub
    tri = jnp.asarray(np.tril(np.ones((tr, tr), np.float32), -1), dtype=BF16)
    tok = lambda n: pl.BlockSpec((1, ts, n), lambda b, i: (b, i, 0))
    ms = _mod_spec(d, batched)
    return pl.pallas_call(
        functools.partial(_outproj_kernel, alpha, sub),
        grid=(bsz, s // ts),
        in_specs=[tok(512), tok(512), _const_spec((512, d)), _const_spec((512, d)), tok(d), ms,
                  _const_spec((1, d)), _const_spec((1, d)), ms, ms,
                  _const_spec((d, LANES)), _const_spec((d, LANES)), _const_spec((1, LANES)),
                  _const_spec((tr, tr)), _const_spec((1, LANES))],
        out_specs=[tok(d), tok(d // 4), tok(d // 4), tok(LANES), tok(LANES), tok(LANES),
                   _const_spec((1, LANES))],
        out_shape=[jax.ShapeDtypeStruct((bsz, s, d), F32),
                   jax.ShapeDtypeStruct((bsz, s, d // 4), jnp.int32),
                   jax.ShapeDtypeStruct((bsz, s, d // 4), jnp.int32),
                   jax.ShapeDtypeStruct((bsz, s, LANES), jnp.int32),
                   jax.ShapeDtypeStruct((bsz, s, LANES), jnp.int32),
                   jax.ShapeDtypeStruct((bsz, s, LANES), F32),
                   jax.ShapeDtypeStruct((1, LANES), F32)],
        scratch_shapes=[pltpu.VMEM((1, LANES), F32)],
        compiler_params=_cparams(2),
    )(o1, o2, w1, w2, x, g1, lng.reshape(1, d), lnb.reshape(1, d), sc2, sh2, rwh, rwl, rb, tri, cnt0)


def _router_weights(router_w, router_b):
    d, e = router_w.shape
    wp = jnp.zeros((d, LANES), F32).at[:, :e].set(router_w)
    hi = wp.astype(BF16)
    lo = (wp - hi.astype(F32)).astype(BF16)
    rb = jnp.full((1, LANES), -jnp.inf, F32).at[0, :e].set(router_b)
    return hi, lo, rb


def _deinterleave_perm():
    p = np.zeros((GU_BLOCK, GU_BLOCK), np.float32)
    m = np.arange(GU_BLOCK // 2)
    p[2 * m, m] = 1.0
    p[2 * m + 1, GU_BLOCK // 2 + m] = 1.0
    return jnp.asarray(p, dtype=BF16)


def _ffn_kernel(te_ref, tv_ref, xa_ref, xb_ref, wgu_ref, bgu_ref, wd_ref, bd_ref, perm_ref, y_ref,
                wgu_s, wd_s):
    j = pl.program_id(0)
    n_blk = wgu_s.shape[1] // GU_BLOCK
    half = GU_BLOCK // 2

    @pl.when(jnp.logical_or(j == 0, te_ref[j] != te_ref[jnp.maximum(j - 1, 0)]))
    def _():
        for b in range(n_blk):
            sl = slice(b * GU_BLOCK, (b + 1) * GU_BLOCK)
            wgu_s[:, sl] = _dot(wgu_ref[0, 0, :, sl].astype(BF16), perm_ref[...]).astype(BF16)
        wd_s[...] = wd_ref[0, 0].astype(BF16)

    @pl.when(tv_ref[j] > 0)
    def _():
        a_hi, a_lo = _unpack_bf16_pairs(xa_ref[...])
        b_hi, b_lo = _unpack_bf16_pairs(xb_ref[...])
        x = jnp.concatenate([a_hi, b_hi, a_lo, b_lo], axis=-1).astype(BF16)
        acts = []
        for b in range(n_blk):
            sl = slice(b * GU_BLOCK, (b + 1) * GU_BLOCK)
            gu = _dot(x, wgu_s[:, sl]) + bgu_ref[0, :, sl]
            glu = jnp.minimum(gu[:, :half], SWIGLU_LIMIT)
            lin = jnp.clip(gu[:, half:], -SWIGLU_LIMIT, SWIGLU_LIMIT)
            acts.append(((lin + 1.0) * (glu * jax.nn.sigmoid(SWIGLU_ALPHA * glu))).astype(BF16))
        a = jnp.concatenate(acts, axis=-1)
        y_ref[...] = (_dot(a, wd_s[...]) + bd_ref[0]).astype(BF16)

    @pl.when(tv_ref[j] == 0)
    def _():
        y_ref[...] = jnp.zeros_like(y_ref)


def _expert_ffn(xsa, xsb, tile_expert, tile_valid, layer, w_gu, b_gu, w_down, b_down, tm):
    p, q = xsa.shape
    d = 4 * q
    _, e, _, f2 = w_gu.shape
    f = f2 // 2
    half = GU_BLOCK // 2
    bgu = jnp.stack([b_gu[:, 0::2].reshape(e, f // half, half),
                     b_gu[:, 1::2].reshape(e, f // half, half)], axis=2).reshape(e, 1, f2)
    wspec = lambda a, b: pl.BlockSpec((1, 1, a, b), lambda j, te, tv: (layer, te[j], 0, 0))
    bspec = lambda b: pl.BlockSpec((1, 1, b), lambda j, te, tv: (te[j], 0, 0))
    return pl.pallas_call(
        _ffn_kernel,
        grid_spec=pltpu.PrefetchScalarGridSpec(
            num_scalar_prefetch=2,
            grid=(p // tm,),
            in_specs=[pl.BlockSpec((tm, q), lambda j, te, tv: (j, 0)),
                      pl.BlockSpec((tm, q), lambda j, te, tv: (j, 0)),
                      wspec(d, f2), bspec(f2), wspec(f, d), bspec(d),
                      pl.BlockSpec((GU_BLOCK, GU_BLOCK), lambda j, te, tv: (0, 0))],
            out_specs=pl.BlockSpec((tm, d), lambda j, te, tv: (j, 0)),
            scratch_shapes=[pltpu.VMEM((d, f2), BF16), pltpu.VMEM((f, d), BF16)]),
        out_shape=jax.ShapeDtypeStruct((p, d), BF16),
        compiler_params=_cparams(1),
    )(tile_expert, tile_valid, xsa, xsb, w_gu, bgu, w_down, b_down.reshape(e, 1, d),
      _deinterleave_perm())


def _moe_layout(counts, t_total, tm):
    n_experts = counts.shape[0]
    pc = ((counts + tm - 1) // tm) * tm
    pend = jnp.cumsum(pc)
    pstart = pend - pc
    p = t_total * TOP_K + n_experts * tm
    tile_start = jnp.arange(p // tm, dtype=jnp.int32) * tm
    tile_expert = jnp.minimum(jnp.sum((tile_start[:, None] >= pend[None, :]).astype(jnp.int32), axis=1),
                              n_experts - 1).astype(jnp.int32)
    tile_valid = (tile_start < pend[-1]).astype(jnp.int32)
    return pstart, p, tile_expert, tile_valid


SC_WINDOW = 128


def _sc_dispatch(parts, p):
    w = parts[0][0].shape[1]
    k = parts[0][1].shape[0]
    mesh = plsc.VectorSubcoreMesh(core_axis_name="core", subcore_axis_name="subcore")

    @functools.partial(pl.kernel, out_type=jax.ShapeDtypeStruct((p, w), parts[0][0].dtype), mesh=mesh)
    def kern(*refs):
        o_hbm = refs[-1]

        def body(x_vmem, *i_vmems):
            for iv in i_vmems:
                pltpu.sync_copy(x_vmem, o_hbm.at[iv.at[0]])

        for n, (rows, _) in enumerate(parts):
            x_hbm, i_hbms = refs[n * (k + 1)], refs[n * (k + 1) + 1:(n + 1) * (k + 1)]
            pltpu.emit_pipeline(
                body,
                grid=(rows.shape[0] // SC_WINDOW,),
                in_specs=[pl.BlockSpec((SC_WINDOW, w), index_map=lambda i: (i, 0))]
                + [pl.BlockSpec((1, SC_WINDOW), index_map=lambda i: (0, i)) for _ in range(k)],
                out_specs=[],
                core_axis_name=("core", "subcore"),
                dimension_semantics=(pltpu.PARALLEL,),
            )(x_hbm, *i_hbms)

    args = []
    for rows, dpos_t in parts:
        args += [rows] + [dpos_t[kk:kk + 1] for kk in range(k)]
    return kern(*args)


def _combine_kernel(alpha, yg_ref, gw_ref, x_ref, g2_ref, lng_ref, lnb_ref, o_ref):
    gw = gw_ref[...]
    y = yg_ref[0].astype(F32) * gw[:, 0:1]
    for k in range(1, TOP_K):
        y = y + yg_ref[k].astype(F32) * gw[:, k:k + 1]
    o_ref[0] = _layer_norm(alpha * x_ref[0] + g2_ref[0] * y, lng_ref[...], lnb_ref[...])


def _combine_ln(yg, gw, x, g2, lng, lnb, alpha, yg_row0, ts):
    bsz, s, d = x.shape
    batched = g2.shape[0] > 1
    nblk = s // ts
    off = yg_row0 // ts
    return pl.pallas_call(
        functools.partial(_combine_kernel, alpha),
        grid=(bsz, nblk),
        in_specs=[pl.BlockSpec((TOP_K, ts, d), lambda b, i: (0, off + b * nblk + i, 0)),
                  pl.BlockSpec((ts, LANES), lambda b, i: (b * nblk + i, 0)),
                  pl.BlockSpec((1, ts, d), lambda b, i: (b, i, 0)),
                  _mod_spec(d, batched), _const_spec((1, d)), _const_spec((1, d))],
        out_specs=pl.BlockSpec((1, ts, d), lambda b, i: (b, i, 0)),
        out_shape=jax.ShapeDtypeStruct((bsz, s, d), F32),
        compiler_params=_cparams(2),
    )(yg, gw, x, g2, lng.reshape(1, d), lnb.reshape(1, d))


def kernel(x, c, ctx, c_ctx, mod_w, mod_b, ln1_g, ln1_b, ln2_g, ln2_b, even_w_in, even_w_out, na_rpb, gqa_q_gain, gqa_k_gain, odd_w_in, odd_w_out, diff_lq1, diff_lk1, diff_lq2, diff_lk2, diff_subln, mla_q_gain, mla_w_uq, mla_kv_gain, mla_w_ukv, router_w, router_b, exp_w_gu, exp_b_gu, exp_w_down, exp_b_down):
    bsz, s, d = x.shape
    l = ctx.shape[1]
    depth = mod_w.shape[0]
    n_experts = router_w.shape[-1]
    alpha = (2 * depth) ** 0.25
    ts = min(512, s)
    cts = min(512, bsz * l)
    tm = MOE_TILE

    pad = (-(bsz + 1)) % 8
    c_all = jnp.concatenate([c, c_ctx[None, :], jnp.zeros((pad, d), F32)], axis=0)
    mod = _modulation_all(c_all, mod_w, mod_b)

    for i in range(depth):
        last = i == depth - 1
        j = i // 2
        ml = [mod[i, :bsz, k * d:(k + 1) * d].reshape(bsz, 1, d) for k in range(6)]
        mc = [mod[i, bsz:bsz + 1, k * d:(k + 1) * d].reshape(1, 1, d) for k in range(6)]
        sh1, sc1, g1, sh2, sc2, g2 = ml
        csh1, csc1, cg1, csh2, csc2, cg2 = mc

        if i % 2 == 0:
            w = _even_weights(even_w_in[j])
            naq, nak, nav, gq, gk, gv = _inproj_even(
                x, sc1, sh1, w, _even_tables(s, gqa_q_gain[j], gqa_k_gain[j], True), ts)
            cnaq, cnak, cnav, cgq, cgk, cgv = [a.reshape(bsz, l, -1) for a in _inproj_even(
                ctx.reshape(1, bsz * l, d), csc1, csh1, w,
                _even_tables(cts, gqa_q_gain[j], gqa_k_gain[j], False), cts)]
            o1 = _na_attention(naq, nak, nav, cnak, cnav, na_rpb[j])
            o2 = _gqa_attention(gq, gk, gv, cgk, cgv, 256, ATTN_SUB)
            if not last:
                co1 = _slot_attention(cnaq, None, None, cnak, cnav, NA_HEADS, False, l)
                co2 = _gqa_attention(cgq, None, None, cgk, cgv, l)
            w_out = even_w_out[j]
        else:
            lam_init = 0.8 - 0.6 * math.exp(-0.3 * i)
            weights = _odd_weights(odd_w_in[j], mla_w_uq[j], mla_w_ukv[j])
            gains = (mla_q_gain[j], mla_kv_gain[j])
            dq, dk, dv, mq, mk, mv = _inproj_odd(x, sc1, sh1, weights, gains, _odd_tables(s, True), ts)
            cdq, cdk, cdv, cmq, cmk, cmv = [a.reshape(bsz, l, -1) for a in _inproj_odd(
                ctx.reshape(1, bsz * l, d), csc1, csh1, weights, gains, _odd_tables(cts, False), cts)]
            lams = (diff_lq1[j], diff_lk1[j], diff_lq2[j], diff_lk2[j])
            o1 = _diff_attention(dq, dk, dv, cdk, cdv, lams, diff_subln[j], lam_init, 256, ATTN_SUB)
            o2 = _slot_attention(mq, mk, mv, cmk, cmv, MLA_HEADS, True, 512)
            if not last:
                co1 = _diff_attention(cdq, None, None, cdk, cdv, lams, diff_subln[j], lam_init, l)
                co2 = _slot_attention(cmq, None, None, cmk, cmv, MLA_HEADS, True, l)
            w_out = odd_w_out[j]

        router = _router_weights(router_w[i], router_b[i])
        cnt0 = jnp.zeros((1, LANES), F32)
        x, h2a, h2b, ridx, rrank, rgw, cnt = _outproj_ln_router(
            o1, o2, w_out, x, g1, ln1_g[i], ln1_b[i], sc2, sh2, router, cnt0, alpha, ts)
        flat = lambda a: a.reshape(-1, a.shape[-1])
        routed = [[flat(a) for a in (h2a, h2b, ridx, rrank, rgw)]]
        if not last:
            one = lambda a: a.reshape(1, bsz * l, a.shape[-1])
            ctx, ch2a, ch2b, cidx, crank, cgw, cnt = _outproj_ln_router(
                one(co1), one(co2), w_out, one(ctx), cg1, ln1_g[i], ln1_b[i], csc2, csh2, router, cnt,
                alpha, cts)
            routed.append([flat(a) for a in (ch2a, ch2b, cidx, crank, cgw)])

        counts = cnt[0, :n_experts].astype(jnp.int32)
        t_total = sum(r[0].shape[0] for r in routed)
        pstart, p, tile_expert, tile_valid = _moe_layout(counts, t_total, tm)
        dpos = [(pstart[r[2][:, :TOP_K]] + r[3][:, :TOP_K]).T.astype(jnp.int32) for r in routed]
        xsa = _sc_dispatch([(r[0], dp) for r, dp in zip(routed, dpos)], p)
        xsb = _sc_dispatch([(r[1], dp) for r, dp in zip(routed, dpos)], p)
        ys = _expert_ffn(xsa, xsb, tile_expert, tile_valid, i, exp_w_gu, exp_b_gu[i],
                         exp_w_down, exp_b_down[i], tm)
        dpos_t = dpos[0] if last else jnp.concatenate(dpos, axis=1)
        yg = ys.at[dpos_t].get(mode="promise_in_bounds")

        x = _combine_ln(yg, routed[0][4], x, g2, ln2_g[i], ln2_b[i], alpha, 0, ts)
        if not last:
            ctx = _combine_ln(yg, routed[1][4], ctx, cg2, ln2_g[i], ln2_b[i], alpha, bsz * s,
                              cts).reshape(bsz, l, d)
    return x
```

```python
import functools
import math

import numpy as np
import jax
import jax.numpy as jnp
from jax import lax
from jax.experimental import pallas as pl
from jax.experimental.pallas import tpu as pltpu
from jax.experimental.pallas import tpu_sc as plsc

F32 = jnp.float32
BF16 = jnp.bfloat16

GRID_W = 64
HEAD_DIM = 64
ROPE_THETA = 10000.0
LN_EPS = 1e-6
RMS_EPS = 1e-6
NEG_INF = -1e30
NA_HEADS = 8
NA_WIN_H = 8
NA_WIN_W = 16
GQA_HEADS = 8
GQA_KV_HEADS = 2
DIFF_HEADS = 4
MLA_HEADS = 8
MLA_Q_RANK = 256
MLA_KV_RANK = 128
MLA_NOPE = 64
MLA_ROPE = 32
MLA_V = 64
MLA_QK = MLA_NOPE + MLA_ROPE
TOP_K = 4
SWIGLU_ALPHA = 1.702
SWIGLU_LIMIT = 7.0
LOG2E = 1.4426950408889634

LANES = 128
VMEM_LIMIT = 56 * 1024 * 1024
MOE_TILE = 1024
GU_BLOCK = 512
ATTN_SUB = 2


def _cparams(n_axes):
    return pltpu.CompilerParams(dimension_semantics=("arbitrary",) * n_axes,
                                vmem_limit_bytes=VMEM_LIMIT)


def _dot(a, b):
    return jnp.dot(a, b, preferred_element_type=F32)


def _dot_nt(a, b):
    return lax.dot_general(a, b, (((1,), (1,)), ((), ())), preferred_element_type=F32)


def _mod_kernel(c_ref, w_ref, b_ref, o_ref):
    cv = c_ref[...]
    a = (cv * jax.nn.sigmoid(cv)).astype(BF16)
    o_ref[0] = _dot(a, w_ref[0].astype(BF16)) + b_ref[0]


def _modulation_all(c_all, mod_w, mod_b):
    depth, d, n = mod_w.shape
    rows = c_all.shape[0]
    tn = 1536
    return pl.pallas_call(
        _mod_kernel,
        grid=(depth, n // tn),
        in_specs=[pl.BlockSpec((rows, d), lambda i, j: (0, 0)),
                  pl.BlockSpec((1, d, tn), lambda i, j: (i, 0, j)),
                  pl.BlockSpec((1, 1, tn), lambda i, j: (i, 0, j))],
        out_specs=pl.BlockSpec((1, rows, tn), lambda i, j: (i, 0, j)),
        out_shape=jax.ShapeDtypeStruct((depth, rows, n), F32),
        compiler_params=_cparams(2),
    )(c_all, mod_w, mod_b.reshape(depth, 1, n))


def _rope_tables(s, dim):
    pos = jnp.arange(s)
    row = (pos // GRID_W).astype(F32)[:, None]
    col = (pos % GRID_W).astype(F32)[:, None]
    quarter = dim // 4
    inv_freq = ROPE_THETA ** (-jnp.arange(quarter, dtype=F32) / quarter)
    ar, ac = row * inv_freq, col * inv_freq
    cos = jnp.concatenate([jnp.cos(ar), jnp.cos(ar), jnp.cos(ac), jnp.cos(ac)], axis=-1)
    sin = jnp.concatenate([-jnp.sin(ar), jnp.sin(ar), -jnp.sin(ac), jnp.sin(ac)], axis=-1)
    return cos, sin


def _swap_perm(dim):
    q = dim // 4
    idx = np.arange(dim)
    return np.where((idx % (2 * q)) < q, idx + q, idx - q)


def _rope_partner(y, dim):
    q = dim // 4
    lane = lax.broadcasted_iota(jnp.int32, (1, LANES), 1)
    first = (lane % (2 * q)) < q
    tiles = []
    for j in range(0, y.shape[1], LANES):
        t = y[:, j:j + LANES]
        tiles.append(jnp.where(first, pltpu.roll(t, LANES - q, 1), pltpu.roll(t, q, 1)))
    return tiles[0] if len(tiles) == 1 else jnp.concatenate(tiles, axis=-1)


def _swap_cols(w, dim):
    n = w.shape[-1] // dim
    perm = (np.arange(n)[:, None] * dim + _swap_perm(dim)[None, :]).reshape(-1)
    return w[..., perm]


def _block_ones(n, blk):
    i = np.arange(n) // blk
    return jnp.asarray((i[:, None] == i[None, :]).astype(np.float32), dtype=BF16)


E_NAQ, E_NAK, E_NAV, E_GQ, E_GK, E_GV, E_END = (0, 512, 1024, 1536, 2048, 2304, 2560)


def _even_weights(w_in):
    naq, nak, nav, gq, gk, gv = jnp.split(w_in, [512, 1024, 1536, 2048, 2176], axis=-1)
    k0, k1 = gk[:, :64], gk[:, 64:]
    v0, v1 = gv[:, :64], gv[:, 64:]
    gk2 = jnp.concatenate([k0, k1, k1, k0], axis=-1)
    gv2 = jnp.concatenate([v0, v1, v1, v0], axis=-1)
    w = jnp.concatenate([naq * (0.125 * LOG2E), nak, nav, gq, gk2, gv2], axis=-1).astype(BF16)
    return w


def _even_tables(s, q_gain, k_gain, rope):
    sw = _swap_perm(64)
    if rope:
        cos, sin = _rope_tables(s, 64)
    else:
        cos, sin = jnp.ones((s, 64), F32), jnp.zeros((s, 64), F32)
    qa = jnp.tile(cos * q_gain[None, :] * (0.125 * LOG2E), (1, 8))
    qb = jnp.tile(sin * q_gain[sw][None, :] * (0.125 * LOG2E), (1, 8))
    ka = jnp.tile(cos * k_gain[None, :], (1, 4))
    kb = jnp.tile(sin * k_gain[sw][None, :], (1, 4))
    return qa, qb, ka, kb


def _inproj_even_kernel(x_ref, sc_ref, sh_ref, w_ref, g512_ref, g256_ref,
                        qa_ref, qb_ref, ka_ref, kb_ref,
                        naq_ref, nak_ref, nav_ref, gq_ref, gk_ref, gv_ref):
    h = (x_ref[0] * (1.0 + sc_ref[0]) + sh_ref[0]).astype(BF16)
    naq_ref[0] = _dot(h, w_ref[:, E_NAQ:E_NAK]).astype(BF16)
    nak_ref[0] = _dot(h, w_ref[:, E_NAK:E_NAV]).astype(BF16)
    nav_ref[0] = _dot(h, w_ref[:, E_NAV:E_GQ]).astype(BF16)
    gv_ref[0] = _dot(h, w_ref[:, E_GV:E_END]).astype(BF16)
    y = _dot(h, w_ref[:, E_GQ:E_GK])
    ys = _rope_partner(y, HEAD_DIM)
    r = lax.rsqrt(_dot((y * y).astype(BF16), g512_ref[...]) * (1.0 / HEAD_DIM) + RMS_EPS)
    gq_ref[0] = (r * (y * qa_ref[...] + ys * qb_ref[...])).astype(BF16)
    y = _dot(h, w_ref[:, E_GK:E_GV])
    ys = _rope_partner(y, HEAD_DIM)
    r = lax.rsqrt(_dot((y * y).astype(BF16), g256_ref[...]) * (1.0 / HEAD_DIM) + RMS_EPS)
    gk_ref[0] = (r * (y * ka_ref[...] + ys * kb_ref[...])).astype(BF16)


def _mod_spec(d, batched):
    if batched:
        return pl.BlockSpec((1, 1, d), lambda b, s: (b, 0, 0))
    return pl.BlockSpec((1, 1, d), lambda b, s: (0, 0, 0))


def _const_spec(shape):
    nd = len(shape)
    return pl.BlockSpec(shape, lambda b, s: (0,) * nd)


def _inproj_even(x, sc, sh, w, tables, ts):
    bsz, s, d = x.shape
    batched = sc.shape[0] > 1
    qa, qb, ka, kb = tables
    tok = lambda n: pl.BlockSpec((1, ts, n), lambda b, i: (b, i, 0))
    nt = qa.shape[0] // ts
    tab = lambda n: pl.BlockSpec((ts, n), lambda b, i: (i % nt, 0))
    widths = (512, 512, 512, 512, 256, 256)
    return pl.pallas_call(
        _inproj_even_kernel,
        grid=(bsz, s // ts),
        in_specs=[tok(d), _mod_spec(d, batched), _mod_spec(d, batched), _const_spec(w.shape),
                  _const_spec((512, 512)), _const_spec((256, 256)),
                  tab(512), tab(512), tab(256), tab(256)],
        out_specs=[tok(n) for n in widths],
        out_shape=[jax.ShapeDtypeStruct((bsz, s, n), BF16) for n in widths],
        compiler_params=_cparams(2),
    )(x, sc, sh, w, _block_ones(512, 64), _block_ones(256, 64), qa, qb, ka, kb)


O_DQ, O_DK, O_DV, O_CQ, O_CKV, O_PE, O_END = (0, 512, 1024, 1536, 1792, 1920, 2048)


def _odd_weights(w_in, w_uq, w_ukv):
    dq, dk, dv, cq, ckv, kpe = jnp.split(w_in, [512, 1024, 1536, 1792, 1920], axis=-1)
    d = w_in.shape[0]
    pe_slot = jnp.concatenate([kpe, _swap_cols(kpe, 32), jnp.zeros((d, 64), F32)], axis=-1)
    w = jnp.concatenate([dq, dk, dv, cq, ckv, pe_slot], axis=-1).astype(BF16)
    uq = w_uq.reshape(MLA_Q_RANK, MLA_HEADS, MLA_QK)
    z32 = jnp.zeros((MLA_Q_RANK, MLA_HEADS, 32), F32)
    z64 = jnp.zeros((MLA_Q_RANK, MLA_HEADS, 64), F32)
    uq_pad = jnp.concatenate([uq, z32], axis=-1).reshape(MLA_Q_RANK, MLA_HEADS * LANES)
    uq_sw = jnp.concatenate([z64, _swap_cols(uq[..., MLA_NOPE:], 32), z32],
                            axis=-1).reshape(MLA_Q_RANK, MLA_HEADS * LANES)
    wuq2 = jnp.concatenate([uq_pad, uq_sw], axis=-1).astype(BF16)
    ukv = w_ukv.reshape(MLA_KV_RANK, MLA_HEADS, MLA_NOPE + MLA_V)
    zk = jnp.zeros((MLA_KV_RANK, MLA_HEADS, 64), F32)
    wk_pad = jnp.concatenate([ukv[..., :MLA_NOPE], zk], axis=-1).reshape(MLA_KV_RANK, MLA_HEADS * LANES)
    place = np.zeros((LANES, MLA_HEADS, LANES), np.float32)
    for j in range(MLA_ROPE):
        place[j, :, MLA_NOPE + j] = 1.0
        place[MLA_ROPE + j, :, MLA_NOPE + j] = 1.0
    wk2 = jnp.concatenate([wk_pad, jnp.asarray(place.reshape(LANES, MLA_HEADS * LANES))],
                          axis=0).astype(BF16)
    wv = ukv[..., MLA_NOPE:].reshape(MLA_KV_RANK, MLA_HEADS * MLA_V).astype(BF16)
    return w, wuq2, wk2, wv


def _odd_tables(s, rope):
    m_scale = MLA_QK ** -0.5 * LOG2E
    if rope:
        cos64, sin64 = _rope_tables(s, 64)
        cos32, sin32 = _rope_tables(s, 32)
    else:
        cos64, sin64 = jnp.ones((s, 64), F32), jnp.zeros((s, 64), F32)
        cos32, sin32 = jnp.ones((s, 32), F32), jnp.zeros((s, 32), F32)
    one64, z32, z64 = jnp.ones((s, 64), F32), jnp.zeros((s, 32), F32), jnp.zeros((s, 64), F32)
    dcos, dsin = jnp.tile(cos64, (1, 8)), jnp.tile(sin64, (1, 8))
    qa = jnp.tile(jnp.concatenate([one64, cos32, z32], axis=-1) * m_scale, (1, MLA_HEADS))
    qb = jnp.tile(jnp.concatenate([z64, sin32, z32], axis=-1) * m_scale, (1, MLA_HEADS))
    pe = jnp.concatenate([cos32, sin32, z64], axis=-1)
    return dcos, dsin, qa, qb, pe


def _inproj_odd_kernel(x_ref, sc_ref, sh_ref, w_ref, wuq_ref, wk_ref, wv_ref, qg_ref, kvg_ref,
                       dcos_ref, dsin_ref, qa_ref, qb_ref, pe_ref,
                       dq_ref, dk_ref, dv_ref, mq_ref, mk_ref, mv_ref):
    h = (x_ref[0] * (1.0 + sc_ref[0]) + sh_ref[0]).astype(BF16)
    dcos, dsin = dcos_ref[...], dsin_ref[...]
    y = _dot(h, w_ref[:, O_DQ:O_DK])
    ys = _rope_partner(y, HEAD_DIM)
    dq_ref[0] = ((y * dcos + ys * dsin) * (0.125 * LOG2E)).astype(BF16)
    y = _dot(h, w_ref[:, O_DK:O_DV])
    ys = _rope_partner(y, HEAD_DIM)
    dk_ref[0] = (y * dcos + ys * dsin).astype(BF16)
    dv_ref[0] = _dot(h, w_ref[:, O_DV:O_CQ]).astype(BF16)
    cq = _dot(h, w_ref[:, O_CQ:O_CKV])
    nq = cq * lax.rsqrt(jnp.mean(cq * cq, axis=-1, keepdims=True) + RMS_EPS) * qg_ref[...]
    y2 = _dot(nq.astype(BF16), wuq_ref[...])
    half = MLA_HEADS * LANES
    mq_ref[0] = (y2[:, :half] * qa_ref[...] + y2[:, half:] * qb_ref[...]).astype(BF16)
    ckv = _dot(h, w_ref[:, O_CKV:O_PE])
    nk = ckv * lax.rsqrt(jnp.mean(ckv * ckv, axis=-1, keepdims=True) + RMS_EPS) * kvg_ref[...]
    pe = _dot(h, w_ref[:, O_PE:O_END]) * pe_ref[...]
    nkb = nk.astype(BF16)
    cat = jnp.concatenate([nkb, pe.astype(BF16)], axis=-1)
    mk_ref[0] = _dot(cat, wk_ref[...]).astype(BF16)
    mv_ref[0] = _dot(nkb, wv_ref[...]).astype(BF16)


def _inproj_odd(x, sc, sh, weights, gains, tables, ts):
    bsz, s, d = x.shape
    batched = sc.shape[0] > 1
    w, wuq2, wk2, wv = weights
    qg, kvg = gains
    dcos, dsin, qa, qb, pe = tables
    tok = lambda n: pl.BlockSpec((1, ts, n), lambda b, i: (b, i, 0))
    nt = dcos.shape[0] // ts
    tab = lambda n: pl.BlockSpec((ts, n), lambda b, i: (i % nt, 0))
    widths = (512, 512, 512, 1024, 1024, 512)
    return pl.pallas_call(
        _inproj_odd_kernel,
        grid=(bsz, s // ts),
        in_specs=[tok(d), _mod_spec(d, batched), _mod_spec(d, batched), _const_spec(w.shape),
                  _const_spec(wuq2.shape), _const_spec(wk2.shape), _const_spec(wv.shape),
                  _const_spec((1, MLA_Q_RANK)), _const_spec((1, MLA_KV_RANK)),
                  tab(512), tab(512), tab(1024), tab(1024), tab(128)],
        out_specs=[tok(n) for n in widths],
        out_shape=[jax.ShapeDtypeStruct((bsz, s, n), BF16) for n in widths],
        compiler_params=_cparams(2),
    )(x, sc, sh, w, wuq2, wk2, wv, qg.reshape(1, -1), kvg.reshape(1, -1), dcos, dsin, qa, qb, pe)


def _half_masks():
    lane = lax.broadcasted_iota(jnp.int32, (1, LANES), 1)
    return lane < HEAD_DIM, lane >= HEAD_DIM


def _softmax_values(ss, vs):
    m = functools.reduce(jnp.maximum, [jnp.max(s, axis=-1, keepdims=True) for s in ss])
    es = [jnp.exp2(s - m) for s in ss]
    l = functools.reduce(lambda a, b: a + b, [jnp.sum(e, axis=-1, keepdims=True) for e in es])
    o = functools.reduce(lambda a, b: a + b, [_dot(e.astype(BF16), v) for e, v in zip(es, vs)])
    return o * (1.0 / l)


def _attend(qm, ks, vs):
    return _softmax_values([_dot_nt(qm, k) for k in ks], vs)


NA_RB = 4
NA_WIN_ROWS = NA_RB + NA_WIN_H - 1
NA_SUB = 2


def _na_block_plan(rows):
    plan = []
    for blk in range(rows // NA_RB):
        r0 = blk * NA_RB
        rs = [int(np.clip(r0 + i - NA_WIN_H // 2, 0, rows - NA_WIN_H)) for i in range(NA_RB)]
        ws = int(np.clip(r0 - NA_WIN_H // 2, 0, rows - NA_WIN_ROWS))
        pat = tuple((rs[i] - ws, r0 + i - ws) for i in range(NA_RB))
        assert all(0 <= o and o + NA_WIN_H <= NA_WIN_ROWS for o, _ in pat)
        plan.append(pat)
    assert all(p == plan[1] for p in plan[1:-1])
    return (plan[0], plan[1], plan[-1])


def _na_bias_table(rpb, rows):
    cols = np.arange(GRID_W)
    col_start = np.clip(cols - NA_WIN_W // 2, 0, GRID_W - NA_WIN_W)
    col_mask = (cols[None, :] >= col_start[:, None]) & (cols[None, :] < col_start[:, None] + NA_WIN_W)
    col_idx = np.clip(cols[None, :] - cols[:, None] + NA_WIN_W - 1, 0, 2 * NA_WIN_W - 2)
    wr = np.arange(NA_WIN_ROWS)
    tables = []
    for pat in _na_block_plan(rows):
        off = np.array([o for o, _ in pat])[:, None]
        rq = np.array([r for _, r in pat])[:, None]
        row_ok = (wr[None, :] >= off) & (wr[None, :] < off + NA_WIN_H)
        ridx = np.clip(wr[None, :] - rq + NA_WIN_H - 1, 0, 2 * NA_WIN_H - 2)
        t = rpb.astype(F32)[:, ridx]
        t = t[..., col_idx]
        t = t.transpose(0, 1, 3, 2, 4)
        ok = row_ok[None, :, None, :, None] & col_mask[None, None, :, None, :]
        t = jnp.where(ok, t * LOG2E, NEG_INF)
        tables.append(t.reshape(NA_HEADS, NA_RB * GRID_W, NA_WIN_ROWS * GRID_W))
    return jnp.stack(tables)


def _na_kernel(rows, sub, q_ref, k_ref, v_ref, kc_ref, vc_ref, *refs):
    bt_refs, o_ref = refs[:sub], refs[sub]
    step = pl.program_id(1)
    tq = NA_RB * GRID_W
    m0, m1 = _half_masks()
    for u in range(sub):
        r0 = (step * sub + u) * NA_RB
        ws = jnp.clip(r0 - NA_WIN_H // 2, 0, rows - NA_WIN_ROWS)
        win = pl.ds(pl.multiple_of(ws * GRID_W, GRID_W), NA_WIN_ROWS * GRID_W)
        qrows = slice(u * tq, (u + 1) * tq)
        for j in range(NA_HEADS // 2):
            sl = slice(j * LANES, (j + 1) * LANES)
            qp = q_ref[0, qrows, sl]
            kp, vp = k_ref[0, win, sl], v_ref[0, win, sl]
            kcp, vcp = kc_ref[0, :, sl], vc_ref[0, :, sl]
            outs = []
            for par, msk in ((0, m0), (1, m1)):
                qm = jnp.where(msk, qp, jnp.zeros_like(qp))
                s_loc = _dot_nt(qm, kp) + bt_refs[u][0, 2 * j + par]
                outs.append(_softmax_values([s_loc, _dot_nt(qm, kcp)], [vp, vcp]))
            o_ref[0, qrows, sl] = jnp.where(m0, outs[0], outs[1]).astype(BF16)


def _na_attention(q, k, v, kc, vc, rpb):
    bsz, s, w = q.shape
    l = kc.shape[1]
    rows = s // GRID_W
    nblk = rows // NA_RB
    sub = NA_SUB
    nstep = nblk // sub
    assert rows % NA_RB == 0 and rows >= NA_WIN_ROWS and nblk % sub == 0 and nblk >= 3
    bt = _na_bias_table(rpb, rows)
    tq = NA_RB * GRID_W * sub
    full = lambda n: pl.BlockSpec((1, n, w), lambda b, r: (b, 0, 0))

    def bt_spec(u):
        def kind(b, r):
            blk = r * sub + u
            return ((blk > 0).astype(jnp.int32) + (blk == nblk - 1).astype(jnp.int32), 0, 0, 0)
        return pl.BlockSpec((1,) + bt.shape[1:], kind)

    return pl.pallas_call(
        functools.partial(_na_kernel, rows, sub),
        grid=(bsz, nstep),
        in_specs=[pl.BlockSpec((1, tq, w), lambda b, r: (b, r, 0)),
                  full(s), full(s), full(l), full(l)] + [bt_spec(u) for u in range(sub)],
        out_specs=pl.BlockSpec((1, tq, w), lambda b, r: (b, r, 0)),
        out_shape=jax.ShapeDtypeStruct((bsz, s, w), BF16),
        compiler_params=_cparams(2),
    )(q, k, v, kc, vc, *([bt] * sub))


def _slot_attn_kernel(n_heads, q_slot, has_lat, sub, *refs):
    if has_lat:
        q_ref, k_ref, v_ref, kc_ref, vc_ref, o_ref = refs
    else:
        q_ref, kc_ref, vc_ref, o_ref = refs
    m0, m1 = _half_masks()
    tq = q_ref.shape[1] // sub
    for u in range(sub):
        qrows = slice(u * tq, (u + 1) * tq)
        for j in range(n_heads // 2):
            vsl = slice(j * LANES, (j + 1) * LANES)
            outs = []
            for par, msk in ((0, m0), (1, m1)):
                h = 2 * j + par
                if q_slot:
                    ksl = slice(h * LANES, (h + 1) * LANES)
                    qm = q_ref[0, qrows, ksl]
                else:
                    ksl = vsl
                    qp = q_ref[0, qrows, vsl]
                    qm = jnp.where(msk, qp, jnp.zeros_like(qp))
                ks, vs = [kc_ref[0, :, ksl]], [vc_ref[0, :, vsl]]
                if has_lat:
                    ks.insert(0, k_ref[0, :, ksl])
                    vs.insert(0, v_ref[0, :, vsl])
                outs.append(_attend(qm, ks, vs))
            o_ref[0, qrows, vsl] = jnp.where(m0, outs[0], outs[1]).astype(BF16)


def _slot_attention(q, k, v, kc, vc, n_heads, q_slot, tq, sub=1):
    bsz, sq, wq = q.shape
    l, wk, wv = kc.shape[1], kc.shape[2], vc.shape[2]
    has_lat = k is not None
    qspec = pl.BlockSpec((1, tq, wq), lambda b, i: (b, i, 0))
    full = lambda n, w: pl.BlockSpec((1, n, w), lambda b, i: (b, 0, 0))
    in_specs, args = [qspec], [q]
    if has_lat:
        s = k.shape[1]
        in_specs += [full(s, wk), full(s, wv)]
        args += [k, v]
    in_specs += [full(l, wk), full(l, wv)]
    args += [kc, vc]
    return pl.pallas_call(
        functools.partial(_slot_attn_kernel, n_heads, q_slot, has_lat, sub),
        grid=(bsz, sq // tq),
        in_specs=in_specs,
        out_specs=pl.BlockSpec((1, tq, wv), lambda b, i: (b, i, 0)),
        out_shape=jax.ShapeDtypeStruct((bsz, sq, wv), BF16),
        compiler_params=_cparams(2),
    )(*args)


def _gqa_kernel(has_lat, tq, sub, *refs):
    if has_lat:
        q_ref, k_ref, v_ref, kc_ref, vc_ref, o_ref = refs
    else:
        q_ref, kc_ref, vc_ref, o_ref = refs
    masks = _half_masks()
    for u in range(sub):
        qrows = slice(u * tq, (u + 1) * tq)
        res = {}
        for g in range(GQA_KV_HEADS):
            for var in range(2):
                par = g if var == 0 else 1 - g
                heads = (4 * g + par, 4 * g + 2 + par)
                vsl = slice(var * LANES, (var + 1) * LANES)
                qs = []
                for h in heads:
                    qp = q_ref[0, qrows, (h // 2) * LANES:(h // 2 + 1) * LANES]
                    qs.append(jnp.where(masks[par], qp, jnp.zeros_like(qp)))
                qm = jnp.concatenate(qs, axis=0)
                ks, vs = [kc_ref[0, :, vsl]], [vc_ref[0, :, vsl]]
                if has_lat:
                    ks.insert(0, k_ref[0, :, vsl])
                    vs.insert(0, v_ref[0, :, vsl])
                o = _attend(qm, ks, vs)
                res[heads[0]] = o[:tq]
                res[heads[1]] = o[tq:]
        for j in range(GQA_HEADS // 2):
            o_ref[0, qrows, j * LANES:(j + 1) * LANES] = jnp.where(
                masks[0], res[2 * j], res[2 * j + 1]).astype(BF16)


def _gqa_attention(q, k2, v2, k2c, v2c, tq, sub=1):
    bsz, sq, wq = q.shape
    l = k2c.shape[1]
    has_lat = k2 is not None
    full = lambda n: pl.BlockSpec((1, n, 2 * LANES), lambda b, i: (b, 0, 0))
    in_specs, args = [pl.BlockSpec((1, tq * sub, wq), lambda b, i: (b, i, 0))], [q]
    if has_lat:
        in_specs += [full(k2.shape[1])] * 2
        args += [k2, v2]
    in_specs += [full(l)] * 2
    args += [k2c, v2c]
    return pl.pallas_call(
        functools.partial(_gqa_kernel, has_lat, tq, sub),
        grid=(bsz, sq // (tq * sub)),
        in_specs=in_specs,
        out_specs=pl.BlockSpec((1, tq * sub, wq), lambda b, i: (b, i, 0)),
        out_shape=jax.ShapeDtypeStruct((bsz, sq, wq), BF16),
        compiler_params=_cparams(2),
    )(*args)


def _diff_kernel(has_lat, tq, sub, lam_init, *refs):
    if has_lat:
        q_ref, k_ref, v_ref, kc_ref, vc_ref, lq1, lk1, lq2, lk2, sub_ref, o_ref = refs
    else:
        q_ref, kc_ref, vc_ref, lq1, lk1, lq2, lk2, sub_ref, o_ref = refs
    lam = (jnp.exp(jnp.sum(lq1[...] * lk1[...], axis=-1, keepdims=True))
           - jnp.exp(jnp.sum(lq2[...] * lk2[...], axis=-1, keepdims=True)) + lam_init)
    m0, m1 = _half_masks()
    for u in range(sub):
        qrows = slice(u * tq, (u + 1) * tq)
        for h in range(DIFF_HEADS):
            sl = slice(h * LANES, (h + 1) * LANES)
            qp = q_ref[0, qrows, sl]
            zero = jnp.zeros_like(qp)
            qm = jnp.concatenate([jnp.where(m0, qp, zero), jnp.where(m1, qp, zero)], axis=0)
            ks, vs = [kc_ref[0, :, sl]], [vc_ref[0, :, sl]]
            if has_lat:
                ks.insert(0, k_ref[0, :, sl])
                vs.insert(0, v_ref[0, :, sl])
            o2 = _attend(qm, ks, vs)
            o = o2[:tq] - lam * o2[tq:]
            o = o * lax.rsqrt(jnp.mean(o * o, axis=-1, keepdims=True) + RMS_EPS) * sub_ref[...]
            o_ref[0, qrows, sl] = (o * (1.0 - lam_init)).astype(BF16)


def _diff_attention(q, k, v, kc, vc, lams, subln, lam_init, tq, sub=1):
    bsz, sq, w = q.shape
    l = kc.shape[1]
    has_lat = k is not None
    full = lambda n: pl.BlockSpec((1, n, w), lambda b, i: (b, 0, 0))
    in_specs, args = [pl.BlockSpec((1, tq * sub, w), lambda b, i: (b, i, 0))], [q]
    if has_lat:
        in_specs += [full(k.shape[1])] * 2
        args += [k, v]
    in_specs += [full(l)] * 2 + [_const_spec((1, HEAD_DIM))] * 4 + [_const_spec((1, LANES))]
    args += [kc, vc] + [a.reshape(1, -1).astype(F32) for a in lams] + [subln.reshape(1, -1).astype(F32)]
    return pl.pallas_call(
        functools.partial(_diff_kernel, has_lat, tq, sub, lam_init),
        grid=(bsz, sq // (tq * sub)),
        in_specs=in_specs,
        out_specs=pl.BlockSpec((1, tq * sub, w), lambda b, i: (b, i, 0)),
        out_shape=jax.ShapeDtypeStruct((bsz, sq, w), BF16),
        compiler_params=_cparams(2),
    )(*args)


def _pack_bf16_pairs(v):
    w = v.shape[1] // 2
    hi = pltpu.bitcast(v[:, :w].astype(F32), jnp.int32)
    lo = pltpu.bitcast(v[:, w:].astype(F32), jnp.int32)
    return hi | lax.shift_right_logical(lo, 16)


def _unpack_bf16_pairs(u):
    hi = pltpu.bitcast(u & jnp.int32(-65536), F32)
    lo = pltpu.bitcast(lax.shift_left(u, 16), F32)
    return hi, lo


def _layer_norm(z, g, b):
    mu = jnp.mean(z, axis=-1, keepdims=True)
    zc = z - mu
    var = jnp.mean(zc * zc, axis=-1, keepdims=True)
    return zc * lax.rsqrt(var + LN_EPS) * g + b


def _outproj_kernel(alpha, o1_ref, o2_ref, w1_ref, w2_ref, x_ref, g1_ref, lng_ref, lnb_ref,
                    sc2_ref, sh2_ref, rwh_ref, rwl_ref, rb_ref, tri_ref, cnt0_ref,
                    xo_ref, h2a_ref, h2b_ref, idx_ref, rank_ref, gw_ref, cnt_ref, carry_ref):
    @pl.when(jnp.logical_and(pl.program_id(0) == 0, pl.program_id(1) == 0))
    def _():
        carry_ref[...] = cnt0_ref[...]

    carry = carry_ref[...]
    o = _dot(o1_ref[0], w1_ref[...]) + _dot(o2_ref[0], w2_ref[...])
    xn = _layer_norm(alpha * x_ref[0] + g1_ref[0] * o, lng_ref[...], lnb_ref[...])
    xo_ref[0] = xn
    h2 = xn * (1.0 + sc2_ref[0]) + sh2_ref[0]
    hi = h2.astype(BF16)
    packed = _pack_bf16_pairs(hi)
    q = packed.shape[1] // 2
    h2a_ref[0] = packed[:, :q]
    h2b_ref[0] = packed[:, q:]
    lo = (h2 - hi.astype(F32)).astype(BF16)
    logits = (_dot(hi, rwh_ref[...]) + _dot(lo, rwh_ref[...]) + _dot(hi, rwl_ref[...])) + rb_ref[...]
    lane = lax.broadcasted_iota(jnp.int32, logits.shape, 1).astype(F32)
    vals, idxs = [], []
    cur = logits
    for _ in range(TOP_K):
        m = jnp.max(cur, axis=-1, keepdims=True)
        ik = jnp.min(jnp.where(cur == m, lane, float(LANES)), axis=-1, keepdims=True)
        vals.append(m)
        idxs.append(ik)
        cur = jnp.where(lane == ik, -jnp.inf, cur)
    ws = [jnp.exp(v - vals[0]) for v in vals]
    inv = 1.0 / functools.reduce(lambda a, b: a + b, ws)
    sel = [lane == ik for ik in idxs]
    onehot = functools.reduce(lambda a, b: a + b, [m.astype(F32) for m in sel])
    before = _dot(tri_ref[...], onehot.astype(BF16)) + carry
    idx_out = jnp.zeros_like(logits)
    rank_out = jnp.zeros_like(logits)
    w_out = jnp.zeros_like(logits)
    for k in range(TOP_K):
        rk = jnp.sum(jnp.where(sel[k], before, 0.0), axis=-1, keepdims=True)
        idx_out = jnp.where(lane == float(k), idxs[k], idx_out)
        rank_out = jnp.where(lane == float(k), rk, rank_out)
        w_out = jnp.where(lane == float(k), ws[k] * inv, w_out)
    idx_ref[0] = idx_out.astype(jnp.int32)
    rank_ref[0] = rank_out.astype(jnp.int32)
    gw_ref[0] = w_out
    carry = carry + jnp.sum(onehot, axis=0, keepdims=True)
    carry_ref[...] = carry
    cnt_ref[...] = carry


def _outproj_ln_router(o1, o2, w_out, x, g1, lng, lnb, sc2, sh2, router, cnt0, alpha, ts):
    bsz, s, d = x.shape
    batched = g1.shape[0] > 1
    rwh, rwl, rb = router
    w1, w2 = w_out[:512].astype(BF16), w_out[512:].astype(BF16)
    tri = jnp.asarray(np.tril(np.ones((ts, ts), np.float32), -1), dtype=BF16)
    tok = lambda n: pl.BlockSpec((1, ts, n), lambda b, i: (b, i, 0))
    ms = _mod_spec(d, batched)
    return pl.pallas_call(
        functools.partial(_outproj_kernel, alpha),
        grid=(bsz, s // ts),
        in_specs=[tok(512), tok(512), _const_spec((512, d)), _const_spec((512, d)), tok(d), ms,
                  _const_spec((1, d)), _const_spec((1, d)), ms, ms,
                  _const_spec((d, LANES)), _const_spec((d, LANES)), _const_spec((1, LANES)),
                  _const_spec((ts, ts)), _const_spec((1, LANES))],
        out_specs=[tok(d), tok(d // 4), tok(d // 4), tok(LANES), tok(LANES), tok(LANES),
                   _const_spec((1, LANES))],
        out_shape=[jax.ShapeDtypeStruct((bsz, s, d), F32),
                   jax.ShapeDtypeStruct((bsz, s, d // 4), jnp.int32),
                   jax.ShapeDtypeStruct((bsz, s, d // 4), jnp.int32),
                   jax.ShapeDtypeStruct((bsz, s, LANES), jnp.int32),
                   jax.ShapeDtypeStruct((bsz, s, LANES), jnp.int32),
                   jax.ShapeDtypeStruct((bsz, s, LANES), F32),
                   jax.ShapeDtypeStruct((1, LANES), F32)],
        scratch_shapes=[pltpu.VMEM((1, LANES), F32)],
        compiler_params=_cparams(2),
    )(o1, o2, w1, w2, x, g1, lng.reshape(1, d), lnb.reshape(1, d), sc2, sh2, rwh, rwl, rb, tri, cnt0)


def _router_weights(router_w, router_b):
    d, e = router_w.shape
    wp = jnp.zeros((d, LANES), F32).at[:, :e].set(router_w)
    hi = wp.astype(BF16)
    lo = (wp - hi.astype(F32)).astype(BF16)
    rb = jnp.full((1, LANES), -jnp.inf, F32).at[0, :e].set(router_b)
    return hi, lo, rb


def _deinterleave_perm():
    p = np.zeros((GU_BLOCK, GU_BLOCK), np.float32)
    m = np.arange(GU_BLOCK // 2)
    p[2 * m, m] = 1.0
    p[2 * m + 1, GU_BLOCK // 2 + m] = 1.0
    return jnp.asarray(p, dtype=BF16)


def _ffn_kernel(te_ref, tv_ref, xa_ref, xb_ref, wgu_ref, bgu_ref, wd_ref, bd_ref, perm_ref, y_ref,
                wgu_s, wd_s):
    j = pl.program_id(0)
    n_blk = wgu_s.shape[1] // GU_BLOCK
    half = GU_BLOCK // 2

    @pl.when(jnp.logical_or(j == 0, te_ref[j] != te_ref[jnp.maximum(j - 1, 0)]))
    def _():
        for b in range(n_blk):
            sl = slice(b * GU_BLOCK, (b + 1) * GU_BLOCK)
            wgu_s[:, sl] = _dot(wgu_ref[0, 0, :, sl].astype(BF16), perm_ref[...]).astype(BF16)
        wd_s[...] = wd_ref[0, 0].astype(BF16)

    @pl.when(tv_ref[j] > 0)
    def _():
        a_hi, a_lo = _unpack_bf16_pairs(xa_ref[...])
        b_hi, b_lo = _unpack_bf16_pairs(xb_ref[...])
        x = jnp.concatenate([a_hi, b_hi, a_lo, b_lo], axis=-1).astype(BF16)
        acts = []
        for b in range(n_blk):
            sl = slice(b * GU_BLOCK, (b + 1) * GU_BLOCK)
            gu = _dot(x, wgu_s[:, sl]) + bgu_ref[0, :, sl]
            glu = jnp.minimum(gu[:, :half], SWIGLU_LIMIT)
            lin = jnp.clip(gu[:, half:], -SWIGLU_LIMIT, SWIGLU_LIMIT)
            acts.append(((lin + 1.0) * (glu * jax.nn.sigmoid(SWIGLU_ALPHA * glu))).astype(BF16))
        a = jnp.concatenate(acts, axis=-1)
        y_ref[...] = (_dot(a, wd_s[...]) + bd_ref[0]).astype(BF16)

    @pl.when(tv_ref[j] == 0)
    def _():
        y_ref[...] = jnp.zeros_like(y_ref)


def _expert_ffn(xsa, xsb, tile_expert, tile_valid, layer, w_gu, b_gu, w_down, b_down, tm):
    p, q = xsa.shape
    d = 4 * q
    _, e, _, f2 = w_gu.shape
    f = f2 // 2
    half = GU_BLOCK // 2
    bgu = jnp.stack([b_gu[:, 0::2].reshape(e, f // half, half),
                     b_gu[:, 1::2].reshape(e, f // half, half)], axis=2).reshape(e, 1, f2)
    wspec = lambda a, b: pl.BlockSpec((1, 1, a, b), lambda j, te, tv: (layer, te[j], 0, 0))
    bspec = lambda b: pl.BlockSpec((1, 1, b), lambda j, te, tv: (te[j], 0, 0))
    return pl.pallas_call(
        _ffn_kernel,
        grid_spec=pltpu.PrefetchScalarGridSpec(
            num_scalar_prefetch=2,
            grid=(p // tm,),
            in_specs=[pl.BlockSpec((tm, q), lambda j, te, tv: (j, 0)),
                      pl.BlockSpec((tm, q), lambda j, te, tv: (j, 0)),
                      wspec(d, f2), bspec(f2), wspec(f, d), bspec(d),
                      pl.BlockSpec((GU_BLOCK, GU_BLOCK), lambda j, te, tv: (0, 0))],
            out_specs=pl.BlockSpec((tm, d), lambda j, te, tv: (j, 0)),
            scratch_shapes=[pltpu.VMEM((d, f2), BF16), pltpu.VMEM((f, d), BF16)]),
        out_shape=jax.ShapeDtypeStruct((p, d), BF16),
        compiler_params=_cparams(1),
    )(tile_expert, tile_valid, xsa, xsb, w_gu, bgu, w_down, b_down.reshape(e, 1, d),
      _deinterleave_perm())


def _moe_layout(counts, t_total, tm):
    n_experts = counts.shape[0]
    pc = ((counts + tm - 1) // tm) * tm
    pend = jnp.cumsum(pc)
    pstart = pend - pc
    p = t_total * TOP_K + n_experts * tm
    tile_start = jnp.arange(p // tm, dtype=jnp.int32) * tm
    tile_expert = jnp.minimum(jnp.sum((tile_start[:, None] >= pend[None, :]).astype(jnp.int32), axis=1),
                              n_experts - 1).astype(jnp.int32)
    tile_valid = (tile_start < pend[-1]).astype(jnp.int32)
    return pstart, p, tile_expert, tile_valid


SC_WINDOW = 128


def _sc_dispatch(parts, p):
    w = parts[0][0].shape[1]
    k = parts[0][1].shape[0]
    mesh = plsc.VectorSubcoreMesh(core_axis_name="core", subcore_axis_name="subcore")

    @functools.partial(pl.kernel, out_type=jax.ShapeDtypeStruct((p, w), parts[0][0].dtype), mesh=mesh)
    def kern(*refs):
        o_hbm = refs[-1]

        def body(x_vmem, *i_vmems):
            for iv in i_vmems:
                pltpu.sync_copy(x_vmem, o_hbm.at[iv.at[0]])

        for n, (rows, _) in enumerate(parts):
            x_hbm, i_hbms = refs[n * (k + 1)], refs[n * (k + 1) + 1:(n + 1) * (k + 1)]
            pltpu.emit_pipeline(
                body,
                grid=(rows.shape[0] // SC_WINDOW,),
                in_specs=[pl.BlockSpec((SC_WINDOW, w), index_map=lambda i: (i, 0))]
                + [pl.BlockSpec((1, SC_WINDOW), index_map=lambda i: (0, i)) for _ in range(k)],
                out_specs=[],
                core_axis_name=("core", "subcore"),
                dimension_semantics=(pltpu.PARALLEL,),
            )(x_hbm, *i_hbms)

    args = []
    for rows, dpos_t in parts:
        args += [rows] + [dpos_t[kk:kk + 1] for kk in range(k)]
    return kern(*args)


def _combine_kernel(alpha, yg_ref, gw_ref, x_ref, g2_ref, lng_ref, lnb_ref, o_ref):
    gw = gw_ref[...]
    y = yg_ref[0].astype(F32) * gw[:, 0:1]
    for k in range(1, TOP_K):
        y = y + yg_ref[k].astype(F32) * gw[:, k:k + 1]
    o_ref[0] = _layer_norm(alpha * x_ref[0] + g2_ref[0] * y, lng_ref[...], lnb_ref[...])


def _combine_ln(yg, gw, x, g2, lng, lnb, alpha, yg_row0, ts):
    bsz, s, d = x.shape
    batched = g2.shape[0] > 1
    nblk = s // ts
    off = yg_row0 // ts
    return pl.pallas_call(
        functools.partial(_combine_kernel, alpha),
        grid=(bsz, nblk),
        in_specs=[pl.BlockSpec((TOP_K, ts, d), lambda b, i: (0, off + b * nblk + i, 0)),
                  pl.BlockSpec((ts, LANES), lambda b, i: (b * nblk + i, 0)),
                  pl.BlockSpec((1, ts, d), lambda b, i: (b, i, 0)),
                  _mod_spec(d, batched), _const_spec((1, d)), _const_spec((1, d))],
        out_specs=pl.BlockSpec((1, ts, d), lambda b, i: (b, i, 0)),
        out_shape=jax.ShapeDtypeStruct((bsz, s, d), F32),
        compiler_params=_cparams(2),
    )(yg, gw, x, g2, lng.reshape(1, d), lnb.reshape(1, d))


def kernel(x, c, ctx, c_ctx, mod_w, mod_b, ln1_g, ln1_b, ln2_g, ln2_b, even_w_in, even_w_out, na_rpb, gqa_q_gain, gqa_k_gain, odd_w_in, odd_w_out, diff_lq1, diff_lk1, diff_lq2, diff_lk2, diff_subln, mla_q_gain, mla_w_uq, mla_kv_gain, mla_w_ukv, router_w, router_b, exp_w_gu, exp_b_gu, exp_w_down, exp_b_down):
    bsz, s, d = x.shape
    l = ctx.shape[1]
    depth = mod_w.shape[0]
    n_experts = router_w.shape[-1]
    alpha = (2 * depth) ** 0.25
    ts = min(512, s)
    cts = min(512, bsz * l)
    tm = MOE_TILE

    pad = (-(bsz + 1)) % 8
    c_all = jnp.concatenate([c, c_ctx[None, :], jnp.zeros((pad, d), F32)], axis=0)
    mod = _modulation_all(c_all, mod_w, mod_b)

    for i in range(depth):
        last = i == depth - 1
        j = i // 2
        ml = [mod[i, :bsz, k * d:(k + 1) * d].reshape(bsz, 1, d) for k in range(6)]
        mc = [mod[i, bsz:bsz + 1, k * d:(k + 1) * d].reshape(1, 1, d) for k in range(6)]
        sh1, sc1, g1, sh2, sc2, g2 = ml
        csh1, csc1, cg1, csh2, csc2, cg2 = mc

        if i % 2 == 0:
            w = _even_weights(even_w_in[j])
            naq, nak, nav, gq, gk, gv = _inproj_even(
                x, sc1, sh1, w, _even_tables(s, gqa_q_gain[j], gqa_k_gain[j], True), ts)
            cnaq, cnak, cnav, cgq, cgk, cgv = [a.reshape(bsz, l, -1) for a in _inproj_even(
                ctx.reshape(1, bsz * l, d), csc1, csh1, w,
                _even_tables(cts, gqa_q_gain[j], gqa_k_gain[j], False), cts)]
            o1 = _na_attention(naq, nak, nav, cnak, cnav, na_rpb[j])
            o2 = _gqa_attention(gq, gk, gv, cgk, cgv, 256, ATTN_SUB)
            if not last:
                co1 = _slot_attention(cnaq, None, None, cnak, cnav, NA_HEADS, False, l)
                co2 = _gqa_attention(cgq, None, None, cgk, cgv, l)
            w_out = even_w_out[j]
        else:
            lam_init = 0.8 - 0.6 * math.exp(-0.3 * i)
            weights = _odd_weights(odd_w_in[j], mla_w_uq[j], mla_w_ukv[j])
            gains = (mla_q_gain[j], mla_kv_gain[j])
            dq, dk, dv, mq, mk, mv = _inproj_odd(x, sc1, sh1, weights, gains, _odd_tables(s, True), ts)
            cdq, cdk, cdv, cmq, cmk, cmv = [a.reshape(bsz, l, -1) for a in _inproj_odd(
                ctx.reshape(1, bsz * l, d), csc1, csh1, weights, gains, _odd_tables(cts, False), cts)]
            lams = (diff_lq1[j], diff_lk1[j], diff_lq2[j], diff_lk2[j])
            o1 = _diff_attention(dq, dk, dv, cdk, cdv, lams, diff_subln[j], lam_init, 256, ATTN_SUB)
            o2 = _slot_attention(mq, mk, mv, cmk, cmv, MLA_HEADS, True, 512)
            if not last:
                co1 = _diff_attention(cdq, None, None, cdk, cdv, lams, diff_subln[j], lam_init, l)
                co2 = _slot_attention(cmq, None, None, cmk, cmv, MLA_HEADS, True, l)
            w_out = odd_w_out[j]

        router = _router_weights(router_w[i], router_b[i])
        cnt0 = jnp.zeros((1, LANES), F32)
        x, h2a, h2b, ridx, rrank, rgw, cnt = _outproj_ln_router(
            o1, o2, w_out, x, g1, ln1_g[i], ln1_b[i], sc2, sh2, router, cnt0, alpha, ts)
        flat = lambda a: a.reshape(-1, a.shape[-1])
        routed = [[flat(a) for a in (h2a, h2b, ridx, rrank, rgw)]]
        if not last:
            one = lambda a: a.reshape(1, bsz * l, a.shape[-1])
            ctx, ch2a, ch2b, cidx, crank, cgw, cnt = _outproj_ln_router(
                one(co1), one(co2), w_out, one(ctx), cg1, ln1_g[i], ln1_b[i], csc2, csh2, router, cnt,
                alpha, cts)
            routed.append([flat(a) for a in (ch2a, ch2b, cidx, crank, cgw)])

        counts = cnt[0, :n_experts].astype(jnp.int32)
        t_total = sum(r[0].shape[0] for r in routed)
        pstart, p, tile_expert, tile_valid = _moe_layout(counts, t_total, tm)
        dpos = [(pstart[r[2][:, :TOP_K]] + r[3][:, :TOP_K]).T.astype(jnp.int32) for r in routed]
        xsa = _sc_dispatch([(r[0], dp) for r, dp in zip(routed, dpos)], p)
        xsb = _sc_dispatch([(r[1], dp) for r, dp in zip(routed, dpos)], p)
        ys = _expert_ffn(xsa, xsb, tile_expert, tile_valid, i, exp_w_gu, exp_b_gu[i],
                         exp_w_down, exp_b_down[i], tm)
        dpos_t = dpos[0] if last else jnp.concatenate(dpos, axis=1)
        yg = ys.at[dpos_t].get(mode="promise_in_bounds")

        x = _combine_ln(yg, routed[0][4], x, g2, ln2_g[i], ln2_b[i], alpha, 0, ts)
        if not last:
            ctx = _combine_ln(yg, routed[1][4], ctx, cg2, ln2_g[i], ln2_b[i], alpha, bsz * s,
                              cts).reshape(bsz, l, d)
    return x
```

```python
import functools
import math

import numpy as np
import jax
import jax.numpy as jnp
from jax import lax
from jax.experimental import pallas as pl
from jax.experimental.pallas import tpu as pltpu
from jax.experimental.pallas import tpu_sc as plsc

F32 = jnp.float32
BF16 = jnp.bfloat16

GRID_W = 64
HEAD_DIM = 64
ROPE_THETA = 10000.0
LN_EPS = 1e-6
RMS_EPS = 1e-6
NEG_INF = -1e30
NA_HEADS = 8
NA_WIN_H = 8
NA_WIN_W = 16
GQA_HEADS = 8
GQA_KV_HEADS = 2
DIFF_HEADS = 4
MLA_HEADS = 8
MLA_Q_RANK = 256
MLA_KV_RANK = 128
MLA_NOPE = 64
MLA_ROPE = 32
MLA_V = 64
MLA_QK = MLA_NOPE + MLA_ROPE
TOP_K = 4
SWIGLU_ALPHA = 1.702
SWIGLU_LIMIT = 7.0
LOG2E = 1.4426950408889634

LANES = 128
VMEM_LIMIT = 56 * 1024 * 1024
MOE_TILE = 1024
GU_BLOCK = 512
ATTN_SUB = 2


def _cparams(n_axes):
    return pltpu.CompilerParams(dimension_semantics=("arbitrary",) * n_axes,
                                vmem_limit_bytes=VMEM_LIMIT)


def _dot(a, b):
    return jnp.dot(a, b, preferred_element_type=F32)


def _dot_nt(a, b):
    return lax.dot_general(a, b, (((1,), (1,)), ((), ())), preferred_element_type=F32)


def _mod_kernel(c_ref, w_ref, b_ref, o_ref):
    cv = c_ref[...]
    a = (cv * jax.nn.sigmoid(cv)).astype(BF16)
    o_ref[0] = _dot(a, w_ref[0].astype(BF16)) + b_ref[0]


def _modulation_all(c_all, mod_w, mod_b):
    depth, d, n = mod_w.shape
    rows = c_all.shape[0]
    tn = 1536
    return pl.pallas_call(
        _mod_kernel,
        grid=(depth, n // tn),
        in_specs=[pl.BlockSpec((rows, d), lambda i, j: (0, 0)),
                  pl.BlockSpec((1, d, tn), lambda i, j: (i, 0, j)),
                  pl.BlockSpec((1, 1, tn), lambda i, j: (i, 0, j))],
        out_specs=pl.BlockSpec((1, rows, tn), lambda i, j: (i, 0, j)),
        out_shape=jax.ShapeDtypeStruct((depth, rows, n), F32),
        compiler_params=_cparams(2),
    )(c_all, mod_w, mod_b.reshape(depth, 1, n))


def _rope_tables(s, dim):
    pos = jnp.arange(s)
    row = (pos // GRID_W).astype(F32)[:, None]
    col = (pos % GRID_W).astype(F32)[:, None]
    quarter = dim // 4
    inv_freq = ROPE_THETA ** (-jnp.arange(quarter, dtype=F32) / quarter)
    ar, ac = row * inv_freq, col * inv_freq
    cos = jnp.concatenate([jnp.cos(ar), jnp.cos(ar), jnp.cos(ac), jnp.cos(ac)], axis=-1)
    sin = jnp.concatenate([-jnp.sin(ar), jnp.sin(ar), -jnp.sin(ac), jnp.sin(ac)], axis=-1)
    return cos, sin


def _swap_perm(dim):
    q = dim // 4
    idx = np.arange(dim)
    return np.where((idx % (2 * q)) < q, idx + q, idx - q)


def _rope_partner(y, dim):
    q = dim // 4
    lane = lax.broadcasted_iota(jnp.int32, (1, LANES), 1)
    first = (lane % (2 * q)) < q
    tiles = []
    for j in range(0, y.shape[1], LANES):
        t = y[:, j:j + LANES]
        tiles.append(jnp.where(first, pltpu.roll(t, LANES - q, 1), pltpu.roll(t, q, 1)))
    return tiles[0] if len(tiles) == 1 else jnp.concatenate(tiles, axis=-1)


def _swap_cols(w, dim):
    n = w.shape[-1] // dim
    perm = (np.arange(n)[:, None] * dim + _swap_perm(dim)[None, :]).reshape(-1)
    return w[..., perm]


def _block_ones(n, blk):
    i = np.arange(n) // blk
    return jnp.asarray((i[:, None] == i[None, :]).astype(np.float32), dtype=BF16)


E_NAQ, E_NAK, E_NAV, E_GQ, E_GK, E_GV, E_END = (0, 512, 1024, 1536, 2048, 2304, 2560)


def _even_weights(w_in):
    naq, nak, nav, gq, gk, gv = jnp.split(w_in, [512, 1024, 1536, 2048, 2176], axis=-1)
    k0, k1 = gk[:, :64], gk[:, 64:]
    v0, v1 = gv[:, :64], gv[:, 64:]
    gk2 = jnp.concatenate([k0, k1, k1, k0], axis=-1)
    gv2 = jnp.concatenate([v0, v1, v1, v0], axis=-1)
    w = jnp.concatenate([naq * (0.125 * LOG2E), nak, nav, gq, gk2, gv2], axis=-1).astype(BF16)
    return w


def _even_tables(s, q_gain, k_gain, rope):
    sw = _swap_perm(64)
    if rope:
        cos, sin = _rope_tables(s, 64)
    else:
        cos, sin = jnp.ones((s, 64), F32), jnp.zeros((s, 64), F32)
    qa = jnp.tile(cos * q_gain[None, :] * (0.125 * LOG2E), (1, 8))
    qb = jnp.tile(sin * q_gain[sw][None, :] * (0.125 * LOG2E), (1, 8))
    ka = jnp.tile(cos * k_gain[None, :], (1, 4))
    kb = jnp.tile(sin * k_gain[sw][None, :], (1, 4))
    return qa, qb, ka, kb


def _inproj_even_kernel(x_ref, sc_ref, sh_ref, w_ref, g512_ref, g256_ref,
                        qa_ref, qb_ref, ka_ref, kb_ref,
                        naq_ref, nak_ref, nav_ref, gq_ref, gk_ref, gv_ref):
    h = (x_ref[0] * (1.0 + sc_ref[0]) + sh_ref[0]).astype(BF16)
    naq_ref[0] = _dot(h, w_ref[:, E_NAQ:E_NAK]).astype(BF16)
    nak_ref[0] = _dot(h, w_ref[:, E_NAK:E_NAV]).astype(BF16)
    nav_ref[0] = _dot(h, w_ref[:, E_NAV:E_GQ]).astype(BF16)
    gv_ref[0] = _dot(h, w_ref[:, E_GV:E_END]).astype(BF16)
    y = _dot(h, w_ref[:, E_GQ:E_GK])
    ys = _rope_partner(y, HEAD_DIM)
    r = lax.rsqrt(_dot((y * y).astype(BF16), g512_ref[...]) * (1.0 / HEAD_DIM) + RMS_EPS)
    gq_ref[0] = (r * (y * qa_ref[...] + ys * qb_ref[...])).astype(BF16)
    y = _dot(h, w_ref[:, E_GK:E_GV])
    ys = _rope_partner(y, HEAD_DIM)
    r = lax.rsqrt(_dot((y * y).astype(BF16), g256_ref[...]) * (1.0 / HEAD_DIM) + RMS_EPS)
    gk_ref[0] = (r * (y * ka_ref[...] + ys * kb_ref[...])).astype(BF16)


def _mod_spec(d, batched):
    if batched:
        return pl.BlockSpec((1, 1, d), lambda b, s: (b, 0, 0))
    return pl.BlockSpec((1, 1, d), lambda b, s: (0, 0, 0))


def _const_spec(shape):
    nd = len(shape)
    return pl.BlockSpec(shape, lambda b, s: (0,) * nd)


def _inproj_even(x, sc, sh, w, tables, ts):
    bsz, s, d = x.shape
    batched = sc.shape[0] > 1
    qa, qb, ka, kb = tables
    tok = lambda n: pl.BlockSpec((1, ts, n), lambda b, i: (b, i, 0))
    nt = qa.shape[0] // ts
    tab = lambda n: pl.BlockSpec((ts, n), lambda b, i: (i % nt, 0))
    widths = (512, 512, 512, 512, 256, 256)
    return pl.pallas_call(
        _inproj_even_kernel,
        grid=(bsz, s // ts),
        in_specs=[tok(d), _mod_spec(d, batched), _mod_spec(d, batched), _const_spec(w.shape),
                  _const_spec((512, 512)), _const_spec((256, 256)),
                  tab(512), tab(512), tab(256), tab(256)],
        out_specs=[tok(n) for n in widths],
        out_shape=[jax.ShapeDtypeStruct((bsz, s, n), BF16) for n in widths],
        compiler_params=_cparams(2),
    )(x, sc, sh, w, _block_ones(512, 64), _block_ones(256, 64), qa, qb, ka, kb)


O_DQ, O_DK, O_DV, O_CQ, O_CKV, O_PE, O_END = (0, 512, 1024, 1536, 1792, 1920, 2048)


def _odd_weights(w_in, w_uq, w_ukv):
    dq, dk, dv, cq, ckv, kpe = jnp.split(w_in, [512, 1024, 1536, 1792, 1920], axis=-1)
    d = w_in.shape[0]
    pe_slot = jnp.concatenate([kpe, _swap_cols(kpe, 32), jnp.zeros((d, 64), F32)], axis=-1)
    w = jnp.concatenate([dq, dk, dv, cq, ckv, pe_slot], axis=-1).astype(BF16)
    uq = w_uq.reshape(MLA_Q_RANK, MLA_HEADS, MLA_QK)
    z32 = jnp.zeros((MLA_Q_RANK, MLA_HEADS, 32), F32)
    z64 = jnp.zeros((MLA_Q_RANK, MLA_HEADS, 64), F32)
    uq_pad = jnp.concatenate([uq, z32], axis=-1).reshape(MLA_Q_RANK, MLA_HEADS * LANES)
    uq_sw = jnp.concatenate([z64, _swap_cols(uq[..., MLA_NOPE:], 32), z32],
                            axis=-1).reshape(MLA_Q_RANK, MLA_HEADS * LANES)
    wuq2 = jnp.concatenate([uq_pad, uq_sw], axis=-1).astype(BF16)
    ukv = w_ukv.reshape(MLA_KV_RANK, MLA_HEADS, MLA_NOPE + MLA_V)
    zk = jnp.zeros((MLA_KV_RANK, MLA_HEADS, 64), F32)
    wk_pad = jnp.concatenate([ukv[..., :MLA_NOPE], zk], axis=-1).reshape(MLA_KV_RANK, MLA_HEADS * LANES)
    place = np.zeros((LANES, MLA_HEADS, LANES), np.float32)
    for j in range(MLA_ROPE):
        place[j, :, MLA_NOPE + j] = 1.0
        place[MLA_ROPE + j, :, MLA_NOPE + j] = 1.0
    wk2 = jnp.concatenate([wk_pad, jnp.asarray(place.reshape(LANES, MLA_HEADS * LANES))],
                          axis=0).astype(BF16)
    wv = ukv[..., MLA_NOPE:].reshape(MLA_KV_RANK, MLA_HEADS * MLA_V).astype(BF16)
    return w, wuq2, wk2, wv


def _odd_tables(s, rope):
    m_scale = MLA_QK ** -0.5 * LOG2E
    if rope:
        cos64, sin64 = _rope_tables(s, 64)
        cos32, sin32 = _rope_tables(s, 32)
    else:
        cos64, sin64 = jnp.ones((s, 64), F32), jnp.zeros((s, 64), F32)
        cos32, sin32 = jnp.ones((s, 32), F32), jnp.zeros((s, 32), F32)
    one64, z32, z64 = jnp.ones((s, 64), F32), jnp.zeros((s, 32), F32), jnp.zeros((s, 64), F32)
    dcos, dsin = jnp.tile(cos64, (1, 8)), jnp.tile(sin64, (1, 8))
    qa = jnp.tile(jnp.concatenate([one64, cos32, z32], axis=-1) * m_scale, (1, MLA_HEADS))
    qb = jnp.tile(jnp.concatenate([z64, sin32, z32], axis=-1) * m_scale, (1, MLA_HEADS))
    pe = jnp.concatenate([cos32, sin32, z64], axis=-1)
    return dcos, dsin, qa, qb, pe


def _inproj_odd_kernel(x_ref, sc_ref, sh_ref, w_ref, wuq_ref, wk_ref, wv_ref, qg_ref, kvg_ref,
                       dcos_ref, dsin_ref, qa_ref, qb_ref, pe_ref,
                       dq_ref, dk_ref, dv_ref, mq_ref, mk_ref, mv_ref):
    h = (x_ref[0] * (1.0 + sc_ref[0]) + sh_ref[0]).astype(BF16)
    dcos, dsin = dcos_ref[...], dsin_ref[...]
    y = _dot(h, w_ref[:, O_DQ:O_DK])
    ys = _rope_partner(y, HEAD_DIM)
    dq_ref[0] = ((y * dcos + ys * dsin) * (0.125 * LOG2E)).astype(BF16)
    y = _dot(h, w_ref[:, O_DK:O_DV])
    ys = _rope_partner(y, HEAD_DIM)
    dk_ref[0] = (y * dcos + ys * dsin).astype(BF16)
    dv_ref[0] = _dot(h, w_ref[:, O_DV:O_CQ]).astype(BF16)
    cq = _dot(h, w_ref[:, O_CQ:O_CKV])
    nq = cq * lax.rsqrt(jnp.mean(cq * cq, axis=-1, keepdims=True) + RMS_EPS) * qg_ref[...]
    y2 = _dot(nq.astype(BF16), wuq_ref[...])
    half = MLA_HEADS * LANES
    mq_ref[0] = (y2[:, :half] * qa_ref[...] + y2[:, half:] * qb_ref[...]).astype(BF16)
    ckv = _dot(h, w_ref[:, O_CKV:O_PE])
    nk = ckv * lax.rsqrt(jnp.mean(ckv * ckv, axis=-1, keepdims=True) + RMS_EPS) * kvg_ref[...]
    pe = _dot(h, w_ref[:, O_PE:O_END]) * pe_ref[...]
    nkb = nk.astype(BF16)
    cat = jnp.concatenate([nkb, pe.astype(BF16)], axis=-1)
    mk_ref[0] = _dot(cat, wk_ref[...]).astype(BF16)
    mv_ref[0] = _dot(nkb, wv_ref[...]).astype(BF16)


def _inproj_odd(x, sc, sh, weights, gains, tables, ts):
    bsz, s, d = x.shape
    batched = sc.shape[0] > 1
    w, wuq2, wk2, wv = weights
    qg, kvg = gains
    dcos, dsin, qa, qb, pe = tables
    tok = lambda n: pl.BlockSpec((1, ts, n), lambda b, i: (b, i, 0))
    nt = dcos.shape[0] // ts
    tab = lambda n: pl.BlockSpec((ts, n), lambda b, i: (i % nt, 0))
    widths = (512, 512, 512, 1024, 1024, 512)
    return pl.pallas_call(
        _inproj_odd_kernel,
        grid=(bsz, s // ts),
        in_specs=[tok(d), _mod_spec(d, batched), _mod_spec(d, batched), _const_spec(w.shape),
                  _const_spec(wuq2.shape), _const_spec(wk2.shape), _const_spec(wv.shape),
                  _const_spec((1, MLA_Q_RANK)), _const_spec((1, MLA_KV_RANK)),
                  tab(512), tab(512), tab(1024), tab(1024), tab(128)],
        out_specs=[tok(n) for n in widths],
        out_shape=[jax.ShapeDtypeStruct((bsz, s, n), BF16) for n in widths],
        compiler_params=_cparams(2),
    )(x, sc, sh, w, wuq2, wk2, wv, qg.reshape(1, -1), kvg.reshape(1, -1), dcos, dsin, qa, qb, pe)


def _half_masks():
    lane = lax.broadcasted_iota(jnp.int32, (1, LANES), 1)
    return lane < HEAD_DIM, lane >= HEAD_DIM


def _softmax_values(ss, vs):
    m = functools.reduce(jnp.maximum, [jnp.max(s, axis=-1, keepdims=True) for s in ss])
    es = [jnp.exp2(s - m) for s in ss]
    l = functools.reduce(lambda a, b: a + b, [jnp.sum(e, axis=-1, keepdims=True) for e in es])
    o = functools.reduce(lambda a, b: a + b, [_dot(e.astype(BF16), v) for e, v in zip(es, vs)])
    return o * (1.0 / l)


def _attend(qm, ks, vs):
    return _softmax_values([_dot_nt(qm, k) for k in ks], vs)


NA_RB = 4
NA_WIN_ROWS = NA_RB + NA_WIN_H - 1
NA_SUB = 2


def _na_block_plan(rows):
    plan = []
    for blk in range(rows // NA_RB):
        r0 = blk * NA_RB
        rs = [int(np.clip(r0 + i - NA_WIN_H // 2, 0, rows - NA_WIN_H)) for i in range(NA_RB)]
        ws = int(np.clip(r0 - NA_WIN_H // 2, 0, rows - NA_WIN_ROWS))
        pat = tuple((rs[i] - ws, r0 + i - ws) for i in range(NA_RB))
        assert all(0 <= o and o + NA_WIN_H <= NA_WIN_ROWS for o, _ in pat)
        plan.append(pat)
    assert all(p == plan[1] for p in plan[1:-1])
    return (plan[0], plan[1], plan[-1])


def _na_bias_table(rpb, rows):
    cols = np.arange(GRID_W)
    col_start = np.clip(cols - NA_WIN_W // 2, 0, GRID_W - NA_WIN_W)
    col_mask = (cols[None, :] >= col_start[:, None]) & (cols[None, :] < col_start[:, None] + NA_WIN_W)
    col_idx = np.clip(cols[None, :] - cols[:, None] + NA_WIN_W - 1, 0, 2 * NA_WIN_W - 2)
    wr = np.arange(NA_WIN_ROWS)
    tables = []
    for pat in _na_block_plan(rows):
        off = np.array([o for o, _ in pat])[:, None]
        rq = np.array([r for _, r in pat])[:, None]
        row_ok = (wr[None, :] >= off) & (wr[None, :] < off + NA_WIN_H)
        ridx = np.clip(wr[None, :] - rq + NA_WIN_H - 1, 0, 2 * NA_WIN_H - 2)
        t = rpb.astype(F32)[:, ridx]
        pick = jnp.asarray((col_idx[None] == np.arange(2 * NA_WIN_W - 1)[:, None, None]).astype(np.float32))
        t = jnp.einsum("hrwc,cqk->hrqwk", t, pick, precision=lax.Precision.HIGHEST)
        ok = row_ok[None, :, None, :, None] & col_mask[None, None, :, None, :]
        t = jnp.where(ok, t * LOG2E, NEG_INF)
        tables.append(t.reshape(NA_HEADS, NA_RB * GRID_W, NA_WIN_ROWS * GRID_W))
    return jnp.stack(tables)


def _na_kernel(rows, sub, q_ref, k_ref, v_ref, kc_ref, vc_ref, *refs):
    bt_refs, o_ref = refs[:sub], refs[sub]
    step = pl.program_id(1)
    tq = NA_RB * GRID_W
    m0, m1 = _half_masks()
    for u in range(sub):
        r0 = (step * sub + u) * NA_RB
        ws = jnp.clip(r0 - NA_WIN_H // 2, 0, rows - NA_WIN_ROWS)
        win = pl.ds(pl.multiple_of(ws * GRID_W, GRID_W), NA_WIN_ROWS * GRID_W)
        qrows = slice(u * tq, (u + 1) * tq)
        for j in range(NA_HEADS // 2):
            sl = slice(j * LANES, (j + 1) * LANES)
            qp = q_ref[0, qrows, sl]
            kp, vp = k_ref[0, win, sl], v_ref[0, win, sl]
            kcp, vcp = kc_ref[0, :, sl], vc_ref[0, :, sl]
            outs = []
            for par, msk in ((0, m0), (1, m1)):
                qm = jnp.where(msk, qp, jnp.zeros_like(qp))
                s_loc = _dot_nt(qm, kp) + bt_refs[u][0, 2 * j + par]
                outs.append(_softmax_values([s_loc, _dot_nt(qm, kcp)], [vp, vcp]))
            o_ref[0, qrows, sl] = jnp.where(m0, outs[0], outs[1]).astype(BF16)


def _na_attention(q, k, v, kc, vc, rpb):
    bsz, s, w = q.shape
    l = kc.shape[1]
    rows = s // GRID_W
    nblk = rows // NA_RB
    sub = NA_SUB
    nstep = nblk // sub
    assert rows % NA_RB == 0 and rows >= NA_WIN_ROWS and nblk % sub == 0 and nblk >= 3
    bt = _na_bias_table(rpb, rows)
    tq = NA_RB * GRID_W * sub
    full = lambda n: pl.BlockSpec((1, n, w), lambda b, r: (b, 0, 0))

    def bt_spec(u):
        def kind(b, r):
            blk = r * sub + u
            return ((blk > 0).astype(jnp.int32) + (blk == nblk - 1).astype(jnp.int32), 0, 0, 0)
        return pl.BlockSpec((1,) + bt.shape[1:], kind)

    return pl.pallas_call(
        functools.partial(_na_kernel, rows, sub),
        grid=(bsz, nstep),
        in_specs=[pl.BlockSpec((1, tq, w), lambda b, r: (b, r, 0)),
                  full(s), full(s), full(l), full(l)] + [bt_spec(u) for u in range(sub)],
        out_specs=pl.BlockSpec((1, tq, w), lambda b, r: (b, r, 0)),
        out_shape=jax.ShapeDtypeStruct((bsz, s, w), BF16),
        compiler_params=_cparams(2),
    )(q, k, v, kc, vc, *([bt] * sub))


def _slot_attn_kernel(n_heads, q_slot, has_lat, sub, *refs):
    if has_lat:
        q_ref, k_ref, v_ref, kc_ref, vc_ref, o_ref = refs
    else:
        q_ref, kc_ref, vc_ref, o_ref = refs
    m0, m1 = _half_masks()
    tq = q_ref.shape[1] // sub
    for u in range(sub):
        qrows = slice(u * tq, (u + 1) * tq)
        for j in range(n_heads // 2):
            vsl = slice(j * LANES, (j + 1) * LANES)
            outs = []
            for par, msk in ((0, m0), (1, m1)):
                h = 2 * j + par
                if q_slot:
                    ksl = slice(h * LANES, (h + 1) * LANES)
                    qm = q_ref[0, qrows, ksl]
                else:
                    ksl = vsl
                    qp = q_ref[0, qrows, vsl]
                    qm = jnp.where(msk, qp, jnp.zeros_like(qp))
                ks, vs = [kc_ref[0, :, ksl]], [vc_ref[0, :, vsl]]
                if has_lat:
                    ks.insert(0, k_ref[0, :, ksl])
                    vs.insert(0, v_ref[0, :, vsl])
                outs.append(_attend(qm, ks, vs))
            o_ref[0, qrows, vsl] = jnp.where(m0, outs[0], outs[1]).astype(BF16)


def _slot_attention(q, k, v, kc, vc, n_heads, q_slot, tq, sub=1):
    bsz, sq, wq = q.shape
    l, wk, wv = kc.shape[1], kc.shape[2], vc.shape[2]
    has_lat = k is not None
    qspec = pl.BlockSpec((1, tq, wq), lambda b, i: (b, i, 0))
    full = lambda n, w: pl.BlockSpec((1, n, w), lambda b, i: (b, 0, 0))
    in_specs, args = [qspec], [q]
    if has_lat:
        s = k.shape[1]
        in_specs += [full(s, wk), full(s, wv)]
        args += [k, v]
    in_specs += [full(l, wk), full(l, wv)]
    args += [kc, vc]
    return pl.pallas_call(
        functools.partial(_slot_attn_kernel, n_heads, q_slot, has_lat, sub),
        grid=(bsz, sq // tq),
        in_specs=in_specs,
        out_specs=pl.BlockSpec((1, tq, wv), lambda b, i: (b, i, 0)),
        out_shape=jax.ShapeDtypeStruct((bsz, sq, wv), BF16),
        compiler_params=_cparams(2),
    )(*args)


def _gqa_kernel(has_lat, tq, sub, *refs):
    if has_lat:
        q_ref, k_ref, v_ref, kc_ref, vc_ref, o_ref = refs
    else:
        q_ref, kc_ref, vc_ref, o_ref = refs
    masks = _half_masks()
    for u in range(sub):
        qrows = slice(u * tq, (u + 1) * tq)
        res = {}
        for g in range(GQA_KV_HEADS):
            for var in range(2):
                par = g if var == 0 else 1 - g
                heads = (4 * g + par, 4 * g + 2 + par)
                vsl = slice(var * LANES, (var + 1) * LANES)
                qs = []
                for h in heads:
                    qp = q_ref[0, qrows, (h // 2) * LANES:(h // 2 + 1) * LANES]
                    qs.append(jnp.where(masks[par], qp, jnp.zeros_like(qp)))
                qm = jnp.concatenate(qs, axis=0)
                ks, vs = [kc_ref[0, :, vsl]], [vc_ref[0, :, vsl]]
                if has_lat:
                    ks.insert(0, k_ref[0, :, vsl])
                    vs.insert(0, v_ref[0, :, vsl])
                o = _attend(qm, ks, vs)
                res[heads[0]] = o[:tq]
                res[heads[1]] = o[tq:]
        for j in range(GQA_HEADS // 2):
            o_ref[0, qrows, j * LANES:(j + 1) * LANES] = jnp.where(
                masks[0], res[2 * j], res[2 * j + 1]).astype(BF16)


def _gqa_attention(q, k2, v2, k2c, v2c, tq, sub=1):
    bsz, sq, wq = q.shape
    l = k2c.shape[1]
    has_lat = k2 is not None
    full = lambda n: pl.BlockSpec((1, n, 2 * LANES), lambda b, i: (b, 0, 0))
    in_specs, args = [pl.BlockSpec((1, tq * sub, wq), lambda b, i: (b, i, 0))], [q]
    if has_lat:
        in_specs += [full(k2.shape[1])] * 2
        args += [k2, v2]
    in_specs += [full(l)] * 2
    args += [k2c, v2c]
    return pl.pallas_call(
        functools.partial(_gqa_kernel, has_lat, tq, sub),
        grid=(bsz, sq // (tq * sub)),
        in_specs=in_specs,
        out_specs=pl.BlockSpec((1, tq * sub, wq), lambda b, i: (b, i, 0)),
        out_shape=jax.ShapeDtypeStruct((bsz, sq, wq), BF16),
        compiler_params=_cparams(2),
    )(*args)


def _diff_kernel(has_lat, tq, sub, lam_init, *refs):
    if has_lat:
        q_ref, k_ref, v_ref, kc_ref, vc_ref, lq1, lk1, lq2, lk2, sub_ref, o_ref = refs
    else:
        q_ref, kc_ref, vc_ref, lq1, lk1, lq2, lk2, sub_ref, o_ref = refs
    lam = (jnp.exp(jnp.sum(lq1[...] * lk1[...], axis=-1, keepdims=True))
           - jnp.exp(jnp.sum(lq2[...] * lk2[...], axis=-1, keepdims=True)) + lam_init)
    m0, m1 = _half_masks()
    for u in range(sub):
        qrows = slice(u * tq, (u + 1) * tq)
        for h in range(DIFF_HEADS):
            sl = slice(h * LANES, (h + 1) * LANES)
            qp = q_ref[0, qrows, sl]
            zero = jnp.zeros_like(qp)
            qm = jnp.concatenate([jnp.where(m0, qp, zero), jnp.where(m1, qp, zero)], axis=0)
            ks, vs = [kc_ref[0, :, sl]], [vc_ref[0, :, sl]]
            if has_lat:
                ks.insert(0, k_ref[0, :, sl])
                vs.insert(0, v_ref[0, :, sl])
            o2 = _attend(qm, ks, vs)
            o = o2[:tq] - lam * o2[tq:]
            o = o * lax.rsqrt(jnp.mean(o * o, axis=-1, keepdims=True) + RMS_EPS) * sub_ref[...]
            o_ref[0, qrows, sl] = (o * (1.0 - lam_init)).astype(BF16)


def _diff_attention(q, k, v, kc, vc, lams, subln, lam_init, tq, sub=1):
    bsz, sq, w = q.shape
    l = kc.shape[1]
    has_lat = k is not None
    full = lambda n: pl.BlockSpec((1, n, w), lambda b, i: (b, 0, 0))
    in_specs, args = [pl.BlockSpec((1, tq * sub, w), lambda b, i: (b, i, 0))], [q]
    if has_lat:
        in_specs += [full(k.shape[1])] * 2
        args += [k, v]
    in_specs += [full(l)] * 2 + [_const_spec((1, HEAD_DIM))] * 4 + [_const_spec((1, LANES))]
    args += [kc, vc] + [a.reshape(1, -1).astype(F32) for a in lams] + [subln.reshape(1, -1).astype(F32)]
    return pl.pallas_call(
        functools.partial(_diff_kernel, has_lat, tq, sub, lam_init),
        grid=(bsz, sq // (tq * sub)),
        in_specs=in_specs,
        out_specs=pl.BlockSpec((1, tq * sub, w), lambda b, i: (b, i, 0)),
        out_shape=jax.ShapeDtypeStruct((bsz, sq, w), BF16),
        compiler_params=_cparams(2),
    )(*args)


def _pack_bf16_pairs(v):
    w = v.shape[1] // 2
    hi = pltpu.bitcast(v[:, :w].astype(F32), jnp.int32)
    lo = pltpu.bitcast(v[:, w:].astype(F32), jnp.int32)
    return hi | lax.shift_right_logical(lo, 16)


def _unpack_bf16_pairs(u):
    hi = pltpu.bitcast(u & jnp.int32(-65536), F32)
    lo = pltpu.bitcast(lax.shift_left(u, 16), F32)
    return hi, lo


def _layer_norm(z, g, b):
    mu = jnp.mean(z, axis=-1, keepdims=True)
    zc = z - mu
    var = jnp.mean(zc * zc, axis=-1, keepdims=True)
    return zc * lax.rsqrt(var + LN_EPS) * g + b


def _outproj_kernel(alpha, o1_ref, o2_ref, w1_ref, w2_ref, x_ref, g1_ref, lng_ref, lnb_ref,
                    sc2_ref, sh2_ref, rwh_ref, rwl_ref, rb_ref, tri_ref, cnt0_ref,
                    xo_ref, h2a_ref, h2b_ref, idx_ref, rank_ref, gw_ref, cnt_ref, carry_ref):
    @pl.when(jnp.logical_and(pl.program_id(0) == 0, pl.program_id(1) == 0))
    def _():
        carry_ref[...] = cnt0_ref[...]

    carry = carry_ref[...]
    o = _dot(o1_ref[0], w1_ref[...]) + _dot(o2_ref[0], w2_ref[...])
    xn = _layer_norm(alpha * x_ref[0] + g1_ref[0] * o, lng_ref[...], lnb_ref[...])
    xo_ref[0] = xn
    h2 = xn * (1.0 + sc2_ref[0]) + sh2_ref[0]
    hi = h2.astype(BF16)
    packed = _pack_bf16_pairs(hi)
    q = packed.shape[1] // 2
    h2a_ref[0] = packed[:, :q]
    h2b_ref[0] = packed[:, q:]
    lo = (h2 - hi.astype(F32)).astype(BF16)
    logits = (_dot(hi, rwh_ref[...]) + _dot(lo, rwh_ref[...]) + _dot(hi, rwl_ref[...])) + rb_ref[...]
    lane = lax.broadcasted_iota(jnp.int32, logits.shape, 1).astype(F32)
    vals, idxs = [], []
    cur = logits
    for _ in range(TOP_K):
        m = jnp.max(cur, axis=-1, keepdims=True)
        ik = jnp.min(jnp.where(cur == m, lane, float(LANES)), axis=-1, keepdims=True)
        vals.append(m)
        idxs.append(ik)
        cur = jnp.where(lane == ik, -jnp.inf, cur)
    ws = [jnp.exp(v - vals[0]) for v in vals]
    inv = 1.0 / functools.reduce(lambda a, b: a + b, ws)
    sel = [lane == ik for ik in idxs]
    onehot = functools.reduce(lambda a, b: a + b, [m.astype(F32) for m in sel])
    before = _dot(tri_ref[...], onehot.astype(BF16)) + carry
    idx_out = jnp.zeros_like(logits)
    rank_out = jnp.zeros_like(logits)
    w_out = jnp.zeros_like(logits)
    for k in range(TOP_K):
        rk = jnp.sum(jnp.where(sel[k], before, 0.0), axis=-1, keepdims=True)
        idx_out = jnp.where(lane == float(k), idxs[k], idx_out)
        rank_out = jnp.where(lane == float(k), rk, rank_out)
        w_out = jnp.where(lane == float(k), ws[k] * inv, w_out)
    idx_ref[0] = idx_out.astype(jnp.int32)
    rank_ref[0] = rank_out.astype(jnp.int32)
    gw_ref[0] = w_out
    carry = carry + jnp.sum(onehot, axis=0, keepdims=True)
    carry_ref[...] = carry
    cnt_ref[...] = carry


def _outproj_ln_router(o1, o2, w_out, x, g1, lng, lnb, sc2, sh2, router, cnt0, alpha, ts):
    bsz, s, d = x.shape
    batched = g1.shape[0] > 1
    rwh, rwl, rb = router
    w1, w2 = w_out[:512].astype(BF16), w_out[512:].astype(BF16)
    tri = jnp.asarray(np.tril(np.ones((ts, ts), np.float32), -1), dtype=BF16)
    tok = lambda n: pl.BlockSpec((1, ts, n), lambda b, i: (b, i, 0))
    ms = _mod_spec(d, batched)
    return pl.pallas_call(
        functools.partial(_outproj_kernel, alpha),
        grid=(bsz, s // ts),
        in_specs=[tok(512), tok(512), _const_spec((512, d)), _const_spec((512, d)), tok(d), ms,
                  _const_spec((1, d)), _const_spec((1, d)), ms, ms,
                  _const_spec((d, LANES)), _const_spec((d, LANES)), _const_spec((1, LANES)),
                  _const_spec((ts, ts)), _const_spec((1, LANES))],
        out_specs=[tok(d), tok(d // 4), tok(d // 4), tok(LANES), tok(LANES), tok(LANES),
                   _const_spec((1, LANES))],
        out_shape=[jax.ShapeDtypeStruct((bsz, s, d), F32),
                   jax.ShapeDtypeStruct((bsz, s, d // 4), jnp.int32),
                   jax.ShapeDtypeStruct((bsz, s, d // 4), jnp.int32),
                   jax.ShapeDtypeStruct((bsz, s, LANES), jnp.int32),
                   jax.ShapeDtypeStruct((bsz, s, LANES), jnp.int32),
                   jax.ShapeDtypeStruct((bsz, s, LANES), F32),
                   jax.ShapeDtypeStruct((1, LANES), F32)],
        scratch_shapes=[pltpu.VMEM((1, LANES), F32)],
        compiler_params=_cparams(2),
    )(o1, o2, w1, w2, x, g1, lng.reshape(1, d), lnb.reshape(1, d), sc2, sh2, rwh, rwl, rb, tri, cnt0)


def _router_weights(router_w, router_b):
    d, e = router_w.shape
    wp = jnp.zeros((d, LANES), F32).at[:, :e].set(router_w)
    hi = wp.astype(BF16)
    lo = (wp - hi.astype(F32)).astype(BF16)
    rb = jnp.full((1, LANES), -jnp.inf, F32).at[0, :e].set(router_b)
    return hi, lo, rb


def _deinterleave_perm():
    p = np.zeros((GU_BLOCK, GU_BLOCK), np.float32)
    m = np.arange(GU_BLOCK // 2)
    p[2 * m, m] = 1.0
    p[2 * m + 1, GU_BLOCK // 2 + m] = 1.0
    return jnp.asarray(p, dtype=BF16)


def _ffn_kernel(te_ref, tv_ref, xa_ref, xb_ref, wgu_ref, bgu_ref, wd_ref, bd_ref, perm_ref, y_ref,
                wgu_s, wd_s):
    j = pl.program_id(0)
    n_blk = wgu_s.shape[1] // GU_BLOCK
    half = GU_BLOCK // 2

    @pl.when(jnp.logical_or(j == 0, te_ref[j] != te_ref[jnp.maximum(j - 1, 0)]))
    def _():
        for b in range(n_blk):
            sl = slice(b * GU_BLOCK, (b + 1) * GU_BLOCK)
            wgu_s[:, sl] = _dot(wgu_ref[0, 0, :, sl].astype(BF16), perm_ref[...]).astype(BF16)
        wd_s[...] = wd_ref[0, 0].astype(BF16)

    @pl.when(tv_ref[j] > 0)
    def _():
        a_hi, a_lo = _unpack_bf16_pairs(xa_ref[...])
        b_hi, b_lo = _unpack_bf16_pairs(xb_ref[...])
        x = jnp.concatenate([a_hi, b_hi, a_lo, b_lo], axis=-1).astype(BF16)
        acts = []
        for b in range(n_blk):
            sl = slice(b * GU_BLOCK, (b + 1) * GU_BLOCK)
            gu = _dot(x, wgu_s[:, sl]) + bgu_ref[0, :, sl]
            glu = jnp.minimum(gu[:, :half], SWIGLU_LIMIT)
            lin = jnp.clip(gu[:, half:], -SWIGLU_LIMIT, SWIGLU_LIMIT)
            acts.append(((lin + 1.0) * (glu * jax.nn.sigmoid(SWIGLU_ALPHA * glu))).astype(BF16))
        a = jnp.concatenate(acts, axis=-1)
        y_ref[...] = (_dot(a, wd_s[...]) + bd_ref[0]).astype(BF16)

    @pl.when(tv_ref[j] == 0)
    def _():
        y_ref[...] = jnp.zeros_like(y_ref)


def _expert_ffn(xsa, xsb, tile_expert, tile_valid, layer, w_gu, b_gu, w_down, b_down, tm):
    p, q = xsa.shape
    d = 4 * q
    _, e, _, f2 = w_gu.shape
    f = f2 // 2
    half = GU_BLOCK // 2
    bgu = jnp.stack([b_gu[:, 0::2].reshape(e, f // half, half),
                     b_gu[:, 1::2].reshape(e, f // half, half)], axis=2).reshape(e, 1, f2)
    wspec = lambda a, b: pl.BlockSpec((1, 1, a, b), lambda j, te, tv: (layer, te[j], 0, 0))
    bspec = lambda b: pl.BlockSpec((1, 1, b), lambda j, te, tv: (te[j], 0, 0))
    return pl.pallas_call(
        _ffn_kernel,
        grid_spec=pltpu.PrefetchScalarGridSpec(
            num_scalar_prefetch=2,
            grid=(p // tm,),
            in_specs=[pl.BlockSpec((tm, q), lambda j, te, tv: (j, 0)),
                      pl.BlockSpec((tm, q), lambda j, te, tv: (j, 0)),
                      wspec(d, f2), bspec(f2), wspec(f, d), bspec(d),
                      pl.BlockSpec((GU_BLOCK, GU_BLOCK), lambda j, te, tv: (0, 0))],
            out_specs=pl.BlockSpec((tm, d), lambda j, te, tv: (j, 0)),
            scratch_shapes=[pltpu.VMEM((d, f2), BF16), pltpu.VMEM((f, d), BF16)]),
        out_shape=jax.ShapeDtypeStruct((p, d), BF16),
        compiler_params=_cparams(1),
    )(tile_expert, tile_valid, xsa, xsb, w_gu, bgu, w_down, b_down.reshape(e, 1, d),
      _deinterleave_perm())


def _moe_layout(counts, t_total, tm):
    n_experts = counts.shape[0]
    pc = ((counts + tm - 1) // tm) * tm
    pend = jnp.cumsum(pc)
    pstart = pend - pc
    p = t_total * TOP_K + n_experts * tm
    tile_start = jnp.arange(p // tm, dtype=jnp.int32) * tm
    tile_expert = jnp.minimum(jnp.sum((tile_start[:, None] >= pend[None, :]).astype(jnp.int32), axis=1),
                              n_experts - 1).astype(jnp.int32)
    tile_valid = (tile_start < pend[-1]).astype(jnp.int32)
    return pstart, p, tile_expert, tile_valid


SC_WINDOW = 128


def _sc_dispatch(parts, p):
    w = parts[0][0].shape[1]
    k = parts[0][1].shape[0]
    mesh = plsc.VectorSubcoreMesh(core_axis_name="core", subcore_axis_name="subcore")

    @functools.partial(pl.kernel, out_type=jax.ShapeDtypeStruct((p, w), parts[0][0].dtype), mesh=mesh)
    def kern(*refs):
        o_hbm = refs[-1]

        def body(x_vmem, *i_vmems):
            for iv in i_vmems:
                pltpu.sync_copy(x_vmem, o_hbm.at[iv.at[0]])

        for n, (rows, _) in enumerate(parts):
            x_hbm, i_hbms = refs[n * (k + 1)], refs[n * (k + 1) + 1:(n + 1) * (k + 1)]
            pltpu.emit_pipeline(
                body,
                grid=(rows.shape[0] // SC_WINDOW,),
                in_specs=[pl.BlockSpec((SC_WINDOW, w), index_map=lambda i: (i, 0))]
                + [pl.BlockSpec((1, SC_WINDOW), index_map=lambda i: (0, i)) for _ in range(k)],
                out_specs=[],
                core_axis_name=("core", "subcore"),
                dimension_semantics=(pltpu.PARALLEL,),
            )(x_hbm, *i_hbms)

    args = []
    for rows, dpos_t in parts:
        args += [rows] + [dpos_t[kk:kk + 1] for kk in range(k)]
    return kern(*args)


def _combine_kernel(alpha, yg_ref, gw_ref, x_ref, g2_ref, lng_ref, lnb_ref, o_ref):
    gw = gw_ref[...]
    y = yg_ref[0].astype(F32) * gw[:, 0:1]
    for k in range(1, TOP_K):
        y = y + yg_ref[k].astype(F32) * gw[:, k:k + 1]
    o_ref[0] = _layer_norm(alpha * x_ref[0] + g2_ref[0] * y, lng_ref[...], lnb_ref[...])


def _combine_ln(yg, gw, x, g2, lng, lnb, alpha, yg_row0, ts):
    bsz, s, d = x.shape
    batched = g2.shape[0] > 1
    nblk = s // ts
    off = yg_row0 // ts
    return pl.pallas_call(
        functools.partial(_combine_kernel, alpha),
        grid=(bsz, nblk),
        in_specs=[pl.BlockSpec((TOP_K, ts, d), lambda b, i: (0, off + b * nblk + i, 0)),
                  pl.BlockSpec((ts, LANES), lambda b, i: (b * nblk + i, 0)),
                  pl.BlockSpec((1, ts, d), lambda b, i: (b, i, 0)),
                  _mod_spec(d, batched), _const_spec((1, d)), _const_spec((1, d))],
        out_specs=pl.BlockSpec((1, ts, d), lambda b, i: (b, i, 0)),
        out_shape=jax.ShapeDtypeStruct((bsz, s, d), F32),
        compiler_params=_cparams(2),
    )(yg, gw, x, g2, lng.reshape(1, d), lnb.reshape(1, d))


def kernel(x, c, ctx, c_ctx, mod_w, mod_b, ln1_g, ln1_b, ln2_g, ln2_b, even_w_in, even_w_out, na_rpb, gqa_q_gain, gqa_k_gain, odd_w_in, odd_w_out, diff_lq1, diff_lk1, diff_lq2, diff_lk2, diff_subln, mla_q_gain, mla_w_uq, mla_kv_gain, mla_w_ukv, router_w, router_b, exp_w_gu, exp_b_gu, exp_w_down, exp_b_down):
    bsz, s, d = x.shape
    l = ctx.shape[1]
    depth = mod_w.shape[0]
    n_experts = router_w.shape[-1]
    alpha = (2 * depth) ** 0.25
    ts = min(512, s)
    cts = min(512, bsz * l)
    tm = MOE_TILE

    pad = (-(bsz + 1)) % 8
    c_all = jnp.concatenate([c, c_ctx[None, :], jnp.zeros((pad, d), F32)], axis=0)
    mod = _modulation_all(c_all, mod_w, mod_b)

    for i in range(depth):
        last = i == depth - 1
        j = i // 2
        ml = [mod[i, :bsz, k * d:(k + 1) * d].reshape(bsz, 1, d) for k in range(6)]
        mc = [mod[i, bsz:bsz + 1, k * d:(k + 1) * d].reshape(1, 1, d) for k in range(6)]
        sh1, sc1, g1, sh2, sc2, g2 = ml
        csh1, csc1, cg1, csh2, csc2, cg2 = mc

        if i % 2 == 0:
            w = _even_weights(even_w_in[j])
            naq, nak, nav, gq, gk, gv = _inproj_even(
                x, sc1, sh1, w, _even_tables(s, gqa_q_gain[j], gqa_k_gain[j], True), ts)
            cnaq, cnak, cnav, cgq, cgk, cgv = [a.reshape(bsz, l, -1) for a in _inproj_even(
                ctx.reshape(1, bsz * l, d), csc1, csh1, w,
                _even_tables(cts, gqa_q_gain[j], gqa_k_gain[j], False), cts)]
            o1 = _na_attention(naq, nak, nav, cnak, cnav, na_rpb[j])
            o2 = _gqa_attention(gq, gk, gv, cgk, cgv, 256, ATTN_SUB)
            if not last:
                co1 = _slot_attention(cnaq, None, None, cnak, cnav, NA_HEADS, False, l)
                co2 = _gqa_attention(cgq, None, None, cgk, cgv, l)
            w_out = even_w_out[j]
        else:
            lam_init = 0.8 - 0.6 * math.exp(-0.3 * i)
            weights = _odd_weights(odd_w_in[j], mla_w_uq[j], mla_w_ukv[j])
            gains = (mla_q_gain[j], mla_kv_gain[j])
            dq, dk, dv, mq, mk, mv = _inproj_odd(x, sc1, sh1, weights, gains, _odd_tables(s, True), ts)
            cdq, cdk, cdv, cmq, cmk, cmv = [a.reshape(bsz, l, -1) for a in _inproj_odd(
                ctx.reshape(1, bsz * l, d), csc1, csh1, weights, gains, _odd_tables(cts, False), cts)]
            lams = (diff_lq1[j], diff_lk1[j], diff_lq2[j], diff_lk2[j])
            o1 = _diff_attention(dq, dk, dv, cdk, cdv, lams, diff_subln[j], lam_init, 256, ATTN_SUB)
            o2 = _slot_attention(mq, mk, mv, cmk, cmv, MLA_HEADS, True, 512)
            if not last:
                co1 = _diff_attention(cdq, None, None, cdk, cdv, lams, diff_subln[j], lam_init, l)
                co2 = _slot_attention(cmq, None, None, cmk, cmv, MLA_HEADS, True, l)
            w_out = odd_w_out[j]

        router = _router_weights(router_w[i], router_b[i])
        cnt0 = jnp.zeros((1, LANES), F32)
        x, h2a, h2b, ridx, rrank, rgw, cnt = _outproj_ln_router(
            o1, o2, w_out, x, g1, ln1_g[i], ln1_b[i], sc2, sh2, router, cnt0, alpha, ts)
        flat = lambda a: a.reshape(-1, a.shape[-1])
        routed = [[flat(a) for a in (h2a, h2b, ridx, rrank, rgw)]]
        if not last:
            one = lambda a: a.reshape(1, bsz * l, a.shape[-1])
            ctx, ch2a, ch2b, cidx, crank, cgw, cnt = _outproj_ln_router(
                one(co1), one(co2), w_out, one(ctx), cg1, ln1_g[i], ln1_b[i], csc2, csh2, router, cnt,
                alpha, cts)
            routed.append([flat(a) for a in (ch2a, ch2b, cidx, crank, cgw)])

        counts = cnt[0, :n_experts].astype(jnp.int32)
        t_total = sum(r[0].shape[0] for r in routed)
        pstart, p, tile_expert, tile_valid = _moe_layout(counts, t_total, tm)
        dpos = [(pstart[r[2][:, :TOP_K]] + r[3][:, :TOP_K]).T.astype(jnp.int32) for r in routed]
        xsa = _sc_dispatch([(r[0], dp) for r, dp in zip(routed, dpos)], p)
        xsb = _sc_dispatch([(r[1], dp) for r, dp in zip(routed, dpos)], p)
        ys = _expert_ffn(xsa, xsb, tile_expert, tile_valid, i, exp_w_gu, exp_b_gu[i],
                         exp_w_down, exp_b_down[i], tm)
        dpos_t = dpos[0] if last else jnp.concatenate(dpos, axis=1)
        yg = ys.at[dpos_t].get(mode="promise_in_bounds")

        x = _combine_ln(yg, routed[0][4], x, g2, ln2_g[i], ln2_b[i], alpha, 0, ts)
        if not last:
            ctx = _combine_ln(yg, routed[1][4], ctx, cg2, ln2_g[i], ln2_b[i], alpha, bsz * s,
                              cts).reshape(bsz, l, d)
    return x
```
